```python
import jax, jax.numpy as jnp
from jax import lax
import numpy as np

D_MODEL = 1024
BATCH = 4
SEQ = 4096
DEPTH = 2
DEC_BATCH = 32
DEC_SEQ = 4
PAST_LEN = 16384
PAGE_SIZE = 128

N_A_LAYERS = DEPTH // 2
N_B_LAYERS = DEPTH - N_A_LAYERS
CHUNK = 128
GMLP_WIDTH = 2 * D_MODEL
GMLP_GROUPS = 4
BRANCHES = ((128, 1), (512, 4), (2048, 16))
N_BRANCH = 3
N_STEPS = 128
HEADS_PER_BRANCH = 8
HEAD_DIM = 64
ATTN_WIDTH = HEADS_PER_BRANCH * HEAD_DIM
NUM_BUCKETS = 32
MAX_EXACT = NUM_BUCKETS // 2
REL_MAX_DIST = 2048
N_GROUPS = 4
EXPERTS_PER_GROUP = 8
N_EXPERTS = N_GROUPS * EXPERTS_PER_GROUP
TOP_K = 2
D_EXPERT = 512
MOE_BLOCK = 128
EPS = 1e-6

kernel_name = 'yoco_gmlp_dilated_attn_hmoe_step'


def rms_norm(x, g):
    xf = x.astype(jnp.float32)
    y = xf * lax.rsqrt(jnp.mean(xf * xf, axis=-1, keepdims=True) + EPS)
    return (y * g.astype(jnp.float32)).astype(x.dtype)


def modulate(h, shift, scale):
    return h * (1 + scale[:, None, :]) + shift[:, None, :]


def rel_buckets(dilation):
    n = np.arange(N_STEPS + 1) * dilation
    large = MAX_EXACT + (np.log(np.maximum(n, 1) / MAX_EXACT) / np.log(REL_MAX_DIST / MAX_EXACT)
                         * (NUM_BUCKETS - MAX_EXACT)).astype(np.int32)
    return np.where(n < MAX_EXACT, n, np.minimum(large, NUM_BUCKETS - 1)).astype(np.int32)


def spatial_gating(v, w_s, b_s):
    B, L, E = v.shape
    C = min(CHUNK, L)
    w = jnp.where(jnp.tril(jnp.ones((C, C), bool)), w_s[:, :C, :C], 0)
    vc = v.reshape(B, L // C, C, GMLP_GROUPS, E // GMLP_GROUPS)
    g = jnp.einsum('gts,bcsgd->bctgd', w, vc) + b_s[:, :C].T[None, None, :, :, None]
    return g.reshape(B, L, E)


def mixer_a(h, w_in, b_in, g_v, w_s, b_s, w_out):
    z = jax.nn.gelu(h @ w_in + b_in)
    u, v = jnp.split(z, 2, axis=-1)
    v = rms_norm(v, g_v)
    return (u * spatial_gating(v, w_s, b_s)) @ w_out, v


def routed_experts(t, experts, weights, w1, w3, w2):
    T, D = t.shape
    P = T * TOP_K
    flat_e = experts.reshape(-1)
    flat_t = jnp.repeat(jnp.arange(T, dtype=jnp.int32), TOP_K)
    flat_w = weights.reshape(-1)
    order = jnp.argsort(flat_e)
    se, st, sw = flat_e[order], flat_t[order], flat_w[order]
    counts = jnp.bincount(flat_e, length=N_EXPERTS)
    padded = (counts + MOE_BLOCK - 1) // MOE_BLOCK * MOE_BLOCK
    pad_end = jnp.cumsum(padded)
    pad_start = pad_end - padded
    start = jnp.cumsum(counts) - counts
    slot = pad_start[se] + (jnp.arange(P) - start[se])
    nb = -(-P // MOE_BLOCK) + N_EXPERTS
    tok = jnp.zeros((nb * MOE_BLOCK,), jnp.int32).at[slot].set(st)
    wt = jnp.zeros((nb * MOE_BLOCK,), t.dtype).at[slot].set(sw)
    blk_e = jnp.minimum(jnp.searchsorted(pad_end, jnp.arange(nb) * MOE_BLOCK, side='right'), N_EXPERTS - 1)

    def run(args):
        e, idx = args
        xb = t[idx]
        return (jax.nn.silu(xb @ w1[e]) * (xb @ w3[e])) @ w2[e]

    y = lax.map(run, (blk_e, tok.reshape(nb, MOE_BLOCK))).reshape(-1, D) * wt[:, None]
    return jax.ops.segment_sum(y, tok, num_segments=T)


def hier_moe(h, w_g, b_g, w_e, b_e, w1, w3, w2):
    B, L, D = h.shape
    t = h.reshape(-1, D)
    T = t.shape[0]
    gp = jax.nn.softmax((t @ w_g + b_g).astype(jnp.float32), axis=-1)
    g_p, g_i = lax.top_k(gp, 1)
    el = (t @ w_e + b_e).astype(jnp.float32).reshape(T, N_GROUPS, EXPERTS_PER_GROUP)
    el = jnp.take_along_axis(el, g_i[:, :, None], axis=1)[:, 0]
    e_p, e_i = lax.top_k(jax.nn.softmax(el, axis=-1), TOP_K)
    e_p = e_p / jnp.sum(e_p, axis=-1, keepdims=True)
    weights = (g_p * e_p).astype(h.dtype)
    experts = g_i * EXPERTS_PER_GROUP + e_i
    return routed_experts(t, experts, weights, w1, w3, w2).reshape(B, L, D)


def shared_kv(x, c, kv_ada_w, kv_ada_b, kv_norm, w_kv, k_norm):
    B, L, _ = x.shape
    shift, scale = jnp.split(jax.nn.silu(c) @ kv_ada_w + kv_ada_b, 2, axis=-1)
    h = modulate(rms_norm(x, kv_norm), shift, scale)
    kv = (h @ w_kv).reshape(B, L, N_BRANCH, 2, HEADS_PER_BRANCH, HEAD_DIM)
    k = rms_norm(kv[:, :, :, 0], k_norm[:, None, :])
    return jnp.stack([k, kv[:, :, :, 1]], axis=3)


def band_attention(q, k, v, bias, dilation):
    B, L, H, hd = q.shape
    QB = N_STEPS
    nb = -(-(-(-L // dilation)) // QB)
    Lp = nb * QB * dilation

    def blocks(a):
        a = jnp.pad(a, ((0, 0), (0, Lp - L), (0, 0), (0, 0)))
        a = a.reshape(B, nb * QB, dilation, H, hd).transpose(0, 2, 1, 3, 4)
        return a.reshape(B, dilation, nb, QB, H, hd)

    def band(a):
        prev = jnp.pad(a[:, :, :-1], ((0, 0), (0, 0), (1, 0), (0, 0), (0, 0), (0, 0)))
        return jnp.concatenate([prev, a], axis=3)

    qb, kb, vb = blocks(q), band(blocks(k)), band(blocks(v))
    delta = jnp.arange(QB)[:, None] + QB - jnp.arange(2 * QB)[None, :]
    ok = (delta >= 0) & (delta <= N_STEPS)
    first = (jnp.arange(nb)[:, None, None] == 0) & (jnp.arange(2 * QB)[None, None, :] < QB)
    ok = ok[None] & ~first
    s = jnp.einsum('brnqhe,brnkhe->brnhqk', qb, kb, preferred_element_type=jnp.float32) * HEAD_DIM ** -0.5
    s = s + bias[:, jnp.clip(delta, 0, N_STEPS)][None, None, None]
    s = jnp.where(ok[None, None, :, None], s, -jnp.inf)
    lse = jax.nn.logsumexp(s, axis=-1)
    p = jnp.exp(s - lse[..., None])
    o = jnp.einsum('brnhqk,brnkhe->brnqhe', p.astype(v.dtype), vb)
    o = o.reshape(B, dilation, nb * QB, H, hd).transpose(0, 2, 1, 3, 4).reshape(B, Lp, H, hd)[:, :L]
    lse = lse.transpose(0, 1, 2, 4, 3).reshape(B, dilation, nb * QB, H).transpose(0, 2, 1, 3)
    return o, lse.reshape(B, Lp, H)[:, :L]


def step_attention(q, k_all, v_all, bias, dilation, n_past):
    S = q.shape[1]
    idx = n_past + jnp.arange(S)[:, None] - dilation * jnp.arange(N_STEPS + 1)[None, :]
    ok = idx >= 0
    idx = jnp.maximum(idx, 0)
    kg, vg = k_all[:, idx], v_all[:, idx]
    s = jnp.einsum('bshe,bskhe->bhsk', q, kg, preferred_element_type=jnp.float32) * HEAD_DIM ** -0.5
    s = jnp.where(ok[None, None], s + bias[:, None, :], -jnp.inf)
    lse = jax.nn.logsumexp(s, axis=-1)
    p = jnp.exp(s - lse[..., None])
    o = jnp.einsum('bhsk,bskhe->bshe', p.astype(v_all.dtype), vg)
    return o, lse.transpose(0, 2, 1)


def mixer_b(h, kv, past, w_q, q_norm, w_o, rel_bias):
    B, L, _ = h.shape
    q = (h @ w_q).reshape(B, L, N_BRANCH, HEADS_PER_BRANCH, HEAD_DIM)
    q = rms_norm(q, q_norm[:, None, :])
    outs, lses = [], []
    for g, (win, dil) in enumerate(BRANCHES):
        bias = rel_bias[rel_buckets(dil), g].T.astype(jnp.float32)
        if past is None:
            o, lse = band_attention(q[:, :, g], kv[:, :, g, 0], kv[:, :, g, 1], bias, dil)
        else:
            kv_all = jnp.concatenate([past[g], kv[:, :, g]], axis=1)
            o, lse = step_attention(q[:, :, g], kv_all[:, :, 0], kv_all[:, :, 1], bias, dil, past[g].shape[1])
        outs.append(o)
        lses.append(lse)
    alpha = jax.nn.softmax(jnp.stack(lses), axis=0)
    o = jnp.sum(alpha[..., None] * jnp.stack(outs).astype(jnp.float32), axis=0)
    return o.reshape(B, L, ATTN_WIDTH).astype(h.dtype) @ w_o


def trunk(x, c, past, p):
    v_rows = []
    kv = None
    for layer in range(DEPTH):
        mod = jax.nn.silu(c) @ p['ada_w'][layer] + p['ada_b'][layer]
        sh1, sc1, g1, sh2, sc2, g2 = jnp.split(mod, 6, axis=-1)
        h = modulate(rms_norm(x, p['norm_mix'][layer]), sh1, sc1)
        if layer < N_A_LAYERS:
            i = layer
            a, v = mixer_a(h, p['a_w_in'][i], p['a_b_in'][i], p['a_norm_v'][i], p['a_w_s'][i],
                           p['a_b_s'][i], p['a_w_out'][i])
            v_rows.append(v)
        else:
            i = layer - N_A_LAYERS
            if kv is None:
                kv = shared_kv(x, c, p['kv_ada_w'], p['kv_ada_b'], p['kv_norm'], p['w_kv'], p['k_norm'])
            a = mixer_b(h, kv, past, p['b_w_q'][i], p['q_norm'][i], p['b_w_o'][i], p['rel_bias'])
        x = x + g1[:, None, :] * a
        h = modulate(rms_norm(x, p['norm_ffn'][layer]), sh2, sc2)
        x = x + g2[:, None, :] * hier_moe(h, p['r_w_group'][layer], p['r_b_group'][layer],
                                           p['r_w_expert'][layer], p['r_b_expert'][layer],
                                           p['e_w1'][layer], p['e_w3'][layer], p['e_w2'][layer])
    return x, kv, v_rows


def setup_inputs(seed: int = 0) -> dict:
    key = jax.random.key(seed)
    ks = list(jax.random.split(key, 40))

    def nrm(shape, s):
        return jax.random.normal(ks.pop(), shape, jnp.float32) * s

    D, H, HD = D_MODEL, HEADS_PER_BRANCH, HEAD_DIM
    E = GMLP_WIDTH
    return {
        'x_prompt': nrm((BATCH, SEQ, D), 1.0),
        'x_sample': nrm((DEC_BATCH, DEC_SEQ, D), 1.0),
        'cache_kv_w128': nrm((DEC_BATCH, min(BRANCHES[0][0], PAST_LEN), 2, H, HD), 1.0),
        'cache_kv_w512': nrm((DEC_BATCH, min(BRANCHES[1][0], PAST_LEN), 2, H, HD), 1.0),
        'cache_kv_w2048': nrm((DEC_BATCH, min(BRANCHES[2][0], PAST_LEN), 2, H, HD), 1.0),
        'c_prompt': nrm((BATCH, D), 1.0),
        'c_sample': nrm((DEC_BATCH, D), 1.0),
        'ada_w': nrm((DEPTH, D, 6 * D), 0.5 * D ** -0.5),
        'ada_b': nrm((DEPTH, 6 * D), 0.02),
        'norm_mix': 1.0 + nrm((DEPTH, D), 0.02),
        'norm_ffn': 1.0 + nrm((DEPTH, D), 0.02),
        'a_w_in': nrm((N_A_LAYERS, D, 2 * E), D ** -0.5),
        'a_b_in': nrm((N_A_LAYERS, 2 * E), 0.02),
        'a_norm_v': 1.0 + nrm((N_A_LAYERS, E), 0.02),
        'a_w_s': nrm((N_A_LAYERS, GMLP_GROUPS, CHUNK, CHUNK), CHUNK ** -0.5),
        'a_b_s': 1.0 + nrm((N_A_LAYERS, GMLP_GROUPS, CHUNK), 0.02),
        'a_w_out': nrm((N_A_LAYERS, E, D), E ** -0.5),
        'kv_ada_w': nrm((D, 2 * D), 0.5 * D ** -0.5),
        'kv_ada_b': nrm((2 * D,), 0.02),
        'kv_norm': 1.0 + nrm((D,), 0.02),
        'w_kv': nrm((D, N_BRANCH * 2 * ATTN_WIDTH), D ** -0.5),
        'k_norm': 1.0 + nrm((N_BRANCH, HD), 0.02),
        'rel_bias': nrm((NUM_BUCKETS, N_BRANCH, H), 0.5),
        'b_w_q': nrm((N_B_LAYERS, D, N_BRANCH * ATTN_WIDTH), D ** -0.5),
        'q_norm': 1.0 + nrm((N_B_LAYERS, N_BRANCH, HD), 0.02),
        'b_w_o': nrm((N_B_LAYERS, ATTN_WIDTH, D), ATTN_WIDTH ** -0.5),
        'r_w_group': nrm((DEPTH, D, N_GROUPS), D ** -0.5),
        'r_b_group': nrm((DEPTH, N_GROUPS), 0.01),
        'r_w_expert': nrm((DEPTH, D, N_EXPERTS), D ** -0.5),
        'r_b_expert': nrm((DEPTH, N_EXPERTS), 0.01),
        'e_w1': nrm((DEPTH, N_EXPERTS, D, D_EXPERT), D ** -0.5),
        'e_w3': nrm((DEPTH, N_EXPERTS, D, D_EXPERT), D ** -0.5),
        'e_w2': nrm((DEPTH, N_EXPERTS, D_EXPERT, D), D_EXPERT ** -0.5),
    }


def reference(x_prompt, x_sample, cache_kv_w128, cache_kv_w512, cache_kv_w2048, c_prompt, c_sample,
              ada_w, ada_b, norm_mix, norm_ffn, a_w_in, a_b_in, a_norm_v, a_w_s, a_b_s, a_w_out,
              kv_ada_w, kv_ada_b, kv_norm, w_kv, k_norm, rel_bias, b_w_q, q_norm, b_w_o,
              r_w_group, r_b_group, r_w_expert, r_b_expert, e_w1, e_w3, e_w2):
    p = dict(ada_w=ada_w, ada_b=ada_b, norm_mix=norm_mix, norm_ffn=norm_ffn, a_w_in=a_w_in, a_b_in=a_b_in,
             a_norm_v=a_norm_v, a_w_s=a_w_s, a_b_s=a_b_s, a_w_out=a_w_out, kv_ada_w=kv_ada_w,
             kv_ada_b=kv_ada_b, kv_norm=kv_norm, w_kv=w_kv, k_norm=k_norm, rel_bias=rel_bias,
             b_w_q=b_w_q, q_norm=q_norm, b_w_o=b_w_o, r_w_group=r_w_group, r_b_group=r_b_group,
             r_w_expert=r_w_expert, r_b_expert=r_b_expert, e_w1=e_w1, e_w3=e_w3, e_w2=e_w2)
    y_prompt, kv_p, _ = trunk(x_prompt, c_prompt, None, p)
    y_sample, kv_s, v_rows = trunk(x_sample, c_sample, (cache_kv_w128, cache_kv_w512, cache_kv_w2048), p)
    L = x_prompt.shape[1]
    n128 = min(BRANCHES[0][0], L)
    n512 = min(BRANCHES[1][0], L)
    n2048 = min(BRANCHES[2][0], L)
    kv_w128_prompt = kv_p[:, L - n128:, 0]
    kv_w512_prompt = kv_p[:, L - n512:, 1]
    kv_w2048_prompt = kv_p[:, L - n2048:, 2]
    v_sample = jnp.stack(v_rows)
    return (y_prompt, y_sample, kv_w128_prompt, kv_w512_prompt, kv_w2048_prompt,
            kv_s[:, :, 0], kv_s[:, :, 1], kv_s[:, :, 2], v_sample)
```

```python
import functools

import numpy as np
import jax
import jax.numpy as jnp
from jax import lax
from jax.experimental import pallas as pl
from jax.experimental.pallas import tpu as pltpu

F32 = jnp.float32
BF16 = jnp.bfloat16

D_MODEL = 1024
GMLP_WIDTH = 2048
GMLP_GROUPS = 4
GROUP_WIDTH = GMLP_WIDTH // GMLP_GROUPS
CHUNK = 128
BRANCHES = ((128, 1), (512, 4), (2048, 16))
N_BRANCH = 3
N_STEPS = 128
HEADS = 8
HEAD_DIM = 64
ATTN_WIDTH = HEADS * HEAD_DIM
NUM_BUCKETS = 32
MAX_EXACT = NUM_BUCKETS // 2
REL_MAX_DIST = 2048
N_GROUPS = 4
EXPERTS_PER_GROUP = 8
N_EXPERTS = N_GROUPS * EXPERTS_PER_GROUP
TOP_K = 2
D_EXPERT = 512
EPS = 1e-6
NEG = -1e30

LANES = 128
ROW_TILES = D_MODEL // LANES
TOKEN_TILE = 256
MOE_BLOCK = 256
VMEM_LIMIT = 52 * 1024 * 1024


def _cparams(sem):
    return pltpu.CompilerParams(dimension_semantics=sem, vmem_limit_bytes=VMEM_LIMIT)


def _resident(shape):
    nd = len(shape)
    return pl.BlockSpec(shape, lambda *_, _nd=nd: (0,) * _nd, pipeline_mode=pl.Buffered(1))


def _gelu_tanh(x):
    return 0.5 * x * (1.0 + jnp.tanh(0.7978845608028654 * (x + 0.044715 * (x * x * x))))


def _rms(x, g):
    return x * lax.rsqrt(jnp.mean(x * x, axis=-1, keepdims=True) + EPS) * g


def _store_rows8(ref, val, n, base=0):
    for s in range(ROW_TILES):
        ref[pl.ds(base + s, n, stride=ROW_TILES), :] = val[:, s * LANES:(s + 1) * LANES]


def _load_rows8(ref, n, base=0):
    return jnp.concatenate([ref[pl.ds(base + s, n, stride=ROW_TILES), :] for s in range(ROW_TILES)], axis=1)


def _split(a):
    hi = a.astype(BF16)
    return hi, (a - hi.astype(F32)).astype(BF16)


def _split_dot(a, e_bf16):
    hi, lo = _split(a)
    return (jnp.dot(hi, e_bf16, preferred_element_type=F32) + jnp.dot(lo, e_bf16, preferred_element_type=F32))


def _dot3(a, w_ref):
    hi, lo = _split(a)
    return (jnp.dot(hi, w_ref[0], preferred_element_type=F32) + jnp.dot(lo, w_ref[0], preferred_element_type=F32)
            + jnp.dot(hi, w_ref[1], preferred_element_type=F32))


def _ffn_pre(x, mod, nf_ref, wr_ref, br_ref, h_ref, lg_ref, n):
    h = _rms(x, nf_ref[...]) * (1.0 + mod[:, 4 * D_MODEL:5 * D_MODEL]) + mod[:, 3 * D_MODEL:4 * D_MODEL]
    _store_rows8(h_ref, h, n)
    lg_ref[...] = _dot3(h, wr_ref) + br_ref[...]


def _ada_kernel(c_ref, w_ref, b_ref, o_ref):
    c = c_ref[...]
    a = (c * jax.nn.sigmoid(c)).astype(BF16)
    o_ref[...] = jnp.dot(a, w_ref[...].astype(BF16), preferred_element_type=F32) + b_ref[...]


def _ada(c, w, b):
    R = c.shape[0]
    N = w.shape[1]
    tn = 1024
    return pl.pallas_call(
        _ada_kernel,
        grid=(N // tn,),
        in_specs=[pl.BlockSpec((R, D_MODEL), lambda j: (0, 0)),
                  pl.BlockSpec((D_MODEL, tn), lambda j: (0, j)),
                  pl.BlockSpec((1, tn), lambda j: (0, j))],
        out_specs=pl.BlockSpec((R, tn), lambda j: (0, j)),
        out_shape=jax.ShapeDtypeStruct((R, N), F32),
        compiler_params=_cparams(("arbitrary",)),
        name="ada",
    )(c, w, b.reshape(1, N))


def _mixer_a_kernel(x_ref, mod_ref, nm_ref, win_ref, bin_ref, gv_ref, ws_ref, bs_ref, wout_ref,
                    nf_ref, wr_ref, br_ref, x1_ref, h2_ref, lg_ref, *v_refs, tm):
    x = x_ref[...]
    mod = mod_ref[0]
    h = (_rms(x, nm_ref[...]) * (1.0 + mod[:, D_MODEL:2 * D_MODEL]) + mod[:, 0:D_MODEL]).astype(BF16)
    zv = jnp.dot(h, win_ref[:, GMLP_WIDTH:], preferred_element_type=F32) + bin_ref[:, GMLP_WIDTH:]
    v = _rms(_gelu_tanh(zv), gv_ref[...])
    if v_refs:
        v_refs[0][...] = v
    vb = v.astype(BF16)
    bs = bs_ref[...]
    acc = jnp.zeros((tm, D_MODEL), F32)
    for g in range(GMLP_GROUPS):
        lo, hi = g * GROUP_WIDTH, (g + 1) * GROUP_WIDTH
        u = _gelu_tanh(jnp.dot(h, win_ref[:, lo:hi], preferred_element_type=F32) + bin_ref[:, lo:hi])
        wg = ws_ref[g]
        gate = jnp.concatenate(
            [jnp.dot(wg, vb[c * CHUNK:(c + 1) * CHUNK, lo:hi], preferred_element_type=F32) + bs[:, g:g + 1]
             for c in range(tm // CHUNK)], axis=0)
        acc = acc + jnp.dot((u * gate).astype(BF16), wout_ref[lo:hi, :], preferred_element_type=F32)
    x1 = x + mod[:, 2 * D_MODEL:3 * D_MODEL] * acc
    x1_ref[...] = x1
    _ffn_pre(x1, mod, nf_ref, wr_ref, br_ref, h2_ref, lg_ref, tm)


def _mod_spec(mod, tiles_per_mod):
    _, rows, width = mod.shape
    return pl.BlockSpec((1, rows, width), lambda i: (i // tiles_per_mod, 0, 0))


def _mixer_a(x, mod, tiles_per_mod, tm, nm, win, bin_, gv, ws, bs_t, wout, nf, wr, br, with_v):
    T = x.shape[0]
    tok = lambda w: pl.BlockSpec((tm, w), lambda i: (i, 0))
    out_shape = [jax.ShapeDtypeStruct((T, D_MODEL), F32),
                 jax.ShapeDtypeStruct((T * ROW_TILES, LANES), F32),
                 jax.ShapeDtypeStruct((T, LANES), F32)]
    out_specs = [tok(D_MODEL), pl.BlockSpec((tm * ROW_TILES, LANES), lambda i: (i, 0)), tok(LANES)]
    if with_v:
        out_shape.append(jax.ShapeDtypeStruct((T, GMLP_WIDTH), F32))
        out_specs.append(tok(GMLP_WIDTH))
    return pl.pallas_call(
        functools.partial(_mixer_a_kernel, tm=tm),
        grid=(T // tm,),
        in_specs=[tok(D_MODEL), _mod_spec(mod, tiles_per_mod), _resident(nm.shape), _resident(win.shape),
                  _resident(bin_.shape), _resident(gv.shape), _resident(ws.shape), _resident(bs_t.shape),
                  _resident(wout.shape), _resident(nf.shape), _resident(wr.shape), _resident(br.shape)],
        out_specs=out_specs,
        out_shape=out_shape,
        compiler_params=_cparams(("arbitrary",)),
        name="mixer_a",
    )(x, mod, nm, win, bin_, gv, ws, bs_t, wout, nf, wr, br)


def _route(logits):
    T = logits.shape[0]
    gp = jax.nn.softmax(logits[:, :N_GROUPS], axis=-1)
    g_p, g_i = lax.top_k(gp, 1)
    el = logits[:, N_GROUPS:N_GROUPS + N_EXPERTS].reshape(T, N_GROUPS, EXPERTS_PER_GROUP)
    el = jnp.take_along_axis(el, g_i[:, :, None], axis=1)[:, 0]
    e_p, e_i = lax.top_k(jax.nn.softmax(el, axis=-1), TOP_K)
    e_p = e_p / jnp.sum(e_p, axis=-1, keepdims=True)
    return g_i * EXPERTS_PER_GROUP + e_i, g_p * e_p


def _slots(experts, block):
    T = experts.shape[0]
    flat = experts.reshape(-1)
    onehot = (flat[:, None] == jnp.arange(N_EXPERTS, dtype=jnp.int32)[None, :]).astype(jnp.int32)
    csum = jnp.cumsum(onehot, axis=0)
    rank = jnp.take_along_axis(csum, flat[:, None], axis=1)[:, 0] - 1
    counts = csum[-1]
    padded = (counts + block - 1) // block * block
    pad_end = jnp.cumsum(padded)
    slot = (pad_end - padded)[flat] + rank
    nb = -(-T * TOP_K // block) + N_EXPERTS
    blk_e = jnp.minimum(jnp.searchsorted(pad_end, jnp.arange(nb, dtype=jnp.int32) * block, side='right'),
                        N_EXPERTS - 1).astype(jnp.int32)
    n_used = (pad_end[-1] // block).astype(jnp.int32).reshape(1)
    return slot.astype(jnp.int32).reshape(T, TOP_K), blk_e, n_used, nb


def _row_copy(src_ref, src_row, dst_ref, dst_row, sem):
    return pltpu.make_async_copy(
        src_ref.at[pl.ds(pl.multiple_of(src_row * ROW_TILES, ROW_TILES), ROW_TILES), :],
        dst_ref.at[pl.ds(pl.multiple_of(dst_row * ROW_TILES, ROW_TILES), ROW_TILES), :], sem)


def _scatter_kernel(slot_ref, src_ref, dst_in_ref, dst_ref, sem, *, ts):
    del dst_in_ref
    base = pl.program_id(0) * ts

    def issue(t, c):
        for k in range(TOP_K):
            _row_copy(src_ref, base + t, dst_ref, slot_ref[0, 0, TOP_K * t + k], sem).start()
        return c
    lax.fori_loop(0, ts, issue, 0)

    def drain(t, c):
        _row_copy(src_ref, 0, dst_ref, 0, sem).wait()
        return c
    lax.fori_loop(0, ts * TOP_K, drain, 0)


def _scatter(slots, src, dst, ts):
    T = slots.shape[0]
    return pl.pallas_call(
        functools.partial(_scatter_kernel, ts=ts),
        grid=(T // ts,),
        in_specs=[pl.BlockSpec((1, 1, ts * TOP_K), lambda i: (i, 0, 0), memory_space=pltpu.SMEM),
                  pl.BlockSpec(memory_space=pl.ANY), pl.BlockSpec(memory_space=pl.ANY)],
        out_specs=pl.BlockSpec(memory_space=pl.ANY),
        out_shape=jax.ShapeDtypeStruct(dst.shape, dst.dtype),
        scratch_shapes=[pltpu.SemaphoreType.DMA(())],
        input_output_aliases={2: 0},
        compiler_params=_cparams(("arbitrary",)),
        name="moe_scatter",
    )(slots.reshape(T // ts, 1, ts * TOP_K), src, dst)


def _experts_kernel(be_ref, nu_ref, x_ref, w1_ref, w3_ref, w2_ref, y_ref, w1b, w3b, w2b, *, block):
    i = pl.program_id(0)

    @pl.when(i < nu_ref[0])
    def _():
        @pl.when(jnp.logical_or(i == 0, be_ref[i] != be_ref[jnp.maximum(i - 1, 0)]))
        def _():
            w1b[...] = w1_ref[0].astype(BF16)
            w3b[...] = w3_ref[0].astype(BF16)
            w2b[...] = w2_ref[0].astype(BF16)

        x = _load_rows8(x_ref, block).astype(BF16)
        h1 = jnp.dot(x, w1b[...], preferred_element_type=F32)
        h3 = jnp.dot(x, w3b[...], preferred_element_type=F32)
        a = (h1 * jax.nn.sigmoid(h1) * h3).astype(BF16)
        _store_rows8(y_ref, jnp.dot(a, w2b[...], preferred_element_type=F32), block)


def _experts(xs, blk_e, n_used, nb, w1, w3, w2, block):
    last = lambda i, nu: jnp.minimum(i, nu[0] - 1)
    rows = pl.BlockSpec((block * ROW_TILES, LANES), lambda i, be, nu: (last(i, nu), 0))
    return pl.pallas_call(
        functools.partial(_experts_kernel, block=block),
        grid_spec=pltpu.PrefetchScalarGridSpec(
            num_scalar_prefetch=2,
            grid=(nb,),
            in_specs=[rows,
                      pl.BlockSpec((1, D_MODEL, D_EXPERT), lambda i, be, nu: (be[last(i, nu)], 0, 0)),
                      pl.BlockSpec((1, D_MODEL, D_EXPERT), lambda i, be, nu: (be[last(i, nu)], 0, 0)),
                      pl.BlockSpec((1, D_EXPERT, D_MODEL), lambda i, be, nu: (be[last(i, nu)], 0, 0))],
            out_specs=rows,
            scratch_shapes=[pltpu.VMEM((D_MODEL, D_EXPERT), BF16), pltpu.VMEM((D_MODEL, D_EXPERT), BF16),
                            pltpu.VMEM((D_EXPERT, D_MODEL), BF16)]),
        out_shape=jax.ShapeDtypeStruct(xs.shape, F32),
        input_output_aliases={2: 0},
        compiler_params=_cparams(("arbitrary",)),
        name="moe_experts",
    )(blk_e, n_used, xs, w1, w3, w2)


def _combine_kernel(slot_ref, y_ref, x_ref, mod_ref, wt_ref, o_ref, ybuf, sem, *, tc):
    def issue(t, c):
        for k in range(TOP_K):
            _row_copy(y_ref, slot_ref[0, 0, TOP_K * t + k], ybuf, k * tc + t, sem).start()
        return c
    lax.fori_loop(0, tc, issue, 0)

    def drain(t, c):
        _row_copy(y_ref, 0, ybuf, 0, sem).wait()
        return c
    lax.fori_loop(0, tc * TOP_K, drain, 0)

    wt = wt_ref[...]
    y = (wt[:, 0:1] * _load_rows8(ybuf, tc) + wt[:, 1:2] * _load_rows8(ybuf, tc, base=tc * ROW_TILES))
    o_ref[...] = x_ref[...] + mod_ref[0][:, 5 * D_MODEL:6 * D_MODEL] * y


def _combine(slots, y, x, mod, tiles_per_mod, wts, tc):
    T = x.shape[0]
    return pl.pallas_call(
        functools.partial(_combine_kernel, tc=tc),
        grid=(T // tc,),
        in_specs=[pl.BlockSpec((1, 1, tc * TOP_K), lambda i: (i, 0, 0), memory_space=pltpu.SMEM),
                  pl.BlockSpec(memory_space=pl.ANY),
                  pl.BlockSpec((tc, D_MODEL), lambda i: (i, 0)),
                  _mod_spec(mod, tiles_per_mod),
                  pl.BlockSpec((tc, TOP_K), lambda i: (i, 0))],
        out_specs=pl.BlockSpec((tc, D_MODEL), lambda i: (i, 0)),
        out_shape=jax.ShapeDtypeStruct((T, D_MODEL), F32),
        scratch_shapes=[pltpu.VMEM((TOP_K * tc * ROW_TILES, LANES), F32), pltpu.SemaphoreType.DMA(())],
        compiler_params=_cparams(("arbitrary",)),
        name="moe_combine",
    )(slots.reshape(T // tc, 1, tc * TOP_K), y, x, mod, wts)


def _moe(parts, w1, w3, w2):
    experts, weights = _route(jnp.concatenate([p[1] for p in parts], axis=0))
    slots, blk_e, n_used, nb = _slots(experts, MOE_BLOCK)
    xs = jnp.zeros((nb * MOE_BLOCK * ROW_TILES, LANES), F32)
    off = 0
    for h, _, x, _, _, tt in parts:
        T = x.shape[0]
        xs = _scatter(slots[off:off + T], h, xs, tt)
        off += T
    y = _experts(xs, blk_e, n_used, nb, w1, w3, w2, MOE_BLOCK)
    outs, off = [], 0
    for _, _, x, mod, tpm, tt in parts:
        T = x.shape[0]
        outs.append(_combine(slots[off:off + T], y, x, mod, tpm, weights[off:off + T], tt))
        off += T
    return outs


def _kvq_kernel(x_ref, modkv_ref, mod_ref, nkv_ref, nm_ref, wkv_ref, wq_ref, kn_ref, qn_ref, eavg_ref,
                kv_ref, q_ref):
    x = x_ref[...]
    xn = x * lax.rsqrt(jnp.mean(x * x, axis=-1, keepdims=True) + EPS)
    modkv = modkv_ref[0]
    mod = mod_ref[0]
    hk = (xn * nkv_ref[...] * (1.0 + modkv[:, D_MODEL:]) + modkv[:, :D_MODEL]).astype(BF16)
    hq = (xn * nm_ref[...] * (1.0 + mod[:, D_MODEL:2 * D_MODEL]) + mod[:, :D_MODEL]).astype(BF16)
    eavg = eavg_ref[...]

    def head_norm(a, g):
        ms = jnp.dot((a * a).astype(BF16), eavg, preferred_element_type=F32)
        return a * lax.rsqrt(ms + EPS) * g

    for g in range(N_BRANCH):
        k0 = 2 * g * ATTN_WIDTH
        kvg = jnp.dot(hk, wkv_ref[:, k0:k0 + 2 * ATTN_WIDTH], preferred_element_type=F32)
        kv_ref[:, k0:k0 + ATTN_WIDTH] = head_norm(kvg[:, :ATTN_WIDTH], kn_ref[g:g + 1, :])
        kv_ref[:, k0 + ATTN_WIDTH:k0 + 2 * ATTN_WIDTH] = kvg[:, ATTN_WIDTH:]
        q0 = g * ATTN_WIDTH
        qg = jnp.dot(hq, wq_ref[:, q0:q0 + ATTN_WIDTH], preferred_element_type=F32)
        q_ref[:, q0:q0 + ATTN_WIDTH] = head_norm(qg, qn_ref[g:g + 1, :]) * (HEAD_DIM ** -0.5)


def _kvq(x, modkv, mod, tiles_per_mod, tm, nkv, nm, wkv, wq, kn, qn, eavg):
    T = x.shape[0]
    tok = lambda w: pl.BlockSpec((tm, w), lambda i: (i, 0))
    return pl.pallas_call(
        _kvq_kernel,
        grid=(T // tm,),
        in_specs=[tok(D_MODEL), _mod_spec(modkv, tiles_per_mod), _mod_spec(mod, tiles_per_mod),
                  _resident(nkv.shape), _resident(nm.shape), _resident(wkv.shape), _resident(wq.shape),
                  _resident(kn.shape), _resident(qn.shape), _resident(eavg.shape)],
        out_specs=[tok(2 * N_BRANCH * ATTN_WIDTH), tok(N_BRANCH * ATTN_WIDTH)],
        out_shape=[jax.ShapeDtypeStruct((T, 2 * N_BRANCH * ATTN_WIDTH), F32),
                   jax.ShapeDtypeStruct((T, N_BRANCH * ATTN_WIDTH), F32)],
        compiler_params=_cparams(("arbitrary",)),
        name="kvq",
    )(x, modkv, mod, nkv, nm, wkv, wq, kn, qn, eavg)


def _band_attn_kernel(q_ref, kp_ref, kc_ref, vp_ref, vc_ref, bias_ref, o_ref, lse_ref):
    n = pl.program_id(2)
    q = q_ref[0].astype(BF16)
    k2 = jnp.concatenate([kp_ref[0], kc_ref[0]], axis=0).astype(BF16)
    v2 = jnp.concatenate([vp_ref[0], vc_ref[0]], axis=0).astype(BF16)
    col = lax.broadcasted_iota(jnp.int32, (N_STEPS, 2 * N_STEPS), 1)
    visible = jnp.logical_or(col >= N_STEPS, n > 0)
    lane = lax.broadcasted_iota(jnp.int32, (N_STEPS, LANES), 1)
    lse_all = jnp.zeros((N_STEPS, LANES), F32)
    for h in range(HEADS):
        sl = slice(h * HEAD_DIM, (h + 1) * HEAD_DIM)
        s = lax.dot_general(q[:, sl], k2[:, sl], (((1,), (1,)), ((), ())), preferred_element_type=F32)
        s = jnp.where(visible, s + bias_ref[h], NEG)
        m = jnp.max(s, axis=1, keepdims=True)
        p = jnp.exp(s - m)
        l = jnp.sum(p, axis=1, keepdims=True)
        o_ref[0, :, sl] = jnp.dot(p.astype(BF16), v2[:, sl], preferred_element_type=F32) / l
        lse_all = jnp.where(lane == h, m + jnp.log(l), lse_all)
    lse_ref[0] = lse_all


def _band_attn(q, kv, bias, g, dil, B, L):
    Ld = L // dil
    nb = Ld // N_STEPS
    qv = q.reshape(B, Ld, dil * N_BRANCH * ATTN_WIDTH)
    kvv = kv.reshape(B, Ld, dil * 2 * N_BRANCH * ATTN_WIDTH)
    blk = lambda f: pl.BlockSpec((1, N_STEPS, ATTN_WIDTH), f)
    kcol, vcol, nq, nk = 2 * g, 2 * g + 1, N_BRANCH, 2 * N_BRANCH
    o, lse = pl.pallas_call(
        _band_attn_kernel,
        grid=(B, dil, nb),
        in_specs=[blk(lambda b, r, n: (b, n, r * nq + g)),
                  blk(lambda b, r, n: (b, jnp.maximum(n - 1, 0), r * nk + kcol)),
                  blk(lambda b, r, n: (b, n, r * nk + kcol)),
                  blk(lambda b, r, n: (b, jnp.maximum(n - 1, 0), r * nk + vcol)),
                  blk(lambda b, r, n: (b, n, r * nk + vcol)),
                  _resident(bias.shape)],
        out_specs=[blk(lambda b, r, n: (b, n, r)),
                   pl.BlockSpec((1, N_STEPS, LANES), lambda b, r, n: (b, n, r))],
        out_shape=[jax.ShapeDtypeStruct((B, Ld, dil * ATTN_WIDTH), F32),
                   jax.ShapeDtypeStruct((B, Ld, dil * LANES), F32)],
        compiler_params=_cparams(("arbitrary", "arbitrary", "arbitrary")),
        name=f"band_attn_d{dil}",
    )(qv, kvv, kvv, kvv, kvv, bias)
    return o.reshape(B * L, ATTN_WIDTH), lse.reshape(B * L, LANES)


def _rel_buckets(dilation):
    n = np.arange(N_STEPS + 1) * dilation
    large = MAX_EXACT + (np.log(np.maximum(n, 1) / MAX_EXACT) / np.log(REL_MAX_DIST / MAX_EXACT)
                         * (NUM_BUCKETS - MAX_EXACT)).astype(np.int32)
    return np.where(n < MAX_EXACT, n, np.minimum(large, NUM_BUCKETS - 1)).astype(np.int32)


def _band_bias(rel_bias, g, dil):
    bias = rel_bias[_rel_buckets(dil), g].T.astype(F32)
    delta = np.arange(N_STEPS)[:, None] + N_STEPS - np.arange(2 * N_STEPS)[None, :]
    ok = (delta >= 0) & (delta <= N_STEPS)
    return jnp.where(ok[None], bias[:, np.clip(delta, 0, N_STEPS)], NEG)


def _step_attn_kernel(q_ref, kvn_ref, c0_ref, c1_ref, c2_ref, bias_ref, o_ref, *, n_new):
    caches = (c0_ref, c1_ref, c2_ref)
    row = lax.broadcasted_iota(jnp.int32, (HEADS, ATTN_WIDTH), 0)
    lane = lax.broadcasted_iota(jnp.int32, (HEADS, ATTN_WIDTH), 1)
    headmask = (lane // HEAD_DIM == row).astype(F32)
    pad = jnp.zeros((N_STEPS - 8, ATTN_WIDTH), F32)
    for s in range(n_new):
        parts = []
        for g, (_, dil) in enumerate(BRANCHES):
            k0 = 2 * g * ATTN_WIDTH
            c0 = (s % dil) * 2 * ATTN_WIDTH
            kc = caches[g][0, :, c0:c0 + ATTN_WIDTH]
            vc = caches[g][0, :, c0 + ATTN_WIDTH:c0 + 2 * ATTN_WIDTH]
            k_all = jnp.concatenate([kc, kvn_ref[0, :, k0:k0 + ATTN_WIDTH], pad], axis=0).astype(BF16)
            v_all = jnp.concatenate([vc, kvn_ref[0, :, k0 + ATTN_WIDTH:k0 + 2 * ATTN_WIDTH], pad],
                                    axis=0).astype(BF16)
            qs = q_ref[0, s:s + 1, g * ATTN_WIDTH:(g + 1) * ATTN_WIDTH]
            qbd = (jnp.broadcast_to(qs, (HEADS, ATTN_WIDTH)) * headmask).astype(BF16)
            sc = lax.dot_general(qbd, k_all, (((1,), (1,)), ((), ())), preferred_element_type=F32)
            sc = sc + bias_ref[g, s]
            m = jnp.max(sc, axis=1, keepdims=True)
            p = jnp.exp(sc - m)
            l = jnp.sum(p, axis=1, keepdims=True)
            pv = jnp.dot(p.astype(BF16), v_all, preferred_element_type=F32)
            parts.append((m + jnp.log(l), pv / l))
        mx = jnp.maximum(jnp.maximum(parts[0][0], parts[1][0]), parts[2][0])
        es = [jnp.exp(lse - mx) for lse, _ in parts]
        den = es[0] + es[1] + es[2]
        mixed = (es[0] * parts[0][1] + es[1] * parts[1][1] + es[2] * parts[2][1]) / den
        o_ref[0, s:s + 1, :] = jnp.sum(mixed * headmask, axis=0, keepdims=True)


def _step_bias(rel_bias, n_new):
    out = []
    for g, (win, dil) in enumerate(BRANCHES):
        bias = rel_bias[_rel_buckets(dil), g].T.astype(F32)
        per_s = []
        for s in range(n_new):
            dist = np.full((2 * N_STEPS,), -1, np.int64)
            j = np.arange(N_STEPS)
            pos = j * dil + (s % dil) if dil > 1 else j
            dist[:N_STEPS] = win + s - pos
            for t in range(n_new):
                dist[N_STEPS + t] = s - t
            ok = (dist >= 0) & (dist % dil == 0) & (dist // dil <= N_STEPS)
            step = np.clip(dist // dil, 0, N_STEPS)
            per_s.append(jnp.where(ok[None], bias[:, step], NEG))
        out.append(jnp.stack(per_s))
    return jnp.stack(out)


def _step_attn(q, kvn, caches, bias, DB, S):
    widths = [min(dil, S) * 2 * ATTN_WIDTH for _, dil in BRANCHES]
    return pl.pallas_call(
        functools.partial(_step_attn_kernel, n_new=S),
        grid=(DB,),
        in_specs=[pl.BlockSpec((1, S, N_BRANCH * ATTN_WIDTH), lambda b: (b, 0, 0)),
                  pl.BlockSpec((1, 8, 2 * N_BRANCH * ATTN_WIDTH), lambda b: (b, 0, 0)),
                  pl.BlockSpec((1, N_STEPS, widths[0]), lambda b: (b, 0, 0)),
                  pl.BlockSpec((1, N_STEPS, widths[1]), lambda b: (b, 0, 0)),
                  pl.BlockSpec((1, N_STEPS, widths[2]), lambda b: (b, 0, 0)),
                  _resident(bias.shape)],
        out_specs=pl.BlockSpec((1, S, ATTN_WIDTH), lambda b: (b, 0, 0)),
        out_shape=jax.ShapeDtypeStruct((DB, S, ATTN_WIDTH), F32),
        compiler_params=_cparams(("arbitrary",)),
        name="step_attn",
    )(q, kvn, caches[0], caches[1], caches[2], bias)


def _attn_out_kernel(*refs, n_o, tm):
    o_refs = refs[:n_o]
    (x_ref, mod_ref, wo_ref, ex_ref, nf_ref, wr_ref, br_ref, x3_ref, h3_ref, lg_ref) = refs[-10:]
    if n_o > 1:
        lses = [r[...] for r in refs[n_o:2 * n_o]]
        mx = functools.reduce(jnp.maximum, lses)
        es = [jnp.exp(l - mx) for l in lses]
        den = functools.reduce(lambda a, b: a + b, es)
        o = None
        for e, o_ref in zip(es, o_refs):
            term = _split_dot(e / den, ex_ref[...]) * o_ref[...]
            o = term if o is None else o + term
    else:
        o = o_refs[0][...]
    mod = mod_ref[0]
    a = jnp.dot(o.astype(BF16), wo_ref[...], preferred_element_type=F32)
    x3 = x_ref[...] + mod[:, 2 * D_MODEL:3 * D_MODEL] * a
    x3_ref[...] = x3
    _ffn_pre(x3, mod, nf_ref, wr_ref, br_ref, h3_ref, lg_ref, tm)


def _attn_out(os_, lses, x, mod, tiles_per_mod, tm, wo, ex, nf, wr, br):
    T = x.shape[0]
    tok = lambda w: pl.BlockSpec((tm, w), lambda i: (i, 0))
    n_o = len(os_)
    return pl.pallas_call(
        functools.partial(_attn_out_kernel, n_o=n_o, tm=tm),
        grid=(T // tm,),
        in_specs=([tok(ATTN_WIDTH)] * n_o + [tok(LANES)] * len(lses)
                  + [tok(D_MODEL), _mod_spec(mod, tiles_per_mod), _resident(wo.shape), _resident(ex.shape),
                     _resident(nf.shape), _resident(wr.shape), _resident(br.shape)]),
        out_specs=[tok(D_MODEL), pl.BlockSpec((tm * ROW_TILES, LANES), lambda i: (i, 0)), tok(LANES)],
        out_shape=[jax.ShapeDtypeStruct((T, D_MODEL), F32),
                   jax.ShapeDtypeStruct((T * ROW_TILES, LANES), F32),
                   jax.ShapeDtypeStruct((T, LANES), F32)],
        compiler_params=_cparams(("arbitrary",)),
        name="attn_out",
    )(*os_, *lses, x, mod, wo, ex, nf, wr, br)


def kernel(x_prompt, x_sample, cache_kv_w128, cache_kv_w512, cache_kv_w2048, c_prompt, c_sample, ada_w, ada_b, norm_mix, norm_ffn, a_w_in, a_b_in, a_norm_v, a_w_s, a_b_s, a_w_out, kv_ada_w, kv_ada_b, kv_norm, w_kv, k_norm, rel_bias, b_w_q, q_norm, b_w_o, r_w_group, r_b_group, r_w_expert, r_b_expert, e_w1, e_w3, e_w2):
    B, L, _ = x_prompt.shape
    DB, S, _ = x_sample.shape
    Tp, Ts = B * L, DB * S
    tm = TOKEN_TILE
    tpm_p = L // tm

    c_all = jnp.concatenate([c_prompt, c_sample], axis=0)
    R = c_all.shape[0]
    c_all = jnp.pad(c_all, ((0, -R % 8), (0, 0)))
    mods = [_ada(c_all, ada_w[l], ada_b[l]) for l in range(2)]
    modkv = _ada(c_all, kv_ada_w, kv_ada_b)

    def split_mod(m):
        return m[:B, None, :], jnp.repeat(m[B:B + DB], S, axis=0)[None]
    mod_p, mod_s = zip(*[split_mod(m) for m in mods])
    modkv_p, modkv_s = split_mod(modkv)

    row = lambda a: a.reshape(1, -1)

    def router(l):
        wr = jnp.zeros((D_MODEL, LANES), F32)
        wr = wr.at[:, :N_GROUPS].set(r_w_group[l]).at[:, N_GROUPS:N_GROUPS + N_EXPERTS].set(r_w_expert[l])
        br = jnp.zeros((1, LANES), F32)
        br = br.at[0, :N_GROUPS].set(r_b_group[l]).at[0, N_GROUPS:N_GROUPS + N_EXPERTS].set(r_b_expert[l])
        hi = wr.astype(BF16)
        return jnp.stack([hi, (wr - hi.astype(F32)).astype(BF16)]), br

    win = a_w_in[0].astype(BF16)
    wout = a_w_out[0].astype(BF16)
    tril = jnp.tril(jnp.ones((CHUNK, CHUNK), bool))
    ws_p = jnp.where(tril, a_w_s[0], 0).astype(BF16)
    bs_p = a_b_s[0].T
    cs = min(CHUNK, S)
    ws_small = jnp.where(jnp.tril(jnp.ones((cs, cs), bool)), a_w_s[0][:, :cs, :cs], 0)
    ws_s = jnp.stack([jnp.kron(jnp.eye(Ts // cs, dtype=F32), ws_small[g]) for g in range(GMLP_GROUPS)]).astype(BF16)
    bs_s = jnp.tile(a_b_s[0][:, :cs], (1, Ts // cs)).T
    wr0, br0 = router(0)
    common = (row(norm_mix[0]), win, row(a_b_in[0]), row(a_norm_v[0]))
    xp = x_prompt.reshape(Tp, D_MODEL)
    xs_ = x_sample.reshape(Ts, D_MODEL)
    x1_p, h2_p, lg_p = _mixer_a(xp, mod_p[0], tpm_p, tm, *common, ws_p, bs_p, wout, row(norm_ffn[0]), wr0, br0,
                                with_v=False)
    x1_s, h2_s, lg_s, v_s = _mixer_a(xs_, mod_s[0], 1, Ts, *common, ws_s, bs_s, wout, row(norm_ffn[0]), wr0, br0,
                                     with_v=True)
    x2_p, x2_s = _moe([(h2_p, lg_p, x1_p, mod_p[0], tpm_p, tm), (h2_s, lg_s, x1_s, mod_s[0], 1, Ts)],
                      e_w1[0], e_w3[0], e_w2[0])

    wkv = w_kv.astype(BF16)
    wq = b_w_q[0].astype(BF16)
    kn = jnp.tile(k_norm, (1, HEADS))
    qn = jnp.tile(q_norm[0], (1, HEADS))
    head = np.arange(ATTN_WIDTH) // HEAD_DIM
    eavg = jnp.asarray((head[:, None] == head[None, :]) / HEAD_DIM, BF16)
    kvq_w = (row(kv_norm), row(norm_mix[1]), wkv, wq, kn, qn, eavg)
    kv_p, q_p = _kvq(x2_p, modkv_p, mod_p[1], tpm_p, tm, *kvq_w)
    kv_s, q_s = _kvq(x2_s, modkv_s, mod_s[1], 1, Ts, *kvq_w)

    os_, lses = [], []
    for g, (_, dil) in enumerate(BRANCHES):
        o, lse = _band_attn(q_p, kv_p, _band_bias(rel_bias, g, dil), g, dil, B, L)
        os_.append(o)
        lses.append(lse)
    caches = [c.reshape(DB, N_STEPS, -1) for c in (cache_kv_w128, cache_kv_w512, cache_kv_w2048)]
    kvn_s = jnp.pad(kv_s.reshape(DB, S, -1), ((0, 0), (0, 8 - S), (0, 0)))
    o_s = _step_attn(q_s.reshape(DB, S, -1), kvn_s, caches, _step_bias(rel_bias, S), DB, S)

    wo = b_w_o[0].astype(BF16)
    ex = jnp.asarray(np.arange(LANES)[:, None] == head[None, :], BF16)
    wr1, br1 = router(1)
    x3_p, h3_p, lg1_p = _attn_out(os_, lses, x2_p, mod_p[1], tpm_p, tm, wo, ex, row(norm_ffn[1]), wr1, br1)
    x3_s, h3_s, lg1_s = _attn_out([o_s.reshape(Ts, ATTN_WIDTH)], [], x2_s, mod_s[1], 1, Ts, wo, ex,
                                  row(norm_ffn[1]), wr1, br1)
    y_p, y_s = _moe([(h3_p, lg1_p, x3_p, mod_p[1], tpm_p, tm), (h3_s, lg1_s, x3_s, mod_s[1], 1, Ts)],
                    e_w1[1], e_w3[1], e_w2[1])

    kv_p4 = kv_p.reshape(B, L, N_BRANCH, 2, HEADS, HEAD_DIM)
    kv_s4 = kv_s.reshape(DB, S, N_BRANCH, 2, HEADS, HEAD_DIM)
    n = [min(w, L) for w, _ in BRANCHES]
    return (y_p.reshape(B, L, D_MODEL), y_s.reshape(DB, S, D_MODEL),
            kv_p4[:, L - n[0]:, 0], kv_p4[:, L - n[1]:, 1], kv_p4[:, L - n[2]:, 2],
            kv_s4[:, :, 0], kv_s4[:, :, 1], kv_s4[:, :, 2],
            v_s.reshape(1, DB, S, GMLP_WIDTH))
```

```python
import functools

import numpy as np
import jax
import jax.numpy as jnp
from jax import lax
from jax.experimental import pallas as pl
from jax.experimental.pallas import tpu as pltpu

F32 = jnp.float32
BF16 = jnp.bfloat16

D_MODEL = 1024
GMLP_WIDTH = 2048
GMLP_GROUPS = 4
GROUP_WIDTH = GMLP_WIDTH // GMLP_GROUPS
CHUNK = 128
BRANCHES = ((128, 1), (512, 4), (2048, 16))
N_BRANCH = 3
N_STEPS = 128
HEADS = 8
HEAD_DIM = 64
ATTN_WIDTH = HEADS * HEAD_DIM
NUM_BUCKETS = 32
MAX_EXACT = NUM_BUCKETS // 2
REL_MAX_DIST = 2048
N_GROUPS = 4
EXPERTS_PER_GROUP = 8
N_EXPERTS = N_GROUPS * EXPERTS_PER_GROUP
TOP_K = 2
D_EXPERT = 512
EPS = 1e-6
NEG = -1e30

LANES = 128
ROW_TILES = D_MODEL // LANES
TOKEN_TILE = 256
MOE_BLOCK = 256
DMA_UNROLL = 8
VMEM_LIMIT = 52 * 1024 * 1024


def _cparams(sem):
    return pltpu.CompilerParams(dimension_semantics=sem, vmem_limit_bytes=VMEM_LIMIT)


def _resident(shape):
    nd = len(shape)
    return pl.BlockSpec(shape, lambda *_, _nd=nd: (0,) * _nd, pipeline_mode=pl.Buffered(1))


def _gelu_tanh(x):
    return 0.5 * x * (1.0 + jnp.tanh(0.7978845608028654 * (x + 0.044715 * (x * x * x))))


def _rms(x, g):
    return x * lax.rsqrt(jnp.mean(x * x, axis=-1, keepdims=True) + EPS) * g


def _store_rows8(ref, val, n, base=0):
    for s in range(ROW_TILES):
        ref[pl.ds(base + s, n, stride=ROW_TILES), :] = val[:, s * LANES:(s + 1) * LANES]


def _load_rows8(ref, n, base=0):
    return jnp.concatenate([ref[pl.ds(base + s, n, stride=ROW_TILES), :] for s in range(ROW_TILES)], axis=1)


def _split(a):
    hi = a.astype(BF16)
    return hi, (a - hi.astype(F32)).astype(BF16)


def _split_dot(a, e_bf16):
    hi, lo = _split(a)
    return (jnp.dot(hi, e_bf16, preferred_element_type=F32) + jnp.dot(lo, e_bf16, preferred_element_type=F32))


def _dot3(a, w_ref):
    hi, lo = _split(a)
    return (jnp.dot(hi, w_ref[0], preferred_element_type=F32) + jnp.dot(lo, w_ref[0], preferred_element_type=F32)
            + jnp.dot(hi, w_ref[1], preferred_element_type=F32))


def _ffn_pre(x, mod, nf_ref, wr_ref, br_ref, h_ref, lg_ref, n):
    h = _rms(x, nf_ref[...]) * (1.0 + mod[:, 4 * D_MODEL:5 * D_MODEL]) + mod[:, 3 * D_MODEL:4 * D_MODEL]
    _store_rows8(h_ref, h, n)
    lg_ref[...] = _dot3(h, wr_ref) + br_ref[...]


def _ada_kernel(c_ref, w_ref, b_ref, o_ref):
    c = c_ref[...]
    a = (c * jax.nn.sigmoid(c)).astype(BF16)
    o_ref[...] = jnp.dot(a, w_ref[...].astype(BF16), preferred_element_type=F32) + b_ref[...]


def _ada(c, w, b):
    R = c.shape[0]
    N = w.shape[1]
    tn = 1024
    return pl.pallas_call(
        _ada_kernel,
        grid=(N // tn,),
        in_specs=[pl.BlockSpec((R, D_MODEL), lambda j: (0, 0)),
                  pl.BlockSpec((D_MODEL, tn), lambda j: (0, j)),
                  pl.BlockSpec((1, tn), lambda j: (0, j))],
        out_specs=pl.BlockSpec((R, tn), lambda j: (0, j)),
        out_shape=jax.ShapeDtypeStruct((R, N), F32),
        compiler_params=_cparams(("arbitrary",)),
        name="ada",
    )(c, w, b.reshape(1, N))


def _mixer_a_kernel(x_ref, mod_ref, nm_ref, win_ref, bin_ref, gv_ref, ws_ref, bs_ref, wout_ref,
                    nf_ref, wr_ref, br_ref, x1_ref, h2_ref, lg_ref, *v_refs, tm):
    x = x_ref[...]
    mod = mod_ref[0]
    h = (_rms(x, nm_ref[...]) * (1.0 + mod[:, D_MODEL:2 * D_MODEL]) + mod[:, 0:D_MODEL]).astype(BF16)
    zv = jnp.dot(h, win_ref[:, GMLP_WIDTH:], preferred_element_type=F32) + bin_ref[:, GMLP_WIDTH:]
    v = _rms(_gelu_tanh(zv), gv_ref[...])
    if v_refs:
        v_refs[0][...] = v
    vb = v.astype(BF16)
    bs = bs_ref[...]
    acc = jnp.zeros((tm, D_MODEL), F32)
    for g in range(GMLP_GROUPS):
        lo, hi = g * GROUP_WIDTH, (g + 1) * GROUP_WIDTH
        u = _gelu_tanh(jnp.dot(h, win_ref[:, lo:hi], preferred_element_type=F32) + bin_ref[:, lo:hi])
        wg = ws_ref[g]
        gate = jnp.concatenate(
            [jnp.dot(wg, vb[c * CHUNK:(c + 1) * CHUNK, lo:hi], preferred_element_type=F32) + bs[:, g:g + 1]
             for c in range(tm // CHUNK)], axis=0)
        acc = acc + jnp.dot((u * gate).astype(BF16), wout_ref[lo:hi, :], preferred_element_type=F32)
    x1 = x + mod[:, 2 * D_MODEL:3 * D_MODEL] * acc
    x1_ref[...] = x1
    _ffn_pre(x1, mod, nf_ref, wr_ref, br_ref, h2_ref, lg_ref, tm)


def _mod_spec(mod, tiles_per_mod):
    _, rows, width = mod.shape
    return pl.BlockSpec((1, rows, width), lambda i: (i // tiles_per_mod, 0, 0))


def _mixer_a(x, mod, tiles_per_mod, tm, nm, win, bin_, gv, ws, bs_t, wout, nf, wr, br, with_v):
    T = x.shape[0]
    tok = lambda w: pl.BlockSpec((tm, w), lambda i: (i, 0))
    out_shape = [jax.ShapeDtypeStruct((T, D_MODEL), F32),
                 jax.ShapeDtypeStruct((T * ROW_TILES, LANES), F32),
                 jax.ShapeDtypeStruct((T, LANES), F32)]
    out_specs = [tok(D_MODEL), pl.BlockSpec((tm * ROW_TILES, LANES), lambda i: (i, 0)), tok(LANES)]
    if with_v:
        out_shape.append(jax.ShapeDtypeStruct((T, GMLP_WIDTH), F32))
        out_specs.append(tok(GMLP_WIDTH))
    return pl.pallas_call(
        functools.partial(_mixer_a_kernel, tm=tm),
        grid=(T // tm,),
        in_specs=[tok(D_MODEL), _mod_spec(mod, tiles_per_mod), _resident(nm.shape), _resident(win.shape),
                  _resident(bin_.shape), _resident(gv.shape), _resident(ws.shape), _resident(bs_t.shape),
                  _resident(wout.shape), _resident(nf.shape), _resident(wr.shape), _resident(br.shape)],
        out_specs=out_specs,
        out_shape=out_shape,
        compiler_params=_cparams(("arbitrary",)),
        name="mixer_a",
    )(x, mod, nm, win, bin_, gv, ws, bs_t, wout, nf, wr, br)


ROUTE_LANE0 = N_GROUPS


def _router_kernel(lg_ref, cin_ref, tri_ref, rt_ref, cnt_ref, carry):
    @pl.when(pl.program_id(0) == 0)
    def _():
        carry[...] = cin_ref[...]

    l = lg_ref[...]
    lane = lax.broadcasted_iota(jnp.int32, l.shape, 1).astype(F32)
    far = float(LANES)

    def first_lane(mask):
        return jnp.min(jnp.where(mask, lane, far), axis=1, keepdims=True)

    is_g = lane < N_GROUPS
    gl = jnp.where(is_g, l, NEG)
    gmax = jnp.max(gl, axis=1, keepdims=True)
    g_i = first_lane(jnp.logical_and(gl == gmax, is_g))
    g_p = 1.0 / jnp.sum(jnp.where(is_g, jnp.exp(gl - gmax), 0.0), axis=1, keepdims=True)
    lo = ROUTE_LANE0 + EXPERTS_PER_GROUP * g_i
    sel = jnp.logical_and(lane >= lo, lane < lo + EXPERTS_PER_GROUP)
    el = jnp.where(sel, l, NEG)
    m1 = jnp.max(el, axis=1, keepdims=True)
    i1 = first_lane(jnp.logical_and(el == m1, sel))
    sel2 = jnp.logical_and(sel, lane != i1)
    el2 = jnp.where(sel2, l, NEG)
    m2 = jnp.max(el2, axis=1, keepdims=True)
    i2 = first_lane(jnp.logical_and(el2 == m2, sel2))
    r = jnp.exp(m2 - m1)
    w1 = g_p / (1.0 + r)
    w2 = g_p * r / (1.0 + r)

    hit1 = lane == i1
    hit2 = lane == i2
    onehot = jnp.where(jnp.logical_or(hit1, hit2), 1.0, 0.0)
    before = carry[...] + jnp.dot(tri_ref[...], onehot.astype(BF16), preferred_element_type=F32)
    rank1 = jnp.sum(jnp.where(hit1, before, 0.0), axis=1, keepdims=True)
    rank2 = jnp.sum(jnp.where(hit2, before, 0.0), axis=1, keepdims=True)
    carry[...] = carry[...] + jnp.sum(onehot, axis=0, keepdims=True)
    cnt_ref[...] = carry[...]

    out = jnp.zeros(l.shape, F32)
    for k, val in enumerate((i1 - ROUTE_LANE0, i2 - ROUTE_LANE0, w1, w2, rank1, rank2)):
        out = jnp.where(lane == k, val, out)
    rt_ref[...] = out


def _router(logits, counts_in, tm):
    T = logits.shape[0]
    tri = jnp.asarray(np.tril(np.ones((tm, tm), np.float32), -1), BF16)
    return pl.pallas_call(
        _router_kernel,
        grid=(T // tm,),
        in_specs=[pl.BlockSpec((tm, LANES), lambda i: (i, 0)), _resident((1, LANES)), _resident((tm, tm))],
        out_specs=[pl.BlockSpec((tm, LANES), lambda i: (i, 0)), pl.BlockSpec((1, LANES), lambda i: (0, 0))],
        out_shape=[jax.ShapeDtypeStruct((T, LANES), F32), jax.ShapeDtypeStruct((1, LANES), F32)],
        scratch_shapes=[pltpu.VMEM((1, LANES), F32)],
        compiler_params=_cparams(("arbitrary",)),
        name="router",
    )(logits, counts_in, tri)


def _layout(route, counts, block, n_tokens):
    counts = counts[0, ROUTE_LANE0:ROUTE_LANE0 + N_EXPERTS].astype(jnp.int32)
    padded = (counts + block - 1) // block * block
    pad_end = jnp.cumsum(padded)
    pad_start = (pad_end - padded).astype(F32)
    experts = route[:, 0:TOP_K]
    onehot = experts[:, :, None] == jnp.arange(N_EXPERTS, dtype=F32)[None, None, :]
    slots = (route[:, 4:4 + TOP_K] + jnp.sum(jnp.where(onehot, pad_start[None, None, :], 0.0), axis=-1)).astype(jnp.int32)
    nb = -(-n_tokens * TOP_K // block) + N_EXPERTS
    blk_e = jnp.minimum(jnp.sum(pad_end[None, :] <= (jnp.arange(nb, dtype=jnp.int32) * block)[:, None], axis=1),
                        N_EXPERTS - 1).astype(jnp.int32)
    n_used = (pad_end[-1] // block).astype(jnp.int32).reshape(1)
    return slots, route[:, 2:2 + TOP_K], blk_e, n_used, nb


def _row_copy(src_ref, src_row, dst_ref, dst_row, sem):
    return pltpu.make_async_copy(
        src_ref.at[pl.ds(pl.multiple_of(src_row * ROW_TILES, ROW_TILES), ROW_TILES), :],
        dst_ref.at[pl.ds(pl.multiple_of(dst_row * ROW_TILES, ROW_TILES), ROW_TILES), :], sem)


def _scatter_kernel(slot_ref, src_ref, dst_in_ref, dst_ref, sem, *, ts):
    del dst_in_ref

    def issue(j, c):
        for u in range(DMA_UNROLL):
            t = j * DMA_UNROLL + u
            for k in range(TOP_K):
                _row_copy(src_ref, t, dst_ref, slot_ref[0, 0, TOP_K * t + k], sem).start()
        return c
    lax.fori_loop(0, ts // DMA_UNROLL, issue, 0)

    for _ in range(TOP_K):
        pltpu.make_async_copy(src_ref, dst_ref.at[pl.ds(0, ts * ROW_TILES), :], sem).wait()


def _scatter(slots, src, dst, ts):
    T = slots.shape[0]
    return pl.pallas_call(
        functools.partial(_scatter_kernel, ts=ts),
        grid=(T // ts,),
        in_specs=[pl.BlockSpec((1, 1, ts * TOP_K), lambda i: (i, 0, 0), memory_space=pltpu.SMEM),
                  pl.BlockSpec((ts * ROW_TILES, LANES), lambda i: (i, 0)),
                  pl.BlockSpec(memory_space=pl.ANY)],
        out_specs=pl.BlockSpec(memory_space=pl.ANY),
        out_shape=jax.ShapeDtypeStruct(dst.shape, dst.dtype),
        scratch_shapes=[pltpu.SemaphoreType.DMA(())],
        input_output_aliases={2: 0},
        compiler_params=_cparams(("arbitrary",)),
        name="moe_scatter",
    )(slots.reshape(T // ts, 1, ts * TOP_K), src, dst)


def _experts_kernel(be_ref, nu_ref, x_ref, w1_ref, w3_ref, w2_ref, y_ref, w1b, w3b, w2b, *, block):
    i = pl.program_id(0)

    @pl.when(i < nu_ref[0])
    def _():
        @pl.when(jnp.logical_or(i == 0, be_ref[i] != be_ref[jnp.maximum(i - 1, 0)]))
        def _():
            w1b[...] = w1_ref[0].astype(BF16)
            w3b[...] = w3_ref[0].astype(BF16)
            w2b[...] = w2_ref[0].astype(BF16)

        x = _load_rows8(x_ref, block).astype(BF16)
        h1 = jnp.dot(x, w1b[...], preferred_element_type=F32)
        h3 = jnp.dot(x, w3b[...], preferred_element_type=F32)
        a = (h1 * jax.nn.sigmoid(h1) * h3).astype(BF16)
        _store_rows8(y_ref, jnp.dot(a, w2b[...], preferred_element_type=F32), block)


def _experts(xs, blk_e, n_used, nb, w1, w3, w2, block):
    last = lambda i, nu: jnp.minimum(i, nu[0] - 1)
    rows = pl.BlockSpec((block * ROW_TILES, LANES), lambda i, be, nu: (last(i, nu), 0))
    return pl.pallas_call(
        functools.partial(_experts_kernel, block=block),
        grid_spec=pltpu.PrefetchScalarGridSpec(
            num_scalar_prefetch=2,
            grid=(nb,),
            in_specs=[rows,
                      pl.BlockSpec((1, D_MODEL, D_EXPERT), lambda i, be, nu: (be[last(i, nu)], 0, 0)),
                      pl.BlockSpec((1, D_MODEL, D_EXPERT), lambda i, be, nu: (be[last(i, nu)], 0, 0)),
                      pl.BlockSpec((1, D_EXPERT, D_MODEL), lambda i, be, nu: (be[last(i, nu)], 0, 0))],
            out_specs=rows,
            scratch_shapes=[pltpu.VMEM((D_MODEL, D_EXPERT), BF16), pltpu.VMEM((D_MODEL, D_EXPERT), BF16),
                            pltpu.VMEM((D_EXPERT, D_MODEL), BF16)]),
        out_shape=jax.ShapeDtypeStruct(xs.shape, F32),
        input_output_aliases={2: 0},
        compiler_params=_cparams(("arbitrary",)),
        name="moe_experts",
    )(blk_e, n_used, xs, w1, w3, w2)


def _combine_kernel(slot_ref, y_ref, x_ref, mod_ref, wt_ref, o_ref, ybuf, sem, *, tc):
    def issue(j, c):
        for u in range(DMA_UNROLL):
            t = j * DMA_UNROLL + u
            for k in range(TOP_K):
                _row_copy(y_ref, slot_ref[0, 0, TOP_K * t + k], ybuf, k * tc + t, sem).start()
        return c
    lax.fori_loop(0, tc // DMA_UNROLL, issue, 0)
    pltpu.make_async_copy(y_ref.at[pl.ds(0, TOP_K * tc * ROW_TILES), :], ybuf, sem).wait()

    wt = wt_ref[...]
    y = (wt[:, 0:1] * _load_rows8(ybuf, tc) + wt[:, 1:2] * _load_rows8(ybuf, tc, base=tc * ROW_TILES))
    o_ref[...] = x_ref[...] + mod_ref[0][:, 5 * D_MODEL:6 * D_MODEL] * y


def _combine(slots, y, x, mod, tiles_per_mod, wts, tc):
    T = x.shape[0]
    return pl.pallas_call(
        functools.partial(_combine_kernel, tc=tc),
        grid=(T // tc,),
        in_specs=[pl.BlockSpec((1, 1, tc * TOP_K), lambda i: (i, 0, 0), memory_space=pltpu.SMEM),
                  pl.BlockSpec(memory_space=pl.ANY),
                  pl.BlockSpec((tc, D_MODEL), lambda i: (i, 0)),
                  _mod_spec(mod, tiles_per_mod),
                  pl.BlockSpec((tc, TOP_K), lambda i: (i, 0))],
        out_specs=pl.BlockSpec((tc, D_MODEL), lambda i: (i, 0)),
        out_shape=jax.ShapeDtypeStruct((T, D_MODEL), F32),
        scratch_shapes=[pltpu.VMEM((TOP_K * tc * ROW_TILES, LANES), F32), pltpu.SemaphoreType.DMA(())],
        compiler_params=_cparams(("arbitrary",)),
        name="moe_combine",
    )(slots.reshape(T // tc, 1, tc * TOP_K), y, x, mod, wts)


def _moe(parts, w1, w3, w2):
    counts = jnp.zeros((1, LANES), F32)
    routes = []
    for _, lg, _, _, _, tt in parts:
        rt, counts = _router(lg, counts, tt)
        routes.append(rt)
    n_tokens = sum(p[2].shape[0] for p in parts)
    slots, wts, blk_e, n_used, nb = _layout(jnp.concatenate(routes, axis=0), counts, MOE_BLOCK, n_tokens)
    xs = jnp.zeros((nb * MOE_BLOCK * ROW_TILES, LANES), F32)
    off = 0
    for h, _, x, _, _, tt in parts:
        T = x.shape[0]
        xs = _scatter(slots[off:off + T], h, xs, tt)
        off += T
    y = _experts(xs, blk_e, n_used, nb, w1, w3, w2, MOE_BLOCK)
    outs, off = [], 0
    for _, _, x, mod, tpm, tt in parts:
        T = x.shape[0]
        outs.append(_combine(slots[off:off + T], y, x, mod, tpm, wts[off:off + T], tt))
        off += T
    return outs


def _kvq_kernel(x_ref, modkv_ref, mod_ref, nkv_ref, nm_ref, wkv_ref, wq_ref, kn_ref, qn_ref, eavg_ref,
                kv_ref, *rest, tm, residue):
    x = x_ref[...]
    xn = x * lax.rsqrt(jnp.mean(x * x, axis=-1, keepdims=True) + EPS)
    modkv = modkv_ref[0]
    mod = mod_ref[0]
    hk = (xn * nkv_ref[...] * (1.0 + modkv[:, D_MODEL:]) + modkv[:, :D_MODEL]).astype(BF16)
    hq = (xn * nm_ref[...] * (1.0 + mod[:, D_MODEL:2 * D_MODEL]) + mod[:, :D_MODEL]).astype(BF16)
    eavg = eavg_ref[...]

    def head_norm(a, g):
        ms = jnp.dot((a * a).astype(BF16), eavg, preferred_element_type=F32)
        return a * lax.rsqrt(ms + EPS) * g

    def residue_major(val, ref, dil):
        if dil == 1:
            ref[0, 0] = val.astype(BF16)
            return
        stage = rest[-1]
        for c in range(stage.shape[0]):
            stage[c] = val[:, c * LANES:(c + 1) * LANES]
        for r in range(dil):
            rows = [stage[c, pl.ds(r, tm // dil, stride=dil), :] for c in range(stage.shape[0])]
            ref[0, r] = jnp.concatenate(rows, axis=1).astype(BF16)

    for g, (_, dil) in enumerate(BRANCHES):
        k0 = 2 * g * ATTN_WIDTH
        kvg = jnp.dot(hk, wkv_ref[:, k0:k0 + 2 * ATTN_WIDTH], preferred_element_type=F32)
        kn = head_norm(kvg[:, :ATTN_WIDTH], kn_ref[g:g + 1, :])
        vv = kvg[:, ATTN_WIDTH:]
        kv_ref[:, k0:k0 + ATTN_WIDTH] = kn
        kv_ref[:, k0 + ATTN_WIDTH:k0 + 2 * ATTN_WIDTH] = vv
        q0 = g * ATTN_WIDTH
        qg = jnp.dot(hq, wq_ref[:, q0:q0 + ATTN_WIDTH], preferred_element_type=F32)
        qg = head_norm(qg, qn_ref[g:g + 1, :]) * (HEAD_DIM ** -0.5)
        if residue:
            residue_major(qg, rest[3 * g], dil)
            residue_major(kn, rest[3 * g + 1], dil)
            residue_major(vv, rest[3 * g + 2], dil)
        else:
            rest[0][:, q0:q0 + ATTN_WIDTH] = qg


def _kvq(x, modkv, mod, tiles_per_mod, tm, nkv, nm, wkv, wq, kn, qn, eavg, batch=None, n_last=None):
    T = x.shape[0]
    tok = lambda w: pl.BlockSpec((tm, w), lambda i: (i, 0))
    kvw = 2 * N_BRANCH * ATTN_WIDTH
    residue = batch is not None
    if residue:
        L = T // batch
        tpb, tl = L // tm, n_last // tm
        out_specs = [pl.BlockSpec((tm, kvw), lambda i: ((i // tpb) * tl + jnp.maximum(i % tpb - (tpb - tl), 0), 0))]
        out_shape = [jax.ShapeDtypeStruct((batch * n_last, kvw), F32)]
        for _, dil in BRANCHES:
            for _ in range(3):
                out_specs.append(pl.BlockSpec((1, dil, tm // dil, ATTN_WIDTH), lambda i: (i // tpb, 0, i % tpb, 0)))
                out_shape.append(jax.ShapeDtypeStruct((batch, dil, L // dil, ATTN_WIDTH), BF16))
        scratch = [pltpu.VMEM((ATTN_WIDTH // LANES, tm, LANES), F32)]
    else:
        out_specs = [tok(kvw), tok(N_BRANCH * ATTN_WIDTH)]
        out_shape = [jax.ShapeDtypeStruct((T, kvw), F32), jax.ShapeDtypeStruct((T, N_BRANCH * ATTN_WIDTH), F32)]
        scratch = []
    return pl.pallas_call(
        functools.partial(_kvq_kernel, tm=tm, residue=residue),
        grid=(T // tm,),
        in_specs=[tok(D_MODEL), _mod_spec(modkv, tiles_per_mod), _mod_spec(mod, tiles_per_mod),
                  _resident(nkv.shape), _resident(nm.shape), _resident(wkv.shape), _resident(wq.shape),
                  _resident(kn.shape), _resident(qn.shape), _resident(eavg.shape)],
        out_specs=out_specs,
        out_shape=out_shape,
        scratch_shapes=scratch,
        compiler_params=_cparams(("arbitrary",)),
        name="kvq",
    )(x, modkv, mod, nkv, nm, wkv, wq, kn, qn, eavg)


def _band_attn_kernel(q_ref, kp_ref, kc_ref, vp_ref, vc_ref, bias_ref, o_ref, lse_ref):
    n = pl.program_id(2)
    q = q_ref[...]
    k2 = jnp.concatenate([kp_ref[...], kc_ref[...]], axis=0)
    v2 = jnp.concatenate([vp_ref[...], vc_ref[...]], axis=0)
    col = lax.broadcasted_iota(jnp.int32, (N_STEPS, 2 * N_STEPS), 1)
    visible = jnp.logical_or(col >= N_STEPS, n > 0)
    lane = lax.broadcasted_iota(jnp.int32, (N_STEPS, LANES), 1)
    lse_all = jnp.zeros((N_STEPS, LANES), F32)
    for h in range(HEADS):
        sl = slice(h * HEAD_DIM, (h + 1) * HEAD_DIM)
        s = lax.dot_general(q[:, sl], k2[:, sl], (((1,), (1,)), ((), ())), preferred_element_type=F32)
        s = jnp.where(visible, s + bias_ref[h], NEG)
        m = jnp.max(s, axis=1, keepdims=True)
        p = jnp.exp(s - m)
        l = jnp.sum(p, axis=1, keepdims=True)
        o_ref[:, sl] = jnp.dot(p.astype(BF16), v2[:, sl], preferred_element_type=F32) / l
        lse_all = jnp.where(lane == h, m + jnp.log(l), lse_all)
    lse_ref[...] = lse_all


def _band_attn(q, k, v, bias, dil):
    B, _, Ld, _ = q.shape
    blk = lambda w, f: pl.BlockSpec((None, None, N_STEPS, w), f)
    cur = lambda b, r, n: (b, r, n, 0)
    prev = lambda b, r, n: (b, r, jnp.maximum(n - 1, 0), 0)
    return pl.pallas_call(
        _band_attn_kernel,
        grid=(B, dil, Ld // N_STEPS),
        in_specs=[blk(ATTN_WIDTH, cur), blk(ATTN_WIDTH, prev), blk(ATTN_WIDTH, cur), blk(ATTN_WIDTH, prev),
                  blk(ATTN_WIDTH, cur), _resident(bias.shape)],
        out_specs=[blk(ATTN_WIDTH, cur), blk(LANES, cur)],
        out_shape=[jax.ShapeDtypeStruct((B, dil, Ld, ATTN_WIDTH), F32),
                   jax.ShapeDtypeStruct((B, dil, Ld, LANES), F32)],
        compiler_params=_cparams(("arbitrary", "arbitrary", "arbitrary")),
        name=f"band_attn_d{dil}",
    )(q, k, k, v, v, bias)


def _rel_buckets(dilation):
    n = np.arange(N_STEPS + 1) * dilation
    large = MAX_EXACT + (np.log(np.maximum(n, 1) / MAX_EXACT) / np.log(REL_MAX_DIST / MAX_EXACT)
                         * (NUM_BUCKETS - MAX_EXACT)).astype(np.int32)
    return np.where(n < MAX_EXACT, n, np.minimum(large, NUM_BUCKETS - 1)).astype(np.int32)


def _band_bias(rel_bias, g, dil):
    bias = rel_bias[_rel_buckets(dil), g].T.astype(F32)
    delta = np.arange(N_STEPS)[:, None] + N_STEPS - np.arange(2 * N_STEPS)[None, :]
    ok = (delta >= 0) & (delta <= N_STEPS)
    return jnp.where(ok[None], bias[:, np.clip(delta, 0, N_STEPS)], NEG)


def _step_attn_kernel(q_ref, kvn_ref, c0_ref, c1_ref, c2_ref, bias_ref, o_ref, *, n_new):
    caches = (c0_ref, c1_ref, c2_ref)
    row = lax.broadcasted_iota(jnp.int32, (HEADS, ATTN_WIDTH), 0)
    lane = lax.broadcasted_iota(jnp.int32, (HEADS, ATTN_WIDTH), 1)
    headmask = (lane // HEAD_DIM == row).astype(F32)
    pad = jnp.zeros((N_STEPS - 8, ATTN_WIDTH), F32)
    for s in range(n_new):
        parts = []
        for g, (_, dil) in enumerate(BRANCHES):
            k0 = 2 * g * ATTN_WIDTH
            c0 = (s % dil) * 2 * ATTN_WIDTH
            kc = caches[g][0, :, c0:c0 + ATTN_WIDTH]
            vc = caches[g][0, :, c0 + ATTN_WIDTH:c0 + 2 * ATTN_WIDTH]
            k_all = jnp.concatenate([kc, kvn_ref[0, :, k0:k0 + ATTN_WIDTH], pad], axis=0).astype(BF16)
            v_all = jnp.concatenate([vc, kvn_ref[0, :, k0 + ATTN_WIDTH:k0 + 2 * ATTN_WIDTH], pad],
                                    axis=0).astype(BF16)
            qs = q_ref[0, s:s + 1, g * ATTN_WIDTH:(g + 1) * ATTN_WIDTH]
            qbd = (jnp.broadcast_to(qs, (HEADS, ATTN_WIDTH)) * headmask).astype(BF16)
            sc = lax.dot_general(qbd, k_all, (((1,), (1,)), ((), ())), preferred_element_type=F32)
            sc = sc + bias_ref[g, s]
            m = jnp.max(sc, axis=1, keepdims=True)
            p = jnp.exp(sc - m)
            l = jnp.sum(p, axis=1, keepdims=True)
            pv = jnp.dot(p.astype(BF16), v_all, preferred_element_type=F32)
            parts.append((m + jnp.log(l), pv / l))
        mx = jnp.maximum(jnp.maximum(parts[0][0], parts[1][0]), parts[2][0])
        es = [jnp.exp(lse - mx) for lse, _ in parts]
        den = es[0] + es[1] + es[2]
        mixed = (es[0] * parts[0][1] + es[1] * parts[1][1] + es[2] * parts[2][1]) / den
        o_ref[0, s:s + 1, :] = jnp.sum(mixed * headmask, axis=0, keepdims=True)


def _step_bias(rel_bias, n_new):
    out = []
    for g, (win, dil) in enumerate(BRANCHES):
        bias = rel_bias[_rel_buckets(dil), g].T.astype(F32)
        per_s = []
        for s in range(n_new):
            dist = np.full((2 * N_STEPS,), -1, np.int64)
            j = np.arange(N_STEPS)
            pos = j * dil + (s % dil) if dil > 1 else j
            dist[:N_STEPS] = win + s - pos
            for t in range(n_new):
                dist[N_STEPS + t] = s - t
            ok = (dist >= 0) & (dist % dil == 0) & (dist // dil <= N_STEPS)
            step = np.clip(dist // dil, 0, N_STEPS)
            per_s.append(jnp.where(ok[None], bias[:, step], NEG))
        out.append(jnp.stack(per_s))
    return jnp.stack(out)


def _step_attn(q, kvn, caches, bias, DB, S):
    widths = [min(dil, S) * 2 * ATTN_WIDTH for _, dil in BRANCHES]
    return pl.pallas_call(
        functools.partial(_step_attn_kernel, n_new=S),
        grid=(DB,),
        in_specs=[pl.BlockSpec((1, S, N_BRANCH * ATTN_WIDTH), lambda b: (b, 0, 0)),
                  pl.BlockSpec((1, 8, 2 * N_BRANCH * ATTN_WIDTH), lambda b: (b, 0, 0)),
                  pl.BlockSpec((1, N_STEPS, widths[0]), lambda b: (b, 0, 0)),
                  pl.BlockSpec((1, N_STEPS, widths[1]), lambda b: (b, 0, 0)),
                  pl.BlockSpec((1, N_STEPS, widths[2]), lambda b: (b, 0, 0)),
                  _resident(bias.shape)],
        out_specs=pl.BlockSpec((1, S, ATTN_WIDTH), lambda b: (b, 0, 0)),
        out_shape=jax.ShapeDtypeStruct((DB, S, ATTN_WIDTH), F32),
        compiler_params=_cparams(("arbitrary",)),
        name="step_attn",
    )(q, kvn, caches[0], caches[1], caches[2], bias)


def _attn_out_kernel(*refs, dils, tm):
    n_o = max(len(dils), 1)
    n_in = n_o + len(dils) + 7
    (x_ref, mod_ref, wo_ref, ex_ref, nf_ref, wr_ref, br_ref, x3_ref, h3_ref, lg_ref) = refs[n_in - 7:n_in + 3]
    stages = refs[n_in + 3:]

    def natural(ref, stage, dil):
        if dil == 1:
            return ref[0, 0]
        for r in range(dil):
            val = ref[0, r]
            for c in range(stage.shape[0]):
                stage[c, pl.ds(r, tm // dil, stride=dil), :] = val[:, c * LANES:(c + 1) * LANES]
        return jnp.concatenate([stage[c] for c in range(stage.shape[0])], axis=1)

    if dils:
        lses = [natural(refs[n_o + g], stages[2 * g + 1], dil) for g, dil in enumerate(dils)]
        mx = functools.reduce(jnp.maximum, lses)
        es = [jnp.exp(l - mx) for l in lses]
        den = functools.reduce(lambda a, b: a + b, es)
        o = None
        for g, dil in enumerate(dils):
            term = _split_dot(es[g] / den, ex_ref[...]) * natural(refs[g], stages[2 * g], dil)
            o = term if o is None else o + term
    else:
        o = refs[0][...]
    mod = mod_ref[0]
    a = jnp.dot(o.astype(BF16), wo_ref[...], preferred_element_type=F32)
    x3 = x_ref[...] + mod[:, 2 * D_MODEL:3 * D_MODEL] * a
    x3_ref[...] = x3
    _ffn_pre(x3, mod, nf_ref, wr_ref, br_ref, h3_ref, lg_ref, tm)


def _attn_out(os_, lses, x, mod, tiles_per_mod, tm, wo, ex, nf, wr, br):
    T = x.shape[0]
    tok = lambda w: pl.BlockSpec((tm, w), lambda i: (i, 0))
    if lses:
        dils = tuple(o.shape[1] for o in os_)
        tpb = tiles_per_mod
        res = lambda a: pl.BlockSpec((1, a.shape[1], tm // a.shape[1], a.shape[3]), lambda i: (i // tpb, 0, i % tpb, 0))
        o_specs = [res(a) for a in os_] + [res(a) for a in lses]
        scratch = []
        for _ in dils:
            scratch += [pltpu.VMEM((ATTN_WIDTH // LANES, tm, LANES), F32), pltpu.VMEM((1, tm, LANES), F32)]
    else:
        dils, o_specs, scratch = (), [tok(ATTN_WIDTH)], []
    return pl.pallas_call(
        functools.partial(_attn_out_kernel, dils=dils, tm=tm),
        grid=(T // tm,),
        in_specs=(o_specs + [tok(D_MODEL), _mod_spec(mod, tiles_per_mod), _resident(wo.shape), _resident(ex.shape),
                             _resident(nf.shape), _resident(wr.shape), _resident(br.shape)]),
        out_specs=[tok(D_MODEL), pl.BlockSpec((tm * ROW_TILES, LANES), lambda i: (i, 0)), tok(LANES)],
        out_shape=[jax.ShapeDtypeStruct((T, D_MODEL), F32),
                   jax.ShapeDtypeStruct((T * ROW_TILES, LANES), F32),
                   jax.ShapeDtypeStruct((T, LANES), F32)],
        scratch_shapes=scratch,
        compiler_params=_cparams(("arbitrary",)),
        name="attn_out",
    )(*os_, *lses, x, mod, wo, ex, nf, wr, br)


def kernel(x_prompt, x_sample, cache_kv_w128, cache_kv_w512, cache_kv_w2048, c_prompt, c_sample, ada_w, ada_b, norm_mix, norm_ffn, a_w_in, a_b_in, a_norm_v, a_w_s, a_b_s, a_w_out, kv_ada_w, kv_ada_b, kv_norm, w_kv, k_norm, rel_bias, b_w_q, q_norm, b_w_o, r_w_group, r_b_group, r_w_expert, r_b_expert, e_w1, e_w3, e_w2):
    B, L, _ = x_prompt.shape
    DB, S, _ = x_sample.shape
    Tp, Ts = B * L, DB * S
    tm = TOKEN_TILE
    tpm_p = L // tm

    c_all = jnp.concatenate([c_prompt, c_sample], axis=0)
    R = c_all.shape[0]
    c_all = jnp.pad(c_all, ((0, -R % 8), (0, 0)))
    mods = [_ada(c_all, ada_w[l], ada_b[l]) for l in range(2)]
    modkv = _ada(c_all, kv_ada_w, kv_ada_b)

    def split_mod(m):
        return m[:B, None, :], jnp.repeat(m[B:B + DB], S, axis=0)[None]
    mod_p, mod_s = zip(*[split_mod(m) for m in mods])
    modkv_p, modkv_s = split_mod(modkv)

    row = lambda a: a.reshape(1, -1)

    def router(l):
        wr = jnp.zeros((D_MODEL, LANES), F32)
        wr = wr.at[:, :N_GROUPS].set(r_w_group[l]).at[:, N_GROUPS:N_GROUPS + N_EXPERTS].set(r_w_expert[l])
        br = jnp.zeros((1, LANES), F32)
        br = br.at[0, :N_GROUPS].set(r_b_group[l]).at[0, N_GROUPS:N_GROUPS + N_EXPERTS].set(r_b_expert[l])
        hi = wr.astype(BF16)
        return jnp.stack([hi, (wr - hi.astype(F32)).astype(BF16)]), br

    win = a_w_in[0].astype(BF16)
    wout = a_w_out[0].astype(BF16)
    tril = jnp.tril(jnp.ones((CHUNK, CHUNK), bool))
    ws_p = jnp.where(tril, a_w_s[0], 0).astype(BF16)
    bs_p = a_b_s[0].T
    cs = min(CHUNK, S)
    ws_small = jnp.where(jnp.tril(jnp.ones((cs, cs), bool)), a_w_s[0][:, :cs, :cs], 0)
    ws_s = jnp.stack([jnp.kron(jnp.eye(Ts // cs, dtype=F32), ws_small[g]) for g in range(GMLP_GROUPS)]).astype(BF16)
    bs_s = jnp.tile(a_b_s[0][:, :cs], (1, Ts // cs)).T
    wr0, br0 = router(0)
    common = (row(norm_mix[0]), win, row(a_b_in[0]), row(a_norm_v[0]))
    xp = x_prompt.reshape(Tp, D_MODEL)
    xs_ = x_sample.reshape(Ts, D_MODEL)
    x1_p, h2_p, lg_p = _mixer_a(xp, mod_p[0], tpm_p, tm, *common, ws_p, bs_p, wout, row(norm_ffn[0]), wr0, br0,
                                with_v=False)
    x1_s, h2_s, lg_s, v_s = _mixer_a(xs_, mod_s[0], 1, Ts, *common, ws_s, bs_s, wout, row(norm_ffn[0]), wr0, br0,
                                     with_v=True)
    x2_p, x2_s = _moe([(h2_p, lg_p, x1_p, mod_p[0], tpm_p, tm), (h2_s, lg_s, x1_s, mod_s[0], 1, Ts)],
                      e_w1[0], e_w3[0], e_w2[0])

    wkv = w_kv.astype(BF16)
    wq = b_w_q[0].astype(BF16)
    kn = jnp.tile(k_norm, (1, HEADS))
    qn = jnp.tile(q_norm[0], (1, HEADS))
    head = np.arange(ATTN_WIDTH) // HEAD_DIM
    eavg = jnp.asarray((head[:, None] == head[None, :]) / HEAD_DIM, BF16)
    kvq_w = (row(kv_norm), row(norm_mix[1]), wkv, wq, kn, qn, eavg)
    n_last = max(min(w, L) for w, _ in BRANCHES)
    kv_last, *qkv_p = _kvq(x2_p, modkv_p, mod_p[1], tpm_p, tm, *kvq_w, batch=B, n_last=n_last)
    kv_s, q_s = _kvq(x2_s, modkv_s, mod_s[1], 1, Ts, *kvq_w)

    os_, lses = [], []
    for g, (_, dil) in enumerate(BRANCHES):
        o, lse = _band_attn(*qkv_p[3 * g:3 * g + 3], _band_bias(rel_bias, g, dil), dil)
        os_.append(o)
        lses.append(lse)
    caches = [c.reshape(DB, N_STEPS, -1) for c in (cache_kv_w128, cache_kv_w512, cache_kv_w2048)]
    kvn_s = jnp.pad(kv_s.reshape(DB, S, -1), ((0, 0), (0, 8 - S), (0, 0)))
    o_s = _step_attn(q_s.reshape(DB, S, -1), kvn_s, caches, _step_bias(rel_bias, S), DB, S)

    wo = b_w_o[0].astype(BF16)
    ex = jnp.asarray(np.arange(LANES)[:, None] == head[None, :], BF16)
    wr1, br1 = router(1)
    x3_p, h3_p, lg1_p = _attn_out(os_, lses, x2_p, mod_p[1], tpm_p, tm, wo, ex, row(norm_ffn[1]), wr1, br1)
    x3_s, h3_s, lg1_s = _attn_out([o_s.reshape(Ts, ATTN_WIDTH)], [], x2_s, mod_s[1], 1, Ts, wo, ex,
                                  row(norm_ffn[1]), wr1, br1)
    y_p, y_s = _moe([(h3_p, lg1_p, x3_p, mod_p[1], tpm_p, tm), (h3_s, lg1_s, x3_s, mod_s[1], 1, Ts)],
                    e_w1[1], e_w3[1], e_w2[1])

    kv_l = kv_last.reshape(B, n_last, N_BRANCH, 2, HEADS, HEAD_DIM)
    kv_s4 = kv_s.reshape(DB, S, N_BRANCH, 2, HEADS, HEAD_DIM)
    n = [min(w, L) for w, _ in BRANCHES]
    return (y_p.reshape(B, L, D_MODEL), y_s.reshape(DB, S, D_MODEL),
            kv_l[:, n_last - n[0]:, 0], kv_l[:, n_last - n[1]:, 1], kv_l[:, n_last - n[2]:, 2],
            kv_s4[:, :, 0], kv_s4[:, :, 1], kv_s4[:, :, 2],
            v_s.reshape(1, DB, S, GMLP_WIDTH))
```

```python
import functools

import numpy as np
import jax
import jax.numpy as jnp
from jax import lax
from jax.experimental import pallas as pl
from jax.experimental.pallas import tpu as pltpu

F32 = jnp.float32
BF16 = jnp.bfloat16

D_MODEL = 1024
GMLP_WIDTH = 2048
GMLP_GROUPS = 4
GROUP_WIDTH = GMLP_WIDTH // GMLP_GROUPS
CHUNK = 128
BRANCHES = ((128, 1), (512, 4), (2048, 16))
N_BRANCH = 3
N_STEPS = 128
HEADS = 8
HEAD_DIM = 64
ATTN_WIDTH = HEADS * HEAD_DIM
NUM_BUCKETS = 32
MAX_EXACT = NUM_BUCKETS // 2
REL_MAX_DIST = 2048
N_GROUPS = 4
EXPERTS_PER_GROUP = 8
N_EXPERTS = N_GROUPS * EXPERTS_PER_GROUP
TOP_K = 2
D_EXPERT = 512
EPS = 1e-6
NEG = -1e30

LANES = 128
ROW_TILES = D_MODEL // LANES
TOKEN_TILE = 256
MOE_BLOCK = 256
DMA_UNROLL = 8
VMEM_LIMIT = 52 * 1024 * 1024


def _cparams(sem):
    return pltpu.CompilerParams(dimension_semantics=sem, vmem_limit_bytes=VMEM_LIMIT)


def _resident(shape):
    nd = len(shape)
    return pl.BlockSpec(shape, lambda *_, _nd=nd: (0,) * _nd, pipeline_mode=pl.Buffered(1))


def _gelu_tanh(x):
    return 0.5 * x * (1.0 + jnp.tanh(0.7978845608028654 * (x + 0.044715 * (x * x * x))))


def _rms(x, g):
    return x * lax.rsqrt(jnp.mean(x * x, axis=-1, keepdims=True) + EPS) * g


def _store_rows8(ref, val, n, base=0):
    for s in range(ROW_TILES):
        ref[pl.ds(base + s, n, stride=ROW_TILES), :] = val[:, s * LANES:(s + 1) * LANES]


def _load_rows8(ref, n, base=0):
    return jnp.concatenate([ref[pl.ds(base + s, n, stride=ROW_TILES), :] for s in range(ROW_TILES)], axis=1)


def _split(a):
    hi = a.astype(BF16)
    return hi, (a - hi.astype(F32)).astype(BF16)


def _split_dot(a, e_bf16):
    hi, lo = _split(a)
    return (jnp.dot(hi, e_bf16, preferred_element_type=F32) + jnp.dot(lo, e_bf16, preferred_element_type=F32))


def _dot3(a, w_ref):
    hi, lo = _split(a)
    return (jnp.dot(hi, w_ref[0], preferred_element_type=F32) + jnp.dot(lo, w_ref[0], preferred_element_type=F32)
            + jnp.dot(hi, w_ref[1], preferred_element_type=F32))


def _ffn_pre(x, mod, nf_ref, wr_ref, br_ref, h_ref, lg_ref, n):
    h = _rms(x, nf_ref[...]) * (1.0 + mod[:, 4 * D_MODEL:5 * D_MODEL]) + mod[:, 3 * D_MODEL:4 * D_MODEL]
    _store_rows8(h_ref, h, n)
    lg_ref[...] = _dot3(h, wr_ref) + br_ref[...]


def _ada_kernel(c_ref, w_ref, b_ref, o_ref):
    c = c_ref[...]
    a = (c * jax.nn.sigmoid(c)).astype(BF16)
    o_ref[...] = jnp.dot(a, w_ref[...].astype(BF16), preferred_element_type=F32) + b_ref[...]


def _ada(c, w, b, layer):
    R = c.shape[0]
    N = w.shape[2]
    tn = 1024
    return pl.pallas_call(
        _ada_kernel,
        grid=(N // tn,),
        in_specs=[pl.BlockSpec((R, D_MODEL), lambda j: (0, 0)),
                  pl.BlockSpec((None, D_MODEL, tn), lambda j: (layer, 0, j)),
                  pl.BlockSpec((None, 1, tn), lambda j: (layer, 0, j))],
        out_specs=pl.BlockSpec((R, tn), lambda j: (0, j)),
        out_shape=jax.ShapeDtypeStruct((R, N), F32),
        compiler_params=_cparams(("arbitrary",)),
        name="ada",
    )(c, w, b.reshape(b.shape[0], 1, N))


def _mixer_a_kernel(x_ref, mod_ref, nm_ref, win_ref, bin_ref, gv_ref, ws_ref, bs_ref, wout_ref,
                    nf_ref, wr_ref, br_ref, x1_ref, h2_ref, lg_ref, *v_refs, tm):
    x = x_ref[...]
    mod = mod_ref[0]
    h = (_rms(x, nm_ref[...]) * (1.0 + mod[:, D_MODEL:2 * D_MODEL]) + mod[:, 0:D_MODEL]).astype(BF16)
    zv = jnp.dot(h, win_ref[:, GMLP_WIDTH:], preferred_element_type=F32) + bin_ref[:, GMLP_WIDTH:]
    v = _rms(_gelu_tanh(zv), gv_ref[...])
    if v_refs:
        v_refs[0][...] = v
    vb = v.astype(BF16)
    bs = bs_ref[...]
    acc = jnp.zeros((tm, D_MODEL), F32)
    for g in range(GMLP_GROUPS):
        lo, hi = g * GROUP_WIDTH, (g + 1) * GROUP_WIDTH
        u = _gelu_tanh(jnp.dot(h, win_ref[:, lo:hi], preferred_element_type=F32) + bin_ref[:, lo:hi])
        wg = ws_ref[g]
        gate = jnp.concatenate(
            [jnp.dot(wg, vb[c * CHUNK:(c + 1) * CHUNK, lo:hi], preferred_element_type=F32) + bs[:, g:g + 1]
             for c in range(tm // CHUNK)], axis=0)
        acc = acc + jnp.dot((u * gate).astype(BF16), wout_ref[lo:hi, :], preferred_element_type=F32)
    x1 = x + mod[:, 2 * D_MODEL:3 * D_MODEL] * acc
    x1_ref[...] = x1
    _ffn_pre(x1, mod, nf_ref, wr_ref, br_ref, h2_ref, lg_ref, tm)


def _mod_spec(mod, tiles_per_mod):
    _, rows, width = mod.shape
    return pl.BlockSpec((1, rows, width), lambda i: (i // tiles_per_mod, 0, 0))


def _mixer_a(x, mod, tiles_per_mod, tm, nm, win, bin_, gv, ws, bs_t, wout, nf, wr, br, with_v):
    T = x.shape[0]
    tok = lambda w: pl.BlockSpec((tm, w), lambda i: (i, 0))
    out_shape = [jax.ShapeDtypeStruct((T, D_MODEL), F32),
                 jax.ShapeDtypeStruct((T * ROW_TILES, LANES), F32),
                 jax.ShapeDtypeStruct((T, LANES), F32)]
    out_specs = [tok(D_MODEL), pl.BlockSpec((tm * ROW_TILES, LANES), lambda i: (i, 0)), tok(LANES)]
    if with_v:
        out_shape.append(jax.ShapeDtypeStruct((T, GMLP_WIDTH), F32))
        out_specs.append(tok(GMLP_WIDTH))
    return pl.pallas_call(
        functools.partial(_mixer_a_kernel, tm=tm),
        grid=(T // tm,),
        in_specs=[tok(D_MODEL), _mod_spec(mod, tiles_per_mod), _resident(nm.shape), _resident(win.shape),
                  _resident(bin_.shape), _resident(gv.shape), _resident(ws.shape), _resident(bs_t.shape),
                  _resident(wout.shape), _resident(nf.shape), _resident(wr.shape), _resident(br.shape)],
        out_specs=out_specs,
        out_shape=out_shape,
        compiler_params=_cparams(("arbitrary",)),
        name="mixer_a",
    )(x, mod, nm, win, bin_, gv, ws, bs_t, wout, nf, wr, br)


ROUTE_LANE0 = N_GROUPS


def _router_kernel(lg_ref, cin_ref, tri_ref, rt_ref, cnt_ref, carry):
    @pl.when(pl.program_id(0) == 0)
    def _():
        carry[...] = cin_ref[...]

    l = lg_ref[...]
    lane = lax.broadcasted_iota(jnp.int32, l.shape, 1).astype(F32)
    far = float(LANES)

    def first_lane(mask):
        return jnp.min(jnp.where(mask, lane, far), axis=1, keepdims=True)

    is_g = lane < N_GROUPS
    gl = jnp.where(is_g, l, NEG)
    gmax = jnp.max(gl, axis=1, keepdims=True)
    g_i = first_lane(jnp.logical_and(gl == gmax, is_g))
    g_p = 1.0 / jnp.sum(jnp.where(is_g, jnp.exp(gl - gmax), 0.0), axis=1, keepdims=True)
    lo = ROUTE_LANE0 + EXPERTS_PER_GROUP * g_i
    sel = jnp.logical_and(lane >= lo, lane < lo + EXPERTS_PER_GROUP)
    el = jnp.where(sel, l, NEG)
    m1 = jnp.max(el, axis=1, keepdims=True)
    i1 = first_lane(jnp.logical_and(el == m1, sel))
    sel2 = jnp.logical_and(sel, lane != i1)
    el2 = jnp.where(sel2, l, NEG)
    m2 = jnp.max(el2, axis=1, keepdims=True)
    i2 = first_lane(jnp.logical_and(el2 == m2, sel2))
    r = jnp.exp(m2 - m1)
    w1 = g_p / (1.0 + r)
    w2 = g_p * r / (1.0 + r)

    hit1 = lane == i1
    hit2 = lane == i2
    onehot = jnp.where(jnp.logical_or(hit1, hit2), 1.0, 0.0)
    before = carry[...] + jnp.dot(tri_ref[...], onehot.astype(BF16), preferred_element_type=F32)
    rank1 = jnp.sum(jnp.where(hit1, before, 0.0), axis=1, keepdims=True)
    rank2 = jnp.sum(jnp.where(hit2, before, 0.0), axis=1, keepdims=True)
    carry[...] = carry[...] + jnp.sum(onehot, axis=0, keepdims=True)
    cnt_ref[...] = carry[...]

    out = jnp.zeros(l.shape, F32)
    for k, val in enumerate((i1 - ROUTE_LANE0, i2 - ROUTE_LANE0, w1, w2, rank1, rank2)):
        out = jnp.where(lane == k, val, out)
    rt_ref[...] = out


def _router(logits, counts_in, tm):
    T = logits.shape[0]
    tri = jnp.asarray(np.tril(np.ones((tm, tm), np.float32), -1), BF16)
    return pl.pallas_call(
        _router_kernel,
        grid=(T // tm,),
        in_specs=[pl.BlockSpec((tm, LANES), lambda i: (i, 0)), _resident((1, LANES)), _resident((tm, tm))],
        out_specs=[pl.BlockSpec((tm, LANES), lambda i: (i, 0)), pl.BlockSpec((1, LANES), lambda i: (0, 0))],
        out_shape=[jax.ShapeDtypeStruct((T, LANES), F32), jax.ShapeDtypeStruct((1, LANES), F32)],
        scratch_shapes=[pltpu.VMEM((1, LANES), F32)],
        compiler_params=_cparams(("arbitrary",)),
        name="router",
    )(logits, counts_in, tri)


def _layout(route, counts, block, n_tokens):
    counts = counts[0, ROUTE_LANE0:ROUTE_LANE0 + N_EXPERTS].astype(jnp.int32)
    padded = (counts + block - 1) // block * block
    pad_end = jnp.cumsum(padded)
    pad_start = (pad_end - padded).astype(F32)
    experts = route[:, 0:TOP_K]
    onehot = experts[:, :, None] == jnp.arange(N_EXPERTS, dtype=F32)[None, None, :]
    slots = (route[:, 4:4 + TOP_K] + jnp.sum(jnp.where(onehot, pad_start[None, None, :], 0.0), axis=-1)).astype(jnp.int32)
    nb = -(-n_tokens * TOP_K // block) + N_EXPERTS
    blk_e = jnp.minimum(jnp.sum(pad_end[None, :] <= (jnp.arange(nb, dtype=jnp.int32) * block)[:, None], axis=1),
                        N_EXPERTS - 1).astype(jnp.int32)
    n_used = (pad_end[-1] // block).astype(jnp.int32).reshape(1)
    return slots, route[:, 2:2 + TOP_K], blk_e, n_used, nb


def _row_copy(src_ref, src_row, dst_ref, dst_row, sem):
    return pltpu.make_async_copy(
        src_ref.at[pl.ds(pl.multiple_of(src_row * ROW_TILES, ROW_TILES), ROW_TILES), :],
        dst_ref.at[pl.ds(pl.multiple_of(dst_row * ROW_TILES, ROW_TILES), ROW_TILES), :], sem)


def _scatter_kernel(slot_ref, src_ref, dst_in_ref, dst_ref, sem, *, ts):
    del dst_in_ref

    def issue(j, c):
        for u in range(DMA_UNROLL):
            t = j * DMA_UNROLL + u
            for k in range(TOP_K):
                _row_copy(src_ref, t, dst_ref, slot_ref[0, 0, TOP_K * t + k], sem).start()
        return c
    lax.fori_loop(0, ts // DMA_UNROLL, issue, 0)

    for _ in range(TOP_K):
        pltpu.make_async_copy(src_ref, dst_ref.at[pl.ds(0, ts * ROW_TILES), :], sem).wait()


def _scatter(slots, src, dst, ts):
    T = slots.shape[0]
    return pl.pallas_call(
        functools.partial(_scatter_kernel, ts=ts),
        grid=(T // ts,),
        in_specs=[pl.BlockSpec((1, 1, ts * TOP_K), lambda i: (i, 0, 0), memory_space=pltpu.SMEM),
                  pl.BlockSpec((ts * ROW_TILES, LANES), lambda i: (i, 0)),
                  pl.BlockSpec(memory_space=pl.ANY)],
        out_specs=pl.BlockSpec(memory_space=pl.ANY),
        out_shape=jax.ShapeDtypeStruct(dst.shape, dst.dtype),
        scratch_shapes=[pltpu.SemaphoreType.DMA(())],
        input_output_aliases={2: 0},
        compiler_params=_cparams(("arbitrary",)),
        name="moe_scatter",
    )(slots.reshape(T // ts, 1, ts * TOP_K), src, dst)


def _experts_kernel(be_ref, nu_ref, x_ref, w1_ref, w3_ref, w2_ref, y_ref, w1b, w3b, w2b, *, block):
    i = pl.program_id(0)

    @pl.when(i < nu_ref[0])
    def _():
        @pl.when(jnp.logical_or(i == 0, be_ref[i] != be_ref[jnp.maximum(i - 1, 0)]))
        def _():
            w1b[...] = w1_ref[...].astype(BF16)
            w3b[...] = w3_ref[...].astype(BF16)
            w2b[...] = w2_ref[...].astype(BF16)

        x = _load_rows8(x_ref, block).astype(BF16)
        h1 = jnp.dot(x, w1b[...], preferred_element_type=F32)
        h3 = jnp.dot(x, w3b[...], preferred_element_type=F32)
        a = (h1 * jax.nn.sigmoid(h1) * h3).astype(BF16)
        _store_rows8(y_ref, jnp.dot(a, w2b[...], preferred_element_type=F32), block)


def _experts(xs, blk_e, n_used, nb, w1, w3, w2, layer, block):
    last = lambda i, nu: jnp.minimum(i, nu[0] - 1)
    rows = pl.BlockSpec((block * ROW_TILES, LANES), lambda i, be, nu: (last(i, nu), 0))
    wspec = lambda r, c: pl.BlockSpec((None, None, r, c), lambda i, be, nu: (layer, be[last(i, nu)], 0, 0))
    return pl.pallas_call(
        functools.partial(_experts_kernel, block=block),
        grid_spec=pltpu.PrefetchScalarGridSpec(
            num_scalar_prefetch=2,
            grid=(nb,),
            in_specs=[rows, wspec(D_MODEL, D_EXPERT), wspec(D_MODEL, D_EXPERT), wspec(D_EXPERT, D_MODEL)],
            out_specs=rows,
            scratch_shapes=[pltpu.VMEM((D_MODEL, D_EXPERT), BF16), pltpu.VMEM((D_MODEL, D_EXPERT), BF16),
                            pltpu.VMEM((D_EXPERT, D_MODEL), BF16)]),
        out_shape=jax.ShapeDtypeStruct(xs.shape, F32),
        input_output_aliases={2: 0},
        compiler_params=_cparams(("arbitrary",)),
        name="moe_experts",
    )(blk_e, n_used, xs, w1, w3, w2)


def _combine_kernel(slot_ref, y_ref, x_ref, mod_ref, wt_ref, o_ref, ybuf, sem, *, tc):
    def issue(j, c):
        for u in range(DMA_UNROLL):
            t = j * DMA_UNROLL + u
            for k in range(TOP_K):
                _row_copy(y_ref, slot_ref[0, 0, TOP_K * t + k], ybuf, k * tc + t, sem).start()
        return c
    lax.fori_loop(0, tc // DMA_UNROLL, issue, 0)
    pltpu.make_async_copy(y_ref.at[pl.ds(0, TOP_K * tc * ROW_TILES), :], ybuf, sem).wait()

    wt = wt_ref[...]
    y = (wt[:, 0:1] * _load_rows8(ybuf, tc) + wt[:, 1:2] * _load_rows8(ybuf, tc, base=tc * ROW_TILES))
    o_ref[...] = x_ref[...] + mod_ref[0][:, 5 * D_MODEL:6 * D_MODEL] * y


def _combine(slots, y, x, mod, tiles_per_mod, wts, tc):
    T = x.shape[0]
    return pl.pallas_call(
        functools.partial(_combine_kernel, tc=tc),
        grid=(T // tc,),
        in_specs=[pl.BlockSpec((1, 1, tc * TOP_K), lambda i: (i, 0, 0), memory_space=pltpu.SMEM),
                  pl.BlockSpec(memory_space=pl.ANY),
                  pl.BlockSpec((tc, D_MODEL), lambda i: (i, 0)),
                  _mod_spec(mod, tiles_per_mod),
                  pl.BlockSpec((tc, TOP_K), lambda i: (i, 0))],
        out_specs=pl.BlockSpec((tc, D_MODEL), lambda i: (i, 0)),
        out_shape=jax.ShapeDtypeStruct((T, D_MODEL), F32),
        scratch_shapes=[pltpu.VMEM((TOP_K * tc * ROW_TILES, LANES), F32), pltpu.SemaphoreType.DMA(())],
        compiler_params=_cparams(("arbitrary",)),
        name="moe_combine",
    )(slots.reshape(T // tc, 1, tc * TOP_K), y, x, mod, wts)


def _moe(parts, w1, w3, w2, layer):
    counts = jnp.zeros((1, LANES), F32)
    routes = []
    for _, lg, _, _, _, tt in parts:
        rt, counts = _router(lg, counts, tt)
        routes.append(rt)
    n_tokens = sum(p[2].shape[0] for p in parts)
    slots, wts, blk_e, n_used, nb = _layout(jnp.concatenate(routes, axis=0), counts, MOE_BLOCK, n_tokens)
    xs = jnp.zeros((nb * MOE_BLOCK * ROW_TILES, LANES), F32)
    off = 0
    for h, _, x, _, _, tt in parts:
        T = x.shape[0]
        xs = _scatter(slots[off:off + T], h, xs, tt)
        off += T
    y = _experts(xs, blk_e, n_used, nb, w1, w3, w2, layer, MOE_BLOCK)
    outs, off = [], 0
    for _, _, x, mod, tpm, tt in parts:
        T = x.shape[0]
        outs.append(_combine(slots[off:off + T], y, x, mod, tpm, wts[off:off + T], tt))
        off += T
    return outs


def _kvq_kernel(x_ref, modkv_ref, mod_ref, nkv_ref, nm_ref, wkv_ref, wq_ref, kn_ref, qn_ref, eavg_ref,
                *rest, tm, keep):
    residue = keep is not None
    x = x_ref[...]
    xn = x * lax.rsqrt(jnp.mean(x * x, axis=-1, keepdims=True) + EPS)
    modkv = modkv_ref[0]
    mod = mod_ref[0]
    hk = (xn * nkv_ref[...] * (1.0 + modkv[:, D_MODEL:]) + modkv[:, :D_MODEL]).astype(BF16)
    hq = (xn * nm_ref[...] * (1.0 + mod[:, D_MODEL:2 * D_MODEL]) + mod[:, :D_MODEL]).astype(BF16)
    eavg = eavg_ref[...]

    def head_norm(a, g):
        ms = jnp.dot((a * a).astype(BF16), eavg, preferred_element_type=F32)
        return a * lax.rsqrt(ms + EPS) * g

    def residue_major(val, ref, dil):
        if dil == 1:
            ref[0, 0] = val.astype(BF16)
            return
        stage = rest[-1]
        for c in range(stage.shape[0]):
            stage[c] = val[:, c * LANES:(c + 1) * LANES]
        for r in range(dil):
            rows = [stage[c, pl.ds(r, tm // dil, stride=dil), :] for c in range(stage.shape[0])]
            ref[0, r] = jnp.concatenate(rows, axis=1).astype(BF16)

    for g, (_, dil) in enumerate(BRANCHES):
        k0 = 2 * g * ATTN_WIDTH
        kvg = jnp.dot(hk, wkv_ref[:, k0:k0 + 2 * ATTN_WIDTH], preferred_element_type=F32)
        kn = head_norm(kvg[:, :ATTN_WIDTH], kn_ref[g:g + 1, :])
        vv = kvg[:, ATTN_WIDTH:]
        q0 = g * ATTN_WIDTH
        qg = jnp.dot(hq, wq_ref[:, q0:q0 + ATTN_WIDTH], preferred_element_type=F32)
        qg = head_norm(qg, qn_ref[g:g + 1, :]) * (HEAD_DIM ** -0.5)
        if residue:
            residue_major(qg, rest[3 * g], dil)
            residue_major(kn, rest[3 * g + 1], dil)
            residue_major(vv, rest[3 * g + 2], dil)
            tiles_per_batch, kept = keep

            @pl.when(pl.program_id(0) % tiles_per_batch >= tiles_per_batch - kept[g])
            def _(kn=kn, vv=vv, kt_ref=rest[3 * N_BRANCH + g]):
                kt_ref[0, :ATTN_WIDTH, :] = kn.T
                kt_ref[0, ATTN_WIDTH:, :] = vv.T
        else:
            rest[0][:, k0:k0 + ATTN_WIDTH] = kn
            rest[0][:, k0 + ATTN_WIDTH:k0 + 2 * ATTN_WIDTH] = vv
            rest[1][:, q0:q0 + ATTN_WIDTH] = qg


def _kvq(x, modkv, mod, tiles_per_mod, tm, nkv, nm, wkv, wq, kn, qn, eavg, batch=None):
    T = x.shape[0]
    tok = lambda w: pl.BlockSpec((tm, w), lambda i: (i, 0))
    kvw = 2 * N_BRANCH * ATTN_WIDTH
    if batch is not None:
        L = T // batch
        tpb = L // tm
        kept = tuple(-(-min(w, L) // tm) for w, _ in BRANCHES)
        keep = (tpb, kept)
        out_specs, out_shape = [], []
        for _, dil in BRANCHES:
            for _ in range(3):
                out_specs.append(pl.BlockSpec((1, dil, tm // dil, ATTN_WIDTH), lambda i: (i // tpb, 0, i % tpb, 0)))
                out_shape.append(jax.ShapeDtypeStruct((batch, dil, L // dil, ATTN_WIDTH), BF16))
        for kg in kept:
            out_specs.append(pl.BlockSpec((1, 2 * ATTN_WIDTH, tm),
                                          lambda i, kg=kg: (i // tpb, 0, jnp.maximum(i % tpb - (tpb - kg), 0))))
            out_shape.append(jax.ShapeDtypeStruct((batch, 2 * ATTN_WIDTH, kg * tm), F32))
        scratch = [pltpu.VMEM((ATTN_WIDTH // LANES, tm, LANES), F32)]
    else:
        keep = None
        out_specs = [tok(kvw), tok(N_BRANCH * ATTN_WIDTH)]
        out_shape = [jax.ShapeDtypeStruct((T, kvw), F32), jax.ShapeDtypeStruct((T, N_BRANCH * ATTN_WIDTH), F32)]
        scratch = []
    return pl.pallas_call(
        functools.partial(_kvq_kernel, tm=tm, keep=keep),
        grid=(T // tm,),
        in_specs=[tok(D_MODEL), _mod_spec(modkv, tiles_per_mod), _mod_spec(mod, tiles_per_mod),
                  _resident(nkv.shape), _resident(nm.shape), _resident(wkv.shape), _resident(wq.shape),
                  _resident(kn.shape), _resident(qn.shape), _resident(eavg.shape)],
        out_specs=out_specs,
        out_shape=out_shape,
        scratch_shapes=scratch,
        compiler_params=_cparams(("arbitrary",)),
        name="kvq",
    )(x, modkv, mod, nkv, nm, wkv, wq, kn, qn, eavg)


def _band_attn_kernel(q_ref, kp_ref, kc_ref, vp_ref, vc_ref, bias_ref, o_ref, lse_ref):
    n = pl.program_id(2)
    q = q_ref[...]
    k2 = jnp.concatenate([kp_ref[...], kc_ref[...]], axis=0)
    v2 = jnp.concatenate([vp_ref[...], vc_ref[...]], axis=0)
    col = lax.broadcasted_iota(jnp.int32, (N_STEPS, 2 * N_STEPS), 1)
    visible = jnp.logical_or(col >= N_STEPS, n > 0)
    lane = lax.broadcasted_iota(jnp.int32, (N_STEPS, LANES), 1)
    lse_all = jnp.zeros((N_STEPS, LANES), F32)
    for h in range(HEADS):
        sl = slice(h * HEAD_DIM, (h + 1) * HEAD_DIM)
        s = lax.dot_general(q[:, sl], k2[:, sl], (((1,), (1,)), ((), ())), preferred_element_type=F32)
        s = jnp.where(visible, s + bias_ref[h], NEG)
        m = jnp.max(s, axis=1, keepdims=True)
        p = jnp.exp(s - m)
        l = jnp.sum(p, axis=1, keepdims=True)
        o_ref[:, sl] = jnp.dot(p.astype(BF16), v2[:, sl], preferred_element_type=F32) / l
        lse_all = jnp.where(lane == h, m + jnp.log(l), lse_all)
    lse_ref[...] = lse_all


def _band_attn(q, k, v, bias, dil):
    B, _, Ld, _ = q.shape
    blk = lambda w, f: pl.BlockSpec((None, None, N_STEPS, w), f)
    cur = lambda b, r, n: (b, r, n, 0)
    prev = lambda b, r, n: (b, r, jnp.maximum(n - 1, 0), 0)
    return pl.pallas_call(
        _band_attn_kernel,
        grid=(B, dil, Ld // N_STEPS),
        in_specs=[blk(ATTN_WIDTH, cur), blk(ATTN_WIDTH, prev), blk(ATTN_WIDTH, cur), blk(ATTN_WIDTH, prev),
                  blk(ATTN_WIDTH, cur), _resident(bias.shape)],
        out_specs=[blk(ATTN_WIDTH, cur), blk(LANES, cur)],
        out_shape=[jax.ShapeDtypeStruct((B, dil, Ld, ATTN_WIDTH), F32),
                   jax.ShapeDtypeStruct((B, dil, Ld, LANES), F32)],
        compiler_params=_cparams(("arbitrary", "arbitrary", "arbitrary")),
        name=f"band_attn_d{dil}",
    )(q, k, k, v, v, bias)


def _rel_buckets(dilation):
    n = np.arange(N_STEPS + 1) * dilation
    large = MAX_EXACT + (np.log(np.maximum(n, 1) / MAX_EXACT) / np.log(REL_MAX_DIST / MAX_EXACT)
                         * (NUM_BUCKETS - MAX_EXACT)).astype(np.int32)
    return np.where(n < MAX_EXACT, n, np.minimum(large, NUM_BUCKETS - 1)).astype(np.int32)


def _step_bias_row(rel_bias, g, dil):
    onehot = np.zeros((N_STEPS + 1, NUM_BUCKETS), np.float32)
    onehot[np.arange(N_STEPS + 1), _rel_buckets(dil)] = 1.0
    return jnp.dot(jnp.asarray(onehot), rel_bias[:, g, :].astype(F32), precision=lax.Precision.HIGHEST).T


def _band_bias(rel_bias, g, dil):
    bias = _step_bias_row(rel_bias, g, dil)
    P = 3 * N_STEPS
    neg = jnp.full((HEADS, N_STEPS), NEG, F32)
    ext = jnp.concatenate([neg, bias[:, ::-1], neg], axis=1)
    flat = jnp.broadcast_to(ext[:, None, :], (HEADS, N_STEPS, P + 1)).reshape(HEADS, N_STEPS * (P + 1))
    skew = flat[:, :N_STEPS * P].reshape(HEADS, N_STEPS, P)
    return skew[:, :, N_STEPS:]


def _step_attn_kernel(q_ref, kvn_ref, c0_ref, c1_ref, c2_ref, b0_ref, b1_ref, b2_ref, bn_ref, o_ref):
    caches = (c0_ref, c1_ref, c2_ref)
    cbias = (b0_ref, b1_ref, b2_ref)
    nt = (((1,), (1,)), ((), ()))
    for h in range(HEADS):
        parts = []
        for g in range(N_BRANCH):
            lo = g * ATTN_WIDTH + h * HEAD_DIM
            klo = 2 * g * ATTN_WIDTH + h * HEAD_DIM
            q = q_ref[0, :, lo:lo + HEAD_DIM].astype(BF16)
            kn = kvn_ref[0, :, klo:klo + HEAD_DIM].astype(BF16)
            vn = kvn_ref[0, :, klo + ATTN_WIDTH:klo + ATTN_WIDTH + HEAD_DIM].astype(BF16)
            kt = caches[g][0, 0, h].astype(BF16)
            vt = caches[g][0, 1, h].astype(BF16)
            sc = jnp.dot(q, kt, preferred_element_type=F32) + cbias[g][h]
            sn = lax.dot_general(q, kn, nt, preferred_element_type=F32) + bn_ref[g, h]
            m = jnp.maximum(jnp.max(sc, axis=1, keepdims=True), jnp.max(sn, axis=1, keepdims=True))
            pc = jnp.exp(sc - m)
            pn = jnp.exp(sn - m)
            l = jnp.sum(pc, axis=1, keepdims=True) + jnp.sum(pn, axis=1, keepdims=True)
            pv = (lax.dot_general(pc.astype(BF16), vt, nt, preferred_element_type=F32)
                  + jnp.dot(pn.astype(BF16), vn, preferred_element_type=F32))
            parts.append((m + jnp.log(l), pv / l))
        mx = jnp.maximum(jnp.maximum(parts[0][0], parts[1][0]), parts[2][0])
        es = [jnp.exp(lse - mx) for lse, _ in parts]
        den = es[0] + es[1] + es[2]
        o_ref[0, :, h * HEAD_DIM:(h + 1) * HEAD_DIM] = (
            es[0] * parts[0][1] + es[1] * parts[1][1] + es[2] * parts[2][1]) / den


def _step_bias(rel_bias, n_new):
    cache_tabs, new_tabs = [], []
    for g, (win, dil) in enumerate(BRANCHES):
        bias = _step_bias_row(rel_bias, g, dil)
        rev = bias[:, ::-1]
        per_s = []
        for s in range(n_new):
            if dil == 1:
                neg = jnp.full((HEADS, s), NEG, F32)
                per_s.append(jnp.concatenate([neg, rev[:, :win - s]], axis=1))
            else:
                cols = [rev[:, :N_STEPS] if r == s % dil else jnp.full((HEADS, N_STEPS), NEG, F32)
                        for r in range(dil)]
                per_s.append(jnp.stack(cols, axis=2).reshape(HEADS, win))
        cache_tabs.append(jnp.stack(per_s, axis=1))
        dist = np.arange(n_new)[:, None] - np.arange(8)[None, :]
        ok = (dist >= 0) & (dist % dil == 0) & (np.arange(8)[None, :] < n_new)
        onehot = np.zeros((N_STEPS + 1, n_new * 8), np.float32)
        onehot[np.where(ok, dist // dil, 0).reshape(-1), np.arange(n_new * 8)] = 1.0
        tab = jnp.dot(bias, jnp.asarray(onehot), precision=lax.Precision.HIGHEST).reshape(HEADS, n_new, 8)
        new_tabs.append(jnp.where(ok[None], tab, NEG))
    return cache_tabs, jnp.stack(new_tabs)


def _step_attn(q, kvn, caches, cache_bias, new_bias, DB, S):
    cspec = lambda c: pl.BlockSpec((1,) + c.shape[1:], lambda b: (b, 0, 0, 0, 0))
    return pl.pallas_call(
        _step_attn_kernel,
        grid=(DB,),
        in_specs=[pl.BlockSpec((1, S, N_BRANCH * ATTN_WIDTH), lambda b: (b, 0, 0)),
                  pl.BlockSpec((1, 8, 2 * N_BRANCH * ATTN_WIDTH), lambda b: (b, 0, 0)),
                  cspec(caches[0]), cspec(caches[1]), cspec(caches[2]),
                  _resident(cache_bias[0].shape), _resident(cache_bias[1].shape), _resident(cache_bias[2].shape),
                  _resident(new_bias.shape)],
        out_specs=pl.BlockSpec((1, S, ATTN_WIDTH), lambda b: (b, 0, 0)),
        out_shape=jax.ShapeDtypeStruct((DB, S, ATTN_WIDTH), F32),
        compiler_params=_cparams(("arbitrary",)),
        name="step_attn",
    )(q, kvn, caches[0], caches[1], caches[2], *cache_bias, new_bias)


def _attn_out_kernel(*refs, dils, tm):
    n_o = max(len(dils), 1)
    n_in = n_o + len(dils) + 7
    (x_ref, mod_ref, wo_ref, ex_ref, nf_ref, wr_ref, br_ref, x3_ref, h3_ref, lg_ref) = refs[n_in - 7:n_in + 3]
    stages = refs[n_in + 3:]

    def natural(ref, stage, dil):
        if dil == 1:
            return ref[0, 0]
        for r in range(dil):
            val = ref[0, r]
            for c in range(stage.shape[0]):
                stage[c, pl.ds(r, tm // dil, stride=dil), :] = val[:, c * LANES:(c + 1) * LANES]
        return jnp.concatenate([stage[c] for c in range(stage.shape[0])], axis=1)

    if dils:
        lses = [natural(refs[n_o + g], stages[2 * g + 1], dil) for g, dil in enumerate(dils)]
        mx = functools.reduce(jnp.maximum, lses)
        es = [jnp.exp(l - mx) for l in lses]
        den = functools.reduce(lambda a, b: a + b, es)
        o = None
        for g, dil in enumerate(dils):
            term = _split_dot(es[g] / den, ex_ref[...]) * natural(refs[g], stages[2 * g], dil)
            o = term if o is None else o + term
    else:
        o = refs[0][...]
    mod = mod_ref[0]
    a = jnp.dot(o.astype(BF16), wo_ref[...], preferred_element_type=F32)
    x3 = x_ref[...] + mod[:, 2 * D_MODEL:3 * D_MODEL] * a
    x3_ref[...] = x3
    _ffn_pre(x3, mod, nf_ref, wr_ref, br_ref, h3_ref, lg_ref, tm)


def _attn_out(os_, lses, x, mod, tiles_per_mod, tm, wo, ex, nf, wr, br):
    T = x.shape[0]
    tok = lambda w: pl.BlockSpec((tm, w), lambda i: (i, 0))
    if lses:
        dils = tuple(o.shape[1] for o in os_)
        tpb = tiles_per_mod
        res = lambda a: pl.BlockSpec((1, a.shape[1], tm // a.shape[1], a.shape[3]), lambda i: (i // tpb, 0, i % tpb, 0))
        o_specs = [res(a) for a in os_] + [res(a) for a in lses]
        scratch = []
        for _ in dils:
            scratch += [pltpu.VMEM((ATTN_WIDTH // LANES, tm, LANES), F32), pltpu.VMEM((1, tm, LANES), F32)]
    else:
        dils, o_specs, scratch = (), [tok(ATTN_WIDTH)], []
    return pl.pallas_call(
        functools.partial(_attn_out_kernel, dils=dils, tm=tm),
        grid=(T // tm,),
        in_specs=(o_specs + [tok(D_MODEL), _mod_spec(mod, tiles_per_mod), _resident(wo.shape), _resident(ex.shape),
                             _resident(nf.shape), _resident(wr.shape), _resident(br.shape)]),
        out_specs=[tok(D_MODEL), pl.BlockSpec((tm * ROW_TILES, LANES), lambda i: (i, 0)), tok(LANES)],
        out_shape=[jax.ShapeDtypeStruct((T, D_MODEL), F32),
                   jax.ShapeDtypeStruct((T * ROW_TILES, LANES), F32),
                   jax.ShapeDtypeStruct((T, LANES), F32)],
        scratch_shapes=scratch,
        compiler_params=_cparams(("arbitrary",)),
        name="attn_out",
    )(*os_, *lses, x, mod, wo, ex, nf, wr, br)


def kernel(x_prompt, x_sample, cache_kv_w128, cache_kv_w512, cache_kv_w2048, c_prompt, c_sample, ada_w, ada_b, norm_mix, norm_ffn, a_w_in, a_b_in, a_norm_v, a_w_s, a_b_s, a_w_out, kv_ada_w, kv_ada_b, kv_norm, w_kv, k_norm, rel_bias, b_w_q, q_norm, b_w_o, r_w_group, r_b_group, r_w_expert, r_b_expert, e_w1, e_w3, e_w2):
    B, L, _ = x_prompt.shape
    DB, S, _ = x_sample.shape
    Tp, Ts = B * L, DB * S
    tm = TOKEN_TILE
    tpm_p = L // tm

    c_all = jnp.concatenate([c_prompt, c_sample], axis=0)
    R = c_all.shape[0]
    c_all = jnp.pad(c_all, ((0, -R % 8), (0, 0)))
    mods = [_ada(c_all, ada_w, ada_b, l) for l in range(2)]
    modkv = _ada(c_all, kv_ada_w[None], kv_ada_b[None], 0)

    def split_mod(m):
        return m[:B, None, :], jnp.repeat(m[B:B + DB], S, axis=0)[None]
    mod_p, mod_s = zip(*[split_mod(m) for m in mods])
    modkv_p, modkv_s = split_mod(modkv)

    row = lambda a: a.reshape(1, -1)

    def router(l):
        wr = jnp.zeros((D_MODEL, LANES), F32)
        wr = wr.at[:, :N_GROUPS].set(r_w_group[l]).at[:, N_GROUPS:N_GROUPS + N_EXPERTS].set(r_w_expert[l])
        br = jnp.zeros((1, LANES), F32)
        br = br.at[0, :N_GROUPS].set(r_b_group[l]).at[0, N_GROUPS:N_GROUPS + N_EXPERTS].set(r_b_expert[l])
        hi = wr.astype(BF16)
        return jnp.stack([hi, (wr - hi.astype(F32)).astype(BF16)]), br

    win = a_w_in[0].astype(BF16)
    wout = a_w_out[0].astype(BF16)
    tril = jnp.tril(jnp.ones((CHUNK, CHUNK), bool))
    ws_p = jnp.where(tril, a_w_s[0], 0).astype(BF16)
    bs_p = a_b_s[0].T
    cs = min(CHUNK, S)
    ws_small = jnp.where(jnp.tril(jnp.ones((cs, cs), bool)), a_w_s[0][:, :cs, :cs], 0)
    ws_s = jnp.stack([jnp.kron(jnp.eye(Ts // cs, dtype=F32), ws_small[g]) for g in range(GMLP_GROUPS)]).astype(BF16)
    bs_s = jnp.tile(a_b_s[0][:, :cs], (1, Ts // cs)).T
    wr0, br0 = router(0)
    common = (row(norm_mix[0]), win, row(a_b_in[0]), row(a_norm_v[0]))
    xp = x_prompt.reshape(Tp, D_MODEL)
    xs_ = x_sample.reshape(Ts, D_MODEL)
    x1_p, h2_p, lg_p = _mixer_a(xp, mod_p[0], tpm_p, tm, *common, ws_p, bs_p, wout, row(norm_ffn[0]), wr0, br0,
                                with_v=False)
    x1_s, h2_s, lg_s, v_s = _mixer_a(xs_, mod_s[0], 1, Ts, *common, ws_s, bs_s, wout, row(norm_ffn[0]), wr0, br0,
                                     with_v=True)
    x2_p, x2_s = _moe([(h2_p, lg_p, x1_p, mod_p[0], tpm_p, tm), (h2_s, lg_s, x1_s, mod_s[0], 1, Ts)],
                      e_w1, e_w3, e_w2, 0)

    wkv = w_kv.astype(BF16)
    wq = b_w_q[0].astype(BF16)
    kn = jnp.tile(k_norm, (1, HEADS))
    qn = jnp.tile(q_norm[0], (1, HEADS))
    head = np.arange(ATTN_WIDTH) // HEAD_DIM
    eavg = jnp.asarray((head[:, None] == head[None, :]) / HEAD_DIM, BF16)
    kvq_w = (row(kv_norm), row(norm_mix[1]), wkv, wq, kn, qn, eavg)
    *qkv_p, kt0, kt1, kt2 = _kvq(x2_p, modkv_p, mod_p[1], tpm_p, tm, *kvq_w, batch=B)
    kv_s, q_s = _kvq(x2_s, modkv_s, mod_s[1], 1, Ts, *kvq_w)

    os_, lses = [], []
    for g, (_, dil) in enumerate(BRANCHES):
        o, lse = _band_attn(*qkv_p[3 * g:3 * g + 3], _band_bias(rel_bias, g, dil), dil)
        os_.append(o)
        lses.append(lse)
    caches = [jnp.transpose(c, (0, 2, 3, 4, 1)) for c in (cache_kv_w128, cache_kv_w512, cache_kv_w2048)]
    kvn_s = jnp.pad(kv_s.reshape(DB, S, -1), ((0, 0), (0, 8 - S), (0, 0)))
    o_s = _step_attn(q_s.reshape(DB, S, -1), kvn_s, caches, *_step_bias(rel_bias, S), DB, S)

    wo = b_w_o[0].astype(BF16)
    ex = jnp.asarray(np.arange(LANES)[:, None] == head[None, :], BF16)
    wr1, br1 = router(1)
    x3_p, h3_p, lg1_p = _attn_out(os_, lses, x2_p, mod_p[1], tpm_p, tm, wo, ex, row(norm_ffn[1]), wr1, br1)
    x3_s, h3_s, lg1_s = _attn_out([o_s.reshape(Ts, ATTN_WIDTH)], [], x2_s, mod_s[1], 1, Ts, wo, ex,
                                  row(norm_ffn[1]), wr1, br1)
    y_p, y_s = _moe([(h3_p, lg1_p, x3_p, mod_p[1], tpm_p, tm), (h3_s, lg1_s, x3_s, mod_s[1], 1, Ts)],
                    e_w1, e_w3, e_w2, 1)

    kv_s4 = kv_s.reshape(DB, S, N_BRANCH, 2, HEADS, HEAD_DIM)

    def window(kt, w):
        n = min(w, L)
        return jnp.transpose(kt[:, :, kt.shape[2] - n:].reshape(B, 2, HEADS, HEAD_DIM, n), (0, 4, 1, 2, 3))
    return (y_p.reshape(B, L, D_MODEL), y_s.reshape(DB, S, D_MODEL),
            window(kt0, BRANCHES[0][0]), window(kt1, BRANCHES[1][0]), window(kt2, BRANCHES[2][0]),
            kv_s4[:, :, 0], kv_s4[:, :, 1], kv_s4[:, :, 2],
            v_s.reshape(1, DB, S, GMLP_WIDTH))
```

```python
import functools

import numpy as np
import jax
import jax.numpy as jnp
from jax import lax
from jax.experimental import pallas as pl
from jax.experimental.pallas import tpu as pltpu

F32 = jnp.float32
BF16 = jnp.bfloat16

D_MODEL = 1024
GMLP_WIDTH = 2048
GMLP_GROUPS = 4
GROUP_WIDTH = GMLP_WIDTH // GMLP_GROUPS
CHUNK = 128
BRANCHES = ((128, 1), (512, 4), (2048, 16))
N_BRANCH = 3
N_STEPS = 128
HEADS = 8
HEAD_DIM = 64
ATTN_WIDTH = HEADS * HEAD_DIM
NUM_BUCKETS = 32
MAX_EXACT = NUM_BUCKETS // 2
REL_MAX_DIST = 2048
N_GROUPS = 4
EXPERTS_PER_GROUP = 8
N_EXPERTS = N_GROUPS * EXPERTS_PER_GROUP
TOP_K = 2
D_EXPERT = 512
EPS = 1e-6
NEG = -1e30

LANES = 128
ROW_TILES = D_MODEL // LANES
TOKEN_TILE = 256
MOE_BLOCK = 256
DMA_UNROLL = 8
VMEM_LIMIT = 52 * 1024 * 1024


def _cparams(sem):
    return pltpu.CompilerParams(dimension_semantics=sem, vmem_limit_bytes=VMEM_LIMIT)


def _resident(shape):
    nd = len(shape)
    return pl.BlockSpec(shape, lambda *_, _nd=nd: (0,) * _nd, pipeline_mode=pl.Buffered(1))


def _gelu_tanh(x):
    return 0.5 * x * (1.0 + jnp.tanh(0.7978845608028654 * (x + 0.044715 * (x * x * x))))


def _rms(x, g):
    return x * lax.rsqrt(jnp.mean(x * x, axis=-1, keepdims=True) + EPS) * g


def _store_rows8(ref, val, n, base=0):
    for s in range(ROW_TILES):
        ref[pl.ds(base + s, n, stride=ROW_TILES), :] = val[:, s * LANES:(s + 1) * LANES]


def _load_rows8(ref, n, base=0):
    return jnp.concatenate([ref[pl.ds(base + s, n, stride=ROW_TILES), :] for s in range(ROW_TILES)], axis=1)


def _split(a):
    hi = a.astype(BF16)
    return hi, (a - hi.astype(F32)).astype(BF16)


def _split_dot(a, e_bf16):
    hi, lo = _split(a)
    return (jnp.dot(hi, e_bf16, preferred_element_type=F32) + jnp.dot(lo, e_bf16, preferred_element_type=F32))


ROUTE_LANE0 = 4


def _route_rows(l, tri_ref, carry):
    lane = lax.broadcasted_iota(jnp.int32, l.shape, 1).astype(F32)
    far = float(LANES)

    def first_lane(mask):
        return jnp.min(jnp.where(mask, lane, far), axis=1, keepdims=True)

    is_g = lane < N_GROUPS
    gl = jnp.where(is_g, l, NEG)
    gmax = jnp.max(gl, axis=1, keepdims=True)
    g_i = first_lane(jnp.logical_and(gl == gmax, is_g))
    g_p = 1.0 / jnp.sum(jnp.where(is_g, jnp.exp(gl - gmax), 0.0), axis=1, keepdims=True)
    lo = ROUTE_LANE0 + EXPERTS_PER_GROUP * g_i
    sel = jnp.logical_and(lane >= lo, lane < lo + EXPERTS_PER_GROUP)
    el = jnp.where(sel, l, NEG)
    m1 = jnp.max(el, axis=1, keepdims=True)
    i1 = first_lane(jnp.logical_and(el == m1, sel))
    sel2 = jnp.logical_and(sel, lane != i1)
    el2 = jnp.where(sel2, l, NEG)
    m2 = jnp.max(el2, axis=1, keepdims=True)
    i2 = first_lane(jnp.logical_and(el2 == m2, sel2))
    r = jnp.exp(m2 - m1)
    w1 = g_p / (1.0 + r)
    w2 = g_p * r / (1.0 + r)

    hit1 = lane == i1
    hit2 = lane == i2
    onehot = jnp.where(jnp.logical_or(hit1, hit2), 1.0, 0.0)
    before = carry[...] + jnp.dot(tri_ref[...], onehot.astype(BF16), preferred_element_type=F32)
    rank1 = jnp.sum(jnp.where(hit1, before, 0.0), axis=1, keepdims=True)
    rank2 = jnp.sum(jnp.where(hit2, before, 0.0), axis=1, keepdims=True)
    carry[...] = carry[...] + jnp.sum(onehot, axis=0, keepdims=True)

    out = jnp.zeros(l.shape, F32)
    for k, val in enumerate((i1 - ROUTE_LANE0, i2 - ROUTE_LANE0, w1, w2, rank1, rank2)):
        out = jnp.where(lane == k, val, out)
    return out


def _ffn_pre(x, mod, nf_ref, wr_ref, br_ref, cin_ref, tri_ref, h_ref, rt_ref, cnt_ref, carry, n):
    @pl.when(pl.program_id(0) == 0)
    def _():
        carry[...] = cin_ref[...]

    h = _rms(x, nf_ref[...]) * (1.0 + mod[:, 4 * D_MODEL:5 * D_MODEL]) + mod[:, 3 * D_MODEL:4 * D_MODEL]
    _store_rows8(h_ref, h, n)
    logits = jnp.dot(h.astype(BF16), wr_ref[...], preferred_element_type=F32) + br_ref[...]
    rt_ref[...] = _route_rows(logits, tri_ref, carry)
    cnt_ref[...] = carry[...]


def _ada_kernel(c_ref, w_ref, b_ref, o_ref):
    c = c_ref[...]
    a = (c * jax.nn.sigmoid(c)).astype(BF16)
    o_ref[...] = jnp.dot(a, w_ref[...].astype(BF16), preferred_element_type=F32) + b_ref[...]


def _ada(c, w, b, layer):
    R = c.shape[0]
    N = w.shape[2]
    tn = 1024
    return pl.pallas_call(
        _ada_kernel,
        grid=(N // tn,),
        in_specs=[pl.BlockSpec((R, D_MODEL), lambda j: (0, 0)),
                  pl.BlockSpec((None, D_MODEL, tn), lambda j: (layer, 0, j)),
                  pl.BlockSpec((None, 1, tn), lambda j: (layer, 0, j))],
        out_specs=pl.BlockSpec((R, tn), lambda j: (0, j)),
        out_shape=jax.ShapeDtypeStruct((R, N), F32),
        compiler_params=_cparams(("arbitrary",)),
        name="ada",
    )(c, w, b.reshape(b.shape[0], 1, N))


def _mixer_a_kernel(x_ref, mod_ref, nm_ref, win_ref, bin_ref, gv_ref, ws_ref, bs_ref, wout_ref,
                    nf_ref, wr_ref, br_ref, cin_ref, tri_ref, x1_ref, h2_ref, rt_ref, cnt_ref, *rest, tm):
    v_refs, carry = rest[:-1], rest[-1]
    x = x_ref[...]
    mod = mod_ref[0]
    h = (_rms(x, nm_ref[...]) * (1.0 + mod[:, D_MODEL:2 * D_MODEL]) + mod[:, 0:D_MODEL]).astype(BF16)
    zv = jnp.dot(h, win_ref[:, GMLP_WIDTH:], preferred_element_type=F32) + bin_ref[:, GMLP_WIDTH:]
    v = _rms(_gelu_tanh(zv), gv_ref[...])
    if v_refs:
        v_refs[0][...] = v
    vb = v.astype(BF16)
    bs = bs_ref[...]
    acc = jnp.zeros((tm, D_MODEL), F32)
    for g in range(GMLP_GROUPS):
        lo, hi = g * GROUP_WIDTH, (g + 1) * GROUP_WIDTH
        u = _gelu_tanh(jnp.dot(h, win_ref[:, lo:hi], preferred_element_type=F32) + bin_ref[:, lo:hi])
        wg = ws_ref[g]
        gate = jnp.concatenate(
            [jnp.dot(wg, vb[c * CHUNK:(c + 1) * CHUNK, lo:hi], preferred_element_type=F32) + bs[:, g:g + 1]
             for c in range(tm // CHUNK)], axis=0)
        acc = acc + jnp.dot((u * gate).astype(BF16), wout_ref[lo:hi, :], preferred_element_type=F32)
    x1 = x + mod[:, 2 * D_MODEL:3 * D_MODEL] * acc
    x1_ref[...] = x1
    _ffn_pre(x1, mod, nf_ref, wr_ref, br_ref, cin_ref, tri_ref, h2_ref, rt_ref, cnt_ref, carry, tm)


def _mod_spec(mod, tiles_per_mod):
    _, rows, width = mod.shape
    return pl.BlockSpec((1, rows, width), lambda i: (i // tiles_per_mod, 0, 0))


def _earlier_rows(tm):
    return jnp.asarray(np.tril(np.ones((tm, tm), np.float32), -1), BF16)


def _route_outs(T, tm):
    return ([pl.BlockSpec((tm, LANES), lambda i: (i, 0)), pl.BlockSpec((1, LANES), lambda i: (0, 0))],
            [jax.ShapeDtypeStruct((T, LANES), F32), jax.ShapeDtypeStruct((1, LANES), F32)])


def _mixer_a(x, mod, tiles_per_mod, tm, nm, win, bin_, gv, ws, bs_t, wout, nf, wr, br, counts, with_v):
    T = x.shape[0]
    tok = lambda w: pl.BlockSpec((tm, w), lambda i: (i, 0))
    r_specs, r_shapes = _route_outs(T, tm)
    out_shape = [jax.ShapeDtypeStruct((T, D_MODEL), F32), jax.ShapeDtypeStruct((T * ROW_TILES, LANES), F32)] + r_shapes
    out_specs = [tok(D_MODEL), pl.BlockSpec((tm * ROW_TILES, LANES), lambda i: (i, 0))] + r_specs
    if with_v:
        out_shape.append(jax.ShapeDtypeStruct((T, GMLP_WIDTH), F32))
        out_specs.append(tok(GMLP_WIDTH))
    tri = _earlier_rows(tm)
    return pl.pallas_call(
        functools.partial(_mixer_a_kernel, tm=tm),
        grid=(T // tm,),
        in_specs=[tok(D_MODEL), _mod_spec(mod, tiles_per_mod), _resident(nm.shape), _resident(win.shape),
                  _resident(bin_.shape), _resident(gv.shape), _resident(ws.shape), _resident(bs_t.shape),
                  _resident(wout.shape), _resident(nf.shape), _resident(wr.shape), _resident(br.shape),
                  _resident(counts.shape), _resident(tri.shape)],
        out_specs=out_specs,
        out_shape=out_shape,
        scratch_shapes=[pltpu.VMEM((1, LANES), F32)],
        compiler_params=_cparams(("arbitrary",)),
        name="mixer_a",
    )(x, mod, nm, win, bin_, gv, ws, bs_t, wout, nf, wr, br, counts, tri)


def _layout(route, counts, block, n_tokens):
    counts = counts[0, ROUTE_LANE0:ROUTE_LANE0 + N_EXPERTS].astype(jnp.int32)
    padded = (counts + block - 1) // block * block
    pad_end = jnp.cumsum(padded)
    pad_start = (pad_end - padded).astype(F32)
    experts = route[:, 0:TOP_K]
    onehot = experts[:, :, None] == jnp.arange(N_EXPERTS, dtype=F32)[None, None, :]
    slots = (route[:, 4:4 + TOP_K] + jnp.sum(jnp.where(onehot, pad_start[None, None, :], 0.0), axis=-1)).astype(jnp.int32)
    nb = -(-n_tokens * TOP_K // block) + N_EXPERTS
    blk_e = jnp.minimum(jnp.sum(pad_end[None, :] <= (jnp.arange(nb, dtype=jnp.int32) * block)[:, None], axis=1),
                        N_EXPERTS - 1).astype(jnp.int32)
    n_used = (pad_end[-1] // block).astype(jnp.int32).reshape(1)
    return slots, route[:, 2:2 + TOP_K], blk_e, n_used, nb


def _row_copy(src_ref, src_row, dst_ref, dst_row, sem):
    return pltpu.make_async_copy(
        src_ref.at[pl.ds(pl.multiple_of(src_row * ROW_TILES, ROW_TILES), ROW_TILES), :],
        dst_ref.at[pl.ds(pl.multiple_of(dst_row * ROW_TILES, ROW_TILES), ROW_TILES), :], sem)


def _scatter_kernel(slot_ref, src_ref, dst_in_ref, dst_ref, sem, *, ts):
    del dst_in_ref

    def issue(j, c):
        for u in range(DMA_UNROLL):
            t = j * DMA_UNROLL + u
            for k in range(TOP_K):
                _row_copy(src_ref, t, dst_ref, slot_ref[0, 0, TOP_K * t + k], sem).start(priority=k)
        return c
    lax.fori_loop(0, ts // DMA_UNROLL, issue, 0)

    for _ in range(TOP_K):
        pltpu.make_async_copy(src_ref, dst_ref.at[pl.ds(0, ts * ROW_TILES), :], sem).wait()


def _scatter(slots, src, dst, ts):
    T = slots.shape[0]
    return pl.pallas_call(
        functools.partial(_scatter_kernel, ts=ts),
        grid=(T // ts,),
        in_specs=[pl.BlockSpec((1, 1, ts * TOP_K), lambda i: (i, 0, 0), memory_space=pltpu.SMEM),
                  pl.BlockSpec((ts * ROW_TILES, LANES), lambda i: (i, 0)),
                  pl.BlockSpec(memory_space=pl.ANY)],
        out_specs=pl.BlockSpec(memory_space=pl.ANY),
        out_shape=jax.ShapeDtypeStruct(dst.shape, dst.dtype),
        scratch_shapes=[pltpu.SemaphoreType.DMA(())],
        input_output_aliases={2: 0},
        compiler_params=_cparams(("arbitrary",)),
        name="moe_scatter",
    )(slots.reshape(T // ts, 1, ts * TOP_K), src, dst)


def _experts_kernel(be_ref, nu_ref, x_ref, w1_ref, w3_ref, w2_ref, y_ref, w1b, w3b, w2b, *, block):
    i = pl.program_id(0)

    @pl.when(i < nu_ref[0])
    def _():
        @pl.when(jnp.logical_or(i == 0, be_ref[i] != be_ref[jnp.maximum(i - 1, 0)]))
        def _():
            w1b[...] = w1_ref[...].astype(BF16)
            w3b[...] = w3_ref[...].astype(BF16)
            w2b[...] = w2_ref[...].astype(BF16)

        x = _load_rows8(x_ref, block).astype(BF16)
        h1 = jnp.dot(x, w1b[...], preferred_element_type=F32)
        h3 = jnp.dot(x, w3b[...], preferred_element_type=F32)
        a = (h1 * jax.nn.sigmoid(h1) * h3).astype(BF16)
        _store_rows8(y_ref, jnp.dot(a, w2b[...], preferred_element_type=F32), block)


def _experts(xs, blk_e, n_used, nb, w1, w3, w2, layer, block):
    last = lambda i, nu: jnp.minimum(i, nu[0] - 1)
    rows = pl.BlockSpec((block * ROW_TILES, LANES), lambda i, be, nu: (last(i, nu), 0))
    wspec = lambda r, c: pl.BlockSpec((None, None, r, c), lambda i, be, nu: (layer, be[last(i, nu)], 0, 0))
    return pl.pallas_call(
        functools.partial(_experts_kernel, block=block),
        grid_spec=pltpu.PrefetchScalarGridSpec(
            num_scalar_prefetch=2,
            grid=(nb,),
            in_specs=[rows, wspec(D_MODEL, D_EXPERT), wspec(D_MODEL, D_EXPERT), wspec(D_EXPERT, D_MODEL)],
            out_specs=rows,
            scratch_shapes=[pltpu.VMEM((D_MODEL, D_EXPERT), BF16), pltpu.VMEM((D_MODEL, D_EXPERT), BF16),
                            pltpu.VMEM((D_EXPERT, D_MODEL), BF16)]),
        out_shape=jax.ShapeDtypeStruct(xs.shape, F32),
        input_output_aliases={2: 0},
        compiler_params=_cparams(("arbitrary",)),
        name="moe_experts",
    )(blk_e, n_used, xs, w1, w3, w2)


def _combine_kernel(slot_ref, y_ref, x_ref, mod_ref, wt_ref, o_ref, ybuf, sem, *, tc):
    def issue(j, c):
        for u in range(DMA_UNROLL):
            t = j * DMA_UNROLL + u
            for k in range(TOP_K):
                _row_copy(y_ref, slot_ref[0, 0, TOP_K * t + k], ybuf, k * tc + t, sem).start(priority=k)
        return c
    lax.fori_loop(0, tc // DMA_UNROLL, issue, 0)
    pltpu.make_async_copy(y_ref.at[pl.ds(0, TOP_K * tc * ROW_TILES), :], ybuf, sem).wait()

    wt = wt_ref[...]
    y = (wt[:, 0:1] * _load_rows8(ybuf, tc) + wt[:, 1:2] * _load_rows8(ybuf, tc, base=tc * ROW_TILES))
    o_ref[...] = x_ref[...] + mod_ref[0][:, 5 * D_MODEL:6 * D_MODEL] * y


def _combine(slots, y, x, mod, tiles_per_mod, wts, tc):
    T = x.shape[0]
    return pl.pallas_call(
        functools.partial(_combine_kernel, tc=tc),
        grid=(T // tc,),
        in_specs=[pl.BlockSpec((1, 1, tc * TOP_K), lambda i: (i, 0, 0), memory_space=pltpu.SMEM),
                  pl.BlockSpec(memory_space=pl.ANY),
                  pl.BlockSpec((tc, D_MODEL), lambda i: (i, 0)),
                  _mod_spec(mod, tiles_per_mod),
                  pl.BlockSpec((tc, TOP_K), lambda i: (i, 0))],
        out_specs=pl.BlockSpec((tc, D_MODEL), lambda i: (i, 0)),
        out_shape=jax.ShapeDtypeStruct((T, D_MODEL), F32),
        scratch_shapes=[pltpu.VMEM((TOP_K * tc * ROW_TILES, LANES), F32), pltpu.SemaphoreType.DMA(())],
        compiler_params=_cparams(("arbitrary",)),
        name="moe_combine",
    )(slots.reshape(T // tc, 1, tc * TOP_K), y, x, mod, wts)


def _moe(parts, counts, w1, w3, w2, layer):
    n_tokens = sum(p[2].shape[0] for p in parts)
    slots, wts, blk_e, n_used, nb = _layout(jnp.concatenate([p[1] for p in parts], axis=0), counts, MOE_BLOCK,
                                            n_tokens)
    xs = jnp.zeros((nb * MOE_BLOCK * ROW_TILES, LANES), F32)
    off = 0
    for h, _, x, _, _, tt in parts:
        T = x.shape[0]
        xs = _scatter(slots[off:off + T], h, xs, tt)
        off += T
    y = _experts(xs, blk_e, n_used, nb, w1, w3, w2, layer, MOE_BLOCK)
    outs, off = [], 0
    for _, _, x, mod, tpm, tt in parts:
        T = x.shape[0]
        outs.append(_combine(slots[off:off + T], y, x, mod, tpm, wts[off:off + T], tt))
        off += T
    return outs


def _kvq_kernel(x_ref, modkv_ref, mod_ref, nkv_ref, nm_ref, wkv_ref, wq_ref, kn_ref, qn_ref, eavg_ref,
                *rest, tm, keep):
    residue = keep is not None
    x = x_ref[...]
    xn = x * lax.rsqrt(jnp.mean(x * x, axis=-1, keepdims=True) + EPS)
    modkv = modkv_ref[0]
    mod = mod_ref[0]
    hk = (xn * nkv_ref[...] * (1.0 + modkv[:, D_MODEL:]) + modkv[:, :D_MODEL]).astype(BF16)
    hq = (xn * nm_ref[...] * (1.0 + mod[:, D_MODEL:2 * D_MODEL]) + mod[:, :D_MODEL]).astype(BF16)
    eavg = eavg_ref[...]

    def head_norm(a, g):
        sq = (a * a).astype(BF16)
        half = ATTN_WIDTH // 2
        ms = jnp.concatenate([jnp.dot(sq[:, :half], eavg, preferred_element_type=F32),
                              jnp.dot(sq[:, half:], eavg, preferred_element_type=F32)], axis=1)
        return a * lax.rsqrt(ms + EPS) * g

    def residue_major(val, ref, dil):
        if dil == 1:
            ref[0, 0] = val.astype(BF16)
            return
        stage = rest[-1]
        for c in range(stage.shape[0]):
            stage[c] = val[:, c * LANES:(c + 1) * LANES]
        for r in range(dil):
            rows = [stage[c, pl.ds(r, tm // dil, stride=dil), :] for c in range(stage.shape[0])]
            ref[0, r] = jnp.concatenate(rows, axis=1).astype(BF16)

    for g, (_, dil) in enumerate(BRANCHES):
        k0 = 2 * g * ATTN_WIDTH
        kvg = jnp.dot(hk, wkv_ref[:, k0:k0 + 2 * ATTN_WIDTH], preferred_element_type=F32)
        kn = head_norm(kvg[:, :ATTN_WIDTH], kn_ref[g:g + 1, :])
        vv = kvg[:, ATTN_WIDTH:]
        q0 = g * ATTN_WIDTH
        qg = jnp.dot(hq, wq_ref[:, q0:q0 + ATTN_WIDTH], preferred_element_type=F32)
        qg = head_norm(qg, qn_ref[g:g + 1, :]) * (HEAD_DIM ** -0.5)
        if residue:
            residue_major(qg, rest[3 * g], dil)
            residue_major(kn, rest[3 * g + 1], dil)
            residue_major(vv, rest[3 * g + 2], dil)
            tiles_per_batch, kept = keep

            @pl.when(pl.program_id(0) % tiles_per_batch >= tiles_per_batch - kept[g])
            def _(kn=kn, vv=vv, kt_ref=rest[3 * N_BRANCH + g]):
                kt_ref[0, :ATTN_WIDTH, :] = kn.T
                kt_ref[0, ATTN_WIDTH:, :] = vv.T
        else:
            rest[0][:, k0:k0 + ATTN_WIDTH] = kn
            rest[0][:, k0 + ATTN_WIDTH:k0 + 2 * ATTN_WIDTH] = vv
            rest[1][:, q0:q0 + ATTN_WIDTH] = qg


def _kvq(x, modkv, mod, tiles_per_mod, tm, nkv, nm, wkv, wq, kn, qn, eavg, batch=None):
    T = x.shape[0]
    tok = lambda w: pl.BlockSpec((tm, w), lambda i: (i, 0))
    kvw = 2 * N_BRANCH * ATTN_WIDTH
    if batch is not None:
        L = T // batch
        tpb = L // tm
        kept = tuple(-(-min(w, L) // tm) for w, _ in BRANCHES)
        keep = (tpb, kept)
        out_specs, out_shape = [], []
        for _, dil in BRANCHES:
            for _ in range(3):
                out_specs.append(pl.BlockSpec((1, dil, tm // dil, ATTN_WIDTH), lambda i: (i // tpb, 0, i % tpb, 0)))
                out_shape.append(jax.ShapeDtypeStruct((batch, dil, L // dil, ATTN_WIDTH), BF16))
        for kg in kept:
            out_specs.append(pl.BlockSpec((1, 2 * ATTN_WIDTH, tm),
                                          lambda i, kg=kg: (i // tpb, 0, jnp.maximum(i % tpb - (tpb - kg), 0))))
            out_shape.append(jax.ShapeDtypeStruct((batch, 2 * ATTN_WIDTH, kg * tm), F32))
        scratch = [pltpu.VMEM((ATTN_WIDTH // LANES, tm, LANES), F32)]
    else:
        keep = None
        out_specs = [tok(kvw), tok(N_BRANCH * ATTN_WIDTH)]
        out_shape = [jax.ShapeDtypeStruct((T, kvw), F32), jax.ShapeDtypeStruct((T, N_BRANCH * ATTN_WIDTH), F32)]
        scratch = []
    return pl.pallas_call(
        functools.partial(_kvq_kernel, tm=tm, keep=keep),
        grid=(T // tm,),
        in_specs=[tok(D_MODEL), _mod_spec(modkv, tiles_per_mod), _mod_spec(mod, tiles_per_mod),
                  _resident(nkv.shape), _resident(nm.shape), _resident(wkv.shape), _resident(wq.shape),
                  _resident(kn.shape), _resident(qn.shape), _resident(eavg.shape)],
        out_specs=out_specs,
        out_shape=out_shape,
        scratch_shapes=scratch,
        compiler_params=_cparams(("arbitrary",)),
        name="kvq",
    )(x, modkv, mod, nkv, nm, wkv, wq, kn, qn, eavg)


def _band_attn_kernel(q_ref, kp_ref, kc_ref, vp_ref, vc_ref, bias_ref, o_ref, lse_ref):
    n = pl.program_id(2)
    q = q_ref[...]
    k2 = jnp.concatenate([kp_ref[...], kc_ref[...]], axis=0)
    v2 = jnp.concatenate([vp_ref[...], vc_ref[...]], axis=0)
    col = lax.broadcasted_iota(jnp.int32, (N_STEPS, 2 * N_STEPS), 1)
    visible = jnp.logical_or(col >= N_STEPS, n > 0)
    lane = lax.broadcasted_iota(jnp.int32, (N_STEPS, LANES), 1)
    lse_all = jnp.zeros((N_STEPS, LANES), F32)
    for h in range(HEADS):
        sl = slice(h * HEAD_DIM, (h + 1) * HEAD_DIM)
        s = lax.dot_general(q[:, sl], k2[:, sl], (((1,), (1,)), ((), ())), preferred_element_type=F32)
        s = jnp.where(visible, s + bias_ref[h], NEG)
        m = jnp.max(s, axis=1, keepdims=True)
        p = jnp.exp(s - m)
        l = jnp.sum(p, axis=1, keepdims=True)
        o_ref[:, sl] = jnp.dot(p.astype(BF16), v2[:, sl], preferred_element_type=F32) / l
        lse_all = jnp.where(lane == h, m + jnp.log(l), lse_all)
    lse_ref[...] = lse_all


def _band_attn(q, k, v, bias, dil):
    B, _, Ld, _ = q.shape
    blk = lambda w, f: pl.BlockSpec((None, None, N_STEPS, w), f)
    cur = lambda b, r, n: (b, r, n, 0)
    prev = lambda b, r, n: (b, r, jnp.maximum(n - 1, 0), 0)
    return pl.pallas_call(
        _band_attn_kernel,
        grid=(B, dil, Ld // N_STEPS),
        in_specs=[blk(ATTN_WIDTH, cur), blk(ATTN_WIDTH, prev), blk(ATTN_WIDTH, cur), blk(ATTN_WIDTH, prev),
                  blk(ATTN_WIDTH, cur), _resident(bias.shape)],
        out_specs=[blk(ATTN_WIDTH, cur), blk(LANES, cur)],
        out_shape=[jax.ShapeDtypeStruct((B, dil, Ld, ATTN_WIDTH), F32),
                   jax.ShapeDtypeStruct((B, dil, Ld, LANES), F32)],
        compiler_params=_cparams(("arbitrary", "arbitrary", "arbitrary")),
        name=f"band_attn_d{dil}",
    )(q, k, k, v, v, bias)


def _rel_buckets(dilation):
    n = np.arange(N_STEPS + 1) * dilation
    large = MAX_EXACT + (np.log(np.maximum(n, 1) / MAX_EXACT) / np.log(REL_MAX_DIST / MAX_EXACT)
                         * (NUM_BUCKETS - MAX_EXACT)).astype(np.int32)
    return np.where(n < MAX_EXACT, n, np.minimum(large, NUM_BUCKETS - 1)).astype(np.int32)


def _step_bias_row(rel_bias, g, dil):
    onehot = np.zeros((N_STEPS + 1, NUM_BUCKETS), np.float32)
    onehot[np.arange(N_STEPS + 1), _rel_buckets(dil)] = 1.0
    return jnp.dot(jnp.asarray(onehot), rel_bias[:, g, :].astype(F32), precision=lax.Precision.HIGHEST).T


def _band_bias(rel_bias, g, dil):
    bias = _step_bias_row(rel_bias, g, dil)
    P = 3 * N_STEPS
    neg = jnp.full((HEADS, N_STEPS), NEG, F32)
    ext = jnp.concatenate([neg, bias[:, ::-1], neg], axis=1)
    flat = jnp.broadcast_to(ext[:, None, :], (HEADS, N_STEPS, P + 1)).reshape(HEADS, N_STEPS * (P + 1))
    skew = flat[:, :N_STEPS * P].reshape(HEADS, N_STEPS, P)
    return skew[:, :, N_STEPS:]


def _step_attn_kernel(q_ref, kvn_ref, c0_ref, c1_ref, c2_ref, b0_ref, b1_ref, b2_ref, bn_ref, o_ref):
    caches = (c0_ref, c1_ref, c2_ref)
    cbias = (b0_ref, b1_ref, b2_ref)
    nt = (((1,), (1,)), ((), ()))
    for h in range(HEADS):
        parts = []
        for g in range(N_BRANCH):
            lo = g * ATTN_WIDTH + h * HEAD_DIM
            klo = 2 * g * ATTN_WIDTH + h * HEAD_DIM
            q = q_ref[0, :, lo:lo + HEAD_DIM].astype(BF16)
            kn = kvn_ref[0, :, klo:klo + HEAD_DIM].astype(BF16)
            vn = kvn_ref[0, :, klo + ATTN_WIDTH:klo + ATTN_WIDTH + HEAD_DIM].astype(BF16)
            kt = caches[g][0, 0, h].astype(BF16)
            vt = caches[g][0, 1, h].astype(BF16)
            sc = jnp.dot(q, kt, preferred_element_type=F32) + cbias[g][h]
            sn = lax.dot_general(q, kn, nt, preferred_element_type=F32) + bn_ref[g, h]
            m = jnp.maximum(jnp.max(sc, axis=1, keepdims=True), jnp.max(sn, axis=1, keepdims=True))
            pc = jnp.exp(sc - m)
            pn = jnp.exp(sn - m)
            l = jnp.sum(pc, axis=1, keepdims=True) + jnp.sum(pn, axis=1, keepdims=True)
            pv = (lax.dot_general(pc.astype(BF16), vt, nt, preferred_element_type=F32)
                  + jnp.dot(pn.astype(BF16), vn, preferred_element_type=F32))
            parts.append((m + jnp.log(l), pv / l))
        mx = jnp.maximum(jnp.maximum(parts[0][0], parts[1][0]), parts[2][0])
        es = [jnp.exp(lse - mx) for lse, _ in parts]
        den = es[0] + es[1] + es[2]
        o_ref[0, :, h * HEAD_DIM:(h + 1) * HEAD_DIM] = (
            es[0] * parts[0][1] + es[1] * parts[1][1] + es[2] * parts[2][1]) / den


def _step_bias(rel_bias, n_new):
    cache_tabs, new_tabs = [], []
    for g, (win, dil) in enumerate(BRANCHES):
        bias = _step_bias_row(rel_bias, g, dil)
        rev = bias[:, ::-1]
        per_s = []
        for s in range(n_new):
            if dil == 1:
                neg = jnp.full((HEADS, s), NEG, F32)
                per_s.append(jnp.concatenate([neg, rev[:, :win - s]], axis=1))
            else:
                cols = [rev[:, :N_STEPS] if r == s % dil else jnp.full((HEADS, N_STEPS), NEG, F32)
                        for r in range(dil)]
                per_s.append(jnp.stack(cols, axis=2).reshape(HEADS, win))
        cache_tabs.append(jnp.stack(per_s, axis=1))
        dist = np.arange(n_new)[:, None] - np.arange(8)[None, :]
        ok = (dist >= 0) & (dist % dil == 0) & (np.arange(8)[None, :] < n_new)
        onehot = np.zeros((N_STEPS + 1, n_new * 8), np.float32)
        onehot[np.where(ok, dist // dil, 0).reshape(-1), np.arange(n_new * 8)] = 1.0
        tab = jnp.dot(bias, jnp.asarray(onehot), precision=lax.Precision.HIGHEST).reshape(HEADS, n_new, 8)
        new_tabs.append(jnp.where(ok[None], tab, NEG))
    return cache_tabs, jnp.stack(new_tabs)


def _step_attn(q, kvn, caches, cache_bias, new_bias, DB, S):
    cspec = lambda c: pl.BlockSpec((1,) + c.shape[1:], lambda b: (b, 0, 0, 0, 0))
    return pl.pallas_call(
        _step_attn_kernel,
        grid=(DB,),
        in_specs=[pl.BlockSpec((1, S, N_BRANCH * ATTN_WIDTH), lambda b: (b, 0, 0)),
                  pl.BlockSpec((1, 8, 2 * N_BRANCH * ATTN_WIDTH), lambda b: (b, 0, 0)),
                  cspec(caches[0]), cspec(caches[1]), cspec(caches[2]),
                  _resident(cache_bias[0].shape), _resident(cache_bias[1].shape), _resident(cache_bias[2].shape),
                  _resident(new_bias.shape)],
        out_specs=pl.BlockSpec((1, S, ATTN_WIDTH), lambda b: (b, 0, 0)),
        out_shape=jax.ShapeDtypeStruct((DB, S, ATTN_WIDTH), F32),
        compiler_params=_cparams(("arbitrary",)),
        name="step_attn",
    )(q, kvn, caches[0], caches[1], caches[2], *cache_bias, new_bias)


def _attn_out_kernel(*refs, dils, tm):
    n_o = max(len(dils), 1)
    n_in = n_o + len(dils) + 9
    (x_ref, mod_ref, wo_ref, ex_ref, nf_ref, wr_ref, br_ref, cin_ref, tri_ref,
     x3_ref, h3_ref, rt_ref, cnt_ref, carry) = refs[n_in - 9:n_in + 5]
    stages = refs[n_in + 5:]

    def natural(ref, stage, dil):
        if dil == 1:
            return ref[0, 0]
        for r in range(dil):
            val = ref[0, r]
            for c in range(stage.shape[0]):
                stage[c, pl.ds(r, tm // dil, stride=dil), :] = val[:, c * LANES:(c + 1) * LANES]
        return jnp.concatenate([stage[c] for c in range(stage.shape[0])], axis=1)

    if dils:
        lses = [natural(refs[n_o + g], stages[2 * g + 1], dil) for g, dil in enumerate(dils)]
        mx = functools.reduce(jnp.maximum, lses)
        es = [jnp.exp(l - mx) for l in lses]
        den = functools.reduce(lambda a, b: a + b, es)
        o = None
        for g, dil in enumerate(dils):
            term = _split_dot(es[g] / den, ex_ref[...]) * natural(refs[g], stages[2 * g], dil)
            o = term if o is None else o + term
    else:
        o = refs[0][...]
    mod = mod_ref[0]
    a = jnp.dot(o.astype(BF16), wo_ref[...], preferred_element_type=F32)
    x3 = x_ref[...] + mod[:, 2 * D_MODEL:3 * D_MODEL] * a
    x3_ref[...] = x3
    _ffn_pre(x3, mod, nf_ref, wr_ref, br_ref, cin_ref, tri_ref, h3_ref, rt_ref, cnt_ref, carry, tm)


def _attn_out(os_, lses, x, mod, tiles_per_mod, tm, wo, ex, nf, wr, br, counts):
    T = x.shape[0]
    tok = lambda w: pl.BlockSpec((tm, w), lambda i: (i, 0))
    r_specs, r_shapes = _route_outs(T, tm)
    tri = _earlier_rows(tm)
    if lses:
        dils = tuple(o.shape[1] for o in os_)
        tpb = tiles_per_mod
        res = lambda a: pl.BlockSpec((1, a.shape[1], tm // a.shape[1], a.shape[3]), lambda i: (i // tpb, 0, i % tpb, 0))
        o_specs = [res(a) for a in os_] + [res(a) for a in lses]
        scratch = []
        for _ in dils:
            scratch += [pltpu.VMEM((ATTN_WIDTH // LANES, tm, LANES), F32), pltpu.VMEM((1, tm, LANES), F32)]
    else:
        dils, o_specs, scratch = (), [tok(ATTN_WIDTH)], []
    return pl.pallas_call(
        functools.partial(_attn_out_kernel, dils=dils, tm=tm),
        grid=(T // tm,),
        in_specs=(o_specs + [tok(D_MODEL), _mod_spec(mod, tiles_per_mod), _resident(wo.shape), _resident(ex.shape),
                             _resident(nf.shape), _resident(wr.shape), _resident(br.shape),
                             _resident(counts.shape), _resident(tri.shape)]),
        out_specs=[tok(D_MODEL), pl.BlockSpec((tm * ROW_TILES, LANES), lambda i: (i, 0))] + r_specs,
        out_shape=[jax.ShapeDtypeStruct((T, D_MODEL), F32),
                   jax.ShapeDtypeStruct((T * ROW_TILES, LANES), F32)] + r_shapes,
        scratch_shapes=[pltpu.VMEM((1, LANES), F32)] + scratch,
        compiler_params=_cparams(("arbitrary",)),
        name="attn_out",
    )(*os_, *lses, x, mod, wo, ex, nf, wr, br, counts, tri)


def kernel(x_prompt, x_sample, cache_kv_w128, cache_kv_w512, cache_kv_w2048, c_prompt, c_sample, ada_w, ada_b, norm_mix, norm_ffn, a_w_in, a_b_in, a_norm_v, a_w_s, a_b_s, a_w_out, kv_ada_w, kv_ada_b, kv_norm, w_kv, k_norm, rel_bias, b_w_q, q_norm, b_w_o, r_w_group, r_b_group, r_w_expert, r_b_expert, e_w1, e_w3, e_w2):
    B, L, _ = x_prompt.shape
    DB, S, _ = x_sample.shape
    Tp, Ts = B * L, DB * S
    tm = TOKEN_TILE
    tpm_p = L // tm

    c_all = jnp.concatenate([c_prompt, c_sample], axis=0)
    R = c_all.shape[0]
    c_all = jnp.pad(c_all, ((0, -R % 8), (0, 0)))
    mods = [_ada(c_all, ada_w, ada_b, l) for l in range(2)]
    modkv = _ada(c_all, kv_ada_w[None], kv_ada_b[None], 0)

    def split_mod(m):
        return m[:B, None, :], jnp.repeat(m[B:B + DB], S, axis=0)[None]
    mod_p, mod_s = zip(*[split_mod(m) for m in mods])
    modkv_p, modkv_s = split_mod(modkv)

    row = lambda a: a.reshape(1, -1)

    def router(l):
        wr = jnp.zeros((D_MODEL, LANES), F32)
        wr = wr.at[:, :N_GROUPS].set(r_w_group[l]).at[:, N_GROUPS:N_GROUPS + N_EXPERTS].set(r_w_expert[l])
        br = jnp.zeros((1, LANES), F32)
        br = br.at[0, :N_GROUPS].set(r_b_group[l]).at[0, N_GROUPS:N_GROUPS + N_EXPERTS].set(r_b_expert[l])
        return wr.astype(BF16), br

    no_counts = jnp.zeros((1, LANES), F32)

    win = a_w_in[0].astype(BF16)
    wout = a_w_out[0].astype(BF16)
    tril = jnp.tril(jnp.ones((CHUNK, CHUNK), bool))
    ws_p = jnp.where(tril, a_w_s[0], 0).astype(BF16)
    bs_p = a_b_s[0].T
    cs = min(CHUNK, S)
    ws_small = jnp.where(jnp.tril(jnp.ones((cs, cs), bool)), a_w_s[0][:, :cs, :cs], 0)
    ws_s = jnp.stack([jnp.kron(jnp.eye(Ts // cs, dtype=F32), ws_small[g]) for g in range(GMLP_GROUPS)]).astype(BF16)
    bs_s = jnp.tile(a_b_s[0][:, :cs], (1, Ts // cs)).T
    wr0, br0 = router(0)
    common = (row(norm_mix[0]), win, row(a_b_in[0]), row(a_norm_v[0]))
    xp = x_prompt.reshape(Tp, D_MODEL)
    xs_ = x_sample.reshape(Ts, D_MODEL)
    x1_p, h2_p, rt_p, cnt = _mixer_a(xp, mod_p[0], tpm_p, tm, *common, ws_p, bs_p, wout, row(norm_ffn[0]), wr0, br0,
                                     no_counts, with_v=False)
    x1_s, h2_s, rt_s, cnt, v_s = _mixer_a(xs_, mod_s[0], 1, Ts, *common, ws_s, bs_s, wout, row(norm_ffn[0]), wr0, br0,
                                          cnt, with_v=True)
    x2_p, x2_s = _moe([(h2_p, rt_p, x1_p, mod_p[0], tpm_p, tm), (h2_s, rt_s, x1_s, mod_s[0], 1, Ts)], cnt,
                      e_w1, e_w3, e_w2, 0)

    wkv = w_kv.astype(BF16)
    wq = b_w_q[0].astype(BF16)
    kn = jnp.tile(k_norm, (1, HEADS))
    qn = jnp.tile(q_norm[0], (1, HEADS))
    head = np.arange(ATTN_WIDTH) // HEAD_DIM
    half_head = head[:ATTN_WIDTH // 2]
    eavg = jnp.asarray((half_head[:, None] == half_head[None, :]) / HEAD_DIM, BF16)
    kvq_w = (row(kv_norm), row(norm_mix[1]), wkv, wq, kn, qn, eavg)
    *qkv_p, kt0, kt1, kt2 = _kvq(x2_p, modkv_p, mod_p[1], tpm_p, tm, *kvq_w, batch=B)
    kv_s, q_s = _kvq(x2_s, modkv_s, mod_s[1], 1, Ts, *kvq_w)

    os_, lses = [], []
    for g, (_, dil) in enumerate(BRANCHES):
        o, lse = _band_attn(*qkv_p[3 * g:3 * g + 3], _band_bias(rel_bias, g, dil), dil)
        os_.append(o)
        lses.append(lse)
    caches = [jnp.transpose(c, (0, 2, 3, 4, 1)) for c in (cache_kv_w128, cache_kv_w512, cache_kv_w2048)]
    kvn_s = jnp.pad(kv_s.reshape(DB, S, -1), ((0, 0), (0, 8 - S), (0, 0)))
    o_s = _step_attn(q_s.reshape(DB, S, -1), kvn_s, caches, *_step_bias(rel_bias, S), DB, S)

    wo = b_w_o[0].astype(BF16)
    ex = jnp.asarray(np.arange(LANES)[:, None] == head[None, :], BF16)
    wr1, br1 = router(1)
    x3_p, h3_p, rt1_p, cnt1 = _attn_out(os_, lses, x2_p, mod_p[1], tpm_p, tm, wo, ex, row(norm_ffn[1]), wr1, br1,
                                        no_counts)
    x3_s, h3_s, rt1_s, cnt1 = _attn_out([o_s.reshape(Ts, ATTN_WIDTH)], [], x2_s, mod_s[1], 1, Ts, wo, ex,
                                        row(norm_ffn[1]), wr1, br1, cnt1)
    y_p, y_s = _moe([(h3_p, rt1_p, x3_p, mod_p[1], tpm_p, tm), (h3_s, rt1_s, x3_s, mod_s[1], 1, Ts)], cnt1,
                    e_w1, e_w3, e_w2, 1)

    kv_s4 = kv_s.reshape(DB, S, N_BRANCH, 2, HEADS, HEAD_DIM)

    def window(kt, w):
        n = min(w, L)
        return jnp.transpose(kt[:, :, kt.shape[2] - n:].reshape(B, 2, HEADS, HEAD_DIM, n), (0, 4, 1, 2, 3))
    return (y_p.reshape(B, L, D_MODEL), y_s.reshape(DB, S, D_MODEL),
            window(kt0, BRANCHES[0][0]), window(kt1, BRANCHES[1][0]), window(kt2, BRANCHES[2][0]),
            kv_s4[:, :, 0], kv_s4[:, :, 1], kv_s4[:, :, 2],
            v_s.reshape(1, DB, S, GMLP_WIDTH))
```

```python
import functools

import numpy as np
import jax
import jax.numpy as jnp
from jax import lax
from jax.experimental import pallas as pl
from jax.experimental.pallas import tpu as pltpu

F32 = jnp.float32
BF16 = jnp.bfloat16

D_MODEL = 1024
GMLP_WIDTH = 2048
GMLP_GROUPS = 4
GROUP_WIDTH = GMLP_WIDTH // GMLP_GROUPS
CHUNK = 128
BRANCHES = ((128, 1), (512, 4), (2048, 16))
N_BRANCH = 3
N_STEPS = 128
HEADS = 8
HEAD_DIM = 64
ATTN_WIDTH = HEADS * HEAD_DIM
NUM_BUCKETS = 32
MAX_EXACT = NUM_BUCKETS // 2
REL_MAX_DIST = 2048
N_GROUPS = 4
EXPERTS_PER_GROUP = 8
N_EXPERTS = N_GROUPS * EXPERTS_PER_GROUP
TOP_K = 2
D_EXPERT = 512
EPS = 1e-6
NEG = -1e30

LANES = 128
ROW_TILES = D_MODEL // LANES
TOKEN_TILE = 256
MOE_BLOCK = 256
DMA_UNROLL = 8
VMEM_LIMIT = 52 * 1024 * 1024


def _cparams(sem):
    return pltpu.CompilerParams(dimension_semantics=sem, vmem_limit_bytes=VMEM_LIMIT)


def _resident(shape):
    nd = len(shape)
    return pl.BlockSpec(shape, lambda *_, _nd=nd: (0,) * _nd, pipeline_mode=pl.Buffered(1))


def _gelu_tanh(x):
    return 0.5 * x * (1.0 + jnp.tanh(0.7978845608028654 * (x + 0.044715 * (x * x * x))))


def _rms(x, g):
    return x * lax.rsqrt(jnp.mean(x * x, axis=-1, keepdims=True) + EPS) * g


def _store_rows8(ref, val, n, base=0):
    for s in range(ROW_TILES):
        ref[pl.ds(base + s, n, stride=ROW_TILES), :] = val[:, s * LANES:(s + 1) * LANES]


def _load_rows8(ref, n, base=0):
    return jnp.concatenate([ref[pl.ds(base + s, n, stride=ROW_TILES), :] for s in range(ROW_TILES)], axis=1)


def _split(a):
    hi = a.astype(BF16)
    return hi, (a - hi.astype(F32)).astype(BF16)


def _split_dot(a, e_bf16):
    hi, lo = _split(a)
    return (jnp.dot(hi, e_bf16, preferred_element_type=F32) + jnp.dot(lo, e_bf16, preferred_element_type=F32))


ROUTE_LANE0 = 4


def _route_rows(l, tri_ref, carry):
    lane = lax.broadcasted_iota(jnp.int32, l.shape, 1).astype(F32)
    far = float(LANES)

    def first_lane(mask):
        return jnp.min(jnp.where(mask, lane, far), axis=1, keepdims=True)

    is_g = lane < N_GROUPS
    gl = jnp.where(is_g, l, NEG)
    gmax = jnp.max(gl, axis=1, keepdims=True)
    g_i = first_lane(jnp.logical_and(gl == gmax, is_g))
    g_p = 1.0 / jnp.sum(jnp.where(is_g, jnp.exp(gl - gmax), 0.0), axis=1, keepdims=True)
    lo = ROUTE_LANE0 + EXPERTS_PER_GROUP * g_i
    sel = jnp.logical_and(lane >= lo, lane < lo + EXPERTS_PER_GROUP)
    el = jnp.where(sel, l, NEG)
    m1 = jnp.max(el, axis=1, keepdims=True)
    i1 = first_lane(jnp.logical_and(el == m1, sel))
    sel2 = jnp.logical_and(sel, lane != i1)
    el2 = jnp.where(sel2, l, NEG)
    m2 = jnp.max(el2, axis=1, keepdims=True)
    i2 = first_lane(jnp.logical_and(el2 == m2, sel2))
    r = jnp.exp(m2 - m1)
    w1 = g_p / (1.0 + r)
    w2 = g_p * r / (1.0 + r)

    hit1 = lane == i1
    hit2 = lane == i2
    onehot = jnp.where(jnp.logical_or(hit1, hit2), 1.0, 0.0)
    before = carry[...] + jnp.dot(tri_ref[...], onehot.astype(BF16), preferred_element_type=F32)
    rank1 = jnp.sum(jnp.where(hit1, before, 0.0), axis=1, keepdims=True)
    rank2 = jnp.sum(jnp.where(hit2, before, 0.0), axis=1, keepdims=True)
    carry[...] = carry[...] + jnp.sum(onehot, axis=0, keepdims=True)

    out = jnp.zeros(l.shape, F32)
    for k, val in enumerate((i1 - ROUTE_LANE0, i2 - ROUTE_LANE0, w1, w2, rank1, rank2)):
        out = jnp.where(lane == k, val, out)
    return out


def _ffn_pre(x, mod, nf_ref, wr_ref, br_ref, cin_ref, tri_ref, h_ref, rt_ref, cnt_ref, carry, n):
    @pl.when(pl.program_id(0) == 0)
    def _():
        carry[...] = cin_ref[...]

    h = _rms(x, nf_ref[...]) * (1.0 + mod[:, 4 * D_MODEL:5 * D_MODEL]) + mod[:, 3 * D_MODEL:4 * D_MODEL]
    _store_rows8(h_ref, h, n)
    logits = jnp.dot(h.astype(BF16), wr_ref[...], preferred_element_type=F32) + br_ref[...]
    rt_ref[...] = _route_rows(logits, tri_ref, carry)
    cnt_ref[...] = carry[...]


def _ada_kernel(c_ref, w_ref, b_ref, o_ref):
    c = c_ref[...]
    a = (c * jax.nn.sigmoid(c)).astype(BF16)
    o_ref[...] = jnp.dot(a, w_ref[...].astype(BF16), preferred_element_type=F32) + b_ref[...]


def _ada(c, w, b, layer):
    R = c.shape[0]
    N = w.shape[2]
    tn = 1024
    return pl.pallas_call(
        _ada_kernel,
        grid=(N // tn,),
        in_specs=[pl.BlockSpec((R, D_MODEL), lambda j: (0, 0)),
                  pl.BlockSpec((None, D_MODEL, tn), lambda j: (layer, 0, j)),
                  pl.BlockSpec((None, 1, tn), lambda j: (layer, 0, j))],
        out_specs=pl.BlockSpec((R, tn), lambda j: (0, j)),
        out_shape=jax.ShapeDtypeStruct((R, N), F32),
        compiler_params=_cparams(("arbitrary",)),
        name="ada",
    )(c, w, b.reshape(b.shape[0], 1, N))


def _mixer_a_kernel(x_ref, mod_ref, nm_ref, win_ref, bin_ref, gv_ref, ws_ref, bs_ref, wout_ref,
                    nf_ref, wr_ref, br_ref, cin_ref, tri_ref, hbuf_ref, x1_ref, h2_ref, rt_ref, cnt_ref, *rest, tm):
    del hbuf_ref
    v_refs, carry = rest[:-1], rest[-1]
    x = x_ref[...]
    mod = mod_ref[0]
    h = (_rms(x, nm_ref[...]) * (1.0 + mod[:, D_MODEL:2 * D_MODEL]) + mod[:, 0:D_MODEL]).astype(BF16)
    zv = jnp.dot(h, win_ref[:, GMLP_WIDTH:], preferred_element_type=F32) + bin_ref[:, GMLP_WIDTH:]
    v = _rms(_gelu_tanh(zv), gv_ref[...])
    if v_refs:
        v_refs[0][...] = v
    vb = v.astype(BF16)
    bs = bs_ref[...]
    acc = jnp.zeros((tm, D_MODEL), F32)
    for g in range(GMLP_GROUPS):
        lo, hi = g * GROUP_WIDTH, (g + 1) * GROUP_WIDTH
        u = _gelu_tanh(jnp.dot(h, win_ref[:, lo:hi], preferred_element_type=F32) + bin_ref[:, lo:hi])
        wg = ws_ref[g]
        gate = jnp.concatenate(
            [jnp.dot(wg, vb[c * CHUNK:(c + 1) * CHUNK, lo:hi], preferred_element_type=F32) + bs[:, g:g + 1]
             for c in range(tm // CHUNK)], axis=0)
        acc = acc + jnp.dot((u * gate).astype(BF16), wout_ref[lo:hi, :], preferred_element_type=F32)
    x1 = x + mod[:, 2 * D_MODEL:3 * D_MODEL] * acc
    x1_ref[...] = x1
    _ffn_pre(x1, mod, nf_ref, wr_ref, br_ref, cin_ref, tri_ref, h2_ref, rt_ref, cnt_ref, carry, tm)


def _mod_spec(mod, tiles_per_mod):
    _, rows, width = mod.shape
    return pl.BlockSpec((1, rows, width), lambda i: (i // tiles_per_mod, 0, 0))


def _earlier_rows(tm):
    return jnp.asarray(np.tril(np.ones((tm, tm), np.float32), -1), BF16)


def _route_outs(T, tm):
    return ([pl.BlockSpec((tm, LANES), lambda i: (i, 0)), pl.BlockSpec((1, LANES), lambda i: (0, 0))],
            [jax.ShapeDtypeStruct((T, LANES), F32), jax.ShapeDtypeStruct((1, LANES), F32)])


def _hbuf_specs(hbuf, tm, tok0):
    off = tok0 // tm
    return (pl.BlockSpec(memory_space=pl.ANY), pl.BlockSpec((tm * ROW_TILES, LANES), lambda i: (i + off, 0)),
            jax.ShapeDtypeStruct(hbuf.shape, hbuf.dtype))


def _mixer_a(x, mod, tiles_per_mod, tm, nm, win, bin_, gv, ws, bs_t, wout, nf, wr, br, counts, hbuf, tok0, with_v):
    T = x.shape[0]
    tok = lambda w: pl.BlockSpec((tm, w), lambda i: (i, 0))
    r_specs, r_shapes = _route_outs(T, tm)
    h_in, h_out, h_shape = _hbuf_specs(hbuf, tm, tok0)
    out_shape = [jax.ShapeDtypeStruct((T, D_MODEL), F32), h_shape] + r_shapes
    out_specs = [tok(D_MODEL), h_out] + r_specs
    if with_v:
        out_shape.append(jax.ShapeDtypeStruct((T, GMLP_WIDTH), F32))
        out_specs.append(tok(GMLP_WIDTH))
    tri = _earlier_rows(tm)
    return pl.pallas_call(
        functools.partial(_mixer_a_kernel, tm=tm),
        grid=(T // tm,),
        in_specs=[tok(D_MODEL), _mod_spec(mod, tiles_per_mod), _resident(nm.shape), _resident(win.shape),
                  _resident(bin_.shape), _resident(gv.shape), _resident(ws.shape), _resident(bs_t.shape),
                  _resident(wout.shape), _resident(nf.shape), _resident(wr.shape), _resident(br.shape),
                  _resident(counts.shape), _resident(tri.shape), h_in],
        out_specs=out_specs,
        out_shape=out_shape,
        scratch_shapes=[pltpu.VMEM((1, LANES), F32)],
        input_output_aliases={14: 1},
        compiler_params=_cparams(("arbitrary",)),
        name="mixer_a",
    )(x, mod, nm, win, bin_, gv, ws, bs_t, wout, nf, wr, br, counts, tri, hbuf)


def _layout(route, counts, block, n_tokens):
    counts = counts[0, ROUTE_LANE0:ROUTE_LANE0 + N_EXPERTS].astype(jnp.int32)
    padded = (counts + block - 1) // block * block
    pad_end = jnp.cumsum(padded)
    pad_start = (pad_end - padded).astype(F32)
    experts = route[:, 0:TOP_K]
    onehot = experts[:, :, None] == jnp.arange(N_EXPERTS, dtype=F32)[None, None, :]
    slots = (route[:, 4:4 + TOP_K] + jnp.sum(jnp.where(onehot, pad_start[None, None, :], 0.0), axis=-1)).astype(jnp.int32)
    nb = -(-n_tokens * TOP_K // block) + N_EXPERTS
    blk_e = jnp.minimum(jnp.sum(pad_end[None, :] <= (jnp.arange(nb, dtype=jnp.int32) * block)[:, None], axis=1),
                        N_EXPERTS - 1).astype(jnp.int32)
    n_used = (pad_end[-1] // block).astype(jnp.int32).reshape(1)
    return slots, route[:, 2:2 + TOP_K], blk_e, n_used, nb


def _row_copy(src_ref, src_row, dst_ref, dst_row, sem):
    return pltpu.make_async_copy(
        src_ref.at[pl.ds(pl.multiple_of(src_row * ROW_TILES, ROW_TILES), ROW_TILES), :],
        dst_ref.at[pl.ds(pl.multiple_of(dst_row * ROW_TILES, ROW_TILES), ROW_TILES), :], sem)


INV_STEP = 256
INV_FILL = 16


def _invmap_kernel(slot_ref, inv_ref, *, n_slots):
    i = pl.program_id(0)

    @pl.when(i == 0)
    def _():
        def fill(j, c):
            for u in range(INV_FILL):
                inv_ref[j * INV_FILL + u] = -1
            return c
        lax.fori_loop(0, n_slots // INV_FILL, fill, 0)

    def place(j, c):
        for u in range(DMA_UNROLL):
            e = j * DMA_UNROLL + u
            inv_ref[slot_ref[0, 0, e]] = i * INV_STEP + e
        return c
    lax.fori_loop(0, INV_STEP // DMA_UNROLL, place, 0)


def _invmap(slots, n_slots):
    P = slots.shape[0]
    return pl.pallas_call(
        functools.partial(_invmap_kernel, n_slots=n_slots),
        grid=(P // INV_STEP,),
        in_specs=[pl.BlockSpec((1, 1, INV_STEP), lambda i: (i, 0, 0), memory_space=pltpu.SMEM)],
        out_specs=pl.BlockSpec(memory_space=pltpu.SMEM),
        out_shape=jax.ShapeDtypeStruct((n_slots,), jnp.int32),
        compiler_params=_cparams(("arbitrary",)),
        name="moe_invmap",
    )(slots.reshape(P // INV_STEP, 1, INV_STEP))


def _experts_kernel(be_ref, nu_ref, inv_ref, h_ref, w1_ref, w3_ref, w2_ref, y_ref, w1b, w3b, w2b, xbuf, sem,
                    *, block):
    i = pl.program_id(0)
    nu = nu_ref[0]

    def gather(blk, buf):
        def issue(j, c):
            for u in range(DMA_UNROLL):
                r = j * DMA_UNROLL + u
                tok = jnp.right_shift(jnp.maximum(inv_ref[blk * block + r], 0), 1)
                _row_copy(h_ref, tok, xbuf.at[buf], r, sem.at[buf]).start(priority=u % 2)
            return c
        lax.fori_loop(0, block // DMA_UNROLL, issue, 0)

    @pl.when(i == 0)
    def _():
        gather(0, 0)

    @pl.when(i + 1 < nu)
    def _():
        gather(i + 1, (i + 1) % 2)

    @pl.when(i < nu)
    def _():
        buf = i % 2
        pltpu.make_async_copy(h_ref.at[pl.ds(0, block * ROW_TILES), :], xbuf.at[buf], sem.at[buf]).wait()

        @pl.when(jnp.logical_or(i == 0, be_ref[i] != be_ref[jnp.maximum(i - 1, 0)]))
        def _():
            w1b[...] = w1_ref[...].astype(BF16)
            w3b[...] = w3_ref[...].astype(BF16)
            w2b[...] = w2_ref[...].astype(BF16)

        x = _load_rows8(xbuf.at[buf], block).astype(BF16)
        h1 = jnp.dot(x, w1b[...], preferred_element_type=F32)
        h3 = jnp.dot(x, w3b[...], preferred_element_type=F32)
        a = (h1 * jax.nn.sigmoid(h1) * h3).astype(BF16)
        _store_rows8(y_ref, jnp.dot(a, w2b[...], preferred_element_type=F32), block)

    @pl.when(i >= nu)
    def _():
        y_ref[...] = jnp.zeros(y_ref.shape, F32)


def _experts(h, inv, blk_e, n_used, nb, w1, w3, w2, layer, block):
    last = lambda i, nu: jnp.minimum(i, nu[0] - 1)
    wspec = lambda r, c: pl.BlockSpec((None, None, r, c), lambda i, be, nu, inv: (layer, be[last(i, nu)], 0, 0))
    return pl.pallas_call(
        functools.partial(_experts_kernel, block=block),
        grid_spec=pltpu.PrefetchScalarGridSpec(
            num_scalar_prefetch=3,
            grid=(nb,),
            in_specs=[pl.BlockSpec(memory_space=pl.ANY),
                      wspec(D_MODEL, D_EXPERT), wspec(D_MODEL, D_EXPERT), wspec(D_EXPERT, D_MODEL)],
            out_specs=pl.BlockSpec((block * ROW_TILES, LANES), lambda i, be, nu, inv: (i, 0)),
            scratch_shapes=[pltpu.VMEM((D_MODEL, D_EXPERT), BF16), pltpu.VMEM((D_MODEL, D_EXPERT), BF16),
                            pltpu.VMEM((D_EXPERT, D_MODEL), BF16),
                            pltpu.VMEM((2, block * ROW_TILES, LANES), F32), pltpu.SemaphoreType.DMA((2,))]),
        out_shape=jax.ShapeDtypeStruct((nb * block * ROW_TILES, LANES), F32),
        compiler_params=_cparams(("arbitrary",)),
        name="moe_experts",
    )(blk_e, n_used, inv, h, w1, w3, w2)


def _combine_kernel(slot_ref, y_ref, x_ref, mod_ref, wt_ref, o_ref, ybuf, sem, *, tc):
    def issue(j, c):
        for u in range(DMA_UNROLL):
            t = j * DMA_UNROLL + u
            for k in range(TOP_K):
                _row_copy(y_ref, slot_ref[0, 0, TOP_K * t + k], ybuf, k * tc + t, sem).start(priority=k)
        return c
    lax.fori_loop(0, tc // DMA_UNROLL, issue, 0)
    pltpu.make_async_copy(y_ref.at[pl.ds(0, TOP_K * tc * ROW_TILES), :], ybuf, sem).wait()

    wt = wt_ref[...]
    y = (wt[:, 0:1] * _load_rows8(ybuf, tc) + wt[:, 1:2] * _load_rows8(ybuf, tc, base=tc * ROW_TILES))
    o_ref[...] = x_ref[...] + mod_ref[0][:, 5 * D_MODEL:6 * D_MODEL] * y


def _combine(slots, y, x, mod, tiles_per_mod, wts, tc):
    T = x.shape[0]
    return pl.pallas_call(
        functools.partial(_combine_kernel, tc=tc),
        grid=(T // tc,),
        in_specs=[pl.BlockSpec((1, 1, tc * TOP_K), lambda i: (i, 0, 0), memory_space=pltpu.SMEM),
                  pl.BlockSpec(memory_space=pl.ANY),
                  pl.BlockSpec((tc, D_MODEL), lambda i: (i, 0)),
                  _mod_spec(mod, tiles_per_mod),
                  pl.BlockSpec((tc, TOP_K), lambda i: (i, 0))],
        out_specs=pl.BlockSpec((tc, D_MODEL), lambda i: (i, 0)),
        out_shape=jax.ShapeDtypeStruct((T, D_MODEL), F32),
        scratch_shapes=[pltpu.VMEM((TOP_K * tc * ROW_TILES, LANES), F32), pltpu.SemaphoreType.DMA(())],
        compiler_params=_cparams(("arbitrary",)),
        name="moe_combine",
    )(slots.reshape(T // tc, 1, tc * TOP_K), y, x, mod, wts)


def _moe(h, parts, counts, w1, w3, w2, layer):
    n_tokens = sum(p[1].shape[0] for p in parts)
    slots, wts, blk_e, n_used, nb = _layout(jnp.concatenate([p[0] for p in parts], axis=0), counts, MOE_BLOCK,
                                            n_tokens)
    inv = _invmap(slots.reshape(-1), nb * MOE_BLOCK)
    y = _experts(h, inv, blk_e, n_used, nb, w1, w3, w2, layer, MOE_BLOCK)
    outs, off = [], 0
    for _, x, mod, tpm, tt in parts:
        T = x.shape[0]
        outs.append(_combine(slots[off:off + T], y, x, mod, tpm, wts[off:off + T], tt))
        off += T
    return outs


def _kvq_kernel(x_ref, modkv_ref, mod_ref, nkv_ref, nm_ref, wkv_ref, wq_ref, kn_ref, qn_ref, eavg_ref,
                *rest, tm, keep):
    residue = keep is not None
    x = x_ref[...]
    xn = x * lax.rsqrt(jnp.mean(x * x, axis=-1, keepdims=True) + EPS)
    modkv = modkv_ref[0]
    mod = mod_ref[0]
    hk = (xn * nkv_ref[...] * (1.0 + modkv[:, D_MODEL:]) + modkv[:, :D_MODEL]).astype(BF16)
    hq = (xn * nm_ref[...] * (1.0 + mod[:, D_MODEL:2 * D_MODEL]) + mod[:, :D_MODEL]).astype(BF16)
    eavg = eavg_ref[...]

    def head_norm(a, g):
        sq = (a * a).astype(BF16)
        half = ATTN_WIDTH // 2
        ms = jnp.concatenate([jnp.dot(sq[:, :half], eavg, preferred_element_type=F32),
                              jnp.dot(sq[:, half:], eavg, preferred_element_type=F32)], axis=1)
        return a * lax.rsqrt(ms + EPS) * g

    def residue_major(val, ref, dil):
        if dil == 1:
            ref[0, 0] = val.astype(BF16)
            return
        stage = rest[-1]
        for c in range(stage.shape[0]):
            stage[c] = val[:, c * LANES:(c + 1) * LANES]
        for r in range(dil):
            rows = [stage[c, pl.ds(r, tm // dil, stride=dil), :] for c in range(stage.shape[0])]
            ref[0, r] = jnp.concatenate(rows, axis=1).astype(BF16)

    for g, (_, dil) in enumerate(BRANCHES):
        k0 = 2 * g * ATTN_WIDTH
        kvg = jnp.dot(hk, wkv_ref[:, k0:k0 + 2 * ATTN_WIDTH], preferred_element_type=F32)
        kn = head_norm(kvg[:, :ATTN_WIDTH], kn_ref[g:g + 1, :])
        vv = kvg[:, ATTN_WIDTH:]
        q0 = g * ATTN_WIDTH
        qg = jnp.dot(hq, wq_ref[:, q0:q0 + ATTN_WIDTH], preferred_element_type=F32)
        qg = head_norm(qg, qn_ref[g:g + 1, :]) * (HEAD_DIM ** -0.5)
        if residue:
            residue_major(qg, rest[3 * g], dil)
            residue_major(kn, rest[3 * g + 1], dil)
            residue_major(vv, rest[3 * g + 2], dil)
            tiles_per_batch, kept = keep

            @pl.when(pl.program_id(0) % tiles_per_batch >= tiles_per_batch - kept[g])
            def _(kn=kn, vv=vv, kt_ref=rest[3 * N_BRANCH + g]):
                kt_ref[0, :ATTN_WIDTH, :] = kn.T
                kt_ref[0, ATTN_WIDTH:, :] = vv.T
        else:
            rest[0][:, k0:k0 + ATTN_WIDTH] = kn
            rest[0][:, k0 + ATTN_WIDTH:k0 + 2 * ATTN_WIDTH] = vv
            rest[1][:, q0:q0 + ATTN_WIDTH] = qg


def _kvq(x, modkv, mod, tiles_per_mod, tm, nkv, nm, wkv, wq, kn, qn, eavg, batch=None):
    T = x.shape[0]
    tok = lambda w: pl.BlockSpec((tm, w), lambda i: (i, 0))
    kvw = 2 * N_BRANCH * ATTN_WIDTH
    if batch is not None:
        L = T // batch
        tpb = L // tm
        kept = tuple(-(-min(w, L) // tm) for w, _ in BRANCHES)
        keep = (tpb, kept)
        out_specs, out_shape = [], []
        for _, dil in BRANCHES:
            for _ in range(3):
                out_specs.append(pl.BlockSpec((1, dil, tm // dil, ATTN_WIDTH), lambda i: (i // tpb, 0, i % tpb, 0)))
                out_shape.append(jax.ShapeDtypeStruct((batch, dil, L // dil, ATTN_WIDTH), BF16))
        for kg in kept:
            out_specs.append(pl.BlockSpec((1, 2 * ATTN_WIDTH, tm),
                                          lambda i, kg=kg: (i // tpb, 0, jnp.maximum(i % tpb - (tpb - kg), 0))))
            out_shape.append(jax.ShapeDtypeStruct((batch, 2 * ATTN_WIDTH, kg * tm), F32))
        scratch = [pltpu.VMEM((ATTN_WIDTH // LANES, tm, LANES), F32)]
    else:
        keep = None
        out_specs = [tok(kvw), tok(N_BRANCH * ATTN_WIDTH)]
        out_shape = [jax.ShapeDtypeStruct((T, kvw), F32), jax.ShapeDtypeStruct((T, N_BRANCH * ATTN_WIDTH), F32)]
        scratch = []
    return pl.pallas_call(
        functools.partial(_kvq_kernel, tm=tm, keep=keep),
        grid=(T // tm,),
        in_specs=[tok(D_MODEL), _mod_spec(modkv, tiles_per_mod), _mod_spec(mod, tiles_per_mod),
                  _resident(nkv.shape), _resident(nm.shape), _resident(wkv.shape), _resident(wq.shape),
                  _resident(kn.shape), _resident(qn.shape), _resident(eavg.shape)],
        out_specs=out_specs,
        out_shape=out_shape,
        scratch_shapes=scratch,
        compiler_params=_cparams(("arbitrary",)),
        name="kvq",
    )(x, modkv, mod, nkv, nm, wkv, wq, kn, qn, eavg)


def _band_attn_kernel(q_ref, kp_ref, kc_ref, vp_ref, vc_ref, bias_ref, o_ref, lse_ref):
    n = pl.program_id(2)
    q = q_ref[...]
    k2 = jnp.concatenate([kp_ref[...], kc_ref[...]], axis=0)
    v2 = jnp.concatenate([vp_ref[...], vc_ref[...]], axis=0)
    col = lax.broadcasted_iota(jnp.int32, (N_STEPS, 2 * N_STEPS), 1)
    visible = jnp.logical_or(col >= N_STEPS, n > 0)
    lane = lax.broadcasted_iota(jnp.int32, (N_STEPS, LANES), 1)
    lse_all = jnp.zeros((N_STEPS, LANES), F32)
    for h in range(HEADS):
        sl = slice(h * HEAD_DIM, (h + 1) * HEAD_DIM)
        s = lax.dot_general(q[:, sl], k2[:, sl], (((1,), (1,)), ((), ())), preferred_element_type=F32)
        s = jnp.where(visible, s + bias_ref[h], NEG)
        m = jnp.max(s, axis=1, keepdims=True)
        p = jnp.exp(s - m)
        l = jnp.sum(p, axis=1, keepdims=True)
        o_ref[:, sl] = jnp.dot(p.astype(BF16), v2[:, sl], preferred_element_type=F32) / l
        lse_all = jnp.where(lane == h, m + jnp.log(l), lse_all)
    lse_ref[...] = lse_all


def _band_attn(q, k, v, bias, dil):
    B, _, Ld, _ = q.shape
    blk = lambda w, f: pl.BlockSpec((None, None, N_STEPS, w), f)
    cur = lambda b, r, n: (b, r, n, 0)
    prev = lambda b, r, n: (b, r, jnp.maximum(n - 1, 0), 0)
    return pl.pallas_call(
        _band_attn_kernel,
        grid=(B, dil, Ld // N_STEPS),
        in_specs=[blk(ATTN_WIDTH, cur), blk(ATTN_WIDTH, prev), blk(ATTN_WIDTH, cur), blk(ATTN_WIDTH, prev),
                  blk(ATTN_WIDTH, cur), _resident(bias.shape)],
        out_specs=[blk(ATTN_WIDTH, cur), blk(LANES, cur)],
        out_shape=[jax.ShapeDtypeStruct((B, dil, Ld, ATTN_WIDTH), F32),
                   jax.ShapeDtypeStruct((B, dil, Ld, LANES), F32)],
        compiler_params=_cparams(("arbitrary", "arbitrary", "arbitrary")),
        name=f"band_attn_d{dil}",
    )(q, k, k, v, v, bias)


def _rel_buckets(dilation):
    n = np.arange(N_STEPS + 1) * dilation
    large = MAX_EXACT + (np.log(np.maximum(n, 1) / MAX_EXACT) / np.log(REL_MAX_DIST / MAX_EXACT)
                         * (NUM_BUCKETS - MAX_EXACT)).astype(np.int32)
    return np.where(n < MAX_EXACT, n, np.minimum(large, NUM_BUCKETS - 1)).astype(np.int32)


def _step_bias_row(rel_bias, g, dil):
    onehot = np.zeros((N_STEPS + 1, NUM_BUCKETS), np.float32)
    onehot[np.arange(N_STEPS + 1), _rel_buckets(dil)] = 1.0
    return jnp.dot(jnp.asarray(onehot), rel_bias[:, g, :].astype(F32), precision=lax.Precision.HIGHEST).T


def _band_bias(rel_bias, g, dil):
    bias = _step_bias_row(rel_bias, g, dil)
    P = 3 * N_STEPS
    neg = jnp.full((HEADS, N_STEPS), NEG, F32)
    ext = jnp.concatenate([neg, bias[:, ::-1], neg], axis=1)
    flat = jnp.broadcast_to(ext[:, None, :], (HEADS, N_STEPS, P + 1)).reshape(HEADS, N_STEPS * (P + 1))
    skew = flat[:, :N_STEPS * P].reshape(HEADS, N_STEPS, P)
    return skew[:, :, N_STEPS:]


def _step_attn_kernel(q_ref, kvn_ref, c0_ref, c1_ref, c2_ref, b0_ref, b1_ref, b2_ref, bn_ref, o_ref):
    caches = (c0_ref, c1_ref, c2_ref)
    cbias = (b0_ref, b1_ref, b2_ref)
    nt = (((1,), (1,)), ((), ()))
    for h in range(HEADS):
        parts = []
        for g in range(N_BRANCH):
            lo = g * ATTN_WIDTH + h * HEAD_DIM
            klo = 2 * g * ATTN_WIDTH + h * HEAD_DIM
            q = q_ref[0, :, lo:lo + HEAD_DIM].astype(BF16)
            kn = kvn_ref[0, :, klo:klo + HEAD_DIM].astype(BF16)
            vn = kvn_ref[0, :, klo + ATTN_WIDTH:klo + ATTN_WIDTH + HEAD_DIM].astype(BF16)
            kt = caches[g][0, 0, h].astype(BF16)
            vt = caches[g][0, 1, h].astype(BF16)
            sc = jnp.dot(q, kt, preferred_element_type=F32) + cbias[g][h]
            sn = lax.dot_general(q, kn, nt, preferred_element_type=F32) + bn_ref[g, h]
            m = jnp.maximum(jnp.max(sc, axis=1, keepdims=True), jnp.max(sn, axis=1, keepdims=True))
            pc = jnp.exp(sc - m)
            pn = jnp.exp(sn - m)
            l = jnp.sum(pc, axis=1, keepdims=True) + jnp.sum(pn, axis=1, keepdims=True)
            pv = (lax.dot_general(pc.astype(BF16), vt, nt, preferred_element_type=F32)
                  + jnp.dot(pn.astype(BF16), vn, preferred_element_type=F32))
            parts.append((m + jnp.log(l), pv / l))
        mx = jnp.maximum(jnp.maximum(parts[0][0], parts[1][0]), parts[2][0])
        es = [jnp.exp(lse - mx) for lse, _ in parts]
        den = es[0] + es[1] + es[2]
        o_ref[0, :, h * HEAD_DIM:(h + 1) * HEAD_DIM] = (
            es[0] * parts[0][1] + es[1] * parts[1][1] + es[2] * parts[2][1]) / den


def _step_bias(rel_bias, n_new):
    cache_tabs, new_tabs = [], []
    for g, (win, dil) in enumerate(BRANCHES):
        bias = _step_bias_row(rel_bias, g, dil)
        rev = bias[:, ::-1]
        per_s = []
        for s in range(n_new):
            if dil == 1:
                neg = jnp.full((HEADS, s), NEG, F32)
                per_s.append(jnp.concatenate([neg, rev[:, :win - s]], axis=1))
            else:
                cols = [rev[:, :N_STEPS] if r == s % dil else jnp.full((HEADS, N_STEPS), NEG, F32)
                        for r in range(dil)]
                per_s.append(jnp.stack(cols, axis=2).reshape(HEADS, win))
        cache_tabs.append(jnp.stack(per_s, axis=1))
        dist = np.arange(n_new)[:, None] - np.arange(8)[None, :]
        ok = (dist >= 0) & (dist % dil == 0) & (np.arange(8)[None, :] < n_new)
        onehot = np.zeros((N_STEPS + 1, n_new * 8), np.float32)
        onehot[np.where(ok, dist // dil, 0).reshape(-1), np.arange(n_new * 8)] = 1.0
        tab = jnp.dot(bias, jnp.asarray(onehot), precision=lax.Precision.HIGHEST).reshape(HEADS, n_new, 8)
        new_tabs.append(jnp.where(ok[None], tab, NEG))
    return cache_tabs, jnp.stack(new_tabs)


def _step_attn(q, kvn, caches, cache_bias, new_bias, DB, S):
    cspec = lambda c: pl.BlockSpec((1,) + c.shape[1:], lambda b: (b, 0, 0, 0, 0))
    return pl.pallas_call(
        _step_attn_kernel,
        grid=(DB,),
        in_specs=[pl.BlockSpec((1, S, N_BRANCH * ATTN_WIDTH), lambda b: (b, 0, 0)),
                  pl.BlockSpec((1, 8, 2 * N_BRANCH * ATTN_WIDTH), lambda b: (b, 0, 0)),
                  cspec(caches[0]), cspec(caches[1]), cspec(caches[2]),
                  _resident(cache_bias[0].shape), _resident(cache_bias[1].shape), _resident(cache_bias[2].shape),
                  _resident(new_bias.shape)],
        out_specs=pl.BlockSpec((1, S, ATTN_WIDTH), lambda b: (b, 0, 0)),
        out_shape=jax.ShapeDtypeStruct((DB, S, ATTN_WIDTH), F32),
        compiler_params=_cparams(("arbitrary",)),
        name="step_attn",
    )(q, kvn, caches[0], caches[1], caches[2], *cache_bias, new_bias)


def _attn_out_kernel(*refs, dils, tm):
    n_o = max(len(dils), 1)
    n_in = n_o + len(dils) + 10
    (x_ref, mod_ref, wo_ref, ex_ref, nf_ref, wr_ref, br_ref, cin_ref, tri_ref, _,
     x3_ref, h3_ref, rt_ref, cnt_ref, carry) = refs[n_in - 10:n_in + 5]
    stages = refs[n_in + 5:]

    def natural(ref, stage, dil):
        if dil == 1:
            return ref[0, 0]
        for r in range(dil):
            val = ref[0, r]
            for c in range(stage.shape[0]):
                stage[c, pl.ds(r, tm // dil, stride=dil), :] = val[:, c * LANES:(c + 1) * LANES]
        return jnp.concatenate([stage[c] for c in range(stage.shape[0])], axis=1)

    if dils:
        lses = [natural(refs[n_o + g], stages[2 * g + 1], dil) for g, dil in enumerate(dils)]
        mx = functools.reduce(jnp.maximum, lses)
        es = [jnp.exp(l - mx) for l in lses]
        den = functools.reduce(lambda a, b: a + b, es)
        o = None
        for g, dil in enumerate(dils):
            term = _split_dot(es[g] / den, ex_ref[...]) * natural(refs[g], stages[2 * g], dil)
            o = term if o is None else o + term
    else:
        o = refs[0][...]
    mod = mod_ref[0]
    a = jnp.dot(o.astype(BF16), wo_ref[...], preferred_element_type=F32)
    x3 = x_ref[...] + mod[:, 2 * D_MODEL:3 * D_MODEL] * a
    x3_ref[...] = x3
    _ffn_pre(x3, mod, nf_ref, wr_ref, br_ref, cin_ref, tri_ref, h3_ref, rt_ref, cnt_ref, carry, tm)


def _attn_out(os_, lses, x, mod, tiles_per_mod, tm, wo, ex, nf, wr, br, counts, hbuf, tok0):
    T = x.shape[0]
    tok = lambda w: pl.BlockSpec((tm, w), lambda i: (i, 0))
    r_specs, r_shapes = _route_outs(T, tm)
    h_in, h_out, h_shape = _hbuf_specs(hbuf, tm, tok0)
    tri = _earlier_rows(tm)
    if lses:
        dils = tuple(o.shape[1] for o in os_)
        tpb = tiles_per_mod
        res = lambda a: pl.BlockSpec((1, a.shape[1], tm // a.shape[1], a.shape[3]), lambda i: (i // tpb, 0, i % tpb, 0))
        o_specs = [res(a) for a in os_] + [res(a) for a in lses]
        scratch = []
        for _ in dils:
            scratch += [pltpu.VMEM((ATTN_WIDTH // LANES, tm, LANES), F32), pltpu.VMEM((1, tm, LANES), F32)]
    else:
        dils, o_specs, scratch = (), [tok(ATTN_WIDTH)], []
    return pl.pallas_call(
        functools.partial(_attn_out_kernel, dils=dils, tm=tm),
        grid=(T // tm,),
        in_specs=(o_specs + [tok(D_MODEL), _mod_spec(mod, tiles_per_mod), _resident(wo.shape), _resident(ex.shape),
                             _resident(nf.shape), _resident(wr.shape), _resident(br.shape),
                             _resident(counts.shape), _resident(tri.shape), h_in]),
        out_specs=[tok(D_MODEL), h_out] + r_specs,
        out_shape=[jax.ShapeDtypeStruct((T, D_MODEL), F32), h_shape] + r_shapes,
        scratch_shapes=[pltpu.VMEM((1, LANES), F32)] + scratch,
        input_output_aliases={len(o_specs) + 9: 1},
        compiler_params=_cparams(("arbitrary",)),
        name="attn_out",
    )(*os_, *lses, x, mod, wo, ex, nf, wr, br, counts, tri, hbuf)


def kernel(x_prompt, x_sample, cache_kv_w128, cache_kv_w512, cache_kv_w2048, c_prompt, c_sample, ada_w, ada_b, norm_mix, norm_ffn, a_w_in, a_b_in, a_norm_v, a_w_s, a_b_s, a_w_out, kv_ada_w, kv_ada_b, kv_norm, w_kv, k_norm, rel_bias, b_w_q, q_norm, b_w_o, r_w_group, r_b_group, r_w_expert, r_b_expert, e_w1, e_w3, e_w2):
    B, L, _ = x_prompt.shape
    DB, S, _ = x_sample.shape
    Tp, Ts = B * L, DB * S
    tm = TOKEN_TILE
    tpm_p = L // tm

    c_all = jnp.concatenate([c_prompt, c_sample], axis=0)
    R = c_all.shape[0]
    c_all = jnp.pad(c_all, ((0, -R % 8), (0, 0)))
    mods = [_ada(c_all, ada_w, ada_b, l) for l in range(2)]
    modkv = _ada(c_all, kv_ada_w[None], kv_ada_b[None], 0)

    def split_mod(m):
        return m[:B, None, :], jnp.repeat(m[B:B + DB], S, axis=0)[None]
    mod_p, mod_s = zip(*[split_mod(m) for m in mods])
    modkv_p, modkv_s = split_mod(modkv)

    row = lambda a: a.reshape(1, -1)

    def router(l):
        wr = jnp.zeros((D_MODEL, LANES), F32)
        wr = wr.at[:, :N_GROUPS].set(r_w_group[l]).at[:, N_GROUPS:N_GROUPS + N_EXPERTS].set(r_w_expert[l])
        br = jnp.zeros((1, LANES), F32)
        br = br.at[0, :N_GROUPS].set(r_b_group[l]).at[0, N_GROUPS:N_GROUPS + N_EXPERTS].set(r_b_expert[l])
        return wr.astype(BF16), br

    no_counts = jnp.zeros((1, LANES), F32)

    win = a_w_in[0].astype(BF16)
    wout = a_w_out[0].astype(BF16)
    tril = jnp.tril(jnp.ones((CHUNK, CHUNK), bool))
    ws_p = jnp.where(tril, a_w_s[0], 0).astype(BF16)
    bs_p = a_b_s[0].T
    cs = min(CHUNK, S)
    ws_small = jnp.where(jnp.tril(jnp.ones((cs, cs), bool)), a_w_s[0][:, :cs, :cs], 0)
    ws_s = jnp.stack([jnp.kron(jnp.eye(Ts // cs, dtype=F32), ws_small[g]) for g in range(GMLP_GROUPS)]).astype(BF16)
    bs_s = jnp.tile(a_b_s[0][:, :cs], (1, Ts // cs)).T
    wr0, br0 = router(0)
    common = (row(norm_mix[0]), win, row(a_b_in[0]), row(a_norm_v[0]))
    xp = x_prompt.reshape(Tp, D_MODEL)
    xs_ = x_sample.reshape(Ts, D_MODEL)
    no_rows = jnp.zeros(((Tp + Ts) * ROW_TILES, LANES), F32)
    x1_p, h2, rt_p, cnt = _mixer_a(xp, mod_p[0], tpm_p, tm, *common, ws_p, bs_p, wout, row(norm_ffn[0]), wr0, br0,
                                   no_counts, no_rows, 0, with_v=False)
    x1_s, h2, rt_s, cnt, v_s = _mixer_a(xs_, mod_s[0], 1, Ts, *common, ws_s, bs_s, wout, row(norm_ffn[0]), wr0, br0,
                                        cnt, h2, Tp, with_v=True)
    x2_p, x2_s = _moe(h2, [(rt_p, x1_p, mod_p[0], tpm_p, tm), (rt_s, x1_s, mod_s[0], 1, Ts)], cnt,
                      e_w1, e_w3, e_w2, 0)

    wkv = w_kv.astype(BF16)
    wq = b_w_q[0].astype(BF16)
    kn = jnp.tile(k_norm, (1, HEADS))
    qn = jnp.tile(q_norm[0], (1, HEADS))
    head = np.arange(ATTN_WIDTH) // HEAD_DIM
    half_head = head[:ATTN_WIDTH // 2]
    eavg = jnp.asarray((half_head[:, None] == half_head[None, :]) / HEAD_DIM, BF16)
    kvq_w = (row(kv_norm), row(norm_mix[1]), wkv, wq, kn, qn, eavg)
    *qkv_p, kt0, kt1, kt2 = _kvq(x2_p, modkv_p, mod_p[1], tpm_p, tm, *kvq_w, batch=B)
    kv_s, q_s = _kvq(x2_s, modkv_s, mod_s[1], 1, Ts, *kvq_w)

    os_, lses = [], []
    for g, (_, dil) in enumerate(BRANCHES):
        o, lse = _band_attn(*qkv_p[3 * g:3 * g + 3], _band_bias(rel_bias, g, dil), dil)
        os_.append(o)
        lses.append(lse)
    caches = [jnp.transpose(c, (0, 2, 3, 4, 1)) for c in (cache_kv_w128, cache_kv_w512, cache_kv_w2048)]
    kvn_s = jnp.pad(kv_s.reshape(DB, S, -1), ((0, 0), (0, 8 - S), (0, 0)))
    o_s = _step_attn(q_s.reshape(DB, S, -1), kvn_s, caches, *_step_bias(rel_bias, S), DB, S)

    wo = b_w_o[0].astype(BF16)
    ex = jnp.asarray(np.arange(LANES)[:, None] == head[None, :], BF16)
    wr1, br1 = router(1)
    x3_p, h3, rt1_p, cnt1 = _attn_out(os_, lses, x2_p, mod_p[1], tpm_p, tm, wo, ex, row(norm_ffn[1]), wr1, br1,
                                      no_counts, no_rows, 0)
    x3_s, h3, rt1_s, cnt1 = _attn_out([o_s.reshape(Ts, ATTN_WIDTH)], [], x2_s, mod_s[1], 1, Ts, wo, ex,
                                      row(norm_ffn[1]), wr1, br1, cnt1, h3, Tp)
    y_p, y_s = _moe(h3, [(rt1_p, x3_p, mod_p[1], tpm_p, tm), (rt1_s, x3_s, mod_s[1], 1, Ts)], cnt1,
                    e_w1, e_w3, e_w2, 1)

    kv_s4 = kv_s.reshape(DB, S, N_BRANCH, 2, HEADS, HEAD_DIM)

    def window(kt, w):
        n = min(w, L)
        return jnp.transpose(kt[:, :, kt.shape[2] - n:].reshape(B, 2, HEADS, HEAD_DIM, n), (0, 4, 1, 2, 3))
    return (y_p.reshape(B, L, D_MODEL), y_s.reshape(DB, S, D_MODEL),
            window(kt0, BRANCHES[0][0]), window(kt1, BRANCHES[1][0]), window(kt2, BRANCHES[2][0]),
            kv_s4[:, :, 0], kv_s4[:, :, 1], kv_s4[:, :, 2],
            v_s.reshape(1, DB, S, GMLP_WIDTH))
```

```python
import functools

import numpy as np
import jax
import jax.numpy as jnp
from jax import lax
from jax.experimental import pallas as pl
from jax.experimental.pallas import tpu as pltpu

F32 = jnp.float32
BF16 = jnp.bfloat16

D_MODEL = 1024
GMLP_WIDTH = 2048
GMLP_GROUPS = 4
GROUP_WIDTH = GMLP_WIDTH // GMLP_GROUPS
CHUNK = 128
BRANCHES = ((128, 1), (512, 4), (2048, 16))
N_BRANCH = 3
N_STEPS = 128
HEADS = 8
HEAD_DIM = 64
ATTN_WIDTH = HEADS * HEAD_DIM
NUM_BUCKETS = 32
MAX_EXACT = NUM_BUCKETS // 2
REL_MAX_DIST = 2048
N_GROUPS = 4
EXPERTS_PER_GROUP = 8
N_EXPERTS = N_GROUPS * EXPERTS_PER_GROUP
TOP_K = 2
D_EXPERT = 512
EPS = 1e-6
NEG = -1e30

LANES = 128
ROW_TILES = D_MODEL // LANES
TOKEN_TILE = 512
MOE_BLOCK = 256
DMA_UNROLL = 8
VMEM_LIMIT = 52 * 1024 * 1024


def _cparams(sem):
    return pltpu.CompilerParams(dimension_semantics=sem, vmem_limit_bytes=VMEM_LIMIT)


def _resident(shape):
    nd = len(shape)
    return pl.BlockSpec(shape, lambda *_, _nd=nd: (0,) * _nd, pipeline_mode=pl.Buffered(1))


def _gelu_tanh(x):
    return 0.5 * x * (1.0 + jnp.tanh(0.7978845608028654 * (x + 0.044715 * (x * x * x))))


def _rms(x, g):
    return x * lax.rsqrt(jnp.mean(x * x, axis=-1, keepdims=True) + EPS) * g


def _store_rows8(ref, val, n, base=0):
    for s in range(ROW_TILES):
        ref[pl.ds(base + s, n, stride=ROW_TILES), :] = val[:, s * LANES:(s + 1) * LANES]


def _load_rows8(ref, n, base=0):
    return jnp.concatenate([ref[pl.ds(base + s, n, stride=ROW_TILES), :] for s in range(ROW_TILES)], axis=1)


def _split(a):
    hi = a.astype(BF16)
    return hi, (a - hi.astype(F32)).astype(BF16)


def _split_dot(a, e_bf16):
    hi, lo = _split(a)
    return (jnp.dot(hi, e_bf16, preferred_element_type=F32) + jnp.dot(lo, e_bf16, preferred_element_type=F32))


ROUTE_LANE0 = 4


def _route_rows(l, tri_ref, carry):
    lane = lax.broadcasted_iota(jnp.int32, l.shape, 1).astype(F32)
    far = float(LANES)

    def first_lane(mask):
        return jnp.min(jnp.where(mask, lane, far), axis=1, keepdims=True)

    is_g = lane < N_GROUPS
    gl = jnp.where(is_g, l, NEG)
    gmax = jnp.max(gl, axis=1, keepdims=True)
    g_i = first_lane(jnp.logical_and(gl == gmax, is_g))
    g_p = 1.0 / jnp.sum(jnp.where(is_g, jnp.exp(gl - gmax), 0.0), axis=1, keepdims=True)
    lo = ROUTE_LANE0 + EXPERTS_PER_GROUP * g_i
    sel = jnp.logical_and(lane >= lo, lane < lo + EXPERTS_PER_GROUP)
    el = jnp.where(sel, l, NEG)
    m1 = jnp.max(el, axis=1, keepdims=True)
    i1 = first_lane(jnp.logical_and(el == m1, sel))
    sel2 = jnp.logical_and(sel, lane != i1)
    el2 = jnp.where(sel2, l, NEG)
    m2 = jnp.max(el2, axis=1, keepdims=True)
    i2 = first_lane(jnp.logical_and(el2 == m2, sel2))
    r = jnp.exp(m2 - m1)
    w1 = g_p / (1.0 + r)
    w2 = g_p * r / (1.0 + r)

    hit1 = lane == i1
    hit2 = lane == i2
    onehot = jnp.where(jnp.logical_or(hit1, hit2), 1.0, 0.0)
    before = carry[...] + jnp.dot(tri_ref[...], onehot.astype(BF16), preferred_element_type=F32)
    rank1 = jnp.sum(jnp.where(hit1, before, 0.0), axis=1, keepdims=True)
    rank2 = jnp.sum(jnp.where(hit2, before, 0.0), axis=1, keepdims=True)
    carry[...] = carry[...] + jnp.sum(onehot, axis=0, keepdims=True)

    out = jnp.zeros(l.shape, F32)
    for k, val in enumerate((i1 - ROUTE_LANE0, i2 - ROUTE_LANE0, w1, w2, rank1, rank2)):
        out = jnp.where(lane == k, val, out)
    return out


def _ffn_pre(x, mod, nf_ref, wr_ref, br_ref, cin_ref, tri_ref, h_ref, rt_ref, cnt_ref, carry, n):
    @pl.when(pl.program_id(0) == 0)
    def _():
        carry[...] = cin_ref[...]

    h = _rms(x, nf_ref[...]) * (1.0 + mod[:, 4 * D_MODEL:5 * D_MODEL]) + mod[:, 3 * D_MODEL:4 * D_MODEL]
    _store_rows8(h_ref, h, n)
    logits = jnp.dot(h.astype(BF16), wr_ref[...], preferred_element_type=F32) + br_ref[...]
    rt_ref[...] = _route_rows(logits, tri_ref, carry)
    cnt_ref[...] = carry[...]


def _ada_kernel(c_ref, w_ref, b_ref, o_ref):
    c = c_ref[...]
    a = (c * jax.nn.sigmoid(c)).astype(BF16)
    o_ref[...] = jnp.dot(a, w_ref[...].astype(BF16), preferred_element_type=F32) + b_ref[...]


def _ada(c, w, b, layer):
    R = c.shape[0]
    N = w.shape[2]
    tn = 1024
    return pl.pallas_call(
        _ada_kernel,
        grid=(N // tn,),
        in_specs=[pl.BlockSpec((R, D_MODEL), lambda j: (0, 0)),
                  pl.BlockSpec((None, D_MODEL, tn), lambda j: (layer, 0, j)),
                  pl.BlockSpec((None, 1, tn), lambda j: (layer, 0, j))],
        out_specs=pl.BlockSpec((R, tn), lambda j: (0, j)),
        out_shape=jax.ShapeDtypeStruct((R, N), F32),
        compiler_params=_cparams(("arbitrary",)),
        name="ada",
    )(c, w, b.reshape(b.shape[0], 1, N))


def _mixer_a_kernel(x_ref, mod_ref, nm_ref, win_ref, bin_ref, gv_ref, ws_ref, bs_ref, wout_ref,
                    nf_ref, wr_ref, br_ref, cin_ref, tri_ref, x1_ref, h2_ref, rt_ref, cnt_ref, *rest, tm):
    v_refs, carry = rest[:-1], rest[-1]
    x = x_ref[...]
    mod = mod_ref[0]
    h = (_rms(x, nm_ref[...]) * (1.0 + mod[:, D_MODEL:2 * D_MODEL]) + mod[:, 0:D_MODEL]).astype(BF16)
    zv = jnp.dot(h, win_ref[:, GMLP_WIDTH:], preferred_element_type=F32) + bin_ref[:, GMLP_WIDTH:]
    v = _rms(_gelu_tanh(zv), gv_ref[...])
    if v_refs:
        v_refs[0][...] = v
    vb = v.astype(BF16)
    bs = bs_ref[...]
    acc = jnp.zeros((tm, D_MODEL), F32)
    for g in range(GMLP_GROUPS):
        lo, hi = g * GROUP_WIDTH, (g + 1) * GROUP_WIDTH
        u = _gelu_tanh(jnp.dot(h, win_ref[:, lo:hi], preferred_element_type=F32) + bin_ref[:, lo:hi])
        wg = ws_ref[g]
        gate = jnp.concatenate(
            [jnp.dot(wg, vb[c * CHUNK:(c + 1) * CHUNK, lo:hi], preferred_element_type=F32) + bs[:, g:g + 1]
             for c in range(tm // CHUNK)], axis=0)
        acc = acc + jnp.dot((u * gate).astype(BF16), wout_ref[lo:hi, :], preferred_element_type=F32)
    x1 = x + mod[:, 2 * D_MODEL:3 * D_MODEL] * acc
    x1_ref[...] = x1
    _ffn_pre(x1, mod, nf_ref, wr_ref, br_ref, cin_ref, tri_ref, h2_ref, rt_ref, cnt_ref, carry, tm)


def _mod_spec(mod, tiles_per_mod):
    _, rows, width = mod.shape
    return pl.BlockSpec((1, rows, width), lambda i: (i // tiles_per_mod, 0, 0))


def _earlier_rows(tm):
    return jnp.asarray(np.tril(np.ones((tm, tm), np.float32), -1), BF16)


def _route_outs(T, tm):
    return ([pl.BlockSpec((tm, LANES), lambda i: (i, 0)), pl.BlockSpec((1, LANES), lambda i: (0, 0))],
            [jax.ShapeDtypeStruct((T, LANES), F32), jax.ShapeDtypeStruct((1, LANES), F32)])


def _mixer_a(x, mod, tiles_per_mod, tm, nm, win, bin_, gv, ws, bs_t, wout, nf, wr, br, counts, with_v):
    T = x.shape[0]
    tok = lambda w: pl.BlockSpec((tm, w), lambda i: (i, 0))
    r_specs, r_shapes = _route_outs(T, tm)
    out_shape = [jax.ShapeDtypeStruct((T, D_MODEL), F32), jax.ShapeDtypeStruct((T * ROW_TILES, LANES), F32)] + r_shapes
    out_specs = [tok(D_MODEL), pl.BlockSpec((tm * ROW_TILES, LANES), lambda i: (i, 0))] + r_specs
    if with_v:
        out_shape.append(jax.ShapeDtypeStruct((T, GMLP_WIDTH), F32))
        out_specs.append(tok(GMLP_WIDTH))
    tri = _earlier_rows(tm)
    return pl.pallas_call(
        functools.partial(_mixer_a_kernel, tm=tm),
        grid=(T // tm,),
        in_specs=[tok(D_MODEL), _mod_spec(mod, tiles_per_mod), _resident(nm.shape), _resident(win.shape),
                  _resident(bin_.shape), _resident(gv.shape), _resident(ws.shape), _resident(bs_t.shape),
                  _resident(wout.shape), _resident(nf.shape), _resident(wr.shape), _resident(br.shape),
                  _resident(counts.shape), _resident(tri.shape)],
        out_specs=out_specs,
        out_shape=out_shape,
        scratch_shapes=[pltpu.VMEM((1, LANES), F32)],
        compiler_params=_cparams(("arbitrary",)),
        name="mixer_a",
    )(x, mod, nm, win, bin_, gv, ws, bs_t, wout, nf, wr, br, counts, tri)


def _layout(route, counts, block, n_tokens):
    counts = counts[0, ROUTE_LANE0:ROUTE_LANE0 + N_EXPERTS].astype(jnp.int32)
    padded = (counts + block - 1) // block * block
    pad_end = jnp.cumsum(padded)
    pad_start = (pad_end - padded).astype(F32)
    experts = route[:, 0:TOP_K]
    onehot = experts[:, :, None] == jnp.arange(N_EXPERTS, dtype=F32)[None, None, :]
    slots = (route[:, 4:4 + TOP_K] + jnp.sum(jnp.where(onehot, pad_start[None, None, :], 0.0), axis=-1)).astype(jnp.int32)
    nb = -(-n_tokens * TOP_K // block) + N_EXPERTS
    blk_e = jnp.minimum(jnp.sum(pad_end[None, :] <= (jnp.arange(nb, dtype=jnp.int32) * block)[:, None], axis=1),
                        N_EXPERTS - 1).astype(jnp.int32)
    n_used = (pad_end[-1] // block).astype(jnp.int32).reshape(1)
    return slots, route[:, 2:2 + TOP_K], blk_e, n_used, nb


def _row_copy(src_ref, src_row, dst_ref, dst_row, sem):
    return pltpu.make_async_copy(
        src_ref.at[pl.ds(pl.multiple_of(src_row * ROW_TILES, ROW_TILES), ROW_TILES), :],
        dst_ref.at[pl.ds(pl.multiple_of(dst_row * ROW_TILES, ROW_TILES), ROW_TILES), :], sem)


def _scatter_kernel(slot_ref, src_ref, dst_in_ref, dst_ref, sem, *, ts):
    del dst_in_ref

    def issue(j, c):
        for u in range(DMA_UNROLL):
            t = j * DMA_UNROLL + u
            for k in range(TOP_K):
                _row_copy(src_ref, t, dst_ref, slot_ref[0, 0, TOP_K * t + k], sem).start(priority=k)
        return c
    lax.fori_loop(0, ts // DMA_UNROLL, issue, 0)

    for _ in range(TOP_K):
        pltpu.make_async_copy(src_ref, dst_ref.at[pl.ds(0, ts * ROW_TILES), :], sem).wait()


def _scatter(slots, src, dst, ts):
    T = slots.shape[0]
    return pl.pallas_call(
        functools.partial(_scatter_kernel, ts=ts),
        grid=(T // ts,),
        in_specs=[pl.BlockSpec((1, 1, ts * TOP_K), lambda i: (i, 0, 0), memory_space=pltpu.SMEM),
                  pl.BlockSpec((ts * ROW_TILES, LANES), lambda i: (i, 0)),
                  pl.BlockSpec(memory_space=pl.ANY)],
        out_specs=pl.BlockSpec(memory_space=pl.ANY),
        out_shape=jax.ShapeDtypeStruct(dst.shape, dst.dtype),
        scratch_shapes=[pltpu.SemaphoreType.DMA(())],
        input_output_aliases={2: 0},
        compiler_params=_cparams(("arbitrary",)),
        name="moe_scatter",
    )(slots.reshape(T // ts, 1, ts * TOP_K), src, dst)


def _experts_kernel(be_ref, nu_ref, x_ref, w1_ref, w3_ref, w2_ref, y_ref, w1b, w3b, w2b, *, block):
    i = pl.program_id(0)

    @pl.when(i < nu_ref[0])
    def _():
        @pl.when(jnp.logical_or(i == 0, be_ref[i] != be_ref[jnp.maximum(i - 1, 0)]))
        def _():
            w1b[...] = w1_ref[...].astype(BF16)
            w3b[...] = w3_ref[...].astype(BF16)
            w2b[...] = w2_ref[...].astype(BF16)

        x = _load_rows8(x_ref, block).astype(BF16)
        h1 = jnp.dot(x, w1b[...], preferred_element_type=F32)
        h3 = jnp.dot(x, w3b[...], preferred_element_type=F32)
        a = (h1 * jax.nn.sigmoid(h1) * h3).astype(BF16)
        _store_rows8(y_ref, jnp.dot(a, w2b[...], preferred_element_type=F32), block)


def _experts(xs, blk_e, n_used, nb, w1, w3, w2, layer, block):
    last = lambda i, nu: jnp.minimum(i, nu[0] - 1)
    rows = pl.BlockSpec((block * ROW_TILES, LANES), lambda i, be, nu: (last(i, nu), 0))
    wspec = lambda r, c: pl.BlockSpec((None, None, r, c), lambda i, be, nu: (layer, be[last(i, nu)], 0, 0))
    return pl.pallas_call(
        functools.partial(_experts_kernel, block=block),
        grid_spec=pltpu.PrefetchScalarGridSpec(
            num_scalar_prefetch=2,
            grid=(nb,),
            in_specs=[rows, wspec(D_MODEL, D_EXPERT), wspec(D_MODEL, D_EXPERT), wspec(D_EXPERT, D_MODEL)],
            out_specs=rows,
            scratch_shapes=[pltpu.VMEM((D_MODEL, D_EXPERT), BF16), pltpu.VMEM((D_MODEL, D_EXPERT), BF16),
                            pltpu.VMEM((D_EXPERT, D_MODEL), BF16)]),
        out_shape=jax.ShapeDtypeStruct(xs.shape, F32),
        input_output_aliases={2: 0},
        compiler_params=_cparams(("arbitrary",)),
        name="moe_experts",
    )(blk_e, n_used, xs, w1, w3, w2)


def _combine_kernel(slot_ref, y_ref, x_ref, mod_ref, wt_ref, o_ref, ybuf, sem, *, tc):
    def issue(j, c):
        for u in range(DMA_UNROLL):
            t = j * DMA_UNROLL + u
            for k in range(TOP_K):
                _row_copy(y_ref, slot_ref[0, 0, TOP_K * t + k], ybuf, k * tc + t, sem).start(priority=k)
        return c
    lax.fori_loop(0, tc // DMA_UNROLL, issue, 0)
    pltpu.make_async_copy(y_ref.at[pl.ds(0, TOP_K * tc * ROW_TILES), :], ybuf, sem).wait()

    wt = wt_ref[...]
    y = (wt[:, 0:1] * _load_rows8(ybuf, tc) + wt[:, 1:2] * _load_rows8(ybuf, tc, base=tc * ROW_TILES))
    o_ref[...] = x_ref[...] + mod_ref[0][:, 5 * D_MODEL:6 * D_MODEL] * y


def _combine(slots, y, x, mod, tiles_per_mod, wts, tc):
    T = x.shape[0]
    return pl.pallas_call(
        functools.partial(_combine_kernel, tc=tc),
        grid=(T // tc,),
        in_specs=[pl.BlockSpec((1, 1, tc * TOP_K), lambda i: (i, 0, 0), memory_space=pltpu.SMEM),
                  pl.BlockSpec(memory_space=pl.ANY),
                  pl.BlockSpec((tc, D_MODEL), lambda i: (i, 0)),
                  _mod_spec(mod, tiles_per_mod),
                  pl.BlockSpec((tc, TOP_K), lambda i: (i, 0))],
        out_specs=pl.BlockSpec((tc, D_MODEL), lambda i: (i, 0)),
        out_shape=jax.ShapeDtypeStruct((T, D_MODEL), F32),
        scratch_shapes=[pltpu.VMEM((TOP_K * tc * ROW_TILES, LANES), F32), pltpu.SemaphoreType.DMA(())],
        compiler_params=_cparams(("arbitrary",)),
        name="moe_combine",
    )(slots.reshape(T // tc, 1, tc * TOP_K), y, x, mod, wts)


def _moe(parts, counts, w1, w3, w2, layer):
    n_tokens = sum(p[2].shape[0] for p in parts)
    slots, wts, blk_e, n_used, nb = _layout(jnp.concatenate([p[1] for p in parts], axis=0), counts, MOE_BLOCK,
                                            n_tokens)
    xs = jnp.zeros((nb * MOE_BLOCK * ROW_TILES, LANES), F32)
    off = 0
    for h, _, x, _, _, tt in parts:
        T = x.shape[0]
        xs = _scatter(slots[off:off + T], h, xs, tt)
        off += T
    y = _experts(xs, blk_e, n_used, nb, w1, w3, w2, layer, MOE_BLOCK)
    outs, off = [], 0
    for _, _, x, mod, tpm, tt in parts:
        T = x.shape[0]
        outs.append(_combine(slots[off:off + T], y, x, mod, tpm, wts[off:off + T], tt))
        off += T
    return outs


def _kvq_kernel(x_ref, modkv_ref, mod_ref, nkv_ref, nm_ref, wkv_ref, wq_ref, kn_ref, qn_ref, eavg_ref,
                *rest, tm, keep):
    residue = keep is not None
    x = x_ref[...]
    xn = x * lax.rsqrt(jnp.mean(x * x, axis=-1, keepdims=True) + EPS)
    modkv = modkv_ref[0]
    mod = mod_ref[0]
    hk = (xn * nkv_ref[...] * (1.0 + modkv[:, D_MODEL:]) + modkv[:, :D_MODEL]).astype(BF16)
    hq = (xn * nm_ref[...] * (1.0 + mod[:, D_MODEL:2 * D_MODEL]) + mod[:, :D_MODEL]).astype(BF16)
    eavg = eavg_ref[...]

    def head_norm(a, g):
        sq = (a * a).astype(BF16)
        half = ATTN_WIDTH // 2
        ms = jnp.concatenate([jnp.dot(sq[:, :half], eavg, preferred_element_type=F32),
                              jnp.dot(sq[:, half:], eavg, preferred_element_type=F32)], axis=1)
        return a * lax.rsqrt(ms + EPS) * g

    def residue_major(val, ref, dil):
        if dil == 1:
            ref[0, 0] = val.astype(BF16)
            return
        stage = rest[-1]
        for c in range(stage.shape[0]):
            stage[c] = val[:, c * LANES:(c + 1) * LANES]
        for r in range(dil):
            rows = [stage[c, pl.ds(r, tm // dil, stride=dil), :] for c in range(stage.shape[0])]
            ref[0, r] = jnp.concatenate(rows, axis=1).astype(BF16)

    for g, (_, dil) in enumerate(BRANCHES):
        k0 = 2 * g * ATTN_WIDTH
        kvg = jnp.dot(hk, wkv_ref[:, k0:k0 + 2 * ATTN_WIDTH], preferred_element_type=F32)
        kn = head_norm(kvg[:, :ATTN_WIDTH], kn_ref[g:g + 1, :])
        vv = kvg[:, ATTN_WIDTH:]
        q0 = g * ATTN_WIDTH
        qg = jnp.dot(hq, wq_ref[:, q0:q0 + ATTN_WIDTH], preferred_element_type=F32)
        qg = head_norm(qg, qn_ref[g:g + 1, :]) * (HEAD_DIM ** -0.5)
        if residue:
            residue_major(qg, rest[3 * g], dil)
            residue_major(kn, rest[3 * g + 1], dil)
            residue_major(vv, rest[3 * g + 2], dil)
            tiles_per_batch, kept = keep

            @pl.when(pl.program_id(0) % tiles_per_batch >= tiles_per_batch - kept[g])
            def _(kn=kn, vv=vv, kt_ref=rest[3 * N_BRANCH + g]):
                kt_ref[0, :ATTN_WIDTH, :] = kn.T
                kt_ref[0, ATTN_WIDTH:, :] = vv.T
        else:
            rest[0][:, k0:k0 + ATTN_WIDTH] = kn
            rest[0][:, k0 + ATTN_WIDTH:k0 + 2 * ATTN_WIDTH] = vv
            rest[1][:, q0:q0 + ATTN_WIDTH] = qg


def _kvq(x, modkv, mod, tiles_per_mod, tm, nkv, nm, wkv, wq, kn, qn, eavg, batch=None):
    T = x.shape[0]
    tok = lambda w: pl.BlockSpec((tm, w), lambda i: (i, 0))
    kvw = 2 * N_BRANCH * ATTN_WIDTH
    if batch is not None:
        L = T // batch
        tpb = L // tm
        kept = tuple(-(-min(w, L) // tm) for w, _ in BRANCHES)
        keep = (tpb, kept)
        out_specs, out_shape = [], []
        for _, dil in BRANCHES:
            for _ in range(3):
                out_specs.append(pl.BlockSpec((1, dil, tm // dil, ATTN_WIDTH), lambda i: (i // tpb, 0, i % tpb, 0)))
                out_shape.append(jax.ShapeDtypeStruct((batch, dil, L // dil, ATTN_WIDTH), BF16))
        for kg in kept:
            out_specs.append(pl.BlockSpec((1, 2 * ATTN_WIDTH, tm),
                                          lambda i, kg=kg: (i // tpb, 0, jnp.maximum(i % tpb - (tpb - kg), 0))))
            out_shape.append(jax.ShapeDtypeStruct((batch, 2 * ATTN_WIDTH, kg * tm), F32))
        scratch = [pltpu.VMEM((ATTN_WIDTH // LANES, tm, LANES), F32)]
    else:
        keep = None
        out_specs = [tok(kvw), tok(N_BRANCH * ATTN_WIDTH)]
        out_shape = [jax.ShapeDtypeStruct((T, kvw), F32), jax.ShapeDtypeStruct((T, N_BRANCH * ATTN_WIDTH), F32)]
        scratch = []
    return pl.pallas_call(
        functools.partial(_kvq_kernel, tm=tm, keep=keep),
        grid=(T // tm,),
        in_specs=[tok(D_MODEL), _mod_spec(modkv, tiles_per_mod), _mod_spec(mod, tiles_per_mod),
                  _resident(nkv.shape), _resident(nm.shape), _resident(wkv.shape), _resident(wq.shape),
                  _resident(kn.shape), _resident(qn.shape), _resident(eavg.shape)],
        out_specs=out_specs,
        out_shape=out_shape,
        scratch_shapes=scratch,
        compiler_params=_cparams(("arbitrary",)),
        name="kvq",
    )(x, modkv, mod, nkv, nm, wkv, wq, kn, qn, eavg)


BAND_QBLOCKS = 2


def _band_attn_kernel(q_ref, kp_ref, kc_ref, vp_ref, vc_ref, bias_ref, o_ref, lse_ref):
    n = pl.program_id(2)
    col = lax.broadcasted_iota(jnp.int32, (N_STEPS, 2 * N_STEPS), 1)
    first = jnp.logical_or(col >= N_STEPS, n > 0)
    lane = lax.broadcasted_iota(jnp.int32, (N_STEPS, LANES), 1)
    k_all = jnp.concatenate([kp_ref[...], kc_ref[...]], axis=0)
    v_all = jnp.concatenate([vp_ref[...], vc_ref[...]], axis=0)
    for j in range(q_ref.shape[0] // N_STEPS):
        rows = slice(j * N_STEPS, (j + 1) * N_STEPS)
        q = q_ref[rows, :]
        k2 = k_all[j * N_STEPS:(j + 2) * N_STEPS]
        v2 = v_all[j * N_STEPS:(j + 2) * N_STEPS]
        lse_all = jnp.zeros((N_STEPS, LANES), F32)
        for h in range(HEADS):
            sl = slice(h * HEAD_DIM, (h + 1) * HEAD_DIM)
            s = lax.dot_general(q[:, sl], k2[:, sl], (((1,), (1,)), ((), ())), preferred_element_type=F32)
            s = s + bias_ref[h]
            if j == 0:
                s = jnp.where(first, s, NEG)
            m = jnp.max(s, axis=1, keepdims=True)
            p = jnp.exp(s - m)
            l = jnp.sum(p, axis=1, keepdims=True)
            o_ref[rows, sl] = jnp.dot(p.astype(BF16), v2[:, sl], preferred_element_type=F32) / l
            lse_all = jnp.where(lane == h, m + jnp.log(l), lse_all)
        lse_ref[rows, :] = lse_all


def _band_attn(q, k, v, bias, dil):
    B, _, Ld, _ = q.shape
    nq = min(BAND_QBLOCKS, Ld // N_STEPS)
    step = nq * N_STEPS
    blk = lambda w, f: pl.BlockSpec((None, None, step, w), f)
    cur = lambda b, r, n: (b, r, n, 0)
    prev_blk = pl.BlockSpec((None, None, N_STEPS, ATTN_WIDTH), lambda b, r, n: (b, r, jnp.maximum(nq * n - 1, 0), 0))
    return pl.pallas_call(
        _band_attn_kernel,
        grid=(B, dil, Ld // step),
        in_specs=[blk(ATTN_WIDTH, cur), prev_blk, blk(ATTN_WIDTH, cur), prev_blk,
                  blk(ATTN_WIDTH, cur), _resident(bias.shape)],
        out_specs=[blk(ATTN_WIDTH, cur), blk(LANES, cur)],
        out_shape=[jax.ShapeDtypeStruct((B, dil, Ld, ATTN_WIDTH), F32),
                   jax.ShapeDtypeStruct((B, dil, Ld, LANES), F32)],
        compiler_params=_cparams(("arbitrary", "arbitrary", "arbitrary")),
        name=f"band_attn_d{dil}",
    )(q, k, k, v, v, bias)


def _rel_buckets(dilation):
    n = np.arange(N_STEPS + 1) * dilation
    large = MAX_EXACT + (np.log(np.maximum(n, 1) / MAX_EXACT) / np.log(REL_MAX_DIST / MAX_EXACT)
                         * (NUM_BUCKETS - MAX_EXACT)).astype(np.int32)
    return np.where(n < MAX_EXACT, n, np.minimum(large, NUM_BUCKETS - 1)).astype(np.int32)


def _step_bias_row(rel_bias, g, dil):
    onehot = np.zeros((N_STEPS + 1, NUM_BUCKETS), np.float32)
    onehot[np.arange(N_STEPS + 1), _rel_buckets(dil)] = 1.0
    return jnp.dot(jnp.asarray(onehot), rel_bias[:, g, :].astype(F32), precision=lax.Precision.HIGHEST).T


def _band_bias(rel_bias, g, dil):
    bias = _step_bias_row(rel_bias, g, dil)
    P = 3 * N_STEPS
    neg = jnp.full((HEADS, N_STEPS), NEG, F32)
    ext = jnp.concatenate([neg, bias[:, ::-1], neg], axis=1)
    flat = jnp.broadcast_to(ext[:, None, :], (HEADS, N_STEPS, P + 1)).reshape(HEADS, N_STEPS * (P + 1))
    skew = flat[:, :N_STEPS * P].reshape(HEADS, N_STEPS, P)
    return skew[:, :, N_STEPS:]


def _step_attn_kernel(q_ref, kvn_ref, c0_ref, c1_ref, c2_ref, b0_ref, b1_ref, b2_ref, bn_ref, o_ref):
    caches = (c0_ref, c1_ref, c2_ref)
    cbias = (b0_ref, b1_ref, b2_ref)
    nt = (((1,), (1,)), ((), ()))
    for h in range(HEADS):
        parts = []
        for g in range(N_BRANCH):
            lo = g * ATTN_WIDTH + h * HEAD_DIM
            klo = 2 * g * ATTN_WIDTH + h * HEAD_DIM
            q = q_ref[0, :, lo:lo + HEAD_DIM].astype(BF16)
            kn = kvn_ref[0, :, klo:klo + HEAD_DIM].astype(BF16)
            vn = kvn_ref[0, :, klo + ATTN_WIDTH:klo + ATTN_WIDTH + HEAD_DIM].astype(BF16)
            kt = caches[g][0, 0, h].astype(BF16)
            vt = caches[g][0, 1, h].astype(BF16)
            sc = jnp.dot(q, kt, preferred_element_type=F32) + cbias[g][h]
            sn = lax.dot_general(q, kn, nt, preferred_element_type=F32) + bn_ref[g, h]
            m = jnp.maximum(jnp.max(sc, axis=1, keepdims=True), jnp.max(sn, axis=1, keepdims=True))
            pc = jnp.exp(sc - m)
            pn = jnp.exp(sn - m)
            l = jnp.sum(pc, axis=1, keepdims=True) + jnp.sum(pn, axis=1, keepdims=True)
            pv = (lax.dot_general(pc.astype(BF16), vt, nt, preferred_element_type=F32)
                  + jnp.dot(pn.astype(BF16), vn, preferred_element_type=F32))
            parts.append((m + jnp.log(l), pv / l))
        mx = jnp.maximum(jnp.maximum(parts[0][0], parts[1][0]), parts[2][0])
        es = [jnp.exp(lse - mx) for lse, _ in parts]
        den = es[0] + es[1] + es[2]
        o_ref[0, :, h * HEAD_DIM:(h + 1) * HEAD_DIM] = (
            es[0] * parts[0][1] + es[1] * parts[1][1] + es[2] * parts[2][1]) / den


def _step_bias(rel_bias, n_new):
    cache_tabs, new_tabs = [], []
    for g, (win, dil) in enumerate(BRANCHES):
        bias = _step_bias_row(rel_bias, g, dil)
        rev = bias[:, ::-1]
        per_s = []
        for s in range(n_new):
            if dil == 1:
                neg = jnp.full((HEADS, s), NEG, F32)
                per_s.append(jnp.concatenate([neg, rev[:, :win - s]], axis=1))
            else:
                cols = [rev[:, :N_STEPS] if r == s % dil else jnp.full((HEADS, N_STEPS), NEG, F32)
                        for r in range(dil)]
                per_s.append(jnp.stack(cols, axis=2).reshape(HEADS, win))
        cache_tabs.append(jnp.stack(per_s, axis=1))
        dist = np.arange(n_new)[:, None] - np.arange(8)[None, :]
        ok = (dist >= 0) & (dist % dil == 0) & (np.arange(8)[None, :] < n_new)
        onehot = np.zeros((N_STEPS + 1, n_new * 8), np.float32)
        onehot[np.where(ok, dist // dil, 0).reshape(-1), np.arange(n_new * 8)] = 1.0
        tab = jnp.dot(bias, jnp.asarray(onehot), precision=lax.Precision.HIGHEST).reshape(HEADS, n_new, 8)
        new_tabs.append(jnp.where(ok[None], tab, NEG))
    return cache_tabs, jnp.stack(new_tabs)


def _step_attn(q, kvn, caches, cache_bias, new_bias, DB, S):
    cspec = lambda c: pl.BlockSpec((1,) + c.shape[1:], lambda b: (b, 0, 0, 0, 0))
    return pl.pallas_call(
        _step_attn_kernel,
        grid=(DB,),
        in_specs=[pl.BlockSpec((1, S, N_BRANCH * ATTN_WIDTH), lambda b: (b, 0, 0)),
                  pl.BlockSpec((1, 8, 2 * N_BRANCH * ATTN_WIDTH), lambda b: (b, 0, 0)),
                  cspec(caches[0]), cspec(caches[1]), cspec(caches[2]),
                  _resident(cache_bias[0].shape), _resident(cache_bias[1].shape), _resident(cache_bias[2].shape),
                  _resident(new_bias.shape)],
        out_specs=pl.BlockSpec((1, S, ATTN_WIDTH), lambda b: (b, 0, 0)),
        out_shape=jax.ShapeDtypeStruct((DB, S, ATTN_WIDTH), F32),
        compiler_params=_cparams(("arbitrary",)),
        name="step_attn",
    )(q, kvn, caches[0], caches[1], caches[2], *cache_bias, new_bias)


def _attn_out_kernel(*refs, dils, tm):
    n_o = max(len(dils), 1)
    n_in = n_o + len(dils) + 9
    (x_ref, mod_ref, wo_ref, ex_ref, nf_ref, wr_ref, br_ref, cin_ref, tri_ref,
     x3_ref, h3_ref, rt_ref, cnt_ref, carry) = refs[n_in - 9:n_in + 5]
    stages = refs[n_in + 5:]

    def natural(ref, stage, dil):
        if dil == 1:
            return ref[0, 0]
        for r in range(dil):
            val = ref[0, r]
            for c in range(stage.shape[0]):
                stage[c, pl.ds(r, tm // dil, stride=dil), :] = val[:, c * LANES:(c + 1) * LANES]
        return jnp.concatenate([stage[c] for c in range(stage.shape[0])], axis=1)

    if dils:
        lses = [natural(refs[n_o + g], stages[2 * g + 1], dil) for g, dil in enumerate(dils)]
        mx = functools.reduce(jnp.maximum, lses)
        es = [jnp.exp(l - mx) for l in lses]
        den = functools.reduce(lambda a, b: a + b, es)
        o = None
        for g, dil in enumerate(dils):
            term = _split_dot(es[g] / den, ex_ref[...]) * natural(refs[g], stages[2 * g], dil)
            o = term if o is None else o + term
    else:
        o = refs[0][...]
    mod = mod_ref[0]
    a = jnp.dot(o.astype(BF16), wo_ref[...], preferred_element_type=F32)
    x3 = x_ref[...] + mod[:, 2 * D_MODEL:3 * D_MODEL] * a
    x3_ref[...] = x3
    _ffn_pre(x3, mod, nf_ref, wr_ref, br_ref, cin_ref, tri_ref, h3_ref, rt_ref, cnt_ref, carry, tm)


def _attn_out(os_, lses, x, mod, tiles_per_mod, tm, wo, ex, nf, wr, br, counts):
    T = x.shape[0]
    tok = lambda w: pl.BlockSpec((tm, w), lambda i: (i, 0))
    r_specs, r_shapes = _route_outs(T, tm)
    tri = _earlier_rows(tm)
    if lses:
        dils = tuple(o.shape[1] for o in os_)
        tpb = tiles_per_mod
        res = lambda a: pl.BlockSpec((1, a.shape[1], tm // a.shape[1], a.shape[3]), lambda i: (i // tpb, 0, i % tpb, 0))
        o_specs = [res(a) for a in os_] + [res(a) for a in lses]
        scratch = []
        for _ in dils:
            scratch += [pltpu.VMEM((ATTN_WIDTH // LANES, tm, LANES), F32), pltpu.VMEM((1, tm, LANES), F32)]
    else:
        dils, o_specs, scratch = (), [tok(ATTN_WIDTH)], []
    return pl.pallas_call(
        functools.partial(_attn_out_kernel, dils=dils, tm=tm),
        grid=(T // tm,),
        in_specs=(o_specs + [tok(D_MODEL), _mod_spec(mod, tiles_per_mod), _resident(wo.shape), _resident(ex.shape),
                             _resident(nf.shape), _resident(wr.shape), _resident(br.shape),
                             _resident(counts.shape), _resident(tri.shape)]),
        out_specs=[tok(D_MODEL), pl.BlockSpec((tm * ROW_TILES, LANES), lambda i: (i, 0))] + r_specs,
        out_shape=[jax.ShapeDtypeStruct((T, D_MODEL), F32),
                   jax.ShapeDtypeStruct((T * ROW_TILES, LANES), F32)] + r_shapes,
        scratch_shapes=[pltpu.VMEM((1, LANES), F32)] + scratch,
        compiler_params=_cparams(("arbitrary",)),
        name="attn_out",
    )(*os_, *lses, x, mod, wo, ex, nf, wr, br, counts, tri)


def kernel(x_prompt, x_sample, cache_kv_w128, cache_kv_w512, cache_kv_w2048, c_prompt, c_sample, ada_w, ada_b, norm_mix, norm_ffn, a_w_in, a_b_in, a_norm_v, a_w_s, a_b_s, a_w_out, kv_ada_w, kv_ada_b, kv_norm, w_kv, k_norm, rel_bias, b_w_q, q_norm, b_w_o, r_w_group, r_b_group, r_w_expert, r_b_expert, e_w1, e_w3, e_w2):
    B, L, _ = x_prompt.shape
    DB, S, _ = x_sample.shape
    Tp, Ts = B * L, DB * S
    tm = TOKEN_TILE
    tpm_p = L // tm

    c_all = jnp.concatenate([c_prompt, c_sample], axis=0)
    R = c_all.shape[0]
    c_all = jnp.pad(c_all, ((0, -R % 8), (0, 0)))
    mods = [_ada(c_all, ada_w, ada_b, l) for l in range(2)]
    modkv = _ada(c_all, kv_ada_w[None], kv_ada_b[None], 0)

    def split_mod(m):
        return m[:B, None, :], jnp.repeat(m[B:B + DB], S, axis=0)[None]
    mod_p, mod_s = zip(*[split_mod(m) for m in mods])
    modkv_p, modkv_s = split_mod(modkv)

    row = lambda a: a.reshape(1, -1)

    def router(l):
        wr = jnp.zeros((D_MODEL, LANES), F32)
        wr = wr.at[:, :N_GROUPS].set(r_w_group[l]).at[:, N_GROUPS:N_GROUPS + N_EXPERTS].set(r_w_expert[l])
        br = jnp.zeros((1, LANES), F32)
        br = br.at[0, :N_GROUPS].set(r_b_group[l]).at[0, N_GROUPS:N_GROUPS + N_EXPERTS].set(r_b_expert[l])
        return wr.astype(BF16), br

    no_counts = jnp.zeros((1, LANES), F32)

    win = a_w_in[0].astype(BF16)
    wout = a_w_out[0].astype(BF16)
    tril = jnp.tril(jnp.ones((CHUNK, CHUNK), bool))
    ws_p = jnp.where(tril, a_w_s[0], 0).astype(BF16)
    bs_p = a_b_s[0].T
    cs = min(CHUNK, S)
    ws_small = jnp.where(jnp.tril(jnp.ones((cs, cs), bool)), a_w_s[0][:, :cs, :cs], 0)
    ws_s = jnp.stack([jnp.kron(jnp.eye(Ts // cs, dtype=F32), ws_small[g]) for g in range(GMLP_GROUPS)]).astype(BF16)
    bs_s = jnp.tile(a_b_s[0][:, :cs], (1, Ts // cs)).T
    wr0, br0 = router(0)
    common = (row(norm_mix[0]), win, row(a_b_in[0]), row(a_norm_v[0]))
    xp = x_prompt.reshape(Tp, D_MODEL)
    xs_ = x_sample.reshape(Ts, D_MODEL)
    x1_p, h2_p, rt_p, cnt = _mixer_a(xp, mod_p[0], tpm_p, tm, *common, ws_p, bs_p, wout, row(norm_ffn[0]), wr0, br0,
                                     no_counts, with_v=False)
    x1_s, h2_s, rt_s, cnt, v_s = _mixer_a(xs_, mod_s[0], 1, Ts, *common, ws_s, bs_s, wout, row(norm_ffn[0]), wr0, br0,
                                          cnt, with_v=True)
    x2_p, x2_s = _moe([(h2_p, rt_p, x1_p, mod_p[0], tpm_p, tm), (h2_s, rt_s, x1_s, mod_s[0], 1, Ts)], cnt,
                      e_w1, e_w3, e_w2, 0)

    wkv = w_kv.astype(BF16)
    wq = b_w_q[0].astype(BF16)
    kn = jnp.tile(k_norm, (1, HEADS))
    qn = jnp.tile(q_norm[0], (1, HEADS))
    head = np.arange(ATTN_WIDTH) // HEAD_DIM
    half_head = head[:ATTN_WIDTH // 2]
    eavg = jnp.asarray((half_head[:, None] == half_head[None, :]) / HEAD_DIM, BF16)
    kvq_w = (row(kv_norm), row(norm_mix[1]), wkv, wq, kn, qn, eavg)
    *qkv_p, kt0, kt1, kt2 = _kvq(x2_p, modkv_p, mod_p[1], tpm_p, tm, *kvq_w, batch=B)
    kv_s, q_s = _kvq(x2_s, modkv_s, mod_s[1], 1, Ts, *kvq_w)

    os_, lses = [], []
    for g, (_, dil) in enumerate(BRANCHES):
        o, lse = _band_attn(*qkv_p[3 * g:3 * g + 3], _band_bias(rel_bias, g, dil), dil)
        os_.append(o)
        lses.append(lse)
    caches = [jnp.transpose(c, (0, 2, 3, 4, 1)) for c in (cache_kv_w128, cache_kv_w512, cache_kv_w2048)]
    kvn_s = jnp.pad(kv_s.reshape(DB, S, -1), ((0, 0), (0, 8 - S), (0, 0)))
    o_s = _step_attn(q_s.reshape(DB, S, -1), kvn_s, caches, *_step_bias(rel_bias, S), DB, S)

    wo = b_w_o[0].astype(BF16)
    ex = jnp.asarray(np.arange(LANES)[:, None] == head[None, :], BF16)
    wr1, br1 = router(1)
    x3_p, h3_p, rt1_p, cnt1 = _attn_out(os_, lses, x2_p, mod_p[1], tpm_p, tm, wo, ex, row(norm_ffn[1]), wr1, br1,
                                        no_counts)
    x3_s, h3_s, rt1_s, cnt1 = _attn_out([o_s.reshape(Ts, ATTN_WIDTH)], [], x2_s, mod_s[1], 1, Ts, wo, ex,
                                        row(norm_ffn[1]), wr1, br1, cnt1)
    y_p, y_s = _moe([(h3_p, rt1_p, x3_p, mod_p[1], tpm_p, tm), (h3_s, rt1_s, x3_s, mod_s[1], 1, Ts)], cnt1,
                    e_w1, e_w3, e_w2, 1)

    kv_s4 = kv_s.reshape(DB, S, N_BRANCH, 2, HEADS, HEAD_DIM)

    def window(kt, w):
        n = min(w, L)
        return jnp.transpose(kt[:, :, kt.shape[2] - n:].reshape(B, 2, HEADS, HEAD_DIM, n), (0, 4, 1, 2, 3))
    return (y_p.reshape(B, L, D_MODEL), y_s.reshape(DB, S, D_MODEL),
            window(kt0, BRANCHES[0][0]), window(kt1, BRANCHES[1][0]), window(kt2, BRANCHES[2][0]),
            kv_s4[:, :, 0], kv_s4[:, :, 1], kv_s4[:, :, 2],
            v_s.reshape(1, DB, S, GMLP_WIDTH))
```

```python
import functools

import numpy as np
import jax
import jax.numpy as jnp
from jax import lax
from jax.experimental import pallas as pl
from jax.experimental.pallas import tpu as pltpu

F32 = jnp.float32
BF16 = jnp.bfloat16

D_MODEL = 1024
GMLP_WIDTH = 2048
GMLP_GROUPS = 4
GROUP_WIDTH = GMLP_WIDTH // GMLP_GROUPS
CHUNK = 128
BRANCHES = ((128, 1), (512, 4), (2048, 16))
N_BRANCH = 3
N_STEPS = 128
HEADS = 8
HEAD_DIM = 64
ATTN_WIDTH = HEADS * HEAD_DIM
NUM_BUCKETS = 32
MAX_EXACT = NUM_BUCKETS // 2
REL_MAX_DIST = 2048
N_GROUPS = 4
EXPERTS_PER_GROUP = 8
N_EXPERTS = N_GROUPS * EXPERTS_PER_GROUP
TOP_K = 2
D_EXPERT = 512
EPS = 1e-6
NEG = -1e30

LANES = 128
ROW_TILES = D_MODEL // LANES
TOKEN_TILE = 512
MOE_BLOCK = 256
DMA_UNROLL = 8
VMEM_LIMIT = 52 * 1024 * 1024


def _cparams(sem):
    return pltpu.CompilerParams(dimension_semantics=sem, vmem_limit_bytes=VMEM_LIMIT)


def _resident(shape):
    nd = len(shape)
    return pl.BlockSpec(shape, lambda *_, _nd=nd: (0,) * _nd, pipeline_mode=pl.Buffered(1))


def _gelu_tanh(x):
    return 0.5 * x * (1.0 + jnp.tanh(0.7978845608028654 * (x + 0.044715 * (x * x * x))))


def _rms(x, g):
    return x * lax.rsqrt(jnp.mean(x * x, axis=-1, keepdims=True) + EPS) * g


def _store_rows8(ref, val, n, base=0):
    for s in range(ROW_TILES):
        ref[pl.ds(base + s, n, stride=ROW_TILES), :] = val[:, s * LANES:(s + 1) * LANES]


def _load_rows8(ref, n, base=0):
    return jnp.concatenate([ref[pl.ds(base + s, n, stride=ROW_TILES), :] for s in range(ROW_TILES)], axis=1)


def _split(a):
    hi = a.astype(BF16)
    return hi, (a - hi.astype(F32)).astype(BF16)


def _split_dot(a, e_bf16):
    hi, lo = _split(a)
    return (jnp.dot(hi, e_bf16, preferred_element_type=F32) + jnp.dot(lo, e_bf16, preferred_element_type=F32))


ROUTE_LANE0 = 4


def _route_rows(l, tri_ref, carry):
    lane = lax.broadcasted_iota(jnp.int32, l.shape, 1).astype(F32)
    far = float(LANES)

    def first_lane(mask):
        return jnp.min(jnp.where(mask, lane, far), axis=1, keepdims=True)

    is_g = lane < N_GROUPS
    gl = jnp.where(is_g, l, NEG)
    gmax = jnp.max(gl, axis=1, keepdims=True)
    g_i = first_lane(jnp.logical_and(gl == gmax, is_g))
    g_p = 1.0 / jnp.sum(jnp.where(is_g, jnp.exp(gl - gmax), 0.0), axis=1, keepdims=True)
    lo = ROUTE_LANE0 + EXPERTS_PER_GROUP * g_i
    sel = jnp.logical_and(lane >= lo, lane < lo + EXPERTS_PER_GROUP)
    el = jnp.where(sel, l, NEG)
    m1 = jnp.max(el, axis=1, keepdims=True)
    i1 = first_lane(jnp.logical_and(el == m1, sel))
    sel2 = jnp.logical_and(sel, lane != i1)
    el2 = jnp.where(sel2, l, NEG)
    m2 = jnp.max(el2, axis=1, keepdims=True)
    i2 = first_lane(jnp.logical_and(el2 == m2, sel2))
    r = jnp.exp(m2 - m1)
    w1 = g_p / (1.0 + r)
    w2 = g_p * r / (1.0 + r)

    hit1 = lane == i1
    hit2 = lane == i2
    onehot = jnp.where(jnp.logical_or(hit1, hit2), 1.0, 0.0)
    before = carry[...] + jnp.dot(tri_ref[...], onehot.astype(BF16), preferred_element_type=F32)
    rank1 = jnp.sum(jnp.where(hit1, before, 0.0), axis=1, keepdims=True)
    rank2 = jnp.sum(jnp.where(hit2, before, 0.0), axis=1, keepdims=True)
    carry[...] = carry[...] + jnp.sum(onehot, axis=0, keepdims=True)

    out = jnp.zeros(l.shape, F32)
    for k, val in enumerate((i1 - ROUTE_LANE0, i2 - ROUTE_LANE0, w1, w2, rank1, rank2)):
        out = jnp.where(lane == k, val, out)
    return out


def _ffn_pre(x, mod, nf_ref, wr_ref, br_ref, cin_ref, tri_ref, h_ref, rt_ref, cnt_ref, carry, n):
    @pl.when(pl.program_id(0) == 0)
    def _():
        carry[...] = cin_ref[...]

    h = _rms(x, nf_ref[...]) * (1.0 + mod[:, 4 * D_MODEL:5 * D_MODEL]) + mod[:, 3 * D_MODEL:4 * D_MODEL]
    _store_rows8(h_ref, h, n)
    logits = jnp.dot(h.astype(BF16), wr_ref[...], preferred_element_type=F32) + br_ref[...]
    rt_ref[...] = _route_rows(logits, tri_ref, carry)
    cnt_ref[...] = carry[...]


def _ada_kernel(c_ref, w_ref, b_ref, o_ref):
    c = c_ref[...]
    a = (c * jax.nn.sigmoid(c)).astype(BF16)
    o_ref[...] = jnp.dot(a, w_ref[...].astype(BF16), preferred_element_type=F32) + b_ref[...]


def _ada(c, w, b, layer):
    R = c.shape[0]
    N = w.shape[2]
    tn = 1024
    return pl.pallas_call(
        _ada_kernel,
        grid=(N // tn,),
        in_specs=[pl.BlockSpec((R, D_MODEL), lambda j: (0, 0)),
                  pl.BlockSpec((None, D_MODEL, tn), lambda j: (layer, 0, j)),
                  pl.BlockSpec((None, 1, tn), lambda j: (layer, 0, j))],
        out_specs=pl.BlockSpec((R, tn), lambda j: (0, j)),
        out_shape=jax.ShapeDtypeStruct((R, N), F32),
        compiler_params=_cparams(("arbitrary",)),
        name="ada",
    )(c, w, b.reshape(b.shape[0], 1, N))


def _mixer_a_kernel(x_ref, mod_ref, nm_ref, win_ref, bin_ref, gv_ref, ws_ref, bs_ref, wout_ref,
                    nf_ref, wr_ref, br_ref, cin_ref, tri_ref, x1_ref, h2_ref, rt_ref, cnt_ref, *rest, tm):
    v_refs, carry = rest[:-1], rest[-1]
    x = x_ref[...]
    mod = mod_ref[0]
    h = (_rms(x, nm_ref[...]) * (1.0 + mod[:, D_MODEL:2 * D_MODEL]) + mod[:, 0:D_MODEL]).astype(BF16)
    zv = jnp.dot(h, win_ref[:, GMLP_WIDTH:], preferred_element_type=F32) + bin_ref[:, GMLP_WIDTH:]
    v = _rms(_gelu_tanh(zv), gv_ref[...])
    if v_refs:
        v_refs[0][...] = v
    vb = v.astype(BF16)
    bs = bs_ref[...]
    acc = jnp.zeros((tm, D_MODEL), F32)
    for g in range(GMLP_GROUPS):
        lo, hi = g * GROUP_WIDTH, (g + 1) * GROUP_WIDTH
        u = _gelu_tanh(jnp.dot(h, win_ref[:, lo:hi], preferred_element_type=F32) + bin_ref[:, lo:hi])
        wg = ws_ref[g]
        gate = jnp.concatenate(
            [jnp.dot(wg, vb[c * CHUNK:(c + 1) * CHUNK, lo:hi], preferred_element_type=F32) + bs[:, g:g + 1]
             for c in range(tm // CHUNK)], axis=0)
        acc = acc + jnp.dot((u * gate).astype(BF16), wout_ref[lo:hi, :], preferred_element_type=F32)
    x1 = x + mod[:, 2 * D_MODEL:3 * D_MODEL] * acc
    x1_ref[...] = x1
    _ffn_pre(x1, mod, nf_ref, wr_ref, br_ref, cin_ref, tri_ref, h2_ref, rt_ref, cnt_ref, carry, tm)


def _mod_spec(mod, tiles_per_mod):
    _, rows, width = mod.shape
    return pl.BlockSpec((1, rows, width), lambda i: (i // tiles_per_mod, 0, 0))


def _earlier_rows(tm):
    return jnp.asarray(np.tril(np.ones((tm, tm), np.float32), -1), BF16)


def _route_outs(T, tm):
    return ([pl.BlockSpec((tm, LANES), lambda i: (i, 0)), pl.BlockSpec((1, LANES), lambda i: (0, 0))],
            [jax.ShapeDtypeStruct((T, LANES), F32), jax.ShapeDtypeStruct((1, LANES), F32)])


def _mixer_a(x, mod, tiles_per_mod, tm, nm, win, bin_, gv, ws, bs_t, wout, nf, wr, br, counts, with_v):
    T = x.shape[0]
    tok = lambda w: pl.BlockSpec((tm, w), lambda i: (i, 0))
    r_specs, r_shapes = _route_outs(T, tm)
    out_shape = [jax.ShapeDtypeStruct((T, D_MODEL), F32), jax.ShapeDtypeStruct((T * ROW_TILES, LANES), F32)] + r_shapes
    out_specs = [tok(D_MODEL), pl.BlockSpec((tm * ROW_TILES, LANES), lambda i: (i, 0))] + r_specs
    if with_v:
        out_shape.append(jax.ShapeDtypeStruct((T, GMLP_WIDTH), F32))
        out_specs.append(tok(GMLP_WIDTH))
    tri = _earlier_rows(tm)
    return pl.pallas_call(
        functools.partial(_mixer_a_kernel, tm=tm),
        grid=(T // tm,),
        in_specs=[tok(D_MODEL), _mod_spec(mod, tiles_per_mod), _resident(nm.shape), _resident(win.shape),
                  _resident(bin_.shape), _resident(gv.shape), _resident(ws.shape), _resident(bs_t.shape),
                  _resident(wout.shape), _resident(nf.shape), _resident(wr.shape), _resident(br.shape),
                  _resident(counts.shape), _resident(tri.shape)],
        out_specs=out_specs,
        out_shape=out_shape,
        scratch_shapes=[pltpu.VMEM((1, LANES), F32)],
        compiler_params=_cparams(("arbitrary",)),
        name="mixer_a",
    )(x, mod, nm, win, bin_, gv, ws, bs_t, wout, nf, wr, br, counts, tri)


def _layout(route, counts, block, n_tokens):
    counts = counts[0, ROUTE_LANE0:ROUTE_LANE0 + N_EXPERTS].astype(jnp.int32)
    padded = (counts + block - 1) // block * block
    pad_end = jnp.cumsum(padded)
    pad_start = (pad_end - padded).astype(F32)
    experts = route[:, 0:TOP_K]
    onehot = experts[:, :, None] == jnp.arange(N_EXPERTS, dtype=F32)[None, None, :]
    slots = (route[:, 4:4 + TOP_K] + jnp.sum(jnp.where(onehot, pad_start[None, None, :], 0.0), axis=-1)).astype(jnp.int32)
    nb = -(-n_tokens * TOP_K // block) + N_EXPERTS
    blk_e = jnp.minimum(jnp.sum(pad_end[None, :] <= (jnp.arange(nb, dtype=jnp.int32) * block)[:, None], axis=1),
                        N_EXPERTS - 1).astype(jnp.int32)
    n_used = (pad_end[-1] // block).astype(jnp.int32).reshape(1)
    starts = jnp.concatenate([jnp.ones((1,), jnp.int32), (blk_e[1:] != blk_e[:-1]).astype(jnp.int32)])
    grp = jnp.cumsum(starts) - 1
    end_blk = (pad_end // block).astype(jnp.int32)
    own = blk_e[:, None] == jnp.arange(N_EXPERTS, dtype=jnp.int32)[None, :]
    nxt_blk = jnp.sum(jnp.where(own, end_blk[None, :], 0), axis=1)
    at_nxt = nxt_blk[:, None] == jnp.arange(nb, dtype=jnp.int32)[None, :]
    nxt_e = jnp.where(nxt_blk < n_used[0], jnp.sum(jnp.where(at_nxt, blk_e[None, :], 0), axis=1), -1).astype(jnp.int32)
    return slots, route[:, 2:2 + TOP_K], (blk_e, n_used, nxt_e, grp.astype(jnp.int32)), nb


def _row_copy(src_ref, src_row, dst_ref, dst_row, sem):
    return pltpu.make_async_copy(
        src_ref.at[pl.ds(pl.multiple_of(src_row * ROW_TILES, ROW_TILES), ROW_TILES), :],
        dst_ref.at[pl.ds(pl.multiple_of(dst_row * ROW_TILES, ROW_TILES), ROW_TILES), :], sem)


def _scatter_kernel(slot_ref, src_ref, dst_in_ref, dst_ref, sem, *, ts):
    del dst_in_ref

    def issue(j, c):
        for u in range(DMA_UNROLL):
            t = j * DMA_UNROLL + u
            for k in range(TOP_K):
                _row_copy(src_ref, t, dst_ref, slot_ref[0, 0, TOP_K * t + k], sem).start(priority=k)
        return c
    lax.fori_loop(0, ts // DMA_UNROLL, issue, 0)

    for _ in range(TOP_K):
        pltpu.make_async_copy(src_ref, dst_ref.at[pl.ds(0, ts * ROW_TILES), :], sem).wait()


def _scatter(slots, src, dst, ts):
    T = slots.shape[0]
    return pl.pallas_call(
        functools.partial(_scatter_kernel, ts=ts),
        grid=(T // ts,),
        in_specs=[pl.BlockSpec((1, 1, ts * TOP_K), lambda i: (i, 0, 0), memory_space=pltpu.SMEM),
                  pl.BlockSpec((ts * ROW_TILES, LANES), lambda i: (i, 0)),
                  pl.BlockSpec(memory_space=pl.ANY)],
        out_specs=pl.BlockSpec(memory_space=pl.ANY),
        out_shape=jax.ShapeDtypeStruct(dst.shape, dst.dtype),
        scratch_shapes=[pltpu.SemaphoreType.DMA(())],
        input_output_aliases={2: 0},
        compiler_params=_cparams(("arbitrary",)),
        name="moe_scatter",
    )(slots.reshape(T // ts, 1, ts * TOP_K), src, dst)


def _experts_kernel(be_ref, nu_ref, nxt_ref, grp_ref, x_ref, w1_hbm, w3_hbm, w2_hbm, y_ref,
                    w1b, w3b, w2b, wf1, wf3, wf2, sem, *, block, layer):
    i = pl.program_id(0)

    def weight_copies(e, buf):
        return [pltpu.make_async_copy(src.at[layer, e], dst.at[buf], sem.at[buf])
                for src, dst in ((w1_hbm, wf1), (w3_hbm, wf3), (w2_hbm, wf2))]

    @pl.when(i == 0)
    def _():
        for c in weight_copies(be_ref[0], 0):
            c.start()

    @pl.when(i < nu_ref[0])
    def _():
        @pl.when(jnp.logical_or(i == 0, be_ref[i] != be_ref[jnp.maximum(i - 1, 0)]))
        def _():
            buf = grp_ref[i] % 2
            for c in weight_copies(be_ref[i], buf):
                c.wait()

            @pl.when(nxt_ref[i] >= 0)
            def _():
                for c in weight_copies(nxt_ref[i], 1 - buf):
                    c.start()

            w1b[...] = wf1[buf].astype(BF16)
            w3b[...] = wf3[buf].astype(BF16)
            w2b[...] = wf2[buf].astype(BF16)

        x = _load_rows8(x_ref, block).astype(BF16)
        h1 = jnp.dot(x, w1b[...], preferred_element_type=F32)
        h3 = jnp.dot(x, w3b[...], preferred_element_type=F32)
        a = (h1 * jax.nn.sigmoid(h1) * h3).astype(BF16)
        _store_rows8(y_ref, jnp.dot(a, w2b[...], preferred_element_type=F32), block)


def _experts(xs, plan, nb, w1, w3, w2, layer, block):
    rows = pl.BlockSpec((block * ROW_TILES, LANES), lambda i, be, nu, nxt, grp: (jnp.minimum(i, nu[0] - 1), 0))
    hbm = pl.BlockSpec(memory_space=pl.ANY)
    return pl.pallas_call(
        functools.partial(_experts_kernel, block=block, layer=layer),
        grid_spec=pltpu.PrefetchScalarGridSpec(
            num_scalar_prefetch=4,
            grid=(nb,),
            in_specs=[rows, hbm, hbm, hbm],
            out_specs=rows,
            scratch_shapes=[pltpu.VMEM((D_MODEL, D_EXPERT), BF16), pltpu.VMEM((D_MODEL, D_EXPERT), BF16),
                            pltpu.VMEM((D_EXPERT, D_MODEL), BF16),
                            pltpu.VMEM((2, D_MODEL, D_EXPERT), F32), pltpu.VMEM((2, D_MODEL, D_EXPERT), F32),
                            pltpu.VMEM((2, D_EXPERT, D_MODEL), F32), pltpu.SemaphoreType.DMA((2,))]),
        out_shape=jax.ShapeDtypeStruct(xs.shape, F32),
        input_output_aliases={4: 0},
        compiler_params=_cparams(("arbitrary",)),
        name="moe_experts",
    )(*plan, xs, w1, w3, w2)


def _combine_kernel(slot_ref, y_ref, x_ref, mod_ref, wt_ref, o_ref, ybuf, sem, *, tc):
    def issue(j, c):
        for u in range(DMA_UNROLL):
            t = j * DMA_UNROLL + u
            for k in range(TOP_K):
                _row_copy(y_ref, slot_ref[0, 0, TOP_K * t + k], ybuf, k * tc + t, sem).start(priority=k)
        return c
    lax.fori_loop(0, tc // DMA_UNROLL, issue, 0)
    pltpu.make_async_copy(y_ref.at[pl.ds(0, TOP_K * tc * ROW_TILES), :], ybuf, sem).wait()

    wt = wt_ref[...]
    y = (wt[:, 0:1] * _load_rows8(ybuf, tc) + wt[:, 1:2] * _load_rows8(ybuf, tc, base=tc * ROW_TILES))
    o_ref[...] = x_ref[...] + mod_ref[0][:, 5 * D_MODEL:6 * D_MODEL] * y


def _combine(slots, y, x, mod, tiles_per_mod, wts, tc):
    T = x.shape[0]
    return pl.pallas_call(
        functools.partial(_combine_kernel, tc=tc),
        grid=(T // tc,),
        in_specs=[pl.BlockSpec((1, 1, tc * TOP_K), lambda i: (i, 0, 0), memory_space=pltpu.SMEM),
                  pl.BlockSpec(memory_space=pl.ANY),
                  pl.BlockSpec((tc, D_MODEL), lambda i: (i, 0)),
                  _mod_spec(mod, tiles_per_mod),
                  pl.BlockSpec((tc, TOP_K), lambda i: (i, 0))],
        out_specs=pl.BlockSpec((tc, D_MODEL), lambda i: (i, 0)),
        out_shape=jax.ShapeDtypeStruct((T, D_MODEL), F32),
        scratch_shapes=[pltpu.VMEM((TOP_K * tc * ROW_TILES, LANES), F32), pltpu.SemaphoreType.DMA(())],
        compiler_params=_cparams(("arbitrary",)),
        name="moe_combine",
    )(slots.reshape(T // tc, 1, tc * TOP_K), y, x, mod, wts)


def _moe(parts, counts, w1, w3, w2, layer):
    n_tokens = sum(p[2].shape[0] for p in parts)
    slots, wts, plan, nb = _layout(jnp.concatenate([p[1] for p in parts], axis=0), counts, MOE_BLOCK, n_tokens)
    xs = jnp.zeros((nb * MOE_BLOCK * ROW_TILES, LANES), F32)
    off = 0
    for h, _, x, _, _, tt in parts:
        T = x.shape[0]
        xs = _scatter(slots[off:off + T], h, xs, tt)
        off += T
    y = _experts(xs, plan, nb, w1, w3, w2, layer, MOE_BLOCK)
    outs, off = [], 0
    for _, _, x, mod, tpm, tt in parts:
        T = x.shape[0]
        outs.append(_combine(slots[off:off + T], y, x, mod, tpm, wts[off:off + T], tt))
        off += T
    return outs


def _kvq_kernel(x_ref, modkv_ref, mod_ref, nkv_ref, nm_ref, wkv_ref, wq_ref, kn_ref, qn_ref, eavg_ref,
                *rest, tm, keep):
    residue = keep is not None
    x = x_ref[...]
    xn = x * lax.rsqrt(jnp.mean(x * x, axis=-1, keepdims=True) + EPS)
    modkv = modkv_ref[0]
    mod = mod_ref[0]
    hk = (xn * nkv_ref[...] * (1.0 + modkv[:, D_MODEL:]) + modkv[:, :D_MODEL]).astype(BF16)
    hq = (xn * nm_ref[...] * (1.0 + mod[:, D_MODEL:2 * D_MODEL]) + mod[:, :D_MODEL]).astype(BF16)
    eavg = eavg_ref[...]

    def head_norm(a, g):
        sq = (a * a).astype(BF16)
        half = ATTN_WIDTH // 2
        ms = jnp.concatenate([jnp.dot(sq[:, :half], eavg, preferred_element_type=F32),
                              jnp.dot(sq[:, half:], eavg, preferred_element_type=F32)], axis=1)
        return a * lax.rsqrt(ms + EPS) * g

    def residue_major(val, ref, dil):
        if dil == 1:
            ref[0, 0] = val.astype(BF16)
            return
        stage = rest[-1]
        for c in range(stage.shape[0]):
            stage[c] = val[:, c * LANES:(c + 1) * LANES]
        for r in range(dil):
            rows = [stage[c, pl.ds(r, tm // dil, stride=dil), :] for c in range(stage.shape[0])]
            ref[0, r] = jnp.concatenate(rows, axis=1).astype(BF16)

    for g, (_, dil) in enumerate(BRANCHES):
        k0 = 2 * g * ATTN_WIDTH
        kvg = jnp.dot(hk, wkv_ref[:, k0:k0 + 2 * ATTN_WIDTH], preferred_element_type=F32)
        kn = head_norm(kvg[:, :ATTN_WIDTH], kn_ref[g:g + 1, :])
        vv = kvg[:, ATTN_WIDTH:]
        q0 = g * ATTN_WIDTH
        qg = jnp.dot(hq, wq_ref[:, q0:q0 + ATTN_WIDTH], preferred_element_type=F32)
        qg = head_norm(qg, qn_ref[g:g + 1, :]) * (HEAD_DIM ** -0.5)
        if residue:
            residue_major(qg, rest[3 * g], dil)
            residue_major(kn, rest[3 * g + 1], dil)
            residue_major(vv, rest[3 * g + 2], dil)
            tiles_per_batch, kept = keep

            @pl.when(pl.program_id(0) % tiles_per_batch >= tiles_per_batch - kept[g])
            def _(kn=kn, vv=vv, kt_ref=rest[3 * N_BRANCH + g]):
                kt_ref[0, :ATTN_WIDTH, :] = kn.T
                kt_ref[0, ATTN_WIDTH:, :] = vv.T
        else:
            rest[0][:, k0:k0 + ATTN_WIDTH] = kn
            rest[0][:, k0 + ATTN_WIDTH:k0 + 2 * ATTN_WIDTH] = vv
            rest[1][:, q0:q0 + ATTN_WIDTH] = qg


def _kvq(x, modkv, mod, tiles_per_mod, tm, nkv, nm, wkv, wq, kn, qn, eavg, batch=None):
    T = x.shape[0]
    tok = lambda w: pl.BlockSpec((tm, w), lambda i: (i, 0))
    kvw = 2 * N_BRANCH * ATTN_WIDTH
    if batch is not None:
        L = T // batch
        tpb = L // tm
        kept = tuple(-(-min(w, L) // tm) for w, _ in BRANCHES)
        keep = (tpb, kept)
        out_specs, out_shape = [], []
        for _, dil in BRANCHES:
            for _ in range(3):
                out_specs.append(pl.BlockSpec((1, dil, tm // dil, ATTN_WIDTH), lambda i: (i // tpb, 0, i % tpb, 0)))
                out_shape.append(jax.ShapeDtypeStruct((batch, dil, L // dil, ATTN_WIDTH), BF16))
        for kg in kept:
            out_specs.append(pl.BlockSpec((1, 2 * ATTN_WIDTH, tm),
                                          lambda i, kg=kg: (i // tpb, 0, jnp.maximum(i % tpb - (tpb - kg), 0))))
            out_shape.append(jax.ShapeDtypeStruct((batch, 2 * ATTN_WIDTH, kg * tm), F32))
        scratch = [pltpu.VMEM((ATTN_WIDTH // LANES, tm, LANES), F32)]
    else:
        keep = None
        out_specs = [tok(kvw), tok(N_BRANCH * ATTN_WIDTH)]
        out_shape = [jax.ShapeDtypeStruct((T, kvw), F32), jax.ShapeDtypeStruct((T, N_BRANCH * ATTN_WIDTH), F32)]
        scratch = []
    return pl.pallas_call(
        functools.partial(_kvq_kernel, tm=tm, keep=keep),
        grid=(T // tm,),
        in_specs=[tok(D_MODEL), _mod_spec(modkv, tiles_per_mod), _mod_spec(mod, tiles_per_mod),
                  _resident(nkv.shape), _resident(nm.shape), _resident(wkv.shape), _resident(wq.shape),
                  _resident(kn.shape), _resident(qn.shape), _resident(eavg.shape)],
        out_specs=out_specs,
        out_shape=out_shape,
        scratch_shapes=scratch,
        compiler_params=_cparams(("arbitrary",)),
        name="kvq",
    )(x, modkv, mod, nkv, nm, wkv, wq, kn, qn, eavg)


BAND_QBLOCKS = 2


def _band_attn_kernel(q_ref, kp_ref, kc_ref, vp_ref, vc_ref, bias_ref, o_ref, lse_ref):
    n = pl.program_id(2)
    col = lax.broadcasted_iota(jnp.int32, (N_STEPS, 2 * N_STEPS), 1)
    first = jnp.logical_or(col >= N_STEPS, n > 0)
    lane = lax.broadcasted_iota(jnp.int32, (N_STEPS, LANES), 1)
    k_all = jnp.concatenate([kp_ref[...], kc_ref[...]], axis=0)
    v_all = jnp.concatenate([vp_ref[...], vc_ref[...]], axis=0)
    for j in range(q_ref.shape[0] // N_STEPS):
        rows = slice(j * N_STEPS, (j + 1) * N_STEPS)
        q = q_ref[rows, :]
        k2 = k_all[j * N_STEPS:(j + 2) * N_STEPS]
        v2 = v_all[j * N_STEPS:(j + 2) * N_STEPS]
        lse_all = jnp.zeros((N_STEPS, LANES), F32)
        for h in range(HEADS):
            sl = slice(h * HEAD_DIM, (h + 1) * HEAD_DIM)
            s = lax.dot_general(q[:, sl], k2[:, sl], (((1,), (1,)), ((), ())), preferred_element_type=F32)
            s = s + bias_ref[h]
            if j == 0:
                s = jnp.where(first, s, NEG)
            m = jnp.max(s, axis=1, keepdims=True)
            p = jnp.exp(s - m)
            l = jnp.sum(p, axis=1, keepdims=True)
            o_ref[rows, sl] = jnp.dot(p.astype(BF16), v2[:, sl], preferred_element_type=F32) / l
            lse_all = jnp.where(lane == h, m + jnp.log(l), lse_all)
        lse_ref[rows, :] = lse_all


def _band_attn(q, k, v, bias, dil):
    B, _, Ld, _ = q.shape
    nq = min(BAND_QBLOCKS, Ld // N_STEPS)
    step = nq * N_STEPS
    blk = lambda w, f: pl.BlockSpec((None, None, step, w), f)
    cur = lambda b, r, n: (b, r, n, 0)
    prev_blk = pl.BlockSpec((None, None, N_STEPS, ATTN_WIDTH), lambda b, r, n: (b, r, jnp.maximum(nq * n - 1, 0), 0))
    return pl.pallas_call(
        _band_attn_kernel,
        grid=(B, dil, Ld // step),
        in_specs=[blk(ATTN_WIDTH, cur), prev_blk, blk(ATTN_WIDTH, cur), prev_blk,
                  blk(ATTN_WIDTH, cur), _resident(bias.shape)],
        out_specs=[blk(ATTN_WIDTH, cur), blk(LANES, cur)],
        out_shape=[jax.ShapeDtypeStruct((B, dil, Ld, ATTN_WIDTH), F32),
                   jax.ShapeDtypeStruct((B, dil, Ld, LANES), F32)],
        compiler_params=_cparams(("arbitrary", "arbitrary", "arbitrary")),
        name=f"band_attn_d{dil}",
    )(q, k, k, v, v, bias)


def _rel_buckets(dilation):
    n = np.arange(N_STEPS + 1) * dilation
    large = MAX_EXACT + (np.log(np.maximum(n, 1) / MAX_EXACT) / np.log(REL_MAX_DIST / MAX_EXACT)
                         * (NUM_BUCKETS - MAX_EXACT)).astype(np.int32)
    return np.where(n < MAX_EXACT, n, np.minimum(large, NUM_BUCKETS - 1)).astype(np.int32)


def _step_bias_row(rel_bias, g, dil):
    onehot = np.zeros((N_STEPS + 1, NUM_BUCKETS), np.float32)
    onehot[np.arange(N_STEPS + 1), _rel_buckets(dil)] = 1.0
    return jnp.dot(jnp.asarray(onehot), rel_bias[:, g, :].astype(F32), precision=lax.Precision.HIGHEST).T


def _band_bias(rel_bias, g, dil):
    bias = _step_bias_row(rel_bias, g, dil)
    P = 3 * N_STEPS
    neg = jnp.full((HEADS, N_STEPS), NEG, F32)
    ext = jnp.concatenate([neg, bias[:, ::-1], neg], axis=1)
    flat = jnp.broadcast_to(ext[:, None, :], (HEADS, N_STEPS, P + 1)).reshape(HEADS, N_STEPS * (P + 1))
    skew = flat[:, :N_STEPS * P].reshape(HEADS, N_STEPS, P)
    return skew[:, :, N_STEPS:]


def _step_attn_kernel(q_ref, kvn_ref, c0_ref, c1_ref, c2_ref, b0_ref, b1_ref, b2_ref, bn_ref, o_ref):
    caches = (c0_ref, c1_ref, c2_ref)
    cbias = (b0_ref, b1_ref, b2_ref)
    nt = (((1,), (1,)), ((), ()))
    for h in range(HEADS):
        parts = []
        for g in range(N_BRANCH):
            lo = g * ATTN_WIDTH + h * HEAD_DIM
            klo = 2 * g * ATTN_WIDTH + h * HEAD_DIM
            q = q_ref[0, :, lo:lo + HEAD_DIM].astype(BF16)
            kn = kvn_ref[0, :, klo:klo + HEAD_DIM].astype(BF16)
            vn = kvn_ref[0, :, klo + ATTN_WIDTH:klo + ATTN_WIDTH + HEAD_DIM].astype(BF16)
            kt = caches[g][0, 0, h].astype(BF16)
            vt = caches[g][0, 1, h].astype(BF16)
            sc = jnp.dot(q, kt, preferred_element_type=F32) + cbias[g][h]
            sn = lax.dot_general(q, kn, nt, preferred_element_type=F32) + bn_ref[g, h]
            m = jnp.maximum(jnp.max(sc, axis=1, keepdims=True), jnp.max(sn, axis=1, keepdims=True))
            pc = jnp.exp(sc - m)
            pn = jnp.exp(sn - m)
            l = jnp.sum(pc, axis=1, keepdims=True) + jnp.sum(pn, axis=1, keepdims=True)
            pv = (lax.dot_general(pc.astype(BF16), vt, nt, preferred_element_type=F32)
                  + jnp.dot(pn.astype(BF16), vn, preferred_element_type=F32))
            parts.append((m + jnp.log(l), pv / l))
        mx = jnp.maximum(jnp.maximum(parts[0][0], parts[1][0]), parts[2][0])
        es = [jnp.exp(lse - mx) for lse, _ in parts]
        den = es[0] + es[1] + es[2]
        o_ref[0, :, h * HEAD_DIM:(h + 1) * HEAD_DIM] = (
            es[0] * parts[0][1] + es[1] * parts[1][1] + es[2] * parts[2][1]) / den


def _step_bias(rel_bias, n_new):
    cache_tabs, new_tabs = [], []
    for g, (win, dil) in enumerate(BRANCHES):
        bias = _step_bias_row(rel_bias, g, dil)
        rev = bias[:, ::-1]
        per_s = []
        for s in range(n_new):
            if dil == 1:
                neg = jnp.full((HEADS, s), NEG, F32)
                per_s.append(jnp.concatenate([neg, rev[:, :win - s]], axis=1))
            else:
                cols = [rev[:, :N_STEPS] if r == s % dil else jnp.full((HEADS, N_STEPS), NEG, F32)
                        for r in range(dil)]
                per_s.append(jnp.stack(cols, axis=2).reshape(HEADS, win))
        cache_tabs.append(jnp.stack(per_s, axis=1))
        dist = np.arange(n_new)[:, None] - np.arange(8)[None, :]
        ok = (dist >= 0) & (dist % dil == 0) & (np.arange(8)[None, :] < n_new)
        onehot = np.zeros((N_STEPS + 1, n_new * 8), np.float32)
        onehot[np.where(ok, dist // dil, 0).reshape(-1), np.arange(n_new * 8)] = 1.0
        tab = jnp.dot(bias, jnp.asarray(onehot), precision=lax.Precision.HIGHEST).reshape(HEADS, n_new, 8)
        new_tabs.append(jnp.where(ok[None], tab, NEG))
    return cache_tabs, jnp.stack(new_tabs)


def _step_attn(q, kvn, caches, cache_bias, new_bias, DB, S):
    cspec = lambda c: pl.BlockSpec((1,) + c.shape[1:], lambda b: (b, 0, 0, 0, 0))
    return pl.pallas_call(
        _step_attn_kernel,
        grid=(DB,),
        in_specs=[pl.BlockSpec((1, S, N_BRANCH * ATTN_WIDTH), lambda b: (b, 0, 0)),
                  pl.BlockSpec((1, 8, 2 * N_BRANCH * ATTN_WIDTH), lambda b: (b, 0, 0)),
                  cspec(caches[0]), cspec(caches[1]), cspec(caches[2]),
                  _resident(cache_bias[0].shape), _resident(cache_bias[1].shape), _resident(cache_bias[2].shape),
                  _resident(new_bias.shape)],
        out_specs=pl.BlockSpec((1, S, ATTN_WIDTH), lambda b: (b, 0, 0)),
        out_shape=jax.ShapeDtypeStruct((DB, S, ATTN_WIDTH), F32),
        compiler_params=_cparams(("arbitrary",)),
        name="step_attn",
    )(q, kvn, caches[0], caches[1], caches[2], *cache_bias, new_bias)


def _attn_out_kernel(*refs, dils, tm):
    n_o = max(len(dils), 1)
    n_in = n_o + len(dils) + 9
    (x_ref, mod_ref, wo_ref, ex_ref, nf_ref, wr_ref, br_ref, cin_ref, tri_ref,
     x3_ref, h3_ref, rt_ref, cnt_ref, carry) = refs[n_in - 9:n_in + 5]
    stages = refs[n_in + 5:]

    def natural(ref, stage, dil):
        if dil == 1:
            return ref[0, 0]
        for r in range(dil):
            val = ref[0, r]
            for c in range(stage.shape[0]):
                stage[c, pl.ds(r, tm // dil, stride=dil), :] = val[:, c * LANES:(c + 1) * LANES]
        return jnp.concatenate([stage[c] for c in range(stage.shape[0])], axis=1)

    if dils:
        lses = [natural(refs[n_o + g], stages[2 * g + 1], dil) for g, dil in enumerate(dils)]
        mx = functools.reduce(jnp.maximum, lses)
        es = [jnp.exp(l - mx) for l in lses]
        den = functools.reduce(lambda a, b: a + b, es)
        o = None
        for g, dil in enumerate(dils):
            term = _split_dot(es[g] / den, ex_ref[...]) * natural(refs[g], stages[2 * g], dil)
            o = term if o is None else o + term
    else:
        o = refs[0][...]
    mod = mod_ref[0]
    a = jnp.dot(o.astype(BF16), wo_ref[...], preferred_element_type=F32)
    x3 = x_ref[...] + mod[:, 2 * D_MODEL:3 * D_MODEL] * a
    x3_ref[...] = x3
    _ffn_pre(x3, mod, nf_ref, wr_ref, br_ref, cin_ref, tri_ref, h3_ref, rt_ref, cnt_ref, carry, tm)


def _attn_out(os_, lses, x, mod, tiles_per_mod, tm, wo, ex, nf, wr, br, counts):
    T = x.shape[0]
    tok = lambda w: pl.BlockSpec((tm, w), lambda i: (i, 0))
    r_specs, r_shapes = _route_outs(T, tm)
    tri = _earlier_rows(tm)
    if lses:
        dils = tuple(o.shape[1] for o in os_)
        tpb = tiles_per_mod
        res = lambda a: pl.BlockSpec((1, a.shape[1], tm // a.shape[1], a.shape[3]), lambda i: (i // tpb, 0, i % tpb, 0))
        o_specs = [res(a) for a in os_] + [res(a) for a in lses]
        scratch = []
        for _ in dils:
            scratch += [pltpu.VMEM((ATTN_WIDTH // LANES, tm, LANES), F32), pltpu.VMEM((1, tm, LANES), F32)]
    else:
        dils, o_specs, scratch = (), [tok(ATTN_WIDTH)], []
    return pl.pallas_call(
        functools.partial(_attn_out_kernel, dils=dils, tm=tm),
        grid=(T // tm,),
        in_specs=(o_specs + [tok(D_MODEL), _mod_spec(mod, tiles_per_mod), _resident(wo.shape), _resident(ex.shape),
                             _resident(nf.shape), _resident(wr.shape), _resident(br.shape),
                             _resident(counts.shape), _resident(tri.shape)]),
        out_specs=[tok(D_MODEL), pl.BlockSpec((tm * ROW_TILES, LANES), lambda i: (i, 0))] + r_specs,
        out_shape=[jax.ShapeDtypeStruct((T, D_MODEL), F32),
                   jax.ShapeDtypeStruct((T * ROW_TILES, LANES), F32)] + r_shapes,
        scratch_shapes=[pltpu.VMEM((1, LANES), F32)] + scratch,
        compiler_params=_cparams(("arbitrary",)),
        name="attn_out",
    )(*os_, *lses, x, mod, wo, ex, nf, wr, br, counts, tri)


def kernel(x_prompt, x_sample, cache_kv_w128, cache_kv_w512, cache_kv_w2048, c_prompt, c_sample, ada_w, ada_b, norm_mix, norm_ffn, a_w_in, a_b_in, a_norm_v, a_w_s, a_b_s, a_w_out, kv_ada_w, kv_ada_b, kv_norm, w_kv, k_norm, rel_bias, b_w_q, q_norm, b_w_o, r_w_group, r_b_group, r_w_expert, r_b_expert, e_w1, e_w3, e_w2):
    B, L, _ = x_prompt.shape
    DB, S, _ = x_sample.shape
    Tp, Ts = B * L, DB * S
    tm = TOKEN_TILE
    tpm_p = L // tm

    c_all = jnp.concatenate([c_prompt, c_sample], axis=0)
    R = c_all.shape[0]
    c_all = jnp.pad(c_all, ((0, -R % 8), (0, 0)))
    mods = [_ada(c_all, ada_w, ada_b, l) for l in range(2)]
    modkv = _ada(c_all, kv_ada_w[None], kv_ada_b[None], 0)

    def split_mod(m):
        return m[:B, None, :], jnp.repeat(m[B:B + DB], S, axis=0)[None]
    mod_p, mod_s = zip(*[split_mod(m) for m in mods])
    modkv_p, modkv_s = split_mod(modkv)

    row = lambda a: a.reshape(1, -1)

    def router(l):
        wr = jnp.zeros((D_MODEL, LANES), F32)
        wr = wr.at[:, :N_GROUPS].set(r_w_group[l]).at[:, N_GROUPS:N_GROUPS + N_EXPERTS].set(r_w_expert[l])
        br = jnp.zeros((1, LANES), F32)
        br = br.at[0, :N_GROUPS].set(r_b_group[l]).at[0, N_GROUPS:N_GROUPS + N_EXPERTS].set(r_b_expert[l])
        return wr.astype(BF16), br

    no_counts = jnp.zeros((1, LANES), F32)

    win = a_w_in[0].astype(BF16)
    wout = a_w_out[0].astype(BF16)
    tril = jnp.tril(jnp.ones((CHUNK, CHUNK), bool))
    ws_p = jnp.where(tril, a_w_s[0], 0).astype(BF16)
    bs_p = a_b_s[0].T
    cs = min(CHUNK, S)
    ws_small = jnp.where(jnp.tril(jnp.ones((cs, cs), bool)), a_w_s[0][:, :cs, :cs], 0)
    ws_s = jnp.stack([jnp.kron(jnp.eye(Ts // cs, dtype=F32), ws_small[g]) for g in range(GMLP_GROUPS)]).astype(BF16)
    bs_s = jnp.tile(a_b_s[0][:, :cs], (1, Ts // cs)).T
    wr0, br0 = router(0)
    common = (row(norm_mix[0]), win, row(a_b_in[0]), row(a_norm_v[0]))
    xp = x_prompt.reshape(Tp, D_MODEL)
    xs_ = x_sample.reshape(Ts, D_MODEL)
    x1_p, h2_p, rt_p, cnt = _mixer_a(xp, mod_p[0], tpm_p, tm, *common, ws_p, bs_p, wout, row(norm_ffn[0]), wr0, br0,
                                     no_counts, with_v=False)
    x1_s, h2_s, rt_s, cnt, v_s = _mixer_a(xs_, mod_s[0], 1, Ts, *common, ws_s, bs_s, wout, row(norm_ffn[0]), wr0, br0,
                                          cnt, with_v=True)
    x2_p, x2_s = _moe([(h2_p, rt_p, x1_p, mod_p[0], tpm_p, tm), (h2_s, rt_s, x1_s, mod_s[0], 1, Ts)], cnt,
                      e_w1, e_w3, e_w2, 0)

    wkv = w_kv.astype(BF16)
    wq = b_w_q[0].astype(BF16)
    kn = jnp.tile(k_norm, (1, HEADS))
    qn = jnp.tile(q_norm[0], (1, HEADS))
    head = np.arange(ATTN_WIDTH) // HEAD_DIM
    half_head = head[:ATTN_WIDTH // 2]
    eavg = jnp.asarray((half_head[:, None] == half_head[None, :]) / HEAD_DIM, BF16)
    kvq_w = (row(kv_norm), row(norm_mix[1]), wkv, wq, kn, qn, eavg)
    *qkv_p, kt0, kt1, kt2 = _kvq(x2_p, modkv_p, mod_p[1], tpm_p, tm, *kvq_w, batch=B)
    kv_s, q_s = _kvq(x2_s, modkv_s, mod_s[1], 1, Ts, *kvq_w)

    os_, lses = [], []
    for g, (_, dil) in enumerate(BRANCHES):
        o, lse = _band_attn(*qkv_p[3 * g:3 * g + 3], _band_bias(rel_bias, g, dil), dil)
        os_.append(o)
        lses.append(lse)
    caches = [jnp.transpose(c, (0, 2, 3, 4, 1)) for c in (cache_kv_w128, cache_kv_w512, cache_kv_w2048)]
    kvn_s = jnp.pad(kv_s.reshape(DB, S, -1), ((0, 0), (0, 8 - S), (0, 0)))
    o_s = _step_attn(q_s.reshape(DB, S, -1), kvn_s, caches, *_step_bias(rel_bias, S), DB, S)

    wo = b_w_o[0].astype(BF16)
    ex = jnp.asarray(np.arange(LANES)[:, None] == head[None, :], BF16)
    wr1, br1 = router(1)
    x3_p, h3_p, rt1_p, cnt1 = _attn_out(os_, lses, x2_p, mod_p[1], tpm_p, tm, wo, ex, row(norm_ffn[1]), wr1, br1,
                                        no_counts)
    x3_s, h3_s, rt1_s, cnt1 = _attn_out([o_s.reshape(Ts, ATTN_WIDTH)], [], x2_s, mod_s[1], 1, Ts, wo, ex,
                                        row(norm_ffn[1]), wr1, br1, cnt1)
    y_p, y_s = _moe([(h3_p, rt1_p, x3_p, mod_p[1], tpm_p, tm), (h3_s, rt1_s, x3_s, mod_s[1], 1, Ts)], cnt1,
                    e_w1, e_w3, e_w2, 1)

    kv_s4 = kv_s.reshape(DB, S, N_BRANCH, 2, HEADS, HEAD_DIM)

    def window(kt, w):
        n = min(w, L)
        return jnp.transpose(kt[:, :, kt.shape[2] - n:].reshape(B, 2, HEADS, HEAD_DIM, n), (0, 4, 1, 2, 3))
    return (y_p.reshape(B, L, D_MODEL), y_s.reshape(DB, S, D_MODEL),
            window(kt0, BRANCHES[0][0]), window(kt1, BRANCHES[1][0]), window(kt2, BRANCHES[2][0]),
            kv_s4[:, :, 0], kv_s4[:, :, 1], kv_s4[:, :, 2],
            v_s.reshape(1, DB, S, GMLP_WIDTH))
```

```python
import functools

import numpy as np
import jax
import jax.numpy as jnp
from jax import lax
from jax.experimental import pallas as pl
from jax.experimental.pallas import tpu as pltpu

F32 = jnp.float32
BF16 = jnp.bfloat16

D_MODEL = 1024
GMLP_WIDTH = 2048
GMLP_GROUPS = 4
GROUP_WIDTH = GMLP_WIDTH // GMLP_GROUPS
CHUNK = 128
BRANCHES = ((128, 1), (512, 4), (2048, 16))
N_BRANCH = 3
N_STEPS = 128
HEADS = 8
HEAD_DIM = 64
ATTN_WIDTH = HEADS * HEAD_DIM
NUM_BUCKETS = 32
MAX_EXACT = NUM_BUCKETS // 2
REL_MAX_DIST = 2048
N_GROUPS = 4
EXPERTS_PER_GROUP = 8
N_EXPERTS = N_GROUPS * EXPERTS_PER_GROUP
TOP_K = 2
D_EXPERT = 512
EPS = 1e-6
NEG = -1e30

LANES = 128
ROW_TILES = D_MODEL // LANES
TOKEN_TILE = 512
MOE_BLOCK = 256
DMA_UNROLL = 8
VMEM_LIMIT = 52 * 1024 * 1024


def _cparams(sem):
    return pltpu.CompilerParams(dimension_semantics=sem, vmem_limit_bytes=VMEM_LIMIT)


def _resident(shape):
    nd = len(shape)
    return pl.BlockSpec(shape, lambda *_, _nd=nd: (0,) * _nd, pipeline_mode=pl.Buffered(1))


def _gelu_tanh(x):
    return 0.5 * x * (1.0 + jnp.tanh(0.7978845608028654 * (x + 0.044715 * (x * x * x))))


def _rms(x, g):
    return x * lax.rsqrt(jnp.mean(x * x, axis=-1, keepdims=True) + EPS) * g


def _store_rows8(ref, val, n, base=0):
    for s in range(ROW_TILES):
        ref[pl.ds(base + s, n, stride=ROW_TILES), :] = val[:, s * LANES:(s + 1) * LANES]


def _load_rows8(ref, n, base=0):
    return jnp.concatenate([ref[pl.ds(base + s, n, stride=ROW_TILES), :] for s in range(ROW_TILES)], axis=1)


def _split(a):
    hi = a.astype(BF16)
    return hi, (a - hi.astype(F32)).astype(BF16)


def _split_dot(a, e_bf16):
    hi, lo = _split(a)
    return (jnp.dot(hi, e_bf16, preferred_element_type=F32) + jnp.dot(lo, e_bf16, preferred_element_type=F32))


ROUTE_LANE0 = 4


def _route_rows(l, tri_ref, carry):
    lane = lax.broadcasted_iota(jnp.int32, l.shape, 1).astype(F32)
    far = float(LANES)

    def first_lane(mask):
        return jnp.min(jnp.where(mask, lane, far), axis=1, keepdims=True)

    is_g = lane < N_GROUPS
    gl = jnp.where(is_g, l, NEG)
    gmax = jnp.max(gl, axis=1, keepdims=True)
    g_i = first_lane(jnp.logical_and(gl == gmax, is_g))
    g_p = 1.0 / jnp.sum(jnp.where(is_g, jnp.exp(gl - gmax), 0.0), axis=1, keepdims=True)
    lo = ROUTE_LANE0 + EXPERTS_PER_GROUP * g_i
    sel = jnp.logical_and(lane >= lo, lane < lo + EXPERTS_PER_GROUP)
    el = jnp.where(sel, l, NEG)
    m1 = jnp.max(el, axis=1, keepdims=True)
    i1 = first_lane(jnp.logical_and(el == m1, sel))
    sel2 = jnp.logical_and(sel, lane != i1)
    el2 = jnp.where(sel2, l, NEG)
    m2 = jnp.max(el2, axis=1, keepdims=True)
    i2 = first_lane(jnp.logical_and(el2 == m2, sel2))
    r = jnp.exp(m2 - m1)
    w1 = g_p / (1.0 + r)
    w2 = g_p * r / (1.0 + r)

    hit1 = lane == i1
    hit2 = lane == i2
    onehot = jnp.where(jnp.logical_or(hit1, hit2), 1.0, 0.0)
    before = carry[...] + jnp.dot(tri_ref[...], onehot.astype(BF16), preferred_element_type=F32)
    rank1 = jnp.sum(jnp.where(hit1, before, 0.0), axis=1, keepdims=True)
    rank2 = jnp.sum(jnp.where(hit2, before, 0.0), axis=1, keepdims=True)
    carry[...] = carry[...] + jnp.sum(onehot, axis=0, keepdims=True)

    out = jnp.zeros(l.shape, F32)
    for k, val in enumerate((i1 - ROUTE_LANE0, i2 - ROUTE_LANE0, w1, w2, rank1, rank2)):
        out = jnp.where(lane == k, val, out)
    return out


ROUTE_FIELDS = 8


def _ffn_pre(x, mod, nf_ref, wr_ref, br_ref, cin_ref, tri_ref, h_ref, rt_ref, rtt_ref, cnt_ref, carry, n):
    @pl.when(pl.program_id(0) == 0)
    def _():
        carry[...] = cin_ref[...]

    h = _rms(x, nf_ref[...]) * (1.0 + mod[:, 4 * D_MODEL:5 * D_MODEL]) + mod[:, 3 * D_MODEL:4 * D_MODEL]
    _store_rows8(h_ref, h, n)
    logits = jnp.dot(h.astype(BF16), wr_ref[...], preferred_element_type=F32) + br_ref[...]
    route = _route_rows(logits, tri_ref, carry)
    rt_ref[...] = route
    rtt_ref[...] = route.T[:ROUTE_FIELDS, :]
    cnt_ref[...] = carry[...]


def _ada_kernel(c_ref, w_ref, b_ref, o_ref):
    c = c_ref[...]
    a = (c * jax.nn.sigmoid(c)).astype(BF16)
    o_ref[...] = jnp.dot(a, w_ref[...].astype(BF16), preferred_element_type=F32) + b_ref[...]


def _ada(c, w, b, layer):
    R = c.shape[0]
    N = w.shape[2]
    tn = 1024
    return pl.pallas_call(
        _ada_kernel,
        grid=(N // tn,),
        in_specs=[pl.BlockSpec((R, D_MODEL), lambda j: (0, 0)),
                  pl.BlockSpec((None, D_MODEL, tn), lambda j: (layer, 0, j)),
                  pl.BlockSpec((None, 1, tn), lambda j: (layer, 0, j))],
        out_specs=pl.BlockSpec((R, tn), lambda j: (0, j)),
        out_shape=jax.ShapeDtypeStruct((R, N), F32),
        compiler_params=_cparams(("arbitrary",)),
        name="ada",
    )(c, w, b.reshape(b.shape[0], 1, N))


def _mixer_a_kernel(x_ref, mod_ref, nm_ref, win_ref, bin_ref, gv_ref, ws_ref, bs_ref, wout_ref,
                    nf_ref, wr_ref, br_ref, cin_ref, tri_ref, x1_ref, h2_ref, rt_ref, rtt_ref, cnt_ref, *rest, tm):
    v_refs, carry = rest[:-1], rest[-1]
    x = x_ref[...]
    mod = mod_ref[0]
    h = (_rms(x, nm_ref[...]) * (1.0 + mod[:, D_MODEL:2 * D_MODEL]) + mod[:, 0:D_MODEL]).astype(BF16)
    zv = jnp.dot(h, win_ref[:, GMLP_WIDTH:], preferred_element_type=F32) + bin_ref[:, GMLP_WIDTH:]
    v = _rms(_gelu_tanh(zv), gv_ref[...])
    if v_refs:
        v_refs[0][...] = v
    vb = v.astype(BF16)
    bs = bs_ref[...]
    acc = jnp.zeros((tm, D_MODEL), F32)
    for g in range(GMLP_GROUPS):
        lo, hi = g * GROUP_WIDTH, (g + 1) * GROUP_WIDTH
        u = _gelu_tanh(jnp.dot(h, win_ref[:, lo:hi], preferred_element_type=F32) + bin_ref[:, lo:hi])
        wg = ws_ref[g]
        gate = jnp.concatenate(
            [jnp.dot(wg, vb[c * CHUNK:(c + 1) * CHUNK, lo:hi], preferred_element_type=F32) + bs[:, g:g + 1]
             for c in range(tm // CHUNK)], axis=0)
        acc = acc + jnp.dot((u * gate).astype(BF16), wout_ref[lo:hi, :], preferred_element_type=F32)
    x1 = x + mod[:, 2 * D_MODEL:3 * D_MODEL] * acc
    x1_ref[...] = x1
    _ffn_pre(x1, mod, nf_ref, wr_ref, br_ref, cin_ref, tri_ref, h2_ref, rt_ref, rtt_ref, cnt_ref, carry, tm)


def _mod_spec(mod, tiles_per_mod):
    _, rows, width = mod.shape
    return pl.BlockSpec((1, rows, width), lambda i: (i // tiles_per_mod, 0, 0))


def _earlier_rows(tm):
    return jnp.asarray(np.tril(np.ones((tm, tm), np.float32), -1), BF16)


def _route_outs(T, tm):
    return ([pl.BlockSpec((tm, LANES), lambda i: (i, 0)), pl.BlockSpec((ROUTE_FIELDS, tm), lambda i: (0, i)),
             pl.BlockSpec((1, LANES), lambda i: (0, 0))],
            [jax.ShapeDtypeStruct((T, LANES), F32), jax.ShapeDtypeStruct((ROUTE_FIELDS, T), F32),
             jax.ShapeDtypeStruct((1, LANES), F32)])


def _mixer_a(x, mod, tiles_per_mod, tm, nm, win, bin_, gv, ws, bs_t, wout, nf, wr, br, counts, with_v):
    T = x.shape[0]
    tok = lambda w: pl.BlockSpec((tm, w), lambda i: (i, 0))
    r_specs, r_shapes = _route_outs(T, tm)
    out_shape = [jax.ShapeDtypeStruct((T, D_MODEL), F32), jax.ShapeDtypeStruct((T * ROW_TILES, LANES), F32)] + r_shapes
    out_specs = [tok(D_MODEL), pl.BlockSpec((tm * ROW_TILES, LANES), lambda i: (i, 0))] + r_specs
    if with_v:
        out_shape.append(jax.ShapeDtypeStruct((T, GMLP_WIDTH), F32))
        out_specs.append(tok(GMLP_WIDTH))
    tri = _earlier_rows(tm)
    return pl.pallas_call(
        functools.partial(_mixer_a_kernel, tm=tm),
        grid=(T // tm,),
        in_specs=[tok(D_MODEL), _mod_spec(mod, tiles_per_mod), _resident(nm.shape), _resident(win.shape),
                  _resident(bin_.shape), _resident(gv.shape), _resident(ws.shape), _resident(bs_t.shape),
                  _resident(wout.shape), _resident(nf.shape), _resident(wr.shape), _resident(br.shape),
                  _resident(counts.shape), _resident(tri.shape)],
        out_specs=out_specs,
        out_shape=out_shape,
        scratch_shapes=[pltpu.VMEM((1, LANES), F32)],
        compiler_params=_cparams(("arbitrary",)),
        name="mixer_a",
    )(x, mod, nm, win, bin_, gv, ws, bs_t, wout, nf, wr, br, counts, tri)


def _layout(rtt, counts, block):
    n_tokens = rtt.shape[1]
    counts = counts[0, ROUTE_LANE0:ROUTE_LANE0 + N_EXPERTS].astype(jnp.int32)
    padded = (counts + block - 1) // block * block
    pad_end = jnp.cumsum(padded)
    pad_start = (pad_end - padded).astype(F32)
    experts, ranks = rtt[0:TOP_K], rtt[4:4 + TOP_K]
    base = jnp.zeros_like(ranks)
    for e in range(N_EXPERTS):
        base = base + jnp.where(experts == float(e), pad_start[e], 0.0)
    slots = (ranks + base).astype(jnp.int32)
    nb = -(-n_tokens * TOP_K // block) + N_EXPERTS
    blk_e = jnp.minimum(jnp.sum(pad_end[None, :] <= (jnp.arange(nb, dtype=jnp.int32) * block)[:, None], axis=1),
                        N_EXPERTS - 1).astype(jnp.int32)
    n_used = (pad_end[-1] // block).astype(jnp.int32).reshape(1)
    starts = jnp.concatenate([jnp.ones((1,), jnp.int32), (blk_e[1:] != blk_e[:-1]).astype(jnp.int32)])
    grp = jnp.cumsum(starts) - 1
    end_blk = (pad_end // block).astype(jnp.int32)
    own = blk_e[:, None] == jnp.arange(N_EXPERTS, dtype=jnp.int32)[None, :]
    nxt_blk = jnp.sum(jnp.where(own, end_blk[None, :], 0), axis=1)
    at_nxt = nxt_blk[:, None] == jnp.arange(nb, dtype=jnp.int32)[None, :]
    nxt_e = jnp.where(nxt_blk < n_used[0], jnp.sum(jnp.where(at_nxt, blk_e[None, :], 0), axis=1), -1).astype(jnp.int32)
    zero_plan = (jnp.maximum(pad_end - block, 0).astype(jnp.int32), (padded > 0).astype(jnp.int32), n_used)
    return slots, (blk_e, n_used, nxt_e, grp.astype(jnp.int32)), zero_plan, nb


def _tile_slots(slots, ts):
    T = slots.shape[1]
    return slots.reshape(TOP_K, T // ts, ts).transpose(1, 0, 2).reshape(T // ts, 1, TOP_K * ts)


def _row_copy(src_ref, src_row, dst_ref, dst_row, sem):
    return pltpu.make_async_copy(
        src_ref.at[pl.ds(pl.multiple_of(src_row * ROW_TILES, ROW_TILES), ROW_TILES), :],
        dst_ref.at[pl.ds(pl.multiple_of(dst_row * ROW_TILES, ROW_TILES), ROW_TILES), :], sem)


def _scatter_kernel(zlo_ref, has_ref, nu_ref, slot_a_ref, slot_b_ref, src_a_ref, src_b_ref, dst_ref, zbuf, sem,
                    *, ts, n_a, n_b, nb, block):
    i = pl.program_id(0)

    def clear(row0):
        return pltpu.make_async_copy(
            zbuf, dst_ref.at[pl.ds(pl.multiple_of(row0 * ROW_TILES, ROW_TILES), block * ROW_TILES), :], sem)

    def for_clears(act):
        for e in range(N_EXPERTS):
            @pl.when(has_ref[e] > 0)
            def _(e=e):
                act(clear(zlo_ref[e]))

        def tail(j, c):
            act(clear(j * block))
            return c
        lax.fori_loop(nu_ref[0], nb, tail, 0)

    @pl.when(i == 0)
    def _():
        zbuf[...] = jnp.zeros(zbuf.shape, F32)
        for_clears(lambda c: c.start())
        for_clears(lambda c: c.wait())

    def scatter(slot_ref, src_ref, n):
        def issue(j, c):
            for u in range(DMA_UNROLL):
                t = j * DMA_UNROLL + u
                for k in range(TOP_K):
                    _row_copy(src_ref, t, dst_ref, slot_ref[0, 0, k * n + t], sem).start(priority=k)
            return c
        lax.fori_loop(0, n // DMA_UNROLL, issue, 0)
        for _ in range(TOP_K):
            pltpu.make_async_copy(src_ref, dst_ref.at[pl.ds(0, n * ROW_TILES), :], sem).wait()

    @pl.when(i < n_a)
    def _():
        scatter(slot_a_ref, src_a_ref, ts)

    @pl.when(i == n_a)
    def _():
        scatter(slot_b_ref, src_b_ref, n_b)


def _scatter(slots_a, slots_b, src_a, src_b, zero_plan, nb, block):
    n_a, _, ts2 = slots_a.shape
    ts, n_b = ts2 // TOP_K, slots_b.shape[2] // TOP_K
    tile_a = lambda i, *_: (jnp.minimum(i, n_a - 1), 0, 0)
    return pl.pallas_call(
        functools.partial(_scatter_kernel, ts=ts, n_a=n_a, n_b=n_b, nb=nb, block=block),
        grid_spec=pltpu.PrefetchScalarGridSpec(
            num_scalar_prefetch=3,
            grid=(n_a + 1,),
            in_specs=[pl.BlockSpec((1, 1, ts2), tile_a, memory_space=pltpu.SMEM),
                      pl.BlockSpec((1, 1, TOP_K * n_b), lambda i, *_: (0, 0, 0), memory_space=pltpu.SMEM),
                      pl.BlockSpec((ts * ROW_TILES, LANES), lambda i, *_: (jnp.minimum(i, n_a - 1), 0)),
                      pl.BlockSpec((n_b * ROW_TILES, LANES), lambda i, *_: (0, 0))],
            out_specs=pl.BlockSpec(memory_space=pl.ANY),
            scratch_shapes=[pltpu.VMEM((block * ROW_TILES, LANES), F32), pltpu.SemaphoreType.DMA(())]),
        out_shape=jax.ShapeDtypeStruct((nb * block * ROW_TILES, LANES), F32),
        compiler_params=_cparams(("arbitrary",)),
        name="moe_scatter",
    )(*zero_plan, slots_a, slots_b, src_a, src_b)


def _experts_kernel(be_ref, nu_ref, nxt_ref, grp_ref, x_ref, w1_hbm, w3_hbm, w2_hbm, y_ref,
                    w1b, w3b, w2b, wf1, wf3, wf2, sem, *, block, layer):
    i = pl.program_id(0)

    def weight_copies(e, buf):
        return [pltpu.make_async_copy(src.at[layer, e], dst.at[buf], sem.at[buf])
                for src, dst in ((w1_hbm, wf1), (w3_hbm, wf3), (w2_hbm, wf2))]

    @pl.when(i == 0)
    def _():
        for c in weight_copies(be_ref[0], 0):
            c.start()

    @pl.when(i < nu_ref[0])
    def _():
        @pl.when(jnp.logical_or(i == 0, be_ref[i] != be_ref[jnp.maximum(i - 1, 0)]))
        def _():
            buf = grp_ref[i] % 2
            for c in weight_copies(be_ref[i], buf):
                c.wait()

            @pl.when(nxt_ref[i] >= 0)
            def _():
                for c in weight_copies(nxt_ref[i], 1 - buf):
                    c.start()

            w1b[...] = wf1[buf].astype(BF16)
            w3b[...] = wf3[buf].astype(BF16)
            w2b[...] = wf2[buf].astype(BF16)

        x = _load_rows8(x_ref, block).astype(BF16)
        h1 = jnp.dot(x, w1b[...], preferred_element_type=F32)
        h3 = jnp.dot(x, w3b[...], preferred_element_type=F32)
        a = (h1 * jax.nn.sigmoid(h1) * h3).astype(BF16)
        _store_rows8(y_ref, jnp.dot(a, w2b[...], preferred_element_type=F32), block)


def _experts(xs, plan, nb, w1, w3, w2, layer, block):
    rows = pl.BlockSpec((block * ROW_TILES, LANES), lambda i, be, nu, nxt, grp: (jnp.minimum(i, nu[0] - 1), 0))
    hbm = pl.BlockSpec(memory_space=pl.ANY)
    return pl.pallas_call(
        functools.partial(_experts_kernel, block=block, layer=layer),
        grid_spec=pltpu.PrefetchScalarGridSpec(
            num_scalar_prefetch=4,
            grid=(nb,),
            in_specs=[rows, hbm, hbm, hbm],
            out_specs=rows,
            scratch_shapes=[pltpu.VMEM((D_MODEL, D_EXPERT), BF16), pltpu.VMEM((D_MODEL, D_EXPERT), BF16),
                            pltpu.VMEM((D_EXPERT, D_MODEL), BF16),
                            pltpu.VMEM((2, D_MODEL, D_EXPERT), F32), pltpu.VMEM((2, D_MODEL, D_EXPERT), F32),
                            pltpu.VMEM((2, D_EXPERT, D_MODEL), F32), pltpu.SemaphoreType.DMA((2,))]),
        out_shape=jax.ShapeDtypeStruct(xs.shape, F32),
        input_output_aliases={4: 0},
        compiler_params=_cparams(("arbitrary",)),
        name="moe_experts",
    )(*plan, xs, w1, w3, w2)


def _combine_kernel(slot_ref, y_ref, x_ref, mod_ref, rt_ref, o_ref, ybuf, sem, *, tc):
    def issue(j, c):
        for u in range(DMA_UNROLL):
            t = j * DMA_UNROLL + u
            for k in range(TOP_K):
                _row_copy(y_ref, slot_ref[0, 0, k * tc + t], ybuf, k * tc + t, sem).start(priority=k)
        return c
    lax.fori_loop(0, tc // DMA_UNROLL, issue, 0)
    pltpu.make_async_copy(y_ref.at[pl.ds(0, TOP_K * tc * ROW_TILES), :], ybuf, sem).wait()

    rt = rt_ref[...]
    y = (rt[:, 2:3] * _load_rows8(ybuf, tc) + rt[:, 3:4] * _load_rows8(ybuf, tc, base=tc * ROW_TILES))
    o_ref[...] = x_ref[...] + mod_ref[0][:, 5 * D_MODEL:6 * D_MODEL] * y


def _combine(slots, y, x, mod, tiles_per_mod, route, tc):
    T = x.shape[0]
    return pl.pallas_call(
        functools.partial(_combine_kernel, tc=tc),
        grid=(T // tc,),
        in_specs=[pl.BlockSpec((1, 1, tc * TOP_K), lambda i: (i, 0, 0), memory_space=pltpu.SMEM),
                  pl.BlockSpec(memory_space=pl.ANY),
                  pl.BlockSpec((tc, D_MODEL), lambda i: (i, 0)),
                  _mod_spec(mod, tiles_per_mod),
                  pl.BlockSpec((tc, LANES), lambda i: (i, 0))],
        out_specs=pl.BlockSpec((tc, D_MODEL), lambda i: (i, 0)),
        out_shape=jax.ShapeDtypeStruct((T, D_MODEL), F32),
        scratch_shapes=[pltpu.VMEM((TOP_K * tc * ROW_TILES, LANES), F32), pltpu.SemaphoreType.DMA(())],
        compiler_params=_cparams(("arbitrary",)),
        name="moe_combine",
    )(slots, y, x, mod, route)


def _moe(part_a, part_b, counts, w1, w3, w2, layer):
    Ta = part_a[3].shape[0]
    slots, plan, zero_plan, nb = _layout(jnp.concatenate([part_a[2], part_b[2]], axis=1), counts, MOE_BLOCK)
    tiled = [_tile_slots(slots[:, :Ta], part_a[6]), _tile_slots(slots[:, Ta:], part_b[6])]
    xs = _scatter(tiled[0], tiled[1], part_a[0], part_b[0], zero_plan, nb, MOE_BLOCK)
    y = _experts(xs, plan, nb, w1, w3, w2, layer, MOE_BLOCK)
    return [_combine(sl, y, x, mod, tpm, rt, tt) for sl, (_, rt, _, x, mod, tpm, tt) in zip(tiled, (part_a, part_b))]


def _kvq_kernel(x_ref, modkv_ref, mod_ref, nkv_ref, nm_ref, wkv_ref, wq_ref, kn_ref, qn_ref, eavg_ref,
                *rest, tm, keep):
    residue = keep is not None
    x = x_ref[...]
    xn = x * lax.rsqrt(jnp.mean(x * x, axis=-1, keepdims=True) + EPS)
    modkv = modkv_ref[0]
    mod = mod_ref[0]
    hk = (xn * nkv_ref[...] * (1.0 + modkv[:, D_MODEL:]) + modkv[:, :D_MODEL]).astype(BF16)
    hq = (xn * nm_ref[...] * (1.0 + mod[:, D_MODEL:2 * D_MODEL]) + mod[:, :D_MODEL]).astype(BF16)
    eavg = eavg_ref[...]

    def head_norm(a, g):
        sq = (a * a).astype(BF16)
        half = ATTN_WIDTH // 2
        ms = jnp.concatenate([jnp.dot(sq[:, :half], eavg, preferred_element_type=F32),
                              jnp.dot(sq[:, half:], eavg, preferred_element_type=F32)], axis=1)
        return a * lax.rsqrt(ms + EPS) * g

    def residue_major(val, ref, dil):
        if dil == 1:
            ref[0, 0] = val.astype(BF16)
            return
        stage = rest[-1]
        for c in range(stage.shape[0]):
            stage[c] = val[:, c * LANES:(c + 1) * LANES]
        for r in range(dil):
            rows = [stage[c, pl.ds(r, tm // dil, stride=dil), :] for c in range(stage.shape[0])]
            ref[0, r] = jnp.concatenate(rows, axis=1).astype(BF16)

    for g, (_, dil) in enumerate(BRANCHES):
        k0 = 2 * g * ATTN_WIDTH
        kvg = jnp.dot(hk, wkv_ref[:, k0:k0 + 2 * ATTN_WIDTH], preferred_element_type=F32)
        kn = head_norm(kvg[:, :ATTN_WIDTH], kn_ref[g:g + 1, :])
        vv = kvg[:, ATTN_WIDTH:]
        q0 = g * ATTN_WIDTH
        qg = jnp.dot(hq, wq_ref[:, q0:q0 + ATTN_WIDTH], preferred_element_type=F32)
        qg = head_norm(qg, qn_ref[g:g + 1, :]) * (HEAD_DIM ** -0.5)
        if residue:
            residue_major(qg, rest[3 * g], dil)
            residue_major(kn, rest[3 * g + 1], dil)
            residue_major(vv, rest[3 * g + 2], dil)
            tiles_per_batch, kept = keep

            @pl.when(pl.program_id(0) % tiles_per_batch >= tiles_per_batch - kept[g])
            def _(kn=kn, vv=vv, kt_ref=rest[3 * N_BRANCH + g]):
                kt_ref[0, :ATTN_WIDTH, :] = kn.T
                kt_ref[0, ATTN_WIDTH:, :] = vv.T
        else:
            rest[0][:, k0:k0 + ATTN_WIDTH] = kn
            rest[0][:, k0 + ATTN_WIDTH:k0 + 2 * ATTN_WIDTH] = vv
            rest[1][:, q0:q0 + ATTN_WIDTH] = qg


def _kvq(x, modkv, mod, tiles_per_mod, tm, nkv, nm, wkv, wq, kn, qn, eavg, batch=None):
    T = x.shape[0]
    tok = lambda w: pl.BlockSpec((tm, w), lambda i: (i, 0))
    kvw = 2 * N_BRANCH * ATTN_WIDTH
    if batch is not None:
        L = T // batch
        tpb = L // tm
        kept = tuple(-(-min(w, L) // tm) for w, _ in BRANCHES)
        keep = (tpb, kept)
        out_specs, out_shape = [], []
        for _, dil in BRANCHES:
            for _ in range(3):
                out_specs.append(pl.BlockSpec((1, dil, tm // dil, ATTN_WIDTH), lambda i: (i // tpb, 0, i % tpb, 0)))
                out_shape.append(jax.ShapeDtypeStruct((batch, dil, L // dil, ATTN_WIDTH), BF16))
        for kg in kept:
            out_specs.append(pl.BlockSpec((1, 2 * ATTN_WIDTH, tm),
                                          lambda i, kg=kg: (i // tpb, 0, jnp.maximum(i % tpb - (tpb - kg), 0))))
            out_shape.append(jax.ShapeDtypeStruct((batch, 2 * ATTN_WIDTH, kg * tm), F32))
        scratch = [pltpu.VMEM((ATTN_WIDTH // LANES, tm, LANES), F32)]
    else:
        keep = None
        out_specs = [tok(kvw), tok(N_BRANCH * ATTN_WIDTH)]
        out_shape = [jax.ShapeDtypeStruct((T, kvw), F32), jax.ShapeDtypeStruct((T, N_BRANCH * ATTN_WIDTH), F32)]
        scratch = []
    return pl.pallas_call(
        functools.partial(_kvq_kernel, tm=tm, keep=keep),
        grid=(T // tm,),
        in_specs=[tok(D_MODEL), _mod_spec(modkv, tiles_per_mod), _mod_spec(mod, tiles_per_mod),
                  _resident(nkv.shape), _resident(nm.shape), _resident(wkv.shape), _resident(wq.shape),
                  _resident(kn.shape), _resident(qn.shape), _resident(eavg.shape)],
        out_specs=out_specs,
        out_shape=out_shape,
        scratch_shapes=scratch,
        compiler_params=_cparams(("arbitrary",)),
        name="kvq",
    )(x, modkv, mod, nkv, nm, wkv, wq, kn, qn, eavg)


BAND_QBLOCKS = 2


def _band_attn_kernel(q_ref, kp_ref, kc_ref, vp_ref, vc_ref, bias_ref, o_ref, lse_ref):
    n = pl.program_id(2)
    col = lax.broadcasted_iota(jnp.int32, (N_STEPS, 2 * N_STEPS), 1)
    first = jnp.logical_or(col >= N_STEPS, n > 0)
    lane = lax.broadcasted_iota(jnp.int32, (N_STEPS, LANES), 1)
    k_all = jnp.concatenate([kp_ref[...], kc_ref[...]], axis=0)
    v_all = jnp.concatenate([vp_ref[...], vc_ref[...]], axis=0)
    for j in range(q_ref.shape[0] // N_STEPS):
        rows = slice(j * N_STEPS, (j + 1) * N_STEPS)
        q = q_ref[rows, :]
        k2 = k_all[j * N_STEPS:(j + 2) * N_STEPS]
        v2 = v_all[j * N_STEPS:(j + 2) * N_STEPS]
        lse_all = jnp.zeros((N_STEPS, LANES), F32)
        for h in range(HEADS):
            sl = slice(h * HEAD_DIM, (h + 1) * HEAD_DIM)
            s = lax.dot_general(q[:, sl], k2[:, sl], (((1,), (1,)), ((), ())), preferred_element_type=F32)
            s = s + bias_ref[h]
            if j == 0:
                s = jnp.where(first, s, NEG)
            m = jnp.max(s, axis=1, keepdims=True)
            p = jnp.exp(s - m)
            l = jnp.sum(p, axis=1, keepdims=True)
            o_ref[rows, sl] = jnp.dot(p.astype(BF16), v2[:, sl], preferred_element_type=F32) / l
            lse_all = jnp.where(lane == h, m + jnp.log(l), lse_all)
        lse_ref[rows, :] = lse_all


def _band_attn(q, k, v, bias, dil):
    B, _, Ld, _ = q.shape
    nq = min(BAND_QBLOCKS, Ld // N_STEPS)
    step = nq * N_STEPS
    blk = lambda w, f: pl.BlockSpec((None, None, step, w), f)
    cur = lambda b, r, n: (b, r, n, 0)
    prev_blk = pl.BlockSpec((None, None, N_STEPS, ATTN_WIDTH), lambda b, r, n: (b, r, jnp.maximum(nq * n - 1, 0), 0))
    return pl.pallas_call(
        _band_attn_kernel,
        grid=(B, dil, Ld // step),
        in_specs=[blk(ATTN_WIDTH, cur), prev_blk, blk(ATTN_WIDTH, cur), prev_blk,
                  blk(ATTN_WIDTH, cur), _resident(bias.shape)],
        out_specs=[blk(ATTN_WIDTH, cur), blk(LANES, cur)],
        out_shape=[jax.ShapeDtypeStruct((B, dil, Ld, ATTN_WIDTH), F32),
                   jax.ShapeDtypeStruct((B, dil, Ld, LANES), F32)],
        compiler_params=_cparams(("arbitrary", "arbitrary", "arbitrary")),
        name=f"band_attn_d{dil}",
    )(q, k, k, v, v, bias)


def _rel_buckets(dilation):
    n = np.arange(N_STEPS + 1) * dilation
    large = MAX_EXACT + (np.log(np.maximum(n, 1) / MAX_EXACT) / np.log(REL_MAX_DIST / MAX_EXACT)
                         * (NUM_BUCKETS - MAX_EXACT)).astype(np.int32)
    return np.where(n < MAX_EXACT, n, np.minimum(large, NUM_BUCKETS - 1)).astype(np.int32)


def _step_bias_row(rel_bias, g, dil):
    onehot = np.zeros((N_STEPS + 1, NUM_BUCKETS), np.float32)
    onehot[np.arange(N_STEPS + 1), _rel_buckets(dil)] = 1.0
    return jnp.dot(jnp.asarray(onehot), rel_bias[:, g, :].astype(F32), precision=lax.Precision.HIGHEST).T


def _band_bias(rel_bias, g, dil):
    bias = _step_bias_row(rel_bias, g, dil)
    P = 3 * N_STEPS
    neg = jnp.full((HEADS, N_STEPS), NEG, F32)
    ext = jnp.concatenate([neg, bias[:, ::-1], neg], axis=1)
    flat = jnp.broadcast_to(ext[:, None, :], (HEADS, N_STEPS, P + 1)).reshape(HEADS, N_STEPS * (P + 1))
    skew = flat[:, :N_STEPS * P].reshape(HEADS, N_STEPS, P)
    return skew[:, :, N_STEPS:]


def _step_attn_kernel(q_ref, kvn_ref, c0_ref, c1_ref, c2_ref, b0_ref, b1_ref, b2_ref, bn_ref, o_ref):
    caches = (c0_ref, c1_ref, c2_ref)
    cbias = (b0_ref, b1_ref, b2_ref)
    nt = (((1,), (1,)), ((), ()))
    for h in range(HEADS):
        parts = []
        for g in range(N_BRANCH):
            lo = g * ATTN_WIDTH + h * HEAD_DIM
            klo = 2 * g * ATTN_WIDTH + h * HEAD_DIM
            q = q_ref[0, :, lo:lo + HEAD_DIM].astype(BF16)
            kn = kvn_ref[0, :, klo:klo + HEAD_DIM].astype(BF16)
            vn = kvn_ref[0, :, klo + ATTN_WIDTH:klo + ATTN_WIDTH + HEAD_DIM].astype(BF16)
            kt = caches[g][0, 0, h].astype(BF16)
            vt = caches[g][0, 1, h].astype(BF16)
            sc = jnp.dot(q, kt, preferred_element_type=F32) + cbias[g][h]
            sn = lax.dot_general(q, kn, nt, preferred_element_type=F32) + bn_ref[g, h]
            m = jnp.maximum(jnp.max(sc, axis=1, keepdims=True), jnp.max(sn, axis=1, keepdims=True))
            pc = jnp.exp(sc - m)
            pn = jnp.exp(sn - m)
            l = jnp.sum(pc, axis=1, keepdims=True) + jnp.sum(pn, axis=1, keepdims=True)
            pv = (lax.dot_general(pc.astype(BF16), vt, nt, preferred_element_type=F32)
                  + jnp.dot(pn.astype(BF16), vn, preferred_element_type=F32))
            parts.append((m + jnp.log(l), pv / l))
        mx = jnp.maximum(jnp.maximum(parts[0][0], parts[1][0]), parts[2][0])
        es = [jnp.exp(lse - mx) for lse, _ in parts]
        den = es[0] + es[1] + es[2]
        o_ref[0, :, h * HEAD_DIM:(h + 1) * HEAD_DIM] = (
            es[0] * parts[0][1] + es[1] * parts[1][1] + es[2] * parts[2][1]) / den


def _step_bias(rel_bias, n_new):
    cache_tabs, new_tabs = [], []
    for g, (win, dil) in enumerate(BRANCHES):
        bias = _step_bias_row(rel_bias, g, dil)
        rev = bias[:, ::-1]
        per_s = []
        for s in range(n_new):
            if dil == 1:
                neg = jnp.full((HEADS, s), NEG, F32)
                per_s.append(jnp.concatenate([neg, rev[:, :win - s]], axis=1))
            else:
                cols = [rev[:, :N_STEPS] if r == s % dil else jnp.full((HEADS, N_STEPS), NEG, F32)
                        for r in range(dil)]
                per_s.append(jnp.stack(cols, axis=2).reshape(HEADS, win))
        cache_tabs.append(jnp.stack(per_s, axis=1))
        dist = np.arange(n_new)[:, None] - np.arange(8)[None, :]
        ok = (dist >= 0) & (dist % dil == 0) & (np.arange(8)[None, :] < n_new)
        onehot = np.zeros((N_STEPS + 1, n_new * 8), np.float32)
        onehot[np.where(ok, dist // dil, 0).reshape(-1), np.arange(n_new * 8)] = 1.0
        tab = jnp.dot(bias, jnp.asarray(onehot), precision=lax.Precision.HIGHEST).reshape(HEADS, n_new, 8)
        new_tabs.append(jnp.where(ok[None], tab, NEG))
    return cache_tabs, jnp.stack(new_tabs)


def _step_attn(q, kvn, caches, cache_bias, new_bias, DB, S):
    cspec = lambda c: pl.BlockSpec((1,) + c.shape[1:], lambda b: (b, 0, 0, 0, 0))
    return pl.pallas_call(
        _step_attn_kernel,
        grid=(DB,),
        in_specs=[pl.BlockSpec((1, S, N_BRANCH * ATTN_WIDTH), lambda b: (b, 0, 0)),
                  pl.BlockSpec((1, 8, 2 * N_BRANCH * ATTN_WIDTH), lambda b: (b, 0, 0)),
                  cspec(caches[0]), cspec(caches[1]), cspec(caches[2]),
                  _resident(cache_bias[0].shape), _resident(cache_bias[1].shape), _resident(cache_bias[2].shape),
                  _resident(new_bias.shape)],
        out_specs=pl.BlockSpec((1, S, ATTN_WIDTH), lambda b: (b, 0, 0)),
        out_shape=jax.ShapeDtypeStruct((DB, S, ATTN_WIDTH), F32),
        compiler_params=_cparams(("arbitrary",)),
        name="step_attn",
    )(q, kvn, caches[0], caches[1], caches[2], *cache_bias, new_bias)


def _attn_out_kernel(*refs, dils, tm):
    n_o = max(len(dils), 1)
    n_in = n_o + len(dils) + 9
    (x_ref, mod_ref, wo_ref, ex_ref, nf_ref, wr_ref, br_ref, cin_ref, tri_ref,
     x3_ref, h3_ref, rt_ref, rtt_ref, cnt_ref, carry) = refs[n_in - 9:n_in + 6]
    stages = refs[n_in + 6:]

    def natural(ref, stage, dil):
        if dil == 1:
            return ref[0, 0]
        for r in range(dil):
            val = ref[0, r]
            for c in range(stage.shape[0]):
                stage[c, pl.ds(r, tm // dil, stride=dil), :] = val[:, c * LANES:(c + 1) * LANES]
        return jnp.concatenate([stage[c] for c in range(stage.shape[0])], axis=1)

    if dils:
        lses = [natural(refs[n_o + g], stages[2 * g + 1], dil) for g, dil in enumerate(dils)]
        mx = functools.reduce(jnp.maximum, lses)
        es = [jnp.exp(l - mx) for l in lses]
        den = functools.reduce(lambda a, b: a + b, es)
        o = None
        for g, dil in enumerate(dils):
            term = _split_dot(es[g] / den, ex_ref[...]) * natural(refs[g], stages[2 * g], dil)
            o = term if o is None else o + term
    else:
        o = refs[0][...]
    mod = mod_ref[0]
    a = jnp.dot(o.astype(BF16), wo_ref[...], preferred_element_type=F32)
    x3 = x_ref[...] + mod[:, 2 * D_MODEL:3 * D_MODEL] * a
    x3_ref[...] = x3
    _ffn_pre(x3, mod, nf_ref, wr_ref, br_ref, cin_ref, tri_ref, h3_ref, rt_ref, rtt_ref, cnt_ref, carry, tm)


def _attn_out(os_, lses, x, mod, tiles_per_mod, tm, wo, ex, nf, wr, br, counts):
    T = x.shape[0]
    tok = lambda w: pl.BlockSpec((tm, w), lambda i: (i, 0))
    r_specs, r_shapes = _route_outs(T, tm)
    tri = _earlier_rows(tm)
    if lses:
        dils = tuple(o.shape[1] for o in os_)
        tpb = tiles_per_mod
        res = lambda a: pl.BlockSpec((1, a.shape[1], tm // a.shape[1], a.shape[3]), lambda i: (i // tpb, 0, i % tpb, 0))
        o_specs = [res(a) for a in os_] + [res(a) for a in lses]
        scratch = []
        for _ in dils:
            scratch += [pltpu.VMEM((ATTN_WIDTH // LANES, tm, LANES), F32), pltpu.VMEM((1, tm, LANES), F32)]
    else:
        dils, o_specs, scratch = (), [tok(ATTN_WIDTH)], []
    return pl.pallas_call(
        functools.partial(_attn_out_kernel, dils=dils, tm=tm),
        grid=(T // tm,),
        in_specs=(o_specs + [tok(D_MODEL), _mod_spec(mod, tiles_per_mod), _resident(wo.shape), _resident(ex.shape),
                             _resident(nf.shape), _resident(wr.shape), _resident(br.shape),
                             _resident(counts.shape), _resident(tri.shape)]),
        out_specs=[tok(D_MODEL), pl.BlockSpec((tm * ROW_TILES, LANES), lambda i: (i, 0))] + r_specs,
        out_shape=[jax.ShapeDtypeStruct((T, D_MODEL), F32),
                   jax.ShapeDtypeStruct((T * ROW_TILES, LANES), F32)] + r_shapes,
        scratch_shapes=[pltpu.VMEM((1, LANES), F32)] + scratch,
        compiler_params=_cparams(("arbitrary",)),
        name="attn_out",
    )(*os_, *lses, x, mod, wo, ex, nf, wr, br, counts, tri)


def kernel(x_prompt, x_sample, cache_kv_w128, cache_kv_w512, cache_kv_w2048, c_prompt, c_sample, ada_w, ada_b, norm_mix, norm_ffn, a_w_in, a_b_in, a_norm_v, a_w_s, a_b_s, a_w_out, kv_ada_w, kv_ada_b, kv_norm, w_kv, k_norm, rel_bias, b_w_q, q_norm, b_w_o, r_w_group, r_b_group, r_w_expert, r_b_expert, e_w1, e_w3, e_w2):
    B, L, _ = x_prompt.shape
    DB, S, _ = x_sample.shape
    Tp, Ts = B * L, DB * S
    tm = TOKEN_TILE
    tpm_p = L // tm

    c_all = jnp.concatenate([c_prompt, c_sample], axis=0)
    R = c_all.shape[0]
    c_all = jnp.pad(c_all, ((0, -R % 8), (0, 0)))
    mods = [_ada(c_all, ada_w, ada_b, l) for l in range(2)]
    modkv = _ada(c_all, kv_ada_w[None], kv_ada_b[None], 0)

    def split_mod(m):
        return m[:B, None, :], jnp.repeat(m[B:B + DB], S, axis=0)[None]
    mod_p, mod_s = zip(*[split_mod(m) for m in mods])
    modkv_p, modkv_s = split_mod(modkv)

    row = lambda a: a.reshape(1, -1)

    def router(l):
        wr = jnp.zeros((D_MODEL, LANES), F32)
        wr = wr.at[:, :N_GROUPS].set(r_w_group[l]).at[:, N_GROUPS:N_GROUPS + N_EXPERTS].set(r_w_expert[l])
        br = jnp.zeros((1, LANES), F32)
        br = br.at[0, :N_GROUPS].set(r_b_group[l]).at[0, N_GROUPS:N_GROUPS + N_EXPERTS].set(r_b_expert[l])
        return wr.astype(BF16), br

    no_counts = jnp.zeros((1, LANES), F32)

    win = a_w_in[0].astype(BF16)
    wout = a_w_out[0].astype(BF16)
    tril = jnp.tril(jnp.ones((CHUNK, CHUNK), bool))
    ws_p = jnp.where(tril, a_w_s[0], 0).astype(BF16)
    bs_p = a_b_s[0].T
    cs = min(CHUNK, S)
    ws_small = jnp.where(jnp.tril(jnp.ones((cs, cs), bool)), a_w_s[0][:, :cs, :cs], 0)
    ws_s = jnp.stack([jnp.kron(jnp.eye(Ts // cs, dtype=F32), ws_small[g]) for g in range(GMLP_GROUPS)]).astype(BF16)
    bs_s = jnp.tile(a_b_s[0][:, :cs], (1, Ts // cs)).T
    wr0, br0 = router(0)
    common = (row(norm_mix[0]), win, row(a_b_in[0]), row(a_norm_v[0]))
    xp = x_prompt.reshape(Tp, D_MODEL)
    xs_ = x_sample.reshape(Ts, D_MODEL)
    x1_p, h2_p, rt_p, rtt_p, cnt = _mixer_a(xp, mod_p[0], tpm_p, tm, *common, ws_p, bs_p, wout, row(norm_ffn[0]),
                                            wr0, br0, no_counts, with_v=False)
    x1_s, h2_s, rt_s, rtt_s, cnt, v_s = _mixer_a(xs_, mod_s[0], 1, Ts, *common, ws_s, bs_s, wout, row(norm_ffn[0]),
                                                 wr0, br0, cnt, with_v=True)
    x2_p, x2_s = _moe((h2_p, rt_p, rtt_p, x1_p, mod_p[0], tpm_p, tm), (h2_s, rt_s, rtt_s, x1_s, mod_s[0], 1, Ts), cnt,
                      e_w1, e_w3, e_w2, 0)

    wkv = w_kv.astype(BF16)
    wq = b_w_q[0].astype(BF16)
    kn = jnp.tile(k_norm, (1, HEADS))
    qn = jnp.tile(q_norm[0], (1, HEADS))
    head = np.arange(ATTN_WIDTH) // HEAD_DIM
    half_head = head[:ATTN_WIDTH // 2]
    eavg = jnp.asarray((half_head[:, None] == half_head[None, :]) / HEAD_DIM, BF16)
    kvq_w = (row(kv_norm), row(norm_mix[1]), wkv, wq, kn, qn, eavg)
    *qkv_p, kt0, kt1, kt2 = _kvq(x2_p, modkv_p, mod_p[1], tpm_p, tm, *kvq_w, batch=B)
    kv_s, q_s = _kvq(x2_s, modkv_s, mod_s[1], 1, Ts, *kvq_w)

    os_, lses = [], []
    for g, (_, dil) in enumerate(BRANCHES):
        o, lse = _band_attn(*qkv_p[3 * g:3 * g + 3], _band_bias(rel_bias, g, dil), dil)
        os_.append(o)
        lses.append(lse)
    caches = [jnp.transpose(c, (0, 2, 3, 4, 1)) for c in (cache_kv_w128, cache_kv_w512, cache_kv_w2048)]
    kvn_s = jnp.pad(kv_s.reshape(DB, S, -1), ((0, 0), (0, 8 - S), (0, 0)))
    o_s = _step_attn(q_s.reshape(DB, S, -1), kvn_s, caches, *_step_bias(rel_bias, S), DB, S)

    wo = b_w_o[0].astype(BF16)
    ex = jnp.asarray(np.arange(LANES)[:, None] == head[None, :], BF16)
    wr1, br1 = router(1)
    x3_p, h3_p, rt1_p, rtt1_p, cnt1 = _attn_out(os_, lses, x2_p, mod_p[1], tpm_p, tm, wo, ex, row(norm_ffn[1]),
                                                wr1, br1, no_counts)
    x3_s, h3_s, rt1_s, rtt1_s, cnt1 = _attn_out([o_s.reshape(Ts, ATTN_WIDTH)], [], x2_s, mod_s[1], 1, Ts, wo, ex,
                                                row(norm_ffn[1]), wr1, br1, cnt1)
    y_p, y_s = _moe((h3_p, rt1_p, rtt1_p, x3_p, mod_p[1], tpm_p, tm), (h3_s, rt1_s, rtt1_s, x3_s, mod_s[1], 1, Ts),
                    cnt1, e_w1, e_w3, e_w2, 1)

    kv_s4 = kv_s.reshape(DB, S, N_BRANCH, 2, HEADS, HEAD_DIM)

    def window(kt, w):
        n = min(w, L)
        return jnp.transpose(kt[:, :, kt.shape[2] - n:].reshape(B, 2, HEADS, HEAD_DIM, n), (0, 4, 1, 2, 3))
    return (y_p.reshape(B, L, D_MODEL), y_s.reshape(DB, S, D_MODEL),
            window(kt0, BRANCHES[0][0]), window(kt1, BRANCHES[1][0]), window(kt2, BRANCHES[2][0]),
            kv_s4[:, :, 0], kv_s4[:, :, 1], kv_s4[:, :, 2],
            v_s.reshape(1, DB, S, GMLP_WIDTH))
```

```python
import functools

import numpy as np
import jax
import jax.numpy as jnp
from jax import lax
from jax.experimental import pallas as pl
from jax.experimental.pallas import tpu as pltpu

F32 = jnp.float32
BF16 = jnp.bfloat16

D_MODEL = 1024
GMLP_WIDTH = 2048
GMLP_GROUPS = 4
GROUP_WIDTH = GMLP_WIDTH // GMLP_GROUPS
CHUNK = 128
BRANCHES = ((128, 1), (512, 4), (2048, 16))
N_BRANCH = 3
N_STEPS = 128
HEADS = 8
HEAD_DIM = 64
ATTN_WIDTH = HEADS * HEAD_DIM
NUM_BUCKETS = 32
MAX_EXACT = NUM_BUCKETS // 2
REL_MAX_DIST = 2048
N_GROUPS = 4
EXPERTS_PER_GROUP = 8
N_EXPERTS = N_GROUPS * EXPERTS_PER_GROUP
TOP_K = 2
D_EXPERT = 512
EPS = 1e-6
NEG = -1e30

LANES = 128
ROW_TILES = D_MODEL // LANES
TOKEN_TILE = 512
MOE_BLOCK = 256
DMA_UNROLL = 16
VMEM_LIMIT = 52 * 1024 * 1024


def _cparams(sem):
    return pltpu.CompilerParams(dimension_semantics=sem, vmem_limit_bytes=VMEM_LIMIT)


def _resident(shape):
    nd = len(shape)
    return pl.BlockSpec(shape, lambda *_, _nd=nd: (0,) * _nd, pipeline_mode=pl.Buffered(1))


def _gelu_tanh(x):
    return 0.5 * x * (1.0 + jnp.tanh(0.7978845608028654 * (x + 0.044715 * (x * x * x))))


def _rms(x, g):
    return x * lax.rsqrt(jnp.mean(x * x, axis=-1, keepdims=True) + EPS) * g


def _store_rows8(ref, val, n, base=0):
    for s in range(ROW_TILES):
        ref[pl.ds(base + s, n, stride=ROW_TILES), :] = val[:, s * LANES:(s + 1) * LANES]


def _load_rows8(ref, n, base=0):
    return jnp.concatenate([ref[pl.ds(base + s, n, stride=ROW_TILES), :] for s in range(ROW_TILES)], axis=1)


def _split(a):
    hi = a.astype(BF16)
    return hi, (a - hi.astype(F32)).astype(BF16)


def _split_dot(a, e_bf16):
    hi, lo = _split(a)
    return (jnp.dot(hi, e_bf16, preferred_element_type=F32) + jnp.dot(lo, e_bf16, preferred_element_type=F32))


ROUTE_LANE0 = 4


def _route_rows(l, tri_ref, carry):
    lane = lax.broadcasted_iota(jnp.int32, l.shape, 1).astype(F32)
    far = float(LANES)

    def first_lane(mask):
        return jnp.min(jnp.where(mask, lane, far), axis=1, keepdims=True)

    is_g = lane < N_GROUPS
    gl = jnp.where(is_g, l, NEG)
    gmax = jnp.max(gl, axis=1, keepdims=True)
    g_i = first_lane(jnp.logical_and(gl == gmax, is_g))
    g_p = 1.0 / jnp.sum(jnp.where(is_g, jnp.exp(gl - gmax), 0.0), axis=1, keepdims=True)
    lo = ROUTE_LANE0 + EXPERTS_PER_GROUP * g_i
    sel = jnp.logical_and(lane >= lo, lane < lo + EXPERTS_PER_GROUP)
    el = jnp.where(sel, l, NEG)
    m1 = jnp.max(el, axis=1, keepdims=True)
    i1 = first_lane(jnp.logical_and(el == m1, sel))
    sel2 = jnp.logical_and(sel, lane != i1)
    el2 = jnp.where(sel2, l, NEG)
    m2 = jnp.max(el2, axis=1, keepdims=True)
    i2 = first_lane(jnp.logical_and(el2 == m2, sel2))
    r = jnp.exp(m2 - m1)
    w1 = g_p / (1.0 + r)
    w2 = g_p * r / (1.0 + r)

    hit1 = lane == i1
    hit2 = lane == i2
    onehot = jnp.where(jnp.logical_or(hit1, hit2), 1.0, 0.0)
    before = carry[...] + jnp.dot(tri_ref[...], onehot.astype(BF16), preferred_element_type=F32)
    rank1 = jnp.sum(jnp.where(hit1, before, 0.0), axis=1, keepdims=True)
    rank2 = jnp.sum(jnp.where(hit2, before, 0.0), axis=1, keepdims=True)
    carry[...] = carry[...] + jnp.sum(onehot, axis=0, keepdims=True)

    out = jnp.zeros(l.shape, F32)
    for k, val in enumerate((i1 - ROUTE_LANE0, i2 - ROUTE_LANE0, w1, w2, rank1, rank2)):
        out = jnp.where(lane == k, val, out)
    return out


ROUTE_FIELDS = 8


def _ffn_pre(x, mod, nf_ref, wr_ref, br_ref, cin_ref, tri_ref, h_ref, rt_ref, rtt_ref, cnt_ref, carry, n):
    @pl.when(pl.program_id(0) == 0)
    def _():
        carry[...] = cin_ref[...]

    h = _rms(x, nf_ref[...]) * (1.0 + mod[:, 4 * D_MODEL:5 * D_MODEL]) + mod[:, 3 * D_MODEL:4 * D_MODEL]
    _store_rows8(h_ref, h, n)
    logits = jnp.dot(h.astype(BF16), wr_ref[...], preferred_element_type=F32) + br_ref[...]
    route = _route_rows(logits, tri_ref, carry)
    rt_ref[...] = route
    rtt_ref[...] = route.T[:ROUTE_FIELDS, :]
    cnt_ref[...] = carry[...]


def _ada_kernel(c_ref, w_ref, b_ref, o_ref):
    c = c_ref[...]
    a = (c * jax.nn.sigmoid(c)).astype(BF16)
    o_ref[...] = jnp.dot(a, w_ref[...].astype(BF16), preferred_element_type=F32) + b_ref[...]


def _ada(c, w, b, layer):
    R = c.shape[0]
    N = w.shape[2]
    tn = 1024
    return pl.pallas_call(
        _ada_kernel,
        grid=(N // tn,),
        in_specs=[pl.BlockSpec((R, D_MODEL), lambda j: (0, 0)),
                  pl.BlockSpec((None, D_MODEL, tn), lambda j: (layer, 0, j)),
                  pl.BlockSpec((None, 1, tn), lambda j: (layer, 0, j))],
        out_specs=pl.BlockSpec((R, tn), lambda j: (0, j)),
        out_shape=jax.ShapeDtypeStruct((R, N), F32),
        compiler_params=_cparams(("arbitrary",)),
        name="ada",
    )(c, w, b.reshape(b.shape[0], 1, N))


def _mixer_a_kernel(x_ref, mod_ref, nm_ref, win_ref, bin_ref, gv_ref, ws_ref, bs_ref, wout_ref,
                    nf_ref, wr_ref, br_ref, cin_ref, tri_ref, x1_ref, h2_ref, rt_ref, rtt_ref, cnt_ref, *rest, tm):
    v_refs, carry = rest[:-1], rest[-1]
    x = x_ref[...]
    mod = mod_ref[0]
    h = (_rms(x, nm_ref[...]) * (1.0 + mod[:, D_MODEL:2 * D_MODEL]) + mod[:, 0:D_MODEL]).astype(BF16)
    zv = jnp.dot(h, win_ref[:, GMLP_WIDTH:], preferred_element_type=F32) + bin_ref[:, GMLP_WIDTH:]
    v = _rms(_gelu_tanh(zv), gv_ref[...])
    if v_refs:
        v_refs[0][...] = v
    vb = v.astype(BF16)
    bs = bs_ref[...]
    acc = jnp.zeros((tm, D_MODEL), F32)
    for g in range(GMLP_GROUPS):
        lo, hi = g * GROUP_WIDTH, (g + 1) * GROUP_WIDTH
        u = _gelu_tanh(jnp.dot(h, win_ref[:, lo:hi], preferred_element_type=F32) + bin_ref[:, lo:hi])
        wg = ws_ref[g]
        gate = jnp.concatenate(
            [jnp.dot(wg, vb[c * CHUNK:(c + 1) * CHUNK, lo:hi], preferred_element_type=F32) + bs[:, g:g + 1]
             for c in range(tm // CHUNK)], axis=0)
        acc = acc + jnp.dot((u * gate).astype(BF16), wout_ref[lo:hi, :], preferred_element_type=F32)
    x1 = x + mod[:, 2 * D_MODEL:3 * D_MODEL] * acc
    x1_ref[...] = x1
    _ffn_pre(x1, mod, nf_ref, wr_ref, br_ref, cin_ref, tri_ref, h2_ref, rt_ref, rtt_ref, cnt_ref, carry, tm)


def _mod_spec(mod, tiles_per_mod):
    _, rows, width = mod.shape
    return pl.BlockSpec((1, rows, width), lambda i: (i // tiles_per_mod, 0, 0))


def _earlier_rows(tm):
    return jnp.asarray(np.tril(np.ones((tm, tm), np.float32), -1), BF16)


def _route_outs(T, tm):
    return ([pl.BlockSpec((tm, LANES), lambda i: (i, 0)), pl.BlockSpec((ROUTE_FIELDS, tm), lambda i: (0, i)),
             pl.BlockSpec((1, LANES), lambda i: (0, 0))],
            [jax.ShapeDtypeStruct((T, LANES), F32), jax.ShapeDtypeStruct((ROUTE_FIELDS, T), F32),
             jax.ShapeDtypeStruct((1, LANES), F32)])


def _mixer_a(x, mod, tiles_per_mod, tm, nm, win, bin_, gv, ws, bs_t, wout, nf, wr, br, counts, with_v):
    T = x.shape[0]
    tok = lambda w: pl.BlockSpec((tm, w), lambda i: (i, 0))
    r_specs, r_shapes = _route_outs(T, tm)
    out_shape = [jax.ShapeDtypeStruct((T, D_MODEL), F32), jax.ShapeDtypeStruct((T * ROW_TILES, LANES), F32)] + r_shapes
    out_specs = [tok(D_MODEL), pl.BlockSpec((tm * ROW_TILES, LANES), lambda i: (i, 0))] + r_specs
    if with_v:
        out_shape.append(jax.ShapeDtypeStruct((T, GMLP_WIDTH), F32))
        out_specs.append(tok(GMLP_WIDTH))
    tri = _earlier_rows(tm)
    return pl.pallas_call(
        functools.partial(_mixer_a_kernel, tm=tm),
        grid=(T // tm,),
        in_specs=[tok(D_MODEL), _mod_spec(mod, tiles_per_mod), _resident(nm.shape), _resident(win.shape),
                  _resident(bin_.shape), _resident(gv.shape), _resident(ws.shape), _resident(bs_t.shape),
                  _resident(wout.shape), _resident(nf.shape), _resident(wr.shape), _resident(br.shape),
                  _resident(counts.shape), _resident(tri.shape)],
        out_specs=out_specs,
        out_shape=out_shape,
        scratch_shapes=[pltpu.VMEM((1, LANES), F32)],
        compiler_params=_cparams(("arbitrary",)),
        name="mixer_a",
    )(x, mod, nm, win, bin_, gv, ws, bs_t, wout, nf, wr, br, counts, tri)


def _layout(rtt, counts, block):
    n_tokens = rtt.shape[1]
    counts = counts[0, ROUTE_LANE0:ROUTE_LANE0 + N_EXPERTS].astype(jnp.int32)
    padded = (counts + block - 1) // block * block
    pad_end = jnp.cumsum(padded)
    pad_start = (pad_end - padded).astype(F32)
    experts, ranks = rtt[0:TOP_K], rtt[4:4 + TOP_K]
    onehot = (experts[None] == jnp.arange(N_EXPERTS, dtype=F32)[:, None, None]).astype(F32)
    base = jnp.einsum('e,ekt->kt', pad_start, onehot, precision=lax.Precision.HIGHEST)
    slots = (ranks + base).astype(jnp.int32)
    nb = -(-n_tokens * TOP_K // block) + N_EXPERTS
    blk_e = jnp.minimum(jnp.sum(pad_end[None, :] <= (jnp.arange(nb, dtype=jnp.int32) * block)[:, None], axis=1),
                        N_EXPERTS - 1).astype(jnp.int32)
    n_used = (pad_end[-1] // block).astype(jnp.int32).reshape(1)
    starts = jnp.concatenate([jnp.ones((1,), jnp.int32), (blk_e[1:] != blk_e[:-1]).astype(jnp.int32)])
    grp = jnp.cumsum(starts) - 1
    end_blk = (pad_end // block).astype(jnp.int32)
    own = blk_e[:, None] == jnp.arange(N_EXPERTS, dtype=jnp.int32)[None, :]
    nxt_blk = jnp.sum(jnp.where(own, end_blk[None, :], 0), axis=1)
    at_nxt = nxt_blk[:, None] == jnp.arange(nb, dtype=jnp.int32)[None, :]
    nxt_e = jnp.where(nxt_blk < n_used[0], jnp.sum(jnp.where(at_nxt, blk_e[None, :], 0), axis=1), -1).astype(jnp.int32)
    zero_plan = (jnp.maximum(pad_end - block, 0).astype(jnp.int32), (padded > 0).astype(jnp.int32), n_used)
    return slots, (blk_e, n_used, nxt_e, grp.astype(jnp.int32)), zero_plan, nb


def _tile_slots(slots, ts):
    T = slots.shape[1]
    return slots.reshape(TOP_K, T // ts, ts).transpose(1, 0, 2).reshape(T // ts, 1, TOP_K * ts)


def _row_copy(src_ref, src_row, dst_ref, dst_row, sem):
    return pltpu.make_async_copy(
        src_ref.at[pl.ds(pl.multiple_of(src_row * ROW_TILES, ROW_TILES), ROW_TILES), :],
        dst_ref.at[pl.ds(pl.multiple_of(dst_row * ROW_TILES, ROW_TILES), ROW_TILES), :], sem)


def _scatter_kernel(zlo_ref, has_ref, nu_ref, slot_a_ref, slot_b_ref, src_a_ref, src_b_ref, dst_ref, zbuf, sem,
                    *, ts, n_a, n_b, nb, block):
    i = pl.program_id(0)

    def clear(row0):
        return pltpu.make_async_copy(
            zbuf, dst_ref.at[pl.ds(pl.multiple_of(row0 * ROW_TILES, ROW_TILES), block * ROW_TILES), :], sem)

    def for_clears(act):
        for e in range(N_EXPERTS):
            @pl.when(has_ref[e] > 0)
            def _(e=e):
                act(clear(zlo_ref[e]))

        def tail(j, c):
            act(clear(j * block))
            return c
        lax.fori_loop(nu_ref[0], nb, tail, 0)

    @pl.when(i == 0)
    def _():
        zbuf[...] = jnp.zeros(zbuf.shape, F32)
        for_clears(lambda c: c.start())
        for_clears(lambda c: c.wait())

    def scatter(slot_ref, src_ref, n):
        def issue(j, c):
            for u in range(DMA_UNROLL):
                t = j * DMA_UNROLL + u
                for k in range(TOP_K):
                    _row_copy(src_ref, t, dst_ref, slot_ref[0, 0, k * n + t], sem).start(priority=k)
            return c
        lax.fori_loop(0, n // DMA_UNROLL, issue, 0)
        for _ in range(TOP_K):
            pltpu.make_async_copy(src_ref, dst_ref.at[pl.ds(0, n * ROW_TILES), :], sem).wait()

    @pl.when(i < n_a)
    def _():
        scatter(slot_a_ref, src_a_ref, ts)

    @pl.when(i == n_a)
    def _():
        scatter(slot_b_ref, src_b_ref, n_b)


def _scatter(slots_a, slots_b, src_a, src_b, zero_plan, nb, block):
    n_a, _, ts2 = slots_a.shape
    ts, n_b = ts2 // TOP_K, slots_b.shape[2] // TOP_K
    tile_a = lambda i, *_: (jnp.minimum(i, n_a - 1), 0, 0)
    return pl.pallas_call(
        functools.partial(_scatter_kernel, ts=ts, n_a=n_a, n_b=n_b, nb=nb, block=block),
        grid_spec=pltpu.PrefetchScalarGridSpec(
            num_scalar_prefetch=3,
            grid=(n_a + 1,),
            in_specs=[pl.BlockSpec((1, 1, ts2), tile_a, memory_space=pltpu.SMEM),
                      pl.BlockSpec((1, 1, TOP_K * n_b), lambda i, *_: (0, 0, 0), memory_space=pltpu.SMEM),
                      pl.BlockSpec((ts * ROW_TILES, LANES), lambda i, *_: (jnp.minimum(i, n_a - 1), 0)),
                      pl.BlockSpec((n_b * ROW_TILES, LANES), lambda i, *_: (0, 0))],
            out_specs=pl.BlockSpec(memory_space=pl.ANY),
            scratch_shapes=[pltpu.VMEM((block * ROW_TILES, LANES), F32), pltpu.SemaphoreType.DMA(())]),
        out_shape=jax.ShapeDtypeStruct((nb * block * ROW_TILES, LANES), F32),
        compiler_params=_cparams(("arbitrary",)),
        name="moe_scatter",
    )(*zero_plan, slots_a, slots_b, src_a, src_b)


def _experts_kernel(be_ref, nu_ref, nxt_ref, grp_ref, x_ref, w1_hbm, w3_hbm, w2_hbm, y_ref,
                    w1b, w3b, w2b, wf1, wf3, wf2, sem, *, block, layer):
    i = pl.program_id(0)

    def weight_copies(e, buf):
        return [pltpu.make_async_copy(src.at[layer, e], dst.at[buf], sem.at[buf])
                for src, dst in ((w1_hbm, wf1), (w3_hbm, wf3), (w2_hbm, wf2))]

    @pl.when(i == 0)
    def _():
        for c in weight_copies(be_ref[0], 0):
            c.start()

    @pl.when(i < nu_ref[0])
    def _():
        @pl.when(jnp.logical_or(i == 0, be_ref[i] != be_ref[jnp.maximum(i - 1, 0)]))
        def _():
            buf = grp_ref[i] % 2
            for c in weight_copies(be_ref[i], buf):
                c.wait()

            @pl.when(nxt_ref[i] >= 0)
            def _():
                for c in weight_copies(nxt_ref[i], 1 - buf):
                    c.start()

            w1b[...] = wf1[buf].astype(BF16)
            w3b[...] = wf3[buf].astype(BF16)
            w2b[...] = wf2[buf].astype(BF16)

        x = _load_rows8(x_ref, block).astype(BF16)
        h1 = jnp.dot(x, w1b[...], preferred_element_type=F32)
        h3 = jnp.dot(x, w3b[...], preferred_element_type=F32)
        a = (h1 * jax.nn.sigmoid(h1) * h3).astype(BF16)
        _store_rows8(y_ref, jnp.dot(a, w2b[...], preferred_element_type=F32), block)


def _experts(xs, plan, nb, w1, w3, w2, layer, block):
    rows = pl.BlockSpec((block * ROW_TILES, LANES), lambda i, be, nu, nxt, grp: (jnp.minimum(i, nu[0] - 1), 0))
    hbm = pl.BlockSpec(memory_space=pl.ANY)
    return pl.pallas_call(
        functools.partial(_experts_kernel, block=block, layer=layer),
        grid_spec=pltpu.PrefetchScalarGridSpec(
            num_scalar_prefetch=4,
            grid=(nb,),
            in_specs=[rows, hbm, hbm, hbm],
            out_specs=rows,
            scratch_shapes=[pltpu.VMEM((D_MODEL, D_EXPERT), BF16), pltpu.VMEM((D_MODEL, D_EXPERT), BF16),
                            pltpu.VMEM((D_EXPERT, D_MODEL), BF16),
                            pltpu.VMEM((2, D_MODEL, D_EXPERT), F32), pltpu.VMEM((2, D_MODEL, D_EXPERT), F32),
                            pltpu.VMEM((2, D_EXPERT, D_MODEL), F32), pltpu.SemaphoreType.DMA((2,))]),
        out_shape=jax.ShapeDtypeStruct(xs.shape, F32),
        input_output_aliases={4: 0},
        compiler_params=_cparams(("arbitrary",)),
        name="moe_experts",
    )(*plan, xs, w1, w3, w2)


def _combine_kernel(slot_ref, y_ref, x_ref, mod_ref, rt_ref, o_ref, ybuf, sem, *, tc):
    def issue(j, c):
        for u in range(DMA_UNROLL):
            t = j * DMA_UNROLL + u
            for k in range(TOP_K):
                _row_copy(y_ref, slot_ref[0, 0, k * tc + t], ybuf, k * tc + t, sem).start(priority=k)
        return c
    lax.fori_loop(0, tc // DMA_UNROLL, issue, 0)
    pltpu.make_async_copy(y_ref.at[pl.ds(0, TOP_K * tc * ROW_TILES), :], ybuf, sem).wait()

    rt = rt_ref[...]
    y = (rt[:, 2:3] * _load_rows8(ybuf, tc) + rt[:, 3:4] * _load_rows8(ybuf, tc, base=tc * ROW_TILES))
    o_ref[...] = x_ref[...] + mod_ref[0][:, 5 * D_MODEL:6 * D_MODEL] * y


def _combine(slots, y, x, mod, tiles_per_mod, route, tc):
    T = x.shape[0]
    return pl.pallas_call(
        functools.partial(_combine_kernel, tc=tc),
        grid=(T // tc,),
        in_specs=[pl.BlockSpec((1, 1, tc * TOP_K), lambda i: (i, 0, 0), memory_space=pltpu.SMEM),
                  pl.BlockSpec(memory_space=pl.ANY),
                  pl.BlockSpec((tc, D_MODEL), lambda i: (i, 0)),
                  _mod_spec(mod, tiles_per_mod),
                  pl.BlockSpec((tc, LANES), lambda i: (i, 0))],
        out_specs=pl.BlockSpec((tc, D_MODEL), lambda i: (i, 0)),
        out_shape=jax.ShapeDtypeStruct((T, D_MODEL), F32),
        scratch_shapes=[pltpu.VMEM((TOP_K * tc * ROW_TILES, LANES), F32), pltpu.SemaphoreType.DMA(())],
        compiler_params=_cparams(("arbitrary",)),
        name="moe_combine",
    )(slots, y, x, mod, route)


def _moe(part_a, part_b, counts, w1, w3, w2, layer):
    Ta = part_a[3].shape[0]
    slots, plan, zero_plan, nb = _layout(jnp.concatenate([part_a[2], part_b[2]], axis=1), counts, MOE_BLOCK)
    tiled = [_tile_slots(slots[:, :Ta], part_a[6]), _tile_slots(slots[:, Ta:], part_b[6])]
    xs = _scatter(tiled[0], tiled[1], part_a[0], part_b[0], zero_plan, nb, MOE_BLOCK)
    y = _experts(xs, plan, nb, w1, w3, w2, layer, MOE_BLOCK)
    return [_combine(sl, y, x, mod, tpm, rt, tt) for sl, (_, rt, _, x, mod, tpm, tt) in zip(tiled, (part_a, part_b))]


def _kvq_kernel(x_ref, modkv_ref, mod_ref, nkv_ref, nm_ref, wkv_ref, wq_ref, kn_ref, qn_ref, eavg_ref,
                *rest, tm, keep):
    residue = keep is not None
    x = x_ref[...]
    xn = x * lax.rsqrt(jnp.mean(x * x, axis=-1, keepdims=True) + EPS)
    modkv = modkv_ref[0]
    mod = mod_ref[0]
    hk = (xn * nkv_ref[...] * (1.0 + modkv[:, D_MODEL:]) + modkv[:, :D_MODEL]).astype(BF16)
    hq = (xn * nm_ref[...] * (1.0 + mod[:, D_MODEL:2 * D_MODEL]) + mod[:, :D_MODEL]).astype(BF16)
    eavg = eavg_ref[...]

    def head_norm(a, g):
        sq = (a * a).astype(BF16)
        half = ATTN_WIDTH // 2
        ms = jnp.concatenate([jnp.dot(sq[:, :half], eavg, preferred_element_type=F32),
                              jnp.dot(sq[:, half:], eavg, preferred_element_type=F32)], axis=1)
        return a * lax.rsqrt(ms + EPS) * g

    def residue_major(val, ref, dil):
        if dil == 1:
            ref[0, 0] = val.astype(BF16)
            return
        stage = rest[-1]
        for c in range(stage.shape[0]):
            stage[c] = val[:, c * LANES:(c + 1) * LANES]
        for r in range(dil):
            rows = [stage[c, pl.ds(r, tm // dil, stride=dil), :] for c in range(stage.shape[0])]
            ref[0, r] = jnp.concatenate(rows, axis=1).astype(BF16)

    for g, (_, dil) in enumerate(BRANCHES):
        k0 = 2 * g * ATTN_WIDTH
        kvg = jnp.dot(hk, wkv_ref[:, k0:k0 + 2 * ATTN_WIDTH], preferred_element_type=F32)
        kn = head_norm(kvg[:, :ATTN_WIDTH], kn_ref[g:g + 1, :])
        vv = kvg[:, ATTN_WIDTH:]
        q0 = g * ATTN_WIDTH
        qg = jnp.dot(hq, wq_ref[:, q0:q0 + ATTN_WIDTH], preferred_element_type=F32)
        qg = head_norm(qg, qn_ref[g:g + 1, :]) * (HEAD_DIM ** -0.5)
        if residue:
            residue_major(qg, rest[3 * g], dil)
            residue_major(kn, rest[3 * g + 1], dil)
            residue_major(vv, rest[3 * g + 2], dil)
            tiles_per_batch, kept = keep

            @pl.when(pl.program_id(0) % tiles_per_batch >= tiles_per_batch - kept[g])
            def _(kn=kn, vv=vv, kt_ref=rest[3 * N_BRANCH + g]):
                kt_ref[0, :ATTN_WIDTH, :] = kn.T
                kt_ref[0, ATTN_WIDTH:, :] = vv.T
        else:
            rest[0][:, k0:k0 + ATTN_WIDTH] = kn
            rest[0][:, k0 + ATTN_WIDTH:k0 + 2 * ATTN_WIDTH] = vv
            rest[1][:, q0:q0 + ATTN_WIDTH] = qg


def _kvq(x, modkv, mod, tiles_per_mod, tm, nkv, nm, wkv, wq, kn, qn, eavg, batch=None):
    T = x.shape[0]
    tok = lambda w: pl.BlockSpec((tm, w), lambda i: (i, 0))
    kvw = 2 * N_BRANCH * ATTN_WIDTH
    if batch is not None:
        L = T // batch
        tpb = L // tm
        kept = tuple(-(-min(w, L) // tm) for w, _ in BRANCHES)
        keep = (tpb, kept)
        out_specs, out_shape = [], []
        for _, dil in BRANCHES:
            for _ in range(3):
                out_specs.append(pl.BlockSpec((1, dil, tm // dil, ATTN_WIDTH), lambda i: (i // tpb, 0, i % tpb, 0)))
                out_shape.append(jax.ShapeDtypeStruct((batch, dil, L // dil, ATTN_WIDTH), BF16))
        for kg in kept:
            out_specs.append(pl.BlockSpec((1, 2 * ATTN_WIDTH, tm),
                                          lambda i, kg=kg: (i // tpb, 0, jnp.maximum(i % tpb - (tpb - kg), 0))))
            out_shape.append(jax.ShapeDtypeStruct((batch, 2 * ATTN_WIDTH, kg * tm), F32))
        scratch = [pltpu.VMEM((ATTN_WIDTH // LANES, tm, LANES), F32)]
    else:
        keep = None
        out_specs = [tok(kvw), tok(N_BRANCH * ATTN_WIDTH)]
        out_shape = [jax.ShapeDtypeStruct((T, kvw), F32), jax.ShapeDtypeStruct((T, N_BRANCH * ATTN_WIDTH), F32)]
        scratch = []
    return pl.pallas_call(
        functools.partial(_kvq_kernel, tm=tm, keep=keep),
        grid=(T // tm,),
        in_specs=[tok(D_MODEL), _mod_spec(modkv, tiles_per_mod), _mod_spec(mod, tiles_per_mod),
                  _resident(nkv.shape), _resident(nm.shape), _resident(wkv.shape), _resident(wq.shape),
                  _resident(kn.shape), _resident(qn.shape), _resident(eavg.shape)],
        out_specs=out_specs,
        out_shape=out_shape,
        scratch_shapes=scratch,
        compiler_params=_cparams(("arbitrary",)),
        name="kvq",
    )(x, modkv, mod, nkv, nm, wkv, wq, kn, qn, eavg)


BAND_QBLOCKS = 2


def _band_attn_kernel(q_ref, kp_ref, kc_ref, vp_ref, vc_ref, bias_ref, o_ref, lse_ref):
    n = pl.program_id(2)
    col = lax.broadcasted_iota(jnp.int32, (N_STEPS, 2 * N_STEPS), 1)
    first = jnp.logical_or(col >= N_STEPS, n > 0)
    lane = lax.broadcasted_iota(jnp.int32, (N_STEPS, LANES), 1)
    k_all = jnp.concatenate([kp_ref[...], kc_ref[...]], axis=0)
    v_all = jnp.concatenate([vp_ref[...], vc_ref[...]], axis=0)
    for j in range(q_ref.shape[0] // N_STEPS):
        rows = slice(j * N_STEPS, (j + 1) * N_STEPS)
        q = q_ref[rows, :]
        k2 = k_all[j * N_STEPS:(j + 2) * N_STEPS]
        v2 = v_all[j * N_STEPS:(j + 2) * N_STEPS]
        lse_all = jnp.zeros((N_STEPS, LANES), F32)
        for h in range(HEADS):
            sl = slice(h * HEAD_DIM, (h + 1) * HEAD_DIM)
            s = lax.dot_general(q[:, sl], k2[:, sl], (((1,), (1,)), ((), ())), preferred_element_type=F32)
            s = s + bias_ref[h]
            if j == 0:
                s = jnp.where(first, s, NEG)
            m = jnp.max(s, axis=1, keepdims=True)
            p = jnp.exp(s - m)
            l = jnp.sum(p, axis=1, keepdims=True)
            o_ref[rows, sl] = jnp.dot(p.astype(BF16), v2[:, sl], preferred_element_type=F32) / l
            lse_all = jnp.where(lane == h, m + jnp.log(l), lse_all)
        lse_ref[rows, :] = lse_all


def _band_attn(q, k, v, bias, dil):
    B, _, Ld, _ = q.shape
    nq = min(BAND_QBLOCKS, Ld // N_STEPS)
    step = nq * N_STEPS
    blk = lambda w, f: pl.BlockSpec((None, None, step, w), f)
    cur = lambda b, r, n: (b, r, n, 0)
    prev_blk = pl.BlockSpec((None, None, N_STEPS, ATTN_WIDTH), lambda b, r, n: (b, r, jnp.maximum(nq * n - 1, 0), 0))
    return pl.pallas_call(
        _band_attn_kernel,
        grid=(B, dil, Ld // step),
        in_specs=[blk(ATTN_WIDTH, cur), prev_blk, blk(ATTN_WIDTH, cur), prev_blk,
                  blk(ATTN_WIDTH, cur), _resident(bias.shape)],
        out_specs=[blk(ATTN_WIDTH, cur), blk(LANES, cur)],
        out_shape=[jax.ShapeDtypeStruct((B, dil, Ld, ATTN_WIDTH), F32),
                   jax.ShapeDtypeStruct((B, dil, Ld, LANES), F32)],
        compiler_params=_cparams(("arbitrary", "arbitrary", "arbitrary")),
        name=f"band_attn_d{dil}",
    )(q, k, k, v, v, bias)


def _rel_buckets(dilation):
    n = np.arange(N_STEPS + 1) * dilation
    large = MAX_EXACT + (np.log(np.maximum(n, 1) / MAX_EXACT) / np.log(REL_MAX_DIST / MAX_EXACT)
                         * (NUM_BUCKETS - MAX_EXACT)).astype(np.int32)
    return np.where(n < MAX_EXACT, n, np.minimum(large, NUM_BUCKETS - 1)).astype(np.int32)


def _step_bias_row(rel_bias, g, dil):
    onehot = np.zeros((N_STEPS + 1, NUM_BUCKETS), np.float32)
    onehot[np.arange(N_STEPS + 1), _rel_buckets(dil)] = 1.0
    return jnp.dot(jnp.asarray(onehot), rel_bias[:, g, :].astype(F32), precision=lax.Precision.HIGHEST).T


def _band_bias(rel_bias, g, dil):
    bias = _step_bias_row(rel_bias, g, dil)
    P = 3 * N_STEPS
    neg = jnp.full((HEADS, N_STEPS), NEG, F32)
    ext = jnp.concatenate([neg, bias[:, ::-1], neg], axis=1)
    flat = jnp.broadcast_to(ext[:, None, :], (HEADS, N_STEPS, P + 1)).reshape(HEADS, N_STEPS * (P + 1))
    skew = flat[:, :N_STEPS * P].reshape(HEADS, N_STEPS, P)
    return skew[:, :, N_STEPS:]


def _step_attn_kernel(q_ref, kvn_ref, c0_ref, c1_ref, c2_ref, b0_ref, b1_ref, b2_ref, bn_ref, o_ref, *, n_new):
    caches = (c0_ref, c1_ref, c2_ref)
    cbias = (b0_ref, b1_ref, b2_ref)
    nt = (((1,), (1,)), ((), ()))
    rows = n_new * HEADS
    row = lax.broadcasted_iota(jnp.int32, (rows, ATTN_WIDTH), 0)
    lane = lax.broadcasted_iota(jnp.int32, (rows, ATTN_WIDTH), 1)
    own_head = lane // HEAD_DIM == row % HEADS
    parts = []
    for g in range(N_BRANCH):
        klo = 2 * g * ATTN_WIDTH
        qg = q_ref[0, :, g * ATTN_WIDTH:(g + 1) * ATTN_WIDTH]
        qrep = jnp.concatenate([jnp.broadcast_to(qg[s:s + 1, :], (HEADS, ATTN_WIDTH)) for s in range(n_new)], axis=0)
        qbd = jnp.where(own_head, qrep, 0.0).astype(BF16)
        kn = kvn_ref[0, :, klo:klo + ATTN_WIDTH].astype(BF16)
        vn = kvn_ref[0, :, klo + ATTN_WIDTH:klo + 2 * ATTN_WIDTH].astype(BF16)
        kt = caches[g][0, 0].astype(BF16)
        vt = caches[g][0, 1].astype(BF16)
        sc = jnp.dot(qbd, kt, preferred_element_type=F32) + cbias[g][...]
        sn = lax.dot_general(qbd, kn, nt, preferred_element_type=F32) + bn_ref[g]
        m = jnp.maximum(jnp.max(sc, axis=1, keepdims=True), jnp.max(sn, axis=1, keepdims=True))
        pc = jnp.exp(sc - m)
        pn = jnp.exp(sn - m)
        l = jnp.sum(pc, axis=1, keepdims=True) + jnp.sum(pn, axis=1, keepdims=True)
        pv = (lax.dot_general(pc.astype(BF16), vt, nt, preferred_element_type=F32)
              + jnp.dot(pn.astype(BF16), vn, preferred_element_type=F32))
        parts.append((m + jnp.log(l), pv / l))
    mx = jnp.maximum(jnp.maximum(parts[0][0], parts[1][0]), parts[2][0])
    es = [jnp.exp(lse - mx) for lse, _ in parts]
    den = es[0] + es[1] + es[2]
    mixed = (es[0] * parts[0][1] + es[1] * parts[1][1] + es[2] * parts[2][1]) / den
    mixed = jnp.where(own_head, mixed, 0.0)
    for s in range(n_new):
        o_ref[0, s:s + 1, :] = jnp.sum(mixed[s * HEADS:(s + 1) * HEADS], axis=0, keepdims=True)


def _step_bias(rel_bias, n_new):
    cache_tabs, new_tabs = [], []
    for g, (win, dil) in enumerate(BRANCHES):
        bias = _step_bias_row(rel_bias, g, dil)
        rev = bias[:, ::-1]
        per_s = []
        for s in range(n_new):
            if dil == 1:
                neg = jnp.full((HEADS, s), NEG, F32)
                per_s.append(jnp.concatenate([neg, rev[:, :win - s]], axis=1))
            else:
                cols = [rev[:, :N_STEPS] if r == s % dil else jnp.full((HEADS, N_STEPS), NEG, F32)
                        for r in range(dil)]
                per_s.append(jnp.stack(cols, axis=2).reshape(HEADS, win))
        cache_tabs.append(jnp.stack(per_s, axis=0).reshape(n_new * HEADS, win))
        dist = np.arange(n_new)[:, None] - np.arange(8)[None, :]
        ok = (dist >= 0) & (dist % dil == 0) & (np.arange(8)[None, :] < n_new)
        onehot = np.zeros((N_STEPS + 1, n_new * 8), np.float32)
        onehot[np.where(ok, dist // dil, 0).reshape(-1), np.arange(n_new * 8)] = 1.0
        tab = jnp.dot(bias, jnp.asarray(onehot), precision=lax.Precision.HIGHEST).reshape(HEADS, n_new, 8)
        new_tabs.append(jnp.where(ok[None], tab, NEG).transpose(1, 0, 2).reshape(n_new * HEADS, 8))
    return cache_tabs, jnp.stack(new_tabs)


def _step_attn(q, kvn, caches, cache_bias, new_bias, DB, S):
    cspec = lambda c: pl.BlockSpec((1,) + c.shape[1:], lambda b: (b, 0, 0, 0))
    return pl.pallas_call(
        functools.partial(_step_attn_kernel, n_new=S),
        grid=(DB,),
        in_specs=[pl.BlockSpec((1, S, N_BRANCH * ATTN_WIDTH), lambda b: (b, 0, 0)),
                  pl.BlockSpec((1, 8, 2 * N_BRANCH * ATTN_WIDTH), lambda b: (b, 0, 0)),
                  cspec(caches[0]), cspec(caches[1]), cspec(caches[2]),
                  _resident(cache_bias[0].shape), _resident(cache_bias[1].shape), _resident(cache_bias[2].shape),
                  _resident(new_bias.shape)],
        out_specs=pl.BlockSpec((1, S, ATTN_WIDTH), lambda b: (b, 0, 0)),
        out_shape=jax.ShapeDtypeStruct((DB, S, ATTN_WIDTH), F32),
        compiler_params=_cparams(("arbitrary",)),
        name="step_attn",
    )(q, kvn, caches[0], caches[1], caches[2], *cache_bias, new_bias)


def _attn_out_kernel(*refs, dils, tm):
    n_o = max(len(dils), 1)
    n_in = n_o + len(dils) + 9
    (x_ref, mod_ref, wo_ref, ex_ref, nf_ref, wr_ref, br_ref, cin_ref, tri_ref,
     x3_ref, h3_ref, rt_ref, rtt_ref, cnt_ref, carry) = refs[n_in - 9:n_in + 6]
    stages = refs[n_in + 6:]

    def natural(ref, stage, dil):
        if dil == 1:
            return ref[0, 0]
        for r in range(dil):
            val = ref[0, r]
            for c in range(stage.shape[0]):
                stage[c, pl.ds(r, tm // dil, stride=dil), :] = val[:, c * LANES:(c + 1) * LANES]
        return jnp.concatenate([stage[c] for c in range(stage.shape[0])], axis=1)

    if dils:
        lses = [natural(refs[n_o + g], stages[2 * g + 1], dil) for g, dil in enumerate(dils)]
        mx = functools.reduce(jnp.maximum, lses)
        es = [jnp.exp(l - mx) for l in lses]
        den = functools.reduce(lambda a, b: a + b, es)
        o = None
        for g, dil in enumerate(dils):
            term = _split_dot(es[g] / den, ex_ref[...]) * natural(refs[g], stages[2 * g], dil)
            o = term if o is None else o + term
    else:
        o = refs[0][...]
    mod = mod_ref[0]
    a = jnp.dot(o.astype(BF16), wo_ref[...], preferred_element_type=F32)
    x3 = x_ref[...] + mod[:, 2 * D_MODEL:3 * D_MODEL] * a
    x3_ref[...] = x3
    _ffn_pre(x3, mod, nf_ref, wr_ref, br_ref, cin_ref, tri_ref, h3_ref, rt_ref, rtt_ref, cnt_ref, carry, tm)


def _attn_out(os_, lses, x, mod, tiles_per_mod, tm, wo, ex, nf, wr, br, counts):
    T = x.shape[0]
    tok = lambda w: pl.BlockSpec((tm, w), lambda i: (i, 0))
    r_specs, r_shapes = _route_outs(T, tm)
    tri = _earlier_rows(tm)
    if lses:
        dils = tuple(o.shape[1] for o in os_)
        tpb = tiles_per_mod
        res = lambda a: pl.BlockSpec((1, a.shape[1], tm // a.shape[1], a.shape[3]), lambda i: (i // tpb, 0, i % tpb, 0))
        o_specs = [res(a) for a in os_] + [res(a) for a in lses]
        scratch = []
        for _ in dils:
            scratch += [pltpu.VMEM((ATTN_WIDTH // LANES, tm, LANES), F32), pltpu.VMEM((1, tm, LANES), F32)]
    else:
        dils, o_specs, scratch = (), [tok(ATTN_WIDTH)], []
    return pl.pallas_call(
        functools.partial(_attn_out_kernel, dils=dils, tm=tm),
        grid=(T // tm,),
        in_specs=(o_specs + [tok(D_MODEL), _mod_spec(mod, tiles_per_mod), _resident(wo.shape), _resident(ex.shape),
                             _resident(nf.shape), _resident(wr.shape), _resident(br.shape),
                             _resident(counts.shape), _resident(tri.shape)]),
        out_specs=[tok(D_MODEL), pl.BlockSpec((tm * ROW_TILES, LANES), lambda i: (i, 0))] + r_specs,
        out_shape=[jax.ShapeDtypeStruct((T, D_MODEL), F32),
                   jax.ShapeDtypeStruct((T * ROW_TILES, LANES), F32)] + r_shapes,
        scratch_shapes=[pltpu.VMEM((1, LANES), F32)] + scratch,
        compiler_params=_cparams(("arbitrary",)),
        name="attn_out",
    )(*os_, *lses, x, mod, wo, ex, nf, wr, br, counts, tri)


def kernel(x_prompt, x_sample, cache_kv_w128, cache_kv_w512, cache_kv_w2048, c_prompt, c_sample, ada_w, ada_b, norm_mix, norm_ffn, a_w_in, a_b_in, a_norm_v, a_w_s, a_b_s, a_w_out, kv_ada_w, kv_ada_b, kv_norm, w_kv, k_norm, rel_bias, b_w_q, q_norm, b_w_o, r_w_group, r_b_group, r_w_expert, r_b_expert, e_w1, e_w3, e_w2):
    B, L, _ = x_prompt.shape
    DB, S, _ = x_sample.shape
    Tp, Ts = B * L, DB * S
    tm = TOKEN_TILE
    tpm_p = L // tm

    c_all = jnp.concatenate([c_prompt, c_sample], axis=0)
    R = c_all.shape[0]
    c_all = jnp.pad(c_all, ((0, -R % 8), (0, 0)))
    mods = [_ada(c_all, ada_w, ada_b, l) for l in range(2)]
    modkv = _ada(c_all, kv_ada_w[None], kv_ada_b[None], 0)

    def split_mod(m):
        return m[:B, None, :], jnp.repeat(m[B:B + DB], S, axis=0)[None]
    mod_p, mod_s = zip(*[split_mod(m) for m in mods])
    modkv_p, modkv_s = split_mod(modkv)

    row = lambda a: a.reshape(1, -1)

    def router(l):
        wr = jnp.zeros((D_MODEL, LANES), F32)
        wr = wr.at[:, :N_GROUPS].set(r_w_group[l]).at[:, N_GROUPS:N_GROUPS + N_EXPERTS].set(r_w_expert[l])
        br = jnp.zeros((1, LANES), F32)
        br = br.at[0, :N_GROUPS].set(r_b_group[l]).at[0, N_GROUPS:N_GROUPS + N_EXPERTS].set(r_b_expert[l])
        return wr.astype(BF16), br

    no_counts = jnp.zeros((1, LANES), F32)

    win = a_w_in[0].astype(BF16)
    wout = a_w_out[0].astype(BF16)
    tril = jnp.tril(jnp.ones((CHUNK, CHUNK), bool))
    ws_p = jnp.where(tril, a_w_s[0], 0).astype(BF16)
    bs_p = a_b_s[0].T
    cs = min(CHUNK, S)
    ws_small = jnp.where(jnp.tril(jnp.ones((cs, cs), bool)), a_w_s[0][:, :cs, :cs], 0)
    ws_s = jnp.stack([jnp.kron(jnp.eye(Ts // cs, dtype=F32), ws_small[g]) for g in range(GMLP_GROUPS)]).astype(BF16)
    bs_s = jnp.tile(a_b_s[0][:, :cs], (1, Ts // cs)).T
    wr0, br0 = router(0)
    common = (row(norm_mix[0]), win, row(a_b_in[0]), row(a_norm_v[0]))
    xp = x_prompt.reshape(Tp, D_MODEL)
    xs_ = x_sample.reshape(Ts, D_MODEL)
    x1_p, h2_p, rt_p, rtt_p, cnt = _mixer_a(xp, mod_p[0], tpm_p, tm, *common, ws_p, bs_p, wout, row(norm_ffn[0]),
                                            wr0, br0, no_counts, with_v=False)
    x1_s, h2_s, rt_s, rtt_s, cnt, v_s = _mixer_a(xs_, mod_s[0], 1, Ts, *common, ws_s, bs_s, wout, row(norm_ffn[0]),
                                                 wr0, br0, cnt, with_v=True)
    x2_p, x2_s = _moe((h2_p, rt_p, rtt_p, x1_p, mod_p[0], tpm_p, tm), (h2_s, rt_s, rtt_s, x1_s, mod_s[0], 1, Ts), cnt,
                      e_w1, e_w3, e_w2, 0)

    wkv = w_kv.astype(BF16)
    wq = b_w_q[0].astype(BF16)
    kn = jnp.tile(k_norm, (1, HEADS))
    qn = jnp.tile(q_norm[0], (1, HEADS))
    head = np.arange(ATTN_WIDTH) // HEAD_DIM
    half_head = head[:ATTN_WIDTH // 2]
    eavg = jnp.asarray((half_head[:, None] == half_head[None, :]) / HEAD_DIM, BF16)
    kvq_w = (row(kv_norm), row(norm_mix[1]), wkv, wq, kn, qn, eavg)
    *qkv_p, kt0, kt1, kt2 = _kvq(x2_p, modkv_p, mod_p[1], tpm_p, tm, *kvq_w, batch=B)
    kv_s, q_s = _kvq(x2_s, modkv_s, mod_s[1], 1, Ts, *kvq_w)

    os_, lses = [], []
    for g, (_, dil) in enumerate(BRANCHES):
        o, lse = _band_attn(*qkv_p[3 * g:3 * g + 3], _band_bias(rel_bias, g, dil), dil)
        os_.append(o)
        lses.append(lse)
    caches = [jnp.transpose(c, (0, 2, 3, 4, 1)).reshape(DB, 2, ATTN_WIDTH, c.shape[1])
              for c in (cache_kv_w128, cache_kv_w512, cache_kv_w2048)]
    kvn_s = jnp.pad(kv_s.reshape(DB, S, -1), ((0, 0), (0, 8 - S), (0, 0)))
    o_s = _step_attn(q_s.reshape(DB, S, -1), kvn_s, caches, *_step_bias(rel_bias, S), DB, S)

    wo = b_w_o[0].astype(BF16)
    ex = jnp.asarray(np.arange(LANES)[:, None] == head[None, :], BF16)
    wr1, br1 = router(1)
    x3_p, h3_p, rt1_p, rtt1_p, cnt1 = _attn_out(os_, lses, x2_p, mod_p[1], tpm_p, tm, wo, ex, row(norm_ffn[1]),
                                                wr1, br1, no_counts)
    x3_s, h3_s, rt1_s, rtt1_s, cnt1 = _attn_out([o_s.reshape(Ts, ATTN_WIDTH)], [], x2_s, mod_s[1], 1, Ts, wo, ex,
                                                row(norm_ffn[1]), wr1, br1, cnt1)
    y_p, y_s = _moe((h3_p, rt1_p, rtt1_p, x3_p, mod_p[1], tpm_p, tm), (h3_s, rt1_s, rtt1_s, x3_s, mod_s[1], 1, Ts),
                    cnt1, e_w1, e_w3, e_w2, 1)

    kv_s4 = kv_s.reshape(DB, S, N_BRANCH, 2, HEADS, HEAD_DIM)

    def window(kt, w):
        n = min(w, L)
        return jnp.transpose(kt[:, :, kt.shape[2] - n:].reshape(B, 2, HEADS, HEAD_DIM, n), (0, 4, 1, 2, 3))
    return (y_p.reshape(B, L, D_MODEL), y_s.reshape(DB, S, D_MODEL),
            window(kt0, BRANCHES[0][0]), window(kt1, BRANCHES[1][0]), window(kt2, BRANCHES[2][0]),
            kv_s4[:, :, 0], kv_s4[:, :, 1], kv_s4[:, :, 2],
            v_s.reshape(1, DB, S, GMLP_WIDTH))
```

```python
import functools

import numpy as np
import jax
import jax.numpy as jnp
from jax import lax
from jax.experimental import pallas as pl
from jax.experimental.pallas import tpu as pltpu

F32 = jnp.float32
BF16 = jnp.bfloat16

D_MODEL = 1024
GMLP_WIDTH = 2048
GMLP_GROUPS = 4
GROUP_WIDTH = GMLP_WIDTH // GMLP_GROUPS
CHUNK = 128
BRANCHES = ((128, 1), (512, 4), (2048, 16))
N_BRANCH = 3
N_STEPS = 128
HEADS = 8
HEAD_DIM = 64
ATTN_WIDTH = HEADS * HEAD_DIM
NUM_BUCKETS = 32
MAX_EXACT = NUM_BUCKETS // 2
REL_MAX_DIST = 2048
N_GROUPS = 4
EXPERTS_PER_GROUP = 8
N_EXPERTS = N_GROUPS * EXPERTS_PER_GROUP
TOP_K = 2
D_EXPERT = 512
EPS = 1e-6
NEG = -1e30

LANES = 128
ROW_TILES = D_MODEL // LANES
TOKEN_TILE = 512
MOE_BLOCK = 256
DMA_UNROLL = 16
VMEM_LIMIT = 52 * 1024 * 1024


def _cparams(sem):
    return pltpu.CompilerParams(dimension_semantics=sem, vmem_limit_bytes=VMEM_LIMIT)


def _resident(shape):
    nd = len(shape)
    return pl.BlockSpec(shape, lambda *_, _nd=nd: (0,) * _nd, pipeline_mode=pl.Buffered(1))


def _gelu_tanh(x):
    return 0.5 * x * (1.0 + jnp.tanh(0.7978845608028654 * (x + 0.044715 * (x * x * x))))


def _rms(x, g):
    return x * lax.rsqrt(jnp.mean(x * x, axis=-1, keepdims=True) + EPS) * g


def _store_rows8(ref, val, n, base=0):
    for s in range(ROW_TILES):
        ref[pl.ds(base + s, n, stride=ROW_TILES), :] = val[:, s * LANES:(s + 1) * LANES]


def _load_rows8(ref, n, base=0):
    return jnp.concatenate([ref[pl.ds(base + s, n, stride=ROW_TILES), :] for s in range(ROW_TILES)], axis=1)


def _split(a):
    hi = a.astype(BF16)
    return hi, (a - hi.astype(F32)).astype(BF16)


def _split_dot(a, e_bf16):
    hi, lo = _split(a)
    return (jnp.dot(hi, e_bf16, preferred_element_type=F32) + jnp.dot(lo, e_bf16, preferred_element_type=F32))


ROUTE_LANE0 = 4


def _route_rows(l, tri_ref, carry):
    lane = lax.broadcasted_iota(jnp.int32, l.shape, 1).astype(F32)
    far = float(LANES)

    def first_lane(mask):
        return jnp.min(jnp.where(mask, lane, far), axis=1, keepdims=True)

    is_g = lane < N_GROUPS
    gl = jnp.where(is_g, l, NEG)
    gmax = jnp.max(gl, axis=1, keepdims=True)
    g_i = first_lane(jnp.logical_and(gl == gmax, is_g))
    g_p = 1.0 / jnp.sum(jnp.where(is_g, jnp.exp(gl - gmax), 0.0), axis=1, keepdims=True)
    lo = ROUTE_LANE0 + EXPERTS_PER_GROUP * g_i
    sel = jnp.logical_and(lane >= lo, lane < lo + EXPERTS_PER_GROUP)
    el = jnp.where(sel, l, NEG)
    m1 = jnp.max(el, axis=1, keepdims=True)
    i1 = first_lane(jnp.logical_and(el == m1, sel))
    sel2 = jnp.logical_and(sel, lane != i1)
    el2 = jnp.where(sel2, l, NEG)
    m2 = jnp.max(el2, axis=1, keepdims=True)
    i2 = first_lane(jnp.logical_and(el2 == m2, sel2))
    r = jnp.exp(m2 - m1)
    w1 = g_p / (1.0 + r)
    w2 = g_p * r / (1.0 + r)

    hit1 = lane == i1
    hit2 = lane == i2
    onehot = jnp.where(jnp.logical_or(hit1, hit2), 1.0, 0.0)
    before = carry[...] + jnp.dot(tri_ref[...], onehot.astype(BF16), preferred_element_type=F32)
    rank1 = jnp.sum(jnp.where(hit1, before, 0.0), axis=1, keepdims=True)
    rank2 = jnp.sum(jnp.where(hit2, before, 0.0), axis=1, keepdims=True)
    carry[...] = carry[...] + jnp.sum(onehot, axis=0, keepdims=True)

    out = jnp.zeros(l.shape, F32)
    for k, val in enumerate((i1 - ROUTE_LANE0, i2 - ROUTE_LANE0, w1, w2, rank1, rank2)):
        out = jnp.where(lane == k, val, out)
    return out


ROUTE_FIELDS = 8


def _ffn_pre(x, mod, nf_ref, wr_ref, br_ref, cin_ref, tri_ref, h_ref, rt_ref, rtt_ref, cnt_ref, carry, n):
    @pl.when(pl.program_id(0) == 0)
    def _():
        carry[...] = cin_ref[...]

    h = _rms(x, nf_ref[...]) * (1.0 + mod[:, 4 * D_MODEL:5 * D_MODEL]) + mod[:, 3 * D_MODEL:4 * D_MODEL]
    _store_rows8(h_ref, h, n)
    logits = jnp.dot(h.astype(BF16), wr_ref[...], preferred_element_type=F32) + br_ref[...]
    route = _route_rows(logits, tri_ref, carry)
    rt_ref[...] = route
    rtt_ref[...] = route.T[:ROUTE_FIELDS, :]
    cnt_ref[...] = carry[...]


def _ada_kernel(c_ref, w_ref, b_ref, o_ref):
    c = c_ref[...]
    a = (c * jax.nn.sigmoid(c)).astype(BF16)
    o_ref[...] = jnp.dot(a, w_ref[...].astype(BF16), preferred_element_type=F32) + b_ref[...]


def _ada(c, w, b, layer):
    R = c.shape[0]
    N = w.shape[2]
    tn = 1024
    return pl.pallas_call(
        _ada_kernel,
        grid=(N // tn,),
        in_specs=[pl.BlockSpec((R, D_MODEL), lambda j: (0, 0)),
                  pl.BlockSpec((None, D_MODEL, tn), lambda j: (layer, 0, j)),
                  pl.BlockSpec((None, 1, tn), lambda j: (layer, 0, j))],
        out_specs=pl.BlockSpec((R, tn), lambda j: (0, j)),
        out_shape=jax.ShapeDtypeStruct((R, N), F32),
        compiler_params=_cparams(("arbitrary",)),
        name="ada",
    )(c, w, b.reshape(b.shape[0], 1, N))


def _mixer_a_kernel(x_ref, mod_ref, nm_ref, win_ref, bin_ref, gv_ref, ws_ref, bs_ref, wout_ref,
                    nf_ref, wr_ref, br_ref, cin_ref, tri_ref, x1_ref, h2_ref, rt_ref, rtt_ref, cnt_ref, *rest, tm):
    v_refs, carry = rest[:-1], rest[-1]
    x = x_ref[...]
    mod = mod_ref[0]
    h = (_rms(x, nm_ref[...]) * (1.0 + mod[:, D_MODEL:2 * D_MODEL]) + mod[:, 0:D_MODEL]).astype(BF16)
    zv = jnp.dot(h, win_ref[:, GMLP_WIDTH:], preferred_element_type=F32) + bin_ref[:, GMLP_WIDTH:]
    v = _rms(_gelu_tanh(zv), gv_ref[...])
    if v_refs:
        v_refs[0][...] = v
    vb = v.astype(BF16)
    bs = bs_ref[...]
    acc = jnp.zeros((tm, D_MODEL), F32)
    for g in range(GMLP_GROUPS):
        lo, hi = g * GROUP_WIDTH, (g + 1) * GROUP_WIDTH
        u = _gelu_tanh(jnp.dot(h, win_ref[:, lo:hi], preferred_element_type=F32) + bin_ref[:, lo:hi])
        wg = ws_ref[g]
        gate = jnp.concatenate(
            [jnp.dot(wg, vb[c * CHUNK:(c + 1) * CHUNK, lo:hi], preferred_element_type=F32) + bs[:, g:g + 1]
             for c in range(tm // CHUNK)], axis=0)
        acc = acc + jnp.dot((u * gate).astype(BF16), wout_ref[lo:hi, :], preferred_element_type=F32)
    x1 = x + mod[:, 2 * D_MODEL:3 * D_MODEL] * acc
    x1_ref[...] = x1
    _ffn_pre(x1, mod, nf_ref, wr_ref, br_ref, cin_ref, tri_ref, h2_ref, rt_ref, rtt_ref, cnt_ref, carry, tm)


def _mod_spec(mod, tiles_per_mod):
    _, rows, width = mod.shape
    return pl.BlockSpec((1, rows, width), lambda i: (i // tiles_per_mod, 0, 0))


def _earlier_rows(tm):
    return jnp.asarray(np.tril(np.ones((tm, tm), np.float32), -1), BF16)


def _route_outs(T, tm):
    return ([pl.BlockSpec((tm, LANES), lambda i: (i, 0)), pl.BlockSpec((ROUTE_FIELDS, tm), lambda i: (0, i)),
             pl.BlockSpec((1, LANES), lambda i: (0, 0))],
            [jax.ShapeDtypeStruct((T, LANES), F32), jax.ShapeDtypeStruct((ROUTE_FIELDS, T), F32),
             jax.ShapeDtypeStruct((1, LANES), F32)])


def _mixer_a(x, mod, tiles_per_mod, tm, nm, win, bin_, gv, ws, bs_t, wout, nf, wr, br, counts, with_v):
    T = x.shape[0]
    tok = lambda w: pl.BlockSpec((tm, w), lambda i: (i, 0))
    r_specs, r_shapes = _route_outs(T, tm)
    out_shape = [jax.ShapeDtypeStruct((T, D_MODEL), F32), jax.ShapeDtypeStruct((T * ROW_TILES, LANES), F32)] + r_shapes
    out_specs = [tok(D_MODEL), pl.BlockSpec((tm * ROW_TILES, LANES), lambda i: (i, 0))] + r_specs
    if with_v:
        out_shape.append(jax.ShapeDtypeStruct((T, GMLP_WIDTH), F32))
        out_specs.append(tok(GMLP_WIDTH))
    tri = _earlier_rows(tm)
    return pl.pallas_call(
        functools.partial(_mixer_a_kernel, tm=tm),
        grid=(T // tm,),
        in_specs=[tok(D_MODEL), _mod_spec(mod, tiles_per_mod), _resident(nm.shape), _resident(win.shape),
                  _resident(bin_.shape), _resident(gv.shape), _resident(ws.shape), _resident(bs_t.shape),
                  _resident(wout.shape), _resident(nf.shape), _resident(wr.shape), _resident(br.shape),
                  _resident(counts.shape), _resident(tri.shape)],
        out_specs=out_specs,
        out_shape=out_shape,
        scratch_shapes=[pltpu.VMEM((1, LANES), F32)],
        compiler_params=_cparams(("arbitrary",)),
        name="mixer_a",
    )(x, mod, nm, win, bin_, gv, ws, bs_t, wout, nf, wr, br, counts, tri)


def _layout(rtt, counts, block):
    n_tokens = rtt.shape[1]
    counts = counts[0, ROUTE_LANE0:ROUTE_LANE0 + N_EXPERTS].astype(jnp.int32)
    padded = (counts + block - 1) // block * block
    pad_end = jnp.cumsum(padded)
    pad_start = (pad_end - padded).astype(F32)
    experts, ranks = rtt[0:TOP_K], rtt[4:4 + TOP_K]
    onehot = (experts[None] == jnp.arange(N_EXPERTS, dtype=F32)[:, None, None]).astype(F32)
    base = jnp.einsum('e,ekt->kt', pad_start, onehot, precision=lax.Precision.HIGHEST)
    slots = (ranks + base).astype(jnp.int32)
    nb = -(-n_tokens * TOP_K // block) + N_EXPERTS
    blk_e = jnp.minimum(jnp.sum(pad_end[None, :] <= (jnp.arange(nb, dtype=jnp.int32) * block)[:, None], axis=1),
                        N_EXPERTS - 1).astype(jnp.int32)
    n_used = (pad_end[-1] // block).astype(jnp.int32).reshape(1)
    starts = jnp.concatenate([jnp.ones((1,), jnp.int32), (blk_e[1:] != blk_e[:-1]).astype(jnp.int32)])
    grp = jnp.cumsum(starts) - 1
    end_blk = (pad_end // block).astype(jnp.int32)
    own = blk_e[:, None] == jnp.arange(N_EXPERTS, dtype=jnp.int32)[None, :]
    nxt_blk = jnp.sum(jnp.where(own, end_blk[None, :], 0), axis=1)
    at_nxt = nxt_blk[:, None] == jnp.arange(nb, dtype=jnp.int32)[None, :]
    nxt_e = jnp.where(nxt_blk < n_used[0], jnp.sum(jnp.where(at_nxt, blk_e[None, :], 0), axis=1), -1).astype(jnp.int32)
    zero_plan = (jnp.maximum(pad_end - block, 0).astype(jnp.int32), (padded > 0).astype(jnp.int32), n_used)
    return slots, (blk_e, n_used, nxt_e, grp.astype(jnp.int32)), zero_plan, nb


def _tile_slots(slots, ts):
    T = slots.shape[1]
    return slots.reshape(TOP_K, T // ts, ts).transpose(1, 0, 2).reshape(T // ts, 1, TOP_K * ts)


def _row_copy(src_ref, src_row, dst_ref, dst_row, sem):
    return pltpu.make_async_copy(
        src_ref.at[pl.ds(pl.multiple_of(src_row * ROW_TILES, ROW_TILES), ROW_TILES), :],
        dst_ref.at[pl.ds(pl.multiple_of(dst_row * ROW_TILES, ROW_TILES), ROW_TILES), :], sem)


def _scatter_kernel(zlo_ref, has_ref, nu_ref, slot_a_ref, slot_b_ref, src_a_ref, src_b_ref, dst_ref, zbuf, sem,
                    *, ts, n_a, n_b, nb, block):
    i = pl.program_id(0)

    def clear(row0):
        return pltpu.make_async_copy(
            zbuf, dst_ref.at[pl.ds(pl.multiple_of(row0 * ROW_TILES, ROW_TILES), block * ROW_TILES), :], sem)

    def for_clears(act):
        for e in range(N_EXPERTS):
            @pl.when(has_ref[e] > 0)
            def _(e=e):
                act(clear(zlo_ref[e]))

        def tail(j, c):
            act(clear(j * block))
            return c
        lax.fori_loop(nu_ref[0], nb, tail, 0)

    @pl.when(i == 0)
    def _():
        zbuf[...] = jnp.zeros(zbuf.shape, F32)
        for_clears(lambda c: c.start())
        for_clears(lambda c: c.wait())

    def scatter(slot_ref, src_ref, n):
        def issue(j, c):
            for u in range(DMA_UNROLL):
                t = j * DMA_UNROLL + u
                for k in range(TOP_K):
                    _row_copy(src_ref, t, dst_ref, slot_ref[0, 0, k * n + t], sem).start(priority=k)
            return c
        lax.fori_loop(0, n // DMA_UNROLL, issue, 0)
        for _ in range(TOP_K):
            pltpu.make_async_copy(src_ref, dst_ref.at[pl.ds(0, n * ROW_TILES), :], sem).wait()

    @pl.when(i < n_a)
    def _():
        scatter(slot_a_ref, src_a_ref, ts)

    @pl.when(i == n_a)
    def _():
        scatter(slot_b_ref, src_b_ref, n_b)


def _scatter(slots_a, slots_b, src_a, src_b, zero_plan, nb, block):
    n_a, _, ts2 = slots_a.shape
    ts, n_b = ts2 // TOP_K, slots_b.shape[2] // TOP_K
    tile_a = lambda i, *_: (jnp.minimum(i, n_a - 1), 0, 0)
    return pl.pallas_call(
        functools.partial(_scatter_kernel, ts=ts, n_a=n_a, n_b=n_b, nb=nb, block=block),
        grid_spec=pltpu.PrefetchScalarGridSpec(
            num_scalar_prefetch=3,
            grid=(n_a + 1,),
            in_specs=[pl.BlockSpec((1, 1, ts2), tile_a, memory_space=pltpu.SMEM),
                      pl.BlockSpec((1, 1, TOP_K * n_b), lambda i, *_: (0, 0, 0), memory_space=pltpu.SMEM),
                      pl.BlockSpec((ts * ROW_TILES, LANES), lambda i, *_: (jnp.minimum(i, n_a - 1), 0)),
                      pl.BlockSpec((n_b * ROW_TILES, LANES), lambda i, *_: (0, 0))],
            out_specs=pl.BlockSpec(memory_space=pl.ANY),
            scratch_shapes=[pltpu.VMEM((block * ROW_TILES, LANES), F32), pltpu.SemaphoreType.DMA(())]),
        out_shape=jax.ShapeDtypeStruct((nb * block * ROW_TILES, LANES), F32),
        compiler_params=_cparams(("arbitrary",)),
        name="moe_scatter",
    )(*zero_plan, slots_a, slots_b, src_a, src_b)


def _experts_kernel(be_ref, nu_ref, nxt_ref, grp_ref, x_ref, w1_hbm, w3_hbm, w2_hbm, y_ref,
                    w1b, w3b, w2b, wf1, wf3, wf2, sem, *, block, layer):
    i = pl.program_id(0)

    def weight_copies(e, buf):
        return [pltpu.make_async_copy(src.at[layer, e], dst.at[buf], sem.at[buf])
                for src, dst in ((w1_hbm, wf1), (w3_hbm, wf3), (w2_hbm, wf2))]

    @pl.when(i == 0)
    def _():
        for c in weight_copies(be_ref[0], 0):
            c.start()

    @pl.when(i < nu_ref[0])
    def _():
        @pl.when(jnp.logical_or(i == 0, be_ref[i] != be_ref[jnp.maximum(i - 1, 0)]))
        def _():
            buf = grp_ref[i] % 2
            for c in weight_copies(be_ref[i], buf):
                c.wait()

            @pl.when(nxt_ref[i] >= 0)
            def _():
                for c in weight_copies(nxt_ref[i], 1 - buf):
                    c.start()

            w1b[...] = wf1[buf].astype(BF16)
            w3b[...] = wf3[buf].astype(BF16)
            w2b[...] = wf2[buf].astype(BF16)

        x = _load_rows8(x_ref, block).astype(BF16)
        h1 = jnp.dot(x, w1b[...], preferred_element_type=F32)
        h3 = jnp.dot(x, w3b[...], preferred_element_type=F32)
        a = (h1 * jax.nn.sigmoid(h1) * h3).astype(BF16)
        _store_rows8(y_ref, jnp.dot(a, w2b[...], preferred_element_type=F32), block)


def _experts(xs, plan, nb, w1, w3, w2, layer, block):
    rows = pl.BlockSpec((block * ROW_TILES, LANES), lambda i, be, nu, nxt, grp: (jnp.minimum(i, nu[0] - 1), 0))
    hbm = pl.BlockSpec(memory_space=pl.ANY)
    return pl.pallas_call(
        functools.partial(_experts_kernel, block=block, layer=layer),
        grid_spec=pltpu.PrefetchScalarGridSpec(
            num_scalar_prefetch=4,
            grid=(nb,),
            in_specs=[rows, hbm, hbm, hbm],
            out_specs=rows,
            scratch_shapes=[pltpu.VMEM((D_MODEL, D_EXPERT), BF16), pltpu.VMEM((D_MODEL, D_EXPERT), BF16),
                            pltpu.VMEM((D_EXPERT, D_MODEL), BF16),
                            pltpu.VMEM((2, D_MODEL, D_EXPERT), F32), pltpu.VMEM((2, D_MODEL, D_EXPERT), F32),
                            pltpu.VMEM((2, D_EXPERT, D_MODEL), F32), pltpu.SemaphoreType.DMA((2,))]),
        out_shape=jax.ShapeDtypeStruct(xs.shape, F32),
        input_output_aliases={4: 0},
        compiler_params=_cparams(("arbitrary",)),
        name="moe_experts",
    )(*plan, xs, w1, w3, w2)


def _combine_kernel(slot_ref, y_ref, x_ref, mod_ref, rt_ref, o_ref, ybuf, sem, *, tc):
    def issue(j, c):
        for u in range(DMA_UNROLL):
            t = j * DMA_UNROLL + u
            for k in range(TOP_K):
                _row_copy(y_ref, slot_ref[0, 0, k * tc + t], ybuf, k * tc + t, sem).start(priority=k)
        return c
    lax.fori_loop(0, tc // DMA_UNROLL, issue, 0)
    pltpu.make_async_copy(y_ref.at[pl.ds(0, TOP_K * tc * ROW_TILES), :], ybuf, sem).wait()

    rt = rt_ref[...]
    y = (rt[:, 2:3] * _load_rows8(ybuf, tc) + rt[:, 3:4] * _load_rows8(ybuf, tc, base=tc * ROW_TILES))
    o_ref[...] = x_ref[...] + mod_ref[0][:, 5 * D_MODEL:6 * D_MODEL] * y


def _combine(slots, y, x, mod, tiles_per_mod, route, tc):
    T = x.shape[0]
    return pl.pallas_call(
        functools.partial(_combine_kernel, tc=tc),
        grid=(T // tc,),
        in_specs=[pl.BlockSpec((1, 1, tc * TOP_K), lambda i: (i, 0, 0), memory_space=pltpu.SMEM),
                  pl.BlockSpec(memory_space=pl.ANY),
                  pl.BlockSpec((tc, D_MODEL), lambda i: (i, 0)),
                  _mod_spec(mod, tiles_per_mod),
                  pl.BlockSpec((tc, LANES), lambda i: (i, 0))],
        out_specs=pl.BlockSpec((tc, D_MODEL), lambda i: (i, 0)),
        out_shape=jax.ShapeDtypeStruct((T, D_MODEL), F32),
        scratch_shapes=[pltpu.VMEM((TOP_K * tc * ROW_TILES, LANES), F32), pltpu.SemaphoreType.DMA(())],
        compiler_params=_cparams(("arbitrary",)),
        name="moe_combine",
    )(slots, y, x, mod, route)


def _moe(part_a, part_b, counts, w1, w3, w2, layer):
    Ta = part_a[3].shape[0]
    slots, plan, zero_plan, nb = _layout(jnp.concatenate([part_a[2], part_b[2]], axis=1), counts, MOE_BLOCK)
    tiled = [_tile_slots(slots[:, :Ta], part_a[6]), _tile_slots(slots[:, Ta:], part_b[6])]
    xs = _scatter(tiled[0], tiled[1], part_a[0], part_b[0], zero_plan, nb, MOE_BLOCK)
    y = _experts(xs, plan, nb, w1, w3, w2, layer, MOE_BLOCK)
    return [_combine(sl, y, x, mod, tpm, rt, tt) for sl, (_, rt, _, x, mod, tpm, tt) in zip(tiled, (part_a, part_b))]


def _kvq_kernel(x_ref, modkv_ref, mod_ref, nkv_ref, nm_ref, wkv_ref, wq_ref, kn_ref, qn_ref, eavg_ref,
                *rest, tm, keep):
    residue = keep is not None
    x = x_ref[...]
    xn = x * lax.rsqrt(jnp.mean(x * x, axis=-1, keepdims=True) + EPS)
    modkv = modkv_ref[0]
    mod = mod_ref[0]
    hk = (xn * nkv_ref[...] * (1.0 + modkv[:, D_MODEL:]) + modkv[:, :D_MODEL]).astype(BF16)
    hq = (xn * nm_ref[...] * (1.0 + mod[:, D_MODEL:2 * D_MODEL]) + mod[:, :D_MODEL]).astype(BF16)
    eavg = eavg_ref[...]

    def head_norm(a, g):
        sq = (a * a).astype(BF16)
        half = ATTN_WIDTH // 2
        ms = jnp.concatenate([jnp.dot(sq[:, :half], eavg, preferred_element_type=F32),
                              jnp.dot(sq[:, half:], eavg, preferred_element_type=F32)], axis=1)
        return a * lax.rsqrt(ms + EPS) * g

    def residue_major(val, ref, dil):
        if dil == 1:
            ref[0, 0] = val.astype(BF16)
            return
        stage = rest[-1]
        for c in range(stage.shape[0]):
            stage[c] = val[:, c * LANES:(c + 1) * LANES]
        for r in range(dil):
            rows = [stage[c, pl.ds(r, tm // dil, stride=dil), :] for c in range(stage.shape[0])]
            ref[0, r] = jnp.concatenate(rows, axis=1).astype(BF16)

    for g, (_, dil) in enumerate(BRANCHES):
        k0 = 2 * g * ATTN_WIDTH
        kvg = jnp.dot(hk, wkv_ref[:, k0:k0 + 2 * ATTN_WIDTH], preferred_element_type=F32)
        kn = head_norm(kvg[:, :ATTN_WIDTH], kn_ref[g:g + 1, :])
        vv = kvg[:, ATTN_WIDTH:]
        q0 = g * ATTN_WIDTH
        qg = jnp.dot(hq, wq_ref[:, q0:q0 + ATTN_WIDTH], preferred_element_type=F32)
        qg = head_norm(qg, qn_ref[g:g + 1, :]) * (HEAD_DIM ** -0.5)
        if residue:
            residue_major(qg, rest[3 * g], dil)
            residue_major(kn, rest[3 * g + 1], dil)
            residue_major(vv, rest[3 * g + 2], dil)
            tiles_per_batch, kept = keep

            @pl.when(pl.program_id(0) % tiles_per_batch >= tiles_per_batch - kept[g])
            def _(kn=kn, vv=vv, kt_ref=rest[3 * N_BRANCH + g]):
                kt_ref[0, :ATTN_WIDTH, :] = kn.T
                kt_ref[0, ATTN_WIDTH:, :] = vv.T
        else:
            rest[0][:, k0:k0 + ATTN_WIDTH] = kn
            rest[0][:, k0 + ATTN_WIDTH:k0 + 2 * ATTN_WIDTH] = vv
            rest[1][:, q0:q0 + ATTN_WIDTH] = qg


def _kvq(x, modkv, mod, tiles_per_mod, tm, nkv, nm, wkv, wq, kn, qn, eavg, batch=None):
    T = x.shape[0]
    tok = lambda w: pl.BlockSpec((tm, w), lambda i: (i, 0))
    kvw = 2 * N_BRANCH * ATTN_WIDTH
    if batch is not None:
        L = T // batch
        tpb = L // tm
        kept = tuple(-(-min(w, L) // tm) for w, _ in BRANCHES)
        keep = (tpb, kept)
        out_specs, out_shape = [], []
        for _, dil in BRANCHES:
            for _ in range(3):
                out_specs.append(pl.BlockSpec((1, dil, tm // dil, ATTN_WIDTH), lambda i: (i // tpb, 0, i % tpb, 0)))
                out_shape.append(jax.ShapeDtypeStruct((batch, dil, L // dil, ATTN_WIDTH), BF16))
        for kg in kept:
            out_specs.append(pl.BlockSpec((1, 2 * ATTN_WIDTH, tm),
                                          lambda i, kg=kg: (i // tpb, 0, jnp.maximum(i % tpb - (tpb - kg), 0))))
            out_shape.append(jax.ShapeDtypeStruct((batch, 2 * ATTN_WIDTH, kg * tm), F32))
        scratch = [pltpu.VMEM((ATTN_WIDTH // LANES, tm, LANES), F32)]
    else:
        keep = None
        out_specs = [tok(kvw), tok(N_BRANCH * ATTN_WIDTH)]
        out_shape = [jax.ShapeDtypeStruct((T, kvw), F32), jax.ShapeDtypeStruct((T, N_BRANCH * ATTN_WIDTH), F32)]
        scratch = []
    return pl.pallas_call(
        functools.partial(_kvq_kernel, tm=tm, keep=keep),
        grid=(T // tm,),
        in_specs=[tok(D_MODEL), _mod_spec(modkv, tiles_per_mod), _mod_spec(mod, tiles_per_mod),
                  _resident(nkv.shape), _resident(nm.shape), _resident(wkv.shape), _resident(wq.shape),
                  _resident(kn.shape), _resident(qn.shape), _resident(eavg.shape)],
        out_specs=out_specs,
        out_shape=out_shape,
        scratch_shapes=scratch,
        compiler_params=_cparams(("arbitrary",)),
        name="kvq",
    )(x, modkv, mod, nkv, nm, wkv, wq, kn, qn, eavg)


BAND_QBLOCKS = 8


def _band_attn_kernel(q_ref, kp_ref, kc_ref, vp_ref, vc_ref, bias_ref, o_ref, lse_ref):
    n = pl.program_id(2)
    hw = ATTN_WIDTH // 2
    hh = HEADS // 2
    col = lax.broadcasted_iota(jnp.int32, (hh * N_STEPS, 2 * N_STEPS), 1)
    first = jnp.logical_or(col >= N_STEPS, n > 0)
    lane = lax.broadcasted_iota(jnp.int32, (N_STEPS, LANES), 1)
    head_of = lax.broadcasted_iota(jnp.int32, (N_STEPS, hw), 1) // HEAD_DIM
    zero = jnp.zeros((N_STEPS, hw), BF16)
    nt = (((1,), (1,)), ((), ()))
    for r, j in [(r, j) for r in range(q_ref.shape[0]) for j in range(q_ref.shape[1] // N_STEPS)]:
        if j == 0:
            k_all = jnp.concatenate([kp_ref[r], kc_ref[r]], axis=0)
            v_all = jnp.concatenate([vp_ref[r], vc_ref[r]], axis=0)
        rows = slice(j * N_STEPS, (j + 1) * N_STEPS)
        q = q_ref[r, rows, :]
        k2 = k_all[j * N_STEPS:(j + 2) * N_STEPS]
        v2 = v_all[j * N_STEPS:(j + 2) * N_STEPS]
        lse_all = jnp.zeros((N_STEPS, LANES), F32)
        halves = []
        for c in range(2):
            cols = slice(c * hw, (c + 1) * hw)
            qc = q[:, cols]
            qbd = jnp.concatenate([jnp.where(head_of == h, qc, zero) for h in range(hh)], axis=0)
            s = (lax.dot_general(qbd, k2[:, cols], nt, preferred_element_type=F32)
                 + bias_ref[c * hh * N_STEPS:(c + 1) * hh * N_STEPS, :])
            if j == 0:
                s = jnp.where(first, s, NEG)
            m = jnp.max(s, axis=1, keepdims=True)
            p = jnp.exp(s - m)
            l = jnp.sum(p, axis=1, keepdims=True)
            pv = jnp.dot(p.astype(BF16), v2[:, cols], preferred_element_type=F32) / l
            lse = m + jnp.log(l)
            o = jnp.zeros((N_STEPS, hw), F32)
            for h in range(hh):
                blk = slice(h * N_STEPS, (h + 1) * N_STEPS)
                o = jnp.where(head_of == h, pv[blk], o)
                lse_all = jnp.where(lane == c * hh + h, lse[blk], lse_all)
            halves.append(o)
        o_ref[r, rows, :] = jnp.concatenate(halves, axis=1)
        lse_ref[r, rows, :] = lse_all


def _band_attn(q, k, v, bias, dil):
    B, _, Ld, _ = q.shape
    nq = min(BAND_QBLOCKS, Ld // N_STEPS)
    nr = min(BAND_QBLOCKS // nq, dil)
    step = nq * N_STEPS
    blk = lambda w, f: pl.BlockSpec((None, nr, step, w), f)
    cur = lambda b, r, n: (b, r, n, 0)
    prev_blk = pl.BlockSpec((None, nr, N_STEPS, ATTN_WIDTH), lambda b, r, n: (b, r, jnp.maximum(nq * n - 1, 0), 0))
    return pl.pallas_call(
        _band_attn_kernel,
        grid=(B, dil // nr, Ld // step),
        in_specs=[blk(ATTN_WIDTH, cur), prev_blk, blk(ATTN_WIDTH, cur), prev_blk,
                  blk(ATTN_WIDTH, cur), _resident(bias.shape)],
        out_specs=[blk(ATTN_WIDTH, cur), blk(LANES, cur)],
        out_shape=[jax.ShapeDtypeStruct((B, dil, Ld, ATTN_WIDTH), F32),
                   jax.ShapeDtypeStruct((B, dil, Ld, LANES), F32)],
        compiler_params=_cparams(("arbitrary", "arbitrary", "arbitrary")),
        name=f"band_attn_d{dil}",
    )(q, k, k, v, v, bias)


def _rel_buckets(dilation):
    n = np.arange(N_STEPS + 1) * dilation
    large = MAX_EXACT + (np.log(np.maximum(n, 1) / MAX_EXACT) / np.log(REL_MAX_DIST / MAX_EXACT)
                         * (NUM_BUCKETS - MAX_EXACT)).astype(np.int32)
    return np.where(n < MAX_EXACT, n, np.minimum(large, NUM_BUCKETS - 1)).astype(np.int32)


def _step_bias_row(rel_bias, g, dil):
    onehot = np.zeros((N_STEPS + 1, NUM_BUCKETS), np.float32)
    onehot[np.arange(N_STEPS + 1), _rel_buckets(dil)] = 1.0
    return jnp.dot(jnp.asarray(onehot), rel_bias[:, g, :].astype(F32), precision=lax.Precision.HIGHEST).T


def _band_bias(rel_bias, g, dil):
    bias = _step_bias_row(rel_bias, g, dil)
    P = 3 * N_STEPS
    neg = jnp.full((HEADS, N_STEPS), NEG, F32)
    ext = jnp.concatenate([neg, bias[:, ::-1], neg], axis=1)
    flat = jnp.broadcast_to(ext[:, None, :], (HEADS, N_STEPS, P + 1)).reshape(HEADS, N_STEPS * (P + 1))
    skew = flat[:, :N_STEPS * P].reshape(HEADS, N_STEPS, P)
    return skew[:, :, N_STEPS:].reshape(HEADS * N_STEPS, 2 * N_STEPS)


def _step_attn_kernel(q_ref, kvn_ref, c0_ref, c1_ref, c2_ref, b0_ref, b1_ref, b2_ref, bn_ref, o_ref, *, n_new):
    caches = (c0_ref, c1_ref, c2_ref)
    cbias = (b0_ref, b1_ref, b2_ref)
    nt = (((1,), (1,)), ((), ()))
    rows = n_new * HEADS
    row = lax.broadcasted_iota(jnp.int32, (rows, ATTN_WIDTH), 0)
    lane = lax.broadcasted_iota(jnp.int32, (rows, ATTN_WIDTH), 1)
    own_head = lane // HEAD_DIM == row % HEADS
    parts = []
    for g in range(N_BRANCH):
        klo = 2 * g * ATTN_WIDTH
        qg = q_ref[0, :, g * ATTN_WIDTH:(g + 1) * ATTN_WIDTH]
        qrep = jnp.concatenate([jnp.broadcast_to(qg[s:s + 1, :], (HEADS, ATTN_WIDTH)) for s in range(n_new)], axis=0)
        qbd = jnp.where(own_head, qrep, 0.0).astype(BF16)
        kn = kvn_ref[0, :, klo:klo + ATTN_WIDTH].astype(BF16)
        vn = kvn_ref[0, :, klo + ATTN_WIDTH:klo + 2 * ATTN_WIDTH].astype(BF16)
        kt = caches[g][0, 0].astype(BF16)
        vt = caches[g][0, 1].astype(BF16)
        sc = jnp.dot(qbd, kt, preferred_element_type=F32) + cbias[g][...]
        sn = lax.dot_general(qbd, kn, nt, preferred_element_type=F32) + bn_ref[g]
        m = jnp.maximum(jnp.max(sc, axis=1, keepdims=True), jnp.max(sn, axis=1, keepdims=True))
        pc = jnp.exp(sc - m)
        pn = jnp.exp(sn - m)
        l = jnp.sum(pc, axis=1, keepdims=True) + jnp.sum(pn, axis=1, keepdims=True)
        pv = (lax.dot_general(pc.astype(BF16), vt, nt, preferred_element_type=F32)
              + jnp.dot(pn.astype(BF16), vn, preferred_element_type=F32))
        parts.append((m + jnp.log(l), pv / l))
    mx = jnp.maximum(jnp.maximum(parts[0][0], parts[1][0]), parts[2][0])
    es = [jnp.exp(lse - mx) for lse, _ in parts]
    den = es[0] + es[1] + es[2]
    mixed = (es[0] * parts[0][1] + es[1] * parts[1][1] + es[2] * parts[2][1]) / den
    mixed = jnp.where(own_head, mixed, 0.0)
    for s in range(n_new):
        o_ref[0, s:s + 1, :] = jnp.sum(mixed[s * HEADS:(s + 1) * HEADS], axis=0, keepdims=True)


def _step_bias(rel_bias, n_new):
    cache_tabs, new_tabs = [], []
    for g, (win, dil) in enumerate(BRANCHES):
        bias = _step_bias_row(rel_bias, g, dil)
        rev = bias[:, ::-1]
        per_s = []
        for s in range(n_new):
            if dil == 1:
                neg = jnp.full((HEADS, s), NEG, F32)
                per_s.append(jnp.concatenate([neg, rev[:, :win - s]], axis=1))
            else:
                cols = [rev[:, :N_STEPS] if r == s % dil else jnp.full((HEADS, N_STEPS), NEG, F32)
                        for r in range(dil)]
                per_s.append(jnp.stack(cols, axis=2).reshape(HEADS, win))
        cache_tabs.append(jnp.stack(per_s, axis=0).reshape(n_new * HEADS, win))
        dist = np.arange(n_new)[:, None] - np.arange(8)[None, :]
        ok = (dist >= 0) & (dist % dil == 0) & (np.arange(8)[None, :] < n_new)
        onehot = np.zeros((N_STEPS + 1, n_new * 8), np.float32)
        onehot[np.where(ok, dist // dil, 0).reshape(-1), np.arange(n_new * 8)] = 1.0
        tab = jnp.dot(bias, jnp.asarray(onehot), precision=lax.Precision.HIGHEST).reshape(HEADS, n_new, 8)
        new_tabs.append(jnp.where(ok[None], tab, NEG).transpose(1, 0, 2).reshape(n_new * HEADS, 8))
    return cache_tabs, jnp.stack(new_tabs)


def _step_attn(q, kvn, caches, cache_bias, new_bias, DB, S):
    cspec = lambda c: pl.BlockSpec((1,) + c.shape[1:], lambda b: (b, 0, 0, 0))
    return pl.pallas_call(
        functools.partial(_step_attn_kernel, n_new=S),
        grid=(DB,),
        in_specs=[pl.BlockSpec((1, S, N_BRANCH * ATTN_WIDTH), lambda b: (b, 0, 0)),
                  pl.BlockSpec((1, 8, 2 * N_BRANCH * ATTN_WIDTH), lambda b: (b, 0, 0)),
                  cspec(caches[0]), cspec(caches[1]), cspec(caches[2]),
                  _resident(cache_bias[0].shape), _resident(cache_bias[1].shape), _resident(cache_bias[2].shape),
                  _resident(new_bias.shape)],
        out_specs=pl.BlockSpec((1, S, ATTN_WIDTH), lambda b: (b, 0, 0)),
        out_shape=jax.ShapeDtypeStruct((DB, S, ATTN_WIDTH), F32),
        compiler_params=_cparams(("arbitrary",)),
        name="step_attn",
    )(q, kvn, caches[0], caches[1], caches[2], *cache_bias, new_bias)


def _attn_out_kernel(*refs, dils, tm):
    n_o = max(len(dils), 1)
    n_in = n_o + len(dils) + 9
    (x_ref, mod_ref, wo_ref, ex_ref, nf_ref, wr_ref, br_ref, cin_ref, tri_ref,
     x3_ref, h3_ref, rt_ref, rtt_ref, cnt_ref, carry) = refs[n_in - 9:n_in + 6]
    stages = refs[n_in + 6:]

    def natural(ref, stage, dil):
        if dil == 1:
            return ref[0, 0]
        for r in range(dil):
            val = ref[0, r]
            for c in range(stage.shape[0]):
                stage[c, pl.ds(r, tm // dil, stride=dil), :] = val[:, c * LANES:(c + 1) * LANES]
        return jnp.concatenate([stage[c] for c in range(stage.shape[0])], axis=1)

    if dils:
        lses = [natural(refs[n_o + g], stages[2 * g + 1], dil) for g, dil in enumerate(dils)]
        mx = functools.reduce(jnp.maximum, lses)
        es = [jnp.exp(l - mx) for l in lses]
        den = functools.reduce(lambda a, b: a + b, es)
        o = None
        for g, dil in enumerate(dils):
            term = _split_dot(es[g] / den, ex_ref[...]) * natural(refs[g], stages[2 * g], dil)
            o = term if o is None else o + term
    else:
        o = refs[0][...]
    mod = mod_ref[0]
    a = jnp.dot(o.astype(BF16), wo_ref[...], preferred_element_type=F32)
    x3 = x_ref[...] + mod[:, 2 * D_MODEL:3 * D_MODEL] * a
    x3_ref[...] = x3
    _ffn_pre(x3, mod, nf_ref, wr_ref, br_ref, cin_ref, tri_ref, h3_ref, rt_ref, rtt_ref, cnt_ref, carry, tm)


def _attn_out(os_, lses, x, mod, tiles_per_mod, tm, wo, ex, nf, wr, br, counts):
    T = x.shape[0]
    tok = lambda w: pl.BlockSpec((tm, w), lambda i: (i, 0))
    r_specs, r_shapes = _route_outs(T, tm)
    tri = _earlier_rows(tm)
    if lses:
        dils = tuple(o.shape[1] for o in os_)
        tpb = tiles_per_mod
        res = lambda a: pl.BlockSpec((1, a.shape[1], tm // a.shape[1], a.shape[3]), lambda i: (i // tpb, 0, i % tpb, 0))
        o_specs = [res(a) for a in os_] + [res(a) for a in lses]
        scratch = []
        for _ in dils:
            scratch += [pltpu.VMEM((ATTN_WIDTH // LANES, tm, LANES), F32), pltpu.VMEM((1, tm, LANES), F32)]
    else:
        dils, o_specs, scratch = (), [tok(ATTN_WIDTH)], []
    return pl.pallas_call(
        functools.partial(_attn_out_kernel, dils=dils, tm=tm),
        grid=(T // tm,),
        in_specs=(o_specs + [tok(D_MODEL), _mod_spec(mod, tiles_per_mod), _resident(wo.shape), _resident(ex.shape),
                             _resident(nf.shape), _resident(wr.shape), _resident(br.shape),
                             _resident(counts.shape), _resident(tri.shape)]),
        out_specs=[tok(D_MODEL), pl.BlockSpec((tm * ROW_TILES, LANES), lambda i: (i, 0))] + r_specs,
        out_shape=[jax.ShapeDtypeStruct((T, D_MODEL), F32),
                   jax.ShapeDtypeStruct((T * ROW_TILES, LANES), F32)] + r_shapes,
        scratch_shapes=[pltpu.VMEM((1, LANES), F32)] + scratch,
        compiler_params=_cparams(("arbitrary",)),
        name="attn_out",
    )(*os_, *lses, x, mod, wo, ex, nf, wr, br, counts, tri)


def kernel(x_prompt, x_sample, cache_kv_w128, cache_kv_w512, cache_kv_w2048, c_prompt, c_sample, ada_w, ada_b, norm_mix, norm_ffn, a_w_in, a_b_in, a_norm_v, a_w_s, a_b_s, a_w_out, kv_ada_w, kv_ada_b, kv_norm, w_kv, k_norm, rel_bias, b_w_q, q_norm, b_w_o, r_w_group, r_b_group, r_w_expert, r_b_expert, e_w1, e_w3, e_w2):
    B, L, _ = x_prompt.shape
    DB, S, _ = x_sample.shape
    Tp, Ts = B * L, DB * S
    tm = TOKEN_TILE
    tpm_p = L // tm

    c_all = jnp.concatenate([c_prompt, c_sample], axis=0)
    R = c_all.shape[0]
    c_all = jnp.pad(c_all, ((0, -R % 8), (0, 0)))
    mods = [_ada(c_all, ada_w, ada_b, l) for l in range(2)]
    modkv = _ada(c_all, kv_ada_w[None], kv_ada_b[None], 0)

    def split_mod(m):
        return m[:B, None, :], jnp.repeat(m[B:B + DB], S, axis=0)[None]
    mod_p, mod_s = zip(*[split_mod(m) for m in mods])
    modkv_p, modkv_s = split_mod(modkv)

    row = lambda a: a.reshape(1, -1)

    def router(l):
        wr = jnp.zeros((D_MODEL, LANES), F32)
        wr = wr.at[:, :N_GROUPS].set(r_w_group[l]).at[:, N_GROUPS:N_GROUPS + N_EXPERTS].set(r_w_expert[l])
        br = jnp.zeros((1, LANES), F32)
        br = br.at[0, :N_GROUPS].set(r_b_group[l]).at[0, N_GROUPS:N_GROUPS + N_EXPERTS].set(r_b_expert[l])
        return wr.astype(BF16), br

    no_counts = jnp.zeros((1, LANES), F32)

    win = a_w_in[0].astype(BF16)
    wout = a_w_out[0].astype(BF16)
    tril = jnp.tril(jnp.ones((CHUNK, CHUNK), bool))
    ws_p = jnp.where(tril, a_w_s[0], 0).astype(BF16)
    bs_p = a_b_s[0].T
    cs = min(CHUNK, S)
    ws_small = jnp.where(jnp.tril(jnp.ones((cs, cs), bool)), a_w_s[0][:, :cs, :cs], 0)
    ws_s = jnp.stack([jnp.kron(jnp.eye(Ts // cs, dtype=F32), ws_small[g]) for g in range(GMLP_GROUPS)]).astype(BF16)
    bs_s = jnp.tile(a_b_s[0][:, :cs], (1, Ts // cs)).T
    wr0, br0 = router(0)
    common = (row(norm_mix[0]), win, row(a_b_in[0]), row(a_norm_v[0]))
    xp = x_prompt.reshape(Tp, D_MODEL)
    xs_ = x_sample.reshape(Ts, D_MODEL)
    x1_p, h2_p, rt_p, rtt_p, cnt = _mixer_a(xp, mod_p[0], tpm_p, tm, *common, ws_p, bs_p, wout, row(norm_ffn[0]),
                                            wr0, br0, no_counts, with_v=False)
    x1_s, h2_s, rt_s, rtt_s, cnt, v_s = _mixer_a(xs_, mod_s[0], 1, Ts, *common, ws_s, bs_s, wout, row(norm_ffn[0]),
                                                 wr0, br0, cnt, with_v=True)
    x2_p, x2_s = _moe((h2_p, rt_p, rtt_p, x1_p, mod_p[0], tpm_p, tm), (h2_s, rt_s, rtt_s, x1_s, mod_s[0], 1, Ts), cnt,
                      e_w1, e_w3, e_w2, 0)

    wkv = w_kv.astype(BF16)
    wq = b_w_q[0].astype(BF16)
    kn = jnp.tile(k_norm, (1, HEADS))
    qn = jnp.tile(q_norm[0], (1, HEADS))
    head = np.arange(ATTN_WIDTH) // HEAD_DIM
    half_head = head[:ATTN_WIDTH // 2]
    eavg = jnp.asarray((half_head[:, None] == half_head[None, :]) / HEAD_DIM, BF16)
    kvq_w = (row(kv_norm), row(norm_mix[1]), wkv, wq, kn, qn, eavg)
    *qkv_p, kt0, kt1, kt2 = _kvq(x2_p, modkv_p, mod_p[1], tpm_p, tm, *kvq_w, batch=B)
    kv_s, q_s = _kvq(x2_s, modkv_s, mod_s[1], 1, Ts, *kvq_w)

    os_, lses = [], []
    for g, (_, dil) in enumerate(BRANCHES):
        o, lse = _band_attn(*qkv_p[3 * g:3 * g + 3], _band_bias(rel_bias, g, dil), dil)
        os_.append(o)
        lses.append(lse)
    caches = [jnp.transpose(c, (0, 2, 3, 4, 1)).reshape(DB, 2, ATTN_WIDTH, c.shape[1])
              for c in (cache_kv_w128, cache_kv_w512, cache_kv_w2048)]
    kvn_s = jnp.pad(kv_s.reshape(DB, S, -1), ((0, 0), (0, 8 - S), (0, 0)))
    o_s = _step_attn(q_s.reshape(DB, S, -1), kvn_s, caches, *_step_bias(rel_bias, S), DB, S)

    wo = b_w_o[0].astype(BF16)
    ex = jnp.asarray(np.arange(LANES)[:, None] == head[None, :], BF16)
    wr1, br1 = router(1)
    x3_p, h3_p, rt1_p, rtt1_p, cnt1 = _attn_out(os_, lses, x2_p, mod_p[1], tpm_p, tm, wo, ex, row(norm_ffn[1]),
                                                wr1, br1, no_counts)
    x3_s, h3_s, rt1_s, rtt1_s, cnt1 = _attn_out([o_s.reshape(Ts, ATTN_WIDTH)], [], x2_s, mod_s[1], 1, Ts, wo, ex,
                                                row(norm_ffn[1]), wr1, br1, cnt1)
    y_p, y_s = _moe((h3_p, rt1_p, rtt1_p, x3_p, mod_p[1], tpm_p, tm), (h3_s, rt1_s, rtt1_s, x3_s, mod_s[1], 1, Ts),
                    cnt1, e_w1, e_w3, e_w2, 1)

    kv_s4 = kv_s.reshape(DB, S, N_BRANCH, 2, HEADS, HEAD_DIM)

    def window(kt, w):
        n = min(w, L)
        return jnp.transpose(kt[:, :, kt.shape[2] - n:].reshape(B, 2, HEADS, HEAD_DIM, n), (0, 4, 1, 2, 3))
    return (y_p.reshape(B, L, D_MODEL), y_s.reshape(DB, S, D_MODEL),
            window(kt0, BRANCHES[0][0]), window(kt1, BRANCHES[1][0]), window(kt2, BRANCHES[2][0]),
            kv_s4[:, :, 0], kv_s4[:, :, 1], kv_s4[:, :, 2],
            v_s.reshape(1, DB, S, GMLP_WIDTH))
```

```python
import functools

import numpy as np
import jax
import jax.numpy as jnp
from jax import lax
from jax.experimental import pallas as pl
from jax.experimental.pallas import tpu as pltpu

F32 = jnp.float32
BF16 = jnp.bfloat16

D_MODEL = 1024
GMLP_WIDTH = 2048
GMLP_GROUPS = 4
GROUP_WIDTH = GMLP_WIDTH // GMLP_GROUPS
CHUNK = 128
BRANCHES = ((128, 1), (512, 4), (2048, 16))
N_BRANCH = 3
N_STEPS = 128
HEADS = 8
HEAD_DIM = 64
ATTN_WIDTH = HEADS * HEAD_DIM
NUM_BUCKETS = 32
MAX_EXACT = NUM_BUCKETS // 2
REL_MAX_DIST = 2048
N_GROUPS = 4
EXPERTS_PER_GROUP = 8
N_EXPERTS = N_GROUPS * EXPERTS_PER_GROUP
TOP_K = 2
D_EXPERT = 512
EPS = 1e-6
NEG = -1e30

LANES = 128
ROW_TILES = D_MODEL // LANES
TOKEN_TILE = 512
MOE_BLOCK = 256
DMA_UNROLL = 16
VMEM_LIMIT = 52 * 1024 * 1024


def _cparams(sem):
    return pltpu.CompilerParams(dimension_semantics=sem, vmem_limit_bytes=VMEM_LIMIT)


def _resident(shape):
    nd = len(shape)
    return pl.BlockSpec(shape, lambda *_, _nd=nd: (0,) * _nd, pipeline_mode=pl.Buffered(1))


def _gelu_tanh(x):
    return 0.5 * x * (1.0 + jnp.tanh(0.7978845608028654 * (x + 0.044715 * (x * x * x))))


def _rms(x, g):
    return x * lax.rsqrt(jnp.mean(x * x, axis=-1, keepdims=True) + EPS) * g


def _store_rows8(ref, val, n, base=0):
    for s in range(ROW_TILES):
        ref[pl.ds(base + s, n, stride=ROW_TILES), :] = val[:, s * LANES:(s + 1) * LANES]


def _load_rows8(ref, n, base=0):
    return jnp.concatenate([ref[pl.ds(base + s, n, stride=ROW_TILES), :] for s in range(ROW_TILES)], axis=1)


def _split(a):
    hi = a.astype(BF16)
    return hi, (a - hi.astype(F32)).astype(BF16)


def _split_dot(a, e_bf16):
    hi, lo = _split(a)
    return (jnp.dot(hi, e_bf16, preferred_element_type=F32) + jnp.dot(lo, e_bf16, preferred_element_type=F32))


ROUTE_LANE0 = 4


def _route_rows(l, tri_ref, carry):
    lane = lax.broadcasted_iota(jnp.int32, l.shape, 1).astype(F32)
    far = float(LANES)

    def first_lane(mask):
        return jnp.min(jnp.where(mask, lane, far), axis=1, keepdims=True)

    is_g = lane < N_GROUPS
    gl = jnp.where(is_g, l, NEG)
    gmax = jnp.max(gl, axis=1, keepdims=True)
    g_i = first_lane(jnp.logical_and(gl == gmax, is_g))
    g_p = 1.0 / jnp.sum(jnp.where(is_g, jnp.exp(gl - gmax), 0.0), axis=1, keepdims=True)
    lo = ROUTE_LANE0 + EXPERTS_PER_GROUP * g_i
    sel = jnp.logical_and(lane >= lo, lane < lo + EXPERTS_PER_GROUP)
    el = jnp.where(sel, l, NEG)
    m1 = jnp.max(el, axis=1, keepdims=True)
    i1 = first_lane(jnp.logical_and(el == m1, sel))
    sel2 = jnp.logical_and(sel, lane != i1)
    el2 = jnp.where(sel2, l, NEG)
    m2 = jnp.max(el2, axis=1, keepdims=True)
    i2 = first_lane(jnp.logical_and(el2 == m2, sel2))
    r = jnp.exp(m2 - m1)
    w1 = g_p / (1.0 + r)
    w2 = g_p * r / (1.0 + r)

    hit1 = lane == i1
    hit2 = lane == i2
    onehot = jnp.where(jnp.logical_or(hit1, hit2), 1.0, 0.0)
    before = carry[...] + jnp.dot(tri_ref[...], onehot.astype(BF16), preferred_element_type=F32)
    rank1 = jnp.sum(jnp.where(hit1, before, 0.0), axis=1, keepdims=True)
    rank2 = jnp.sum(jnp.where(hit2, before, 0.0), axis=1, keepdims=True)
    carry[...] = carry[...] + jnp.sum(onehot, axis=0, keepdims=True)

    out = jnp.zeros(l.shape, F32)
    for k, val in enumerate((i1 - ROUTE_LANE0, i2 - ROUTE_LANE0, w1, w2, rank1, rank2)):
        out = jnp.where(lane == k, val, out)
    return out


ROUTE_FIELDS = 8


def _start_counts(cin_ref, carry):
    @pl.when(pl.program_id(0) == 0)
    def _():
        carry[...] = cin_ref[...]


def _ffn_pre(x, mod, nf_ref, wr_ref, br_ref, tri_ref, h_ref, rt_ref, rtt_ref, cnt_ref, carry, n):
    h = _rms(x, nf_ref[...]) * (1.0 + mod[:, 4 * D_MODEL:5 * D_MODEL]) + mod[:, 3 * D_MODEL:4 * D_MODEL]
    _store_rows8(h_ref, h, n)
    logits = jnp.dot(h.astype(BF16), wr_ref[...], preferred_element_type=F32) + br_ref[...]
    route = _route_rows(logits, tri_ref, carry)
    rt_ref[...] = route
    rtt_ref[...] = route.T[:ROUTE_FIELDS, :]
    cnt_ref[...] = carry[...]


def _ada_kernel(c_ref, w_ref, b_ref, o_ref):
    c = c_ref[...]
    a = (c * jax.nn.sigmoid(c)).astype(BF16)
    o_ref[...] = jnp.dot(a, w_ref[...].astype(BF16), preferred_element_type=F32) + b_ref[...]


def _ada(c, w, b, layer):
    R = c.shape[0]
    N = w.shape[2]
    tn = 1024
    return pl.pallas_call(
        _ada_kernel,
        grid=(N // tn,),
        in_specs=[pl.BlockSpec((R, D_MODEL), lambda j: (0, 0)),
                  pl.BlockSpec((None, D_MODEL, tn), lambda j: (layer, 0, j)),
                  pl.BlockSpec((None, 1, tn), lambda j: (layer, 0, j))],
        out_specs=pl.BlockSpec((R, tn), lambda j: (0, j)),
        out_shape=jax.ShapeDtypeStruct((R, N), F32),
        compiler_params=_cparams(("arbitrary",)),
        name="ada",
    )(c, w, b.reshape(b.shape[0], 1, N))


def _mixer_a_kernel(x_ref, mod_ref, nm_ref, win_ref, bin_ref, gv_ref, ws_ref, bs_ref, wout_ref,
                    nf_ref, wr_ref, br_ref, cin_ref, tri_ref, x1_ref, h2_ref, rt_ref, rtt_ref, cnt_ref, *rest, tm):
    v_refs, carry = rest[:-1], rest[-1]
    _start_counts(cin_ref, carry)
    x = x_ref[...]
    mod = mod_ref[0]
    h = (_rms(x, nm_ref[...]) * (1.0 + mod[:, D_MODEL:2 * D_MODEL]) + mod[:, 0:D_MODEL]).astype(BF16)
    zv = jnp.dot(h, win_ref[:, GMLP_WIDTH:], preferred_element_type=F32) + bin_ref[:, GMLP_WIDTH:]
    v = _rms(_gelu_tanh(zv), gv_ref[...])
    if v_refs:
        v_refs[0][...] = v
    vb = v.astype(BF16)
    bs = bs_ref[...]
    acc = jnp.zeros((tm, D_MODEL), F32)
    for g in range(GMLP_GROUPS):
        lo, hi = g * GROUP_WIDTH, (g + 1) * GROUP_WIDTH
        u = _gelu_tanh(jnp.dot(h, win_ref[:, lo:hi], preferred_element_type=F32) + bin_ref[:, lo:hi])
        wg = ws_ref[g]
        gate = jnp.concatenate(
            [jnp.dot(wg, vb[c * CHUNK:(c + 1) * CHUNK, lo:hi], preferred_element_type=F32) + bs[:, g:g + 1]
             for c in range(tm // CHUNK)], axis=0)
        acc = acc + jnp.dot((u * gate).astype(BF16), wout_ref[lo:hi, :], preferred_element_type=F32)
    x1 = x + mod[:, 2 * D_MODEL:3 * D_MODEL] * acc
    x1_ref[...] = x1
    _ffn_pre(x1, mod, nf_ref, wr_ref, br_ref, tri_ref, h2_ref, rt_ref, rtt_ref, cnt_ref, carry, tm)


def _mod_spec(mod, tiles_per_mod):
    _, rows, width = mod.shape
    return pl.BlockSpec((1, rows, width), lambda i: (i // tiles_per_mod, 0, 0))


def _earlier_rows(tm):
    return jnp.asarray(np.tril(np.ones((tm, tm), np.float32), -1), BF16)


def _route_outs(T, tm):
    return ([pl.BlockSpec((tm, LANES), lambda i: (i, 0)), pl.BlockSpec((ROUTE_FIELDS, tm), lambda i: (0, i)),
             pl.BlockSpec((1, LANES), lambda i: (0, 0))],
            [jax.ShapeDtypeStruct((T, LANES), F32), jax.ShapeDtypeStruct((ROUTE_FIELDS, T), F32),
             jax.ShapeDtypeStruct((1, LANES), F32)])


def _mixer_a(x, mod, tiles_per_mod, tm, nm, win, bin_, gv, ws, bs_t, wout, nf, wr, br, counts, with_v):
    T = x.shape[0]
    tok = lambda w: pl.BlockSpec((tm, w), lambda i: (i, 0))
    r_specs, r_shapes = _route_outs(T, tm)
    out_shape = [jax.ShapeDtypeStruct((T, D_MODEL), F32), jax.ShapeDtypeStruct((T * ROW_TILES, LANES), F32)] + r_shapes
    out_specs = [tok(D_MODEL), pl.BlockSpec((tm * ROW_TILES, LANES), lambda i: (i, 0))] + r_specs
    if with_v:
        out_shape.append(jax.ShapeDtypeStruct((T, GMLP_WIDTH), F32))
        out_specs.append(tok(GMLP_WIDTH))
    tri = _earlier_rows(tm)
    return pl.pallas_call(
        functools.partial(_mixer_a_kernel, tm=tm),
        grid=(T // tm,),
        in_specs=[tok(D_MODEL), _mod_spec(mod, tiles_per_mod), _resident(nm.shape), _resident(win.shape),
                  _resident(bin_.shape), _resident(gv.shape), _resident(ws.shape), _resident(bs_t.shape),
                  _resident(wout.shape), _resident(nf.shape), _resident(wr.shape), _resident(br.shape),
                  _resident(counts.shape), _resident(tri.shape)],
        out_specs=out_specs,
        out_shape=out_shape,
        scratch_shapes=[pltpu.VMEM((1, LANES), F32)],
        compiler_params=_cparams(("arbitrary",)),
        name="mixer_a",
    )(x, mod, nm, win, bin_, gv, ws, bs_t, wout, nf, wr, br, counts, tri)


def _layout(rtt, counts, block):
    n_tokens = rtt.shape[1]
    counts = counts[0, ROUTE_LANE0:ROUTE_LANE0 + N_EXPERTS].astype(jnp.int32)
    padded = (counts + block - 1) // block * block
    pad_end = jnp.cumsum(padded)
    pad_start = (pad_end - padded).astype(F32)
    experts, ranks = rtt[0:TOP_K], rtt[4:4 + TOP_K]
    onehot = (experts[None] == jnp.arange(N_EXPERTS, dtype=F32)[:, None, None]).astype(F32)
    base = jnp.einsum('e,ekt->kt', pad_start, onehot, precision=lax.Precision.HIGHEST)
    slots = (ranks + base).astype(jnp.int32)
    nb = -(-n_tokens * TOP_K // block) + N_EXPERTS
    blk_e = jnp.minimum(jnp.sum(pad_end[None, :] <= (jnp.arange(nb, dtype=jnp.int32) * block)[:, None], axis=1),
                        N_EXPERTS - 1).astype(jnp.int32)
    n_used = (pad_end[-1] // block).astype(jnp.int32).reshape(1)
    starts = jnp.concatenate([jnp.ones((1,), jnp.int32), (blk_e[1:] != blk_e[:-1]).astype(jnp.int32)])
    grp = jnp.cumsum(starts) - 1
    end_blk = (pad_end // block).astype(jnp.int32)
    own = blk_e[:, None] == jnp.arange(N_EXPERTS, dtype=jnp.int32)[None, :]
    nxt_blk = jnp.sum(jnp.where(own, end_blk[None, :], 0), axis=1)
    at_nxt = nxt_blk[:, None] == jnp.arange(nb, dtype=jnp.int32)[None, :]
    nxt_e = jnp.where(nxt_blk < n_used[0], jnp.sum(jnp.where(at_nxt, blk_e[None, :], 0), axis=1), -1).astype(jnp.int32)
    zero_plan = (jnp.maximum(pad_end - block, 0).astype(jnp.int32), (padded > 0).astype(jnp.int32), n_used)
    return slots, (blk_e, n_used, nxt_e, grp.astype(jnp.int32)), zero_plan, nb


def _tile_slots(slots, ts):
    T = slots.shape[1]
    return slots.reshape(TOP_K, T // ts, ts).transpose(1, 0, 2).reshape(T // ts, 1, TOP_K * ts)


def _row_copy(src_ref, src_row, dst_ref, dst_row, sem):
    return pltpu.make_async_copy(
        src_ref.at[pl.ds(pl.multiple_of(src_row * ROW_TILES, ROW_TILES), ROW_TILES), :],
        dst_ref.at[pl.ds(pl.multiple_of(dst_row * ROW_TILES, ROW_TILES), ROW_TILES), :], sem)


def _scatter_kernel(zlo_ref, has_ref, nu_ref, slot_a_ref, slot_b_ref, src_a_ref, src_b_ref, dst_ref, zbuf, sem,
                    *, ts, n_a, n_b, nb, block):
    i = pl.program_id(0)

    def clear(row0):
        return pltpu.make_async_copy(
            zbuf, dst_ref.at[pl.ds(pl.multiple_of(row0 * ROW_TILES, ROW_TILES), block * ROW_TILES), :], sem)

    def for_clears(act):
        for e in range(N_EXPERTS):
            @pl.when(has_ref[e] > 0)
            def _(e=e):
                act(clear(zlo_ref[e]))

        def tail(j, c):
            act(clear(j * block))
            return c
        lax.fori_loop(nu_ref[0], nb, tail, 0)

    @pl.when(i == 0)
    def _():
        zbuf[...] = jnp.zeros(zbuf.shape, F32)
        for_clears(lambda c: c.start())
        for_clears(lambda c: c.wait())

    def scatter(slot_ref, src_ref, n):
        def issue(j, c):
            for u in range(DMA_UNROLL):
                t = j * DMA_UNROLL + u
                for k in range(TOP_K):
                    _row_copy(src_ref, t, dst_ref, slot_ref[0, 0, k * n + t], sem).start(priority=k)
            return c
        lax.fori_loop(0, n // DMA_UNROLL, issue, 0)
        for _ in range(TOP_K):
            pltpu.make_async_copy(src_ref, dst_ref.at[pl.ds(0, n * ROW_TILES), :], sem).wait()

    @pl.when(i < n_a)
    def _():
        scatter(slot_a_ref, src_a_ref, ts)

    @pl.when(i == n_a)
    def _():
        scatter(slot_b_ref, src_b_ref, n_b)


def _scatter(slots_a, slots_b, src_a, src_b, zero_plan, nb, block):
    n_a, _, ts2 = slots_a.shape
    ts, n_b = ts2 // TOP_K, slots_b.shape[2] // TOP_K
    tile_a = lambda i, *_: (jnp.minimum(i, n_a - 1), 0, 0)
    return pl.pallas_call(
        functools.partial(_scatter_kernel, ts=ts, n_a=n_a, n_b=n_b, nb=nb, block=block),
        grid_spec=pltpu.PrefetchScalarGridSpec(
            num_scalar_prefetch=3,
            grid=(n_a + 1,),
            in_specs=[pl.BlockSpec((1, 1, ts2), tile_a, memory_space=pltpu.SMEM),
                      pl.BlockSpec((1, 1, TOP_K * n_b), lambda i, *_: (0, 0, 0), memory_space=pltpu.SMEM),
                      pl.BlockSpec((ts * ROW_TILES, LANES), lambda i, *_: (jnp.minimum(i, n_a - 1), 0)),
                      pl.BlockSpec((n_b * ROW_TILES, LANES), lambda i, *_: (0, 0))],
            out_specs=pl.BlockSpec(memory_space=pl.ANY),
            scratch_shapes=[pltpu.VMEM((block * ROW_TILES, LANES), F32), pltpu.SemaphoreType.DMA(())]),
        out_shape=jax.ShapeDtypeStruct((nb * block * ROW_TILES, LANES), F32),
        compiler_params=_cparams(("arbitrary",)),
        name="moe_scatter",
    )(*zero_plan, slots_a, slots_b, src_a, src_b)


def _experts_kernel(be_ref, nu_ref, nxt_ref, grp_ref, x_ref, w1_hbm, w3_hbm, w2_hbm, y_ref,
                    w1b, w3b, w2b, wf1, wf3, wf2, sem, *, block, layer):
    i = pl.program_id(0)

    def weight_copies(e, buf):
        return [pltpu.make_async_copy(src.at[layer, e], dst.at[buf], sem.at[buf])
                for src, dst in ((w1_hbm, wf1), (w3_hbm, wf3), (w2_hbm, wf2))]

    @pl.when(i == 0)
    def _():
        for c in weight_copies(be_ref[0], 0):
            c.start()

    @pl.when(i < nu_ref[0])
    def _():
        @pl.when(jnp.logical_or(i == 0, be_ref[i] != be_ref[jnp.maximum(i - 1, 0)]))
        def _():
            buf = grp_ref[i] % 2
            for c in weight_copies(be_ref[i], buf):
                c.wait()

            @pl.when(nxt_ref[i] >= 0)
            def _():
                for c in weight_copies(nxt_ref[i], 1 - buf):
                    c.start()

            w1b[...] = wf1[buf].astype(BF16)
            w3b[...] = wf3[buf].astype(BF16)
            w2b[...] = wf2[buf].astype(BF16)

        x = _load_rows8(x_ref, block).astype(BF16)
        h1 = jnp.dot(x, w1b[...], preferred_element_type=F32)
        h3 = jnp.dot(x, w3b[...], preferred_element_type=F32)
        a = (h1 * jax.nn.sigmoid(h1) * h3).astype(BF16)
        _store_rows8(y_ref, jnp.dot(a, w2b[...], preferred_element_type=F32), block)


def _experts(xs, plan, nb, w1, w3, w2, layer, block):
    rows = pl.BlockSpec((block * ROW_TILES, LANES), lambda i, be, nu, nxt, grp: (jnp.minimum(i, nu[0] - 1), 0))
    hbm = pl.BlockSpec(memory_space=pl.ANY)
    return pl.pallas_call(
        functools.partial(_experts_kernel, block=block, layer=layer),
        grid_spec=pltpu.PrefetchScalarGridSpec(
            num_scalar_prefetch=4,
            grid=(nb,),
            in_specs=[rows, hbm, hbm, hbm],
            out_specs=rows,
            scratch_shapes=[pltpu.VMEM((D_MODEL, D_EXPERT), BF16), pltpu.VMEM((D_MODEL, D_EXPERT), BF16),
                            pltpu.VMEM((D_EXPERT, D_MODEL), BF16),
                            pltpu.VMEM((2, D_MODEL, D_EXPERT), F32), pltpu.VMEM((2, D_MODEL, D_EXPERT), F32),
                            pltpu.VMEM((2, D_EXPERT, D_MODEL), F32), pltpu.SemaphoreType.DMA((2,))]),
        out_shape=jax.ShapeDtypeStruct(xs.shape, F32),
        input_output_aliases={4: 0},
        compiler_params=_cparams(("arbitrary",)),
        name="moe_experts",
    )(*plan, xs, w1, w3, w2)


def _combine_kernel(slot_ref, y_ref, x_ref, mod_ref, rt_ref, o_ref, ybuf, sem, *, tc):
    def issue(j, c):
        for u in range(DMA_UNROLL):
            t = j * DMA_UNROLL + u
            for k in range(TOP_K):
                _row_copy(y_ref, slot_ref[0, 0, k * tc + t], ybuf, k * tc + t, sem).start(priority=k)
        return c
    lax.fori_loop(0, tc // DMA_UNROLL, issue, 0)
    pltpu.make_async_copy(y_ref.at[pl.ds(0, TOP_K * tc * ROW_TILES), :], ybuf, sem).wait()

    rt = rt_ref[...]
    y = (rt[:, 2:3] * _load_rows8(ybuf, tc) + rt[:, 3:4] * _load_rows8(ybuf, tc, base=tc * ROW_TILES))
    o_ref[...] = x_ref[...] + mod_ref[0][:, 5 * D_MODEL:6 * D_MODEL] * y


def _combine(slots, y, x, mod, tiles_per_mod, route, tc):
    T = x.shape[0]
    return pl.pallas_call(
        functools.partial(_combine_kernel, tc=tc),
        grid=(T // tc,),
        in_specs=[pl.BlockSpec((1, 1, tc * TOP_K), lambda i: (i, 0, 0), memory_space=pltpu.SMEM),
                  pl.BlockSpec(memory_space=pl.ANY),
                  pl.BlockSpec((tc, D_MODEL), lambda i: (i, 0)),
                  _mod_spec(mod, tiles_per_mod),
                  pl.BlockSpec((tc, LANES), lambda i: (i, 0))],
        out_specs=pl.BlockSpec((tc, D_MODEL), lambda i: (i, 0)),
        out_shape=jax.ShapeDtypeStruct((T, D_MODEL), F32),
        scratch_shapes=[pltpu.VMEM((TOP_K * tc * ROW_TILES, LANES), F32), pltpu.SemaphoreType.DMA(())],
        compiler_params=_cparams(("arbitrary",)),
        name="moe_combine",
    )(slots, y, x, mod, route)


def _moe(part_a, part_b, counts, w1, w3, w2, layer):
    Ta = part_a[3].shape[0]
    slots, plan, zero_plan, nb = _layout(jnp.concatenate([part_a[2], part_b[2]], axis=1), counts, MOE_BLOCK)
    tiled = [_tile_slots(slots[:, :Ta], part_a[6]), _tile_slots(slots[:, Ta:], part_b[6])]
    xs = _scatter(tiled[0], tiled[1], part_a[0], part_b[0], zero_plan, nb, MOE_BLOCK)
    y = _experts(xs, plan, nb, w1, w3, w2, layer, MOE_BLOCK)
    return [_combine(sl, y, x, mod, tpm, rt, tt) for sl, (_, rt, _, x, mod, tpm, tt) in zip(tiled, (part_a, part_b))]


def _kvq_kernel(x_ref, modkv_ref, mod_ref, nkv_ref, nm_ref, wkv_ref, wq_ref, kn_ref, qn_ref, eavg_ref,
                *rest, tm, keep):
    residue = keep is not None
    x = x_ref[...]
    xn = x * lax.rsqrt(jnp.mean(x * x, axis=-1, keepdims=True) + EPS)
    modkv = modkv_ref[0]
    mod = mod_ref[0]
    hk = (xn * nkv_ref[...] * (1.0 + modkv[:, D_MODEL:]) + modkv[:, :D_MODEL]).astype(BF16)
    hq = (xn * nm_ref[...] * (1.0 + mod[:, D_MODEL:2 * D_MODEL]) + mod[:, :D_MODEL]).astype(BF16)
    eavg = eavg_ref[...]

    def head_norm(a, g):
        sq = (a * a).astype(BF16)
        half = ATTN_WIDTH // 2
        ms = jnp.concatenate([jnp.dot(sq[:, :half], eavg, preferred_element_type=F32),
                              jnp.dot(sq[:, half:], eavg, preferred_element_type=F32)], axis=1)
        return a * lax.rsqrt(ms + EPS) * g

    def residue_major(val, ref, dil, stage):
        if dil == 1:
            ref[0, 0] = val.astype(BF16)
            return
        for c in range(stage.shape[0]):
            stage[c] = val[:, c * LANES:(c + 1) * LANES]
        for r in range(dil):
            rows = [stage[c, pl.ds(r, tm // dil, stride=dil), :] for c in range(stage.shape[0])]
            ref[0, r] = jnp.concatenate(rows, axis=1).astype(BF16)

    for g, (_, dil) in enumerate(BRANCHES):
        k0 = 2 * g * ATTN_WIDTH
        kvg = jnp.dot(hk, wkv_ref[:, k0:k0 + 2 * ATTN_WIDTH], preferred_element_type=F32)
        kn = head_norm(kvg[:, :ATTN_WIDTH], kn_ref[g:g + 1, :])
        vv = kvg[:, ATTN_WIDTH:]
        q0 = g * ATTN_WIDTH
        qg = jnp.dot(hq, wq_ref[:, q0:q0 + ATTN_WIDTH], preferred_element_type=F32)
        qg = head_norm(qg, qn_ref[g:g + 1, :]) * (HEAD_DIM ** -0.5)
        if residue:
            stages = rest[4 * N_BRANCH:]
            residue_major(qg, rest[3 * g], dil, stages[3 * g])
            residue_major(kn, rest[3 * g + 1], dil, stages[3 * g + 1])
            residue_major(vv, rest[3 * g + 2], dil, stages[3 * g + 2])
            kt_ref = rest[3 * N_BRANCH + g]
            kt_ref[0, :ATTN_WIDTH, :] = kn.T
            kt_ref[0, ATTN_WIDTH:, :] = vv.T
        else:
            rest[0][:, k0:k0 + ATTN_WIDTH] = kn
            rest[0][:, k0 + ATTN_WIDTH:k0 + 2 * ATTN_WIDTH] = vv
            rest[1][:, q0:q0 + ATTN_WIDTH] = qg


def _kvq(x, modkv, mod, tiles_per_mod, tm, nkv, nm, wkv, wq, kn, qn, eavg, batch=None):
    T = x.shape[0]
    tok = lambda w: pl.BlockSpec((tm, w), lambda i: (i, 0))
    kvw = 2 * N_BRANCH * ATTN_WIDTH
    if batch is not None:
        L = T // batch
        tpb = L // tm
        kept = tuple(-(-min(w, L) // tm) for w, _ in BRANCHES)
        keep = (tpb, kept)
        out_specs, out_shape = [], []
        for _, dil in BRANCHES:
            for _ in range(3):
                out_specs.append(pl.BlockSpec((1, dil, tm // dil, ATTN_WIDTH), lambda i: (i // tpb, 0, i % tpb, 0)))
                out_shape.append(jax.ShapeDtypeStruct((batch, dil, L // dil, ATTN_WIDTH), BF16))
        for kg in kept:
            out_specs.append(pl.BlockSpec((1, 2 * ATTN_WIDTH, tm),
                                          lambda i, kg=kg: (i // tpb, 0, jnp.maximum(i % tpb - (tpb - kg), 0))))
            out_shape.append(jax.ShapeDtypeStruct((batch, 2 * ATTN_WIDTH, kg * tm), F32))
        scratch = [pltpu.VMEM((ATTN_WIDTH // LANES, tm if dil > 1 else 8, LANES), F32)
                   for _, dil in BRANCHES for _ in range(3)]
    else:
        keep = None
        out_specs = [tok(kvw), tok(N_BRANCH * ATTN_WIDTH)]
        out_shape = [jax.ShapeDtypeStruct((T, kvw), F32), jax.ShapeDtypeStruct((T, N_BRANCH * ATTN_WIDTH), F32)]
        scratch = []
    return pl.pallas_call(
        functools.partial(_kvq_kernel, tm=tm, keep=keep),
        grid=(T // tm,),
        in_specs=[tok(D_MODEL), _mod_spec(modkv, tiles_per_mod), _mod_spec(mod, tiles_per_mod),
                  _resident(nkv.shape), _resident(nm.shape), _resident(wkv.shape), _resident(wq.shape),
                  _resident(kn.shape), _resident(qn.shape), _resident(eavg.shape)],
        out_specs=out_specs,
        out_shape=out_shape,
        scratch_shapes=scratch,
        compiler_params=_cparams(("arbitrary",)),
        name="kvq",
    )(x, modkv, mod, nkv, nm, wkv, wq, kn, qn, eavg)


BAND_QBLOCKS = 8


def _band_attn_kernel(q_ref, kp_ref, kc_ref, vp_ref, vc_ref, bias_ref, o_ref, lse_ref):
    n = pl.program_id(2)
    hw = ATTN_WIDTH // 2
    hh = HEADS // 2
    col = lax.broadcasted_iota(jnp.int32, (hh * N_STEPS, 2 * N_STEPS), 1)
    first = jnp.logical_or(col >= N_STEPS, n > 0)
    lane = lax.broadcasted_iota(jnp.int32, (N_STEPS, LANES), 1)
    head_of = lax.broadcasted_iota(jnp.int32, (N_STEPS, hw), 1) // HEAD_DIM
    zero = jnp.zeros((N_STEPS, hw), BF16)
    nt = (((1,), (1,)), ((), ()))
    for r, j in [(r, j) for r in range(q_ref.shape[0]) for j in range(q_ref.shape[1] // N_STEPS)]:
        if j == 0:
            k_all = jnp.concatenate([kp_ref[r], kc_ref[r]], axis=0)
            v_all = jnp.concatenate([vp_ref[r], vc_ref[r]], axis=0)
        rows = slice(j * N_STEPS, (j + 1) * N_STEPS)
        q = q_ref[r, rows, :]
        k2 = k_all[j * N_STEPS:(j + 2) * N_STEPS]
        v2 = v_all[j * N_STEPS:(j + 2) * N_STEPS]
        lse_all = jnp.zeros((N_STEPS, LANES), F32)
        halves = []
        for c in range(2):
            cols = slice(c * hw, (c + 1) * hw)
            qc = q[:, cols]
            qbd = jnp.concatenate([jnp.where(head_of == h, qc, zero) for h in range(hh)], axis=0)
            s = (lax.dot_general(qbd, k2[:, cols], nt, preferred_element_type=F32)
                 + bias_ref[c * hh * N_STEPS:(c + 1) * hh * N_STEPS, :])
            if j == 0:
                s = jnp.where(first, s, NEG)
            m = jnp.max(s, axis=1, keepdims=True)
            p = jnp.exp(s - m)
            l = jnp.sum(p, axis=1, keepdims=True)
            pv = jnp.dot(p.astype(BF16), v2[:, cols], preferred_element_type=F32) / l
            lse = m + jnp.log(l)
            o = jnp.zeros((N_STEPS, hw), F32)
            for h in range(hh):
                blk = slice(h * N_STEPS, (h + 1) * N_STEPS)
                o = jnp.where(head_of == h, pv[blk], o)
                lse_all = jnp.where(lane == c * hh + h, lse[blk], lse_all)
            halves.append(o)
        o_ref[r, rows, :] = jnp.concatenate(halves, axis=1)
        lse_ref[r, rows, :] = lse_all


def _band_attn(q, k, v, bias, dil):
    B, _, Ld, _ = q.shape
    nq = min(BAND_QBLOCKS, Ld // N_STEPS)
    nr = min(BAND_QBLOCKS // nq, dil)
    step = nq * N_STEPS
    blk = lambda w, f: pl.BlockSpec((None, nr, step, w), f)
    cur = lambda b, r, n: (b, r, n, 0)
    prev_blk = pl.BlockSpec((None, nr, N_STEPS, ATTN_WIDTH), lambda b, r, n: (b, r, jnp.maximum(nq * n - 1, 0), 0))
    return pl.pallas_call(
        _band_attn_kernel,
        grid=(B, dil // nr, Ld // step),
        in_specs=[blk(ATTN_WIDTH, cur), prev_blk, blk(ATTN_WIDTH, cur), prev_blk,
                  blk(ATTN_WIDTH, cur), _resident(bias.shape)],
        out_specs=[blk(ATTN_WIDTH, cur), blk(LANES, cur)],
        out_shape=[jax.ShapeDtypeStruct((B, dil, Ld, ATTN_WIDTH), F32),
                   jax.ShapeDtypeStruct((B, dil, Ld, LANES), F32)],
        compiler_params=_cparams(("arbitrary", "arbitrary", "arbitrary")),
        name=f"band_attn_d{dil}",
    )(q, k, k, v, v, bias)


def _rel_buckets(dilation):
    n = np.arange(N_STEPS + 1) * dilation
    large = MAX_EXACT + (np.log(np.maximum(n, 1) / MAX_EXACT) / np.log(REL_MAX_DIST / MAX_EXACT)
                         * (NUM_BUCKETS - MAX_EXACT)).astype(np.int32)
    return np.where(n < MAX_EXACT, n, np.minimum(large, NUM_BUCKETS - 1)).astype(np.int32)


def _step_bias_row(rel_bias, g, dil):
    onehot = np.zeros((N_STEPS + 1, NUM_BUCKETS), np.float32)
    onehot[np.arange(N_STEPS + 1), _rel_buckets(dil)] = 1.0
    return jnp.dot(jnp.asarray(onehot), rel_bias[:, g, :].astype(F32), precision=lax.Precision.HIGHEST).T


def _band_bias(rel_bias, g, dil):
    bias = _step_bias_row(rel_bias, g, dil)
    P = 3 * N_STEPS
    neg = jnp.full((HEADS, N_STEPS), NEG, F32)
    ext = jnp.concatenate([neg, bias[:, ::-1], neg], axis=1)
    flat = jnp.broadcast_to(ext[:, None, :], (HEADS, N_STEPS, P + 1)).reshape(HEADS, N_STEPS * (P + 1))
    skew = flat[:, :N_STEPS * P].reshape(HEADS, N_STEPS, P)
    return skew[:, :, N_STEPS:].reshape(HEADS * N_STEPS, 2 * N_STEPS)


def _step_attn_kernel(q_ref, kvn_ref, c0_ref, c1_ref, c2_ref, b0_ref, b1_ref, b2_ref, bn_ref, o_ref, *, n_new):
    caches = (c0_ref, c1_ref, c2_ref)
    cbias = (b0_ref, b1_ref, b2_ref)
    nt = (((1,), (1,)), ((), ()))
    rows = n_new * HEADS
    row = lax.broadcasted_iota(jnp.int32, (rows, ATTN_WIDTH), 0)
    lane = lax.broadcasted_iota(jnp.int32, (rows, ATTN_WIDTH), 1)
    own_head = lane // HEAD_DIM == row % HEADS
    parts = []
    for g in range(N_BRANCH):
        klo = 2 * g * ATTN_WIDTH
        qg = q_ref[0, :, g * ATTN_WIDTH:(g + 1) * ATTN_WIDTH]
        qrep = jnp.concatenate([jnp.broadcast_to(qg[s:s + 1, :], (HEADS, ATTN_WIDTH)) for s in range(n_new)], axis=0)
        qbd = jnp.where(own_head, qrep, 0.0).astype(BF16)
        kn = kvn_ref[0, :, klo:klo + ATTN_WIDTH].astype(BF16)
        vn = kvn_ref[0, :, klo + ATTN_WIDTH:klo + 2 * ATTN_WIDTH].astype(BF16)
        kt = caches[g][0, 0].astype(BF16)
        vt = caches[g][0, 1].astype(BF16)
        sc = jnp.dot(qbd, kt, preferred_element_type=F32) + cbias[g][...]
        sn = lax.dot_general(qbd, kn, nt, preferred_element_type=F32) + bn_ref[g]
        m = jnp.maximum(jnp.max(sc, axis=1, keepdims=True), jnp.max(sn, axis=1, keepdims=True))
        pc = jnp.exp(sc - m)
        pn = jnp.exp(sn - m)
        l = jnp.sum(pc, axis=1, keepdims=True) + jnp.sum(pn, axis=1, keepdims=True)
        pv = (lax.dot_general(pc.astype(BF16), vt, nt, preferred_element_type=F32)
              + jnp.dot(pn.astype(BF16), vn, preferred_element_type=F32))
        parts.append((m + jnp.log(l), pv / l))
    mx = jnp.maximum(jnp.maximum(parts[0][0], parts[1][0]), parts[2][0])
    es = [jnp.exp(lse - mx) for lse, _ in parts]
    den = es[0] + es[1] + es[2]
    mixed = (es[0] * parts[0][1] + es[1] * parts[1][1] + es[2] * parts[2][1]) / den
    mixed = jnp.where(own_head, mixed, 0.0)
    for s in range(n_new):
        o_ref[0, s:s + 1, :] = jnp.sum(mixed[s * HEADS:(s + 1) * HEADS], axis=0, keepdims=True)


def _step_bias(rel_bias, n_new):
    cache_tabs, new_tabs = [], []
    for g, (win, dil) in enumerate(BRANCHES):
        bias = _step_bias_row(rel_bias, g, dil)
        rev = bias[:, ::-1]
        per_s = []
        for s in range(n_new):
            if dil == 1:
                neg = jnp.full((HEADS, s), NEG, F32)
                per_s.append(jnp.concatenate([neg, rev[:, :win - s]], axis=1))
            else:
                cols = [rev[:, :N_STEPS] if r == s % dil else jnp.full((HEADS, N_STEPS), NEG, F32)
                        for r in range(dil)]
                per_s.append(jnp.stack(cols, axis=2).reshape(HEADS, win))
        cache_tabs.append(jnp.stack(per_s, axis=0).reshape(n_new * HEADS, win))
        dist = np.arange(n_new)[:, None] - np.arange(8)[None, :]
        ok = (dist >= 0) & (dist % dil == 0) & (np.arange(8)[None, :] < n_new)
        onehot = np.zeros((N_STEPS + 1, n_new * 8), np.float32)
        onehot[np.where(ok, dist // dil, 0).reshape(-1), np.arange(n_new * 8)] = 1.0
        tab = jnp.dot(bias, jnp.asarray(onehot), precision=lax.Precision.HIGHEST).reshape(HEADS, n_new, 8)
        new_tabs.append(jnp.where(ok[None], tab, NEG).transpose(1, 0, 2).reshape(n_new * HEADS, 8))
    return cache_tabs, jnp.stack(new_tabs)


def _step_attn(q, kvn, caches, cache_bias, new_bias, DB, S):
    cspec = lambda c: pl.BlockSpec((1,) + c.shape[1:], lambda b: (b, 0, 0, 0))
    return pl.pallas_call(
        functools.partial(_step_attn_kernel, n_new=S),
        grid=(DB,),
        in_specs=[pl.BlockSpec((1, S, N_BRANCH * ATTN_WIDTH), lambda b: (b, 0, 0)),
                  pl.BlockSpec((1, 8, 2 * N_BRANCH * ATTN_WIDTH), lambda b: (b, 0, 0)),
                  cspec(caches[0]), cspec(caches[1]), cspec(caches[2]),
                  _resident(cache_bias[0].shape), _resident(cache_bias[1].shape), _resident(cache_bias[2].shape),
                  _resident(new_bias.shape)],
        out_specs=pl.BlockSpec((1, S, ATTN_WIDTH), lambda b: (b, 0, 0)),
        out_shape=jax.ShapeDtypeStruct((DB, S, ATTN_WIDTH), F32),
        compiler_params=_cparams(("arbitrary",)),
        name="step_attn",
    )(q, kvn, caches[0], caches[1], caches[2], *cache_bias, new_bias)


def _attn_out_kernel(*refs, dils, tm):
    n_o = max(len(dils), 1)
    n_in = n_o + len(dils) + 9
    (x_ref, mod_ref, wo_ref, ex_ref, nf_ref, wr_ref, br_ref, cin_ref, tri_ref,
     x3_ref, h3_ref, rt_ref, rtt_ref, cnt_ref, carry) = refs[n_in - 9:n_in + 6]
    stages = refs[n_in + 6:]
    _start_counts(cin_ref, carry)

    def natural(ref, stage, dil):
        if dil == 1:
            return ref[0, 0]
        for r in range(dil):
            val = ref[0, r]
            for c in range(stage.shape[0]):
                stage[c, pl.ds(r, tm // dil, stride=dil), :] = val[:, c * LANES:(c + 1) * LANES]
        return jnp.concatenate([stage[c] for c in range(stage.shape[0])], axis=1)

    if dils:
        lses = [natural(refs[n_o + g], stages[2 * g + 1], dil) for g, dil in enumerate(dils)]
        mx = functools.reduce(jnp.maximum, lses)
        es = [jnp.exp(l - mx) for l in lses]
        den = functools.reduce(lambda a, b: a + b, es)
        o = None
        for g, dil in enumerate(dils):
            term = _split_dot(es[g] / den, ex_ref[...]) * natural(refs[g], stages[2 * g], dil)
            o = term if o is None else o + term
    else:
        o = refs[0][...]
    mod = mod_ref[0]
    a = jnp.dot(o.astype(BF16), wo_ref[...], preferred_element_type=F32)
    x3 = x_ref[...] + mod[:, 2 * D_MODEL:3 * D_MODEL] * a
    x3_ref[...] = x3
    _ffn_pre(x3, mod, nf_ref, wr_ref, br_ref, tri_ref, h3_ref, rt_ref, rtt_ref, cnt_ref, carry, tm)


def _attn_out(os_, lses, x, mod, tiles_per_mod, tm, wo, ex, nf, wr, br, counts):
    T = x.shape[0]
    tok = lambda w: pl.BlockSpec((tm, w), lambda i: (i, 0))
    r_specs, r_shapes = _route_outs(T, tm)
    tri = _earlier_rows(tm)
    if lses:
        dils = tuple(o.shape[1] for o in os_)
        tpb = tiles_per_mod
        res = lambda a: pl.BlockSpec((1, a.shape[1], tm // a.shape[1], a.shape[3]), lambda i: (i // tpb, 0, i % tpb, 0))
        o_specs = [res(a) for a in os_] + [res(a) for a in lses]
        scratch = []
        for _ in dils:
            scratch += [pltpu.VMEM((ATTN_WIDTH // LANES, tm, LANES), F32), pltpu.VMEM((1, tm, LANES), F32)]
    else:
        dils, o_specs, scratch = (), [tok(ATTN_WIDTH)], []
    return pl.pallas_call(
        functools.partial(_attn_out_kernel, dils=dils, tm=tm),
        grid=(T // tm,),
        in_specs=(o_specs + [tok(D_MODEL), _mod_spec(mod, tiles_per_mod), _resident(wo.shape), _resident(ex.shape),
                             _resident(nf.shape), _resident(wr.shape), _resident(br.shape),
                             _resident(counts.shape), _resident(tri.shape)]),
        out_specs=[tok(D_MODEL), pl.BlockSpec((tm * ROW_TILES, LANES), lambda i: (i, 0))] + r_specs,
        out_shape=[jax.ShapeDtypeStruct((T, D_MODEL), F32),
                   jax.ShapeDtypeStruct((T * ROW_TILES, LANES), F32)] + r_shapes,
        scratch_shapes=[pltpu.VMEM((1, LANES), F32)] + scratch,
        compiler_params=_cparams(("arbitrary",)),
        name="attn_out",
    )(*os_, *lses, x, mod, wo, ex, nf, wr, br, counts, tri)


def kernel(x_prompt, x_sample, cache_kv_w128, cache_kv_w512, cache_kv_w2048, c_prompt, c_sample, ada_w, ada_b, norm_mix, norm_ffn, a_w_in, a_b_in, a_norm_v, a_w_s, a_b_s, a_w_out, kv_ada_w, kv_ada_b, kv_norm, w_kv, k_norm, rel_bias, b_w_q, q_norm, b_w_o, r_w_group, r_b_group, r_w_expert, r_b_expert, e_w1, e_w3, e_w2):
    B, L, _ = x_prompt.shape
    DB, S, _ = x_sample.shape
    Tp, Ts = B * L, DB * S
    tm = TOKEN_TILE
    tpm_p = L // tm

    c_all = jnp.concatenate([c_prompt, c_sample], axis=0)
    R = c_all.shape[0]
    c_all = jnp.pad(c_all, ((0, -R % 8), (0, 0)))
    mods = [_ada(c_all, ada_w, ada_b, l) for l in range(2)]
    modkv = _ada(c_all, kv_ada_w[None], kv_ada_b[None], 0)

    def split_mod(m):
        return m[:B, None, :], jnp.repeat(m[B:B + DB], S, axis=0)[None]
    mod_p, mod_s = zip(*[split_mod(m) for m in mods])
    modkv_p, modkv_s = split_mod(modkv)

    row = lambda a: a.reshape(1, -1)

    def router(l):
        wr = jnp.zeros((D_MODEL, LANES), F32)
        wr = wr.at[:, :N_GROUPS].set(r_w_group[l]).at[:, N_GROUPS:N_GROUPS + N_EXPERTS].set(r_w_expert[l])
        br = jnp.zeros((1, LANES), F32)
        br = br.at[0, :N_GROUPS].set(r_b_group[l]).at[0, N_GROUPS:N_GROUPS + N_EXPERTS].set(r_b_expert[l])
        return wr.astype(BF16), br

    no_counts = jnp.zeros((1, LANES), F32)

    win = a_w_in[0].astype(BF16)
    wout = a_w_out[0].astype(BF16)
    tril = jnp.tril(jnp.ones((CHUNK, CHUNK), bool))
    ws_p = jnp.where(tril, a_w_s[0], 0).astype(BF16)
    bs_p = a_b_s[0].T
    cs = min(CHUNK, S)
    ws_small = jnp.where(jnp.tril(jnp.ones((cs, cs), bool)), a_w_s[0][:, :cs, :cs], 0)
    ws_s = jnp.stack([jnp.kron(jnp.eye(Ts // cs, dtype=F32), ws_small[g]) for g in range(GMLP_GROUPS)]).astype(BF16)
    bs_s = jnp.tile(a_b_s[0][:, :cs], (1, Ts // cs)).T
    wr0, br0 = router(0)
    common = (row(norm_mix[0]), win, row(a_b_in[0]), row(a_norm_v[0]))
    xp = x_prompt.reshape(Tp, D_MODEL)
    xs_ = x_sample.reshape(Ts, D_MODEL)
    x1_p, h2_p, rt_p, rtt_p, cnt = _mixer_a(xp, mod_p[0], tpm_p, tm, *common, ws_p, bs_p, wout, row(norm_ffn[0]),
                                            wr0, br0, no_counts, with_v=False)
    x1_s, h2_s, rt_s, rtt_s, cnt, v_s = _mixer_a(xs_, mod_s[0], 1, Ts, *common, ws_s, bs_s, wout, row(norm_ffn[0]),
                                                 wr0, br0, cnt, with_v=True)
    x2_p, x2_s = _moe((h2_p, rt_p, rtt_p, x1_p, mod_p[0], tpm_p, tm), (h2_s, rt_s, rtt_s, x1_s, mod_s[0], 1, Ts), cnt,
                      e_w1, e_w3, e_w2, 0)

    wkv = w_kv.astype(BF16)
    wq = b_w_q[0].astype(BF16)
    kn = jnp.tile(k_norm, (1, HEADS))
    qn = jnp.tile(q_norm[0], (1, HEADS))
    head = np.arange(ATTN_WIDTH) // HEAD_DIM
    half_head = head[:ATTN_WIDTH // 2]
    eavg = jnp.asarray((half_head[:, None] == half_head[None, :]) / HEAD_DIM, BF16)
    kvq_w = (row(kv_norm), row(norm_mix[1]), wkv, wq, kn, qn, eavg)
    *qkv_p, kt0, kt1, kt2 = _kvq(x2_p, modkv_p, mod_p[1], tpm_p, tm, *kvq_w, batch=B)
    kv_s, q_s = _kvq(x2_s, modkv_s, mod_s[1], 1, Ts, *kvq_w)

    os_, lses = [], []
    for g, (_, dil) in enumerate(BRANCHES):
        o, lse = _band_attn(*qkv_p[3 * g:3 * g + 3], _band_bias(rel_bias, g, dil), dil)
        os_.append(o)
        lses.append(lse)
    caches = [jnp.transpose(c, (0, 2, 3, 4, 1)).reshape(DB, 2, ATTN_WIDTH, c.shape[1])
              for c in (cache_kv_w128, cache_kv_w512, cache_kv_w2048)]
    kvn_s = jnp.pad(kv_s.reshape(DB, S, -1), ((0, 0), (0, 8 - S), (0, 0)))
    o_s = _step_attn(q_s.reshape(DB, S, -1), kvn_s, caches, *_step_bias(rel_bias, S), DB, S)

    wo = b_w_o[0].astype(BF16)
    ex = jnp.asarray(np.arange(LANES)[:, None] == head[None, :], BF16)
    wr1, br1 = router(1)
    x3_p, h3_p, rt1_p, rtt1_p, cnt1 = _attn_out(os_, lses, x2_p, mod_p[1], tpm_p, tm, wo, ex, row(norm_ffn[1]),
                                                wr1, br1, no_counts)
    x3_s, h3_s, rt1_s, rtt1_s, cnt1 = _attn_out([o_s.reshape(Ts, ATTN_WIDTH)], [], x2_s, mod_s[1], 1, Ts, wo, ex,
                                                row(norm_ffn[1]), wr1, br1, cnt1)
    y_p, y_s = _moe((h3_p, rt1_p, rtt1_p, x3_p, mod_p[1], tpm_p, tm), (h3_s, rt1_s, rtt1_s, x3_s, mod_s[1], 1, Ts),
                    cnt1, e_w1, e_w3, e_w2, 1)

    kv_s4 = kv_s.reshape(DB, S, N_BRANCH, 2, HEADS, HEAD_DIM)

    def window(kt, w):
        n = min(w, L)
        return jnp.transpose(kt[:, :, kt.shape[2] - n:].reshape(B, 2, HEADS, HEAD_DIM, n), (0, 4, 1, 2, 3))
    return (y_p.reshape(B, L, D_MODEL), y_s.reshape(DB, S, D_MODEL),
            window(kt0, BRANCHES[0][0]), window(kt1, BRANCHES[1][0]), window(kt2, BRANCHES[2][0]),
            kv_s4[:, :, 0], kv_s4[:, :, 1], kv_s4[:, :, 2],
            v_s.reshape(1, DB, S, GMLP_WIDTH))
```

```python
import functools

import numpy as np
import jax
import jax.numpy as jnp
from jax import lax
from jax.experimental import pallas as pl
from jax.experimental.pallas import tpu as pltpu

F32 = jnp.float32
BF16 = jnp.bfloat16

D_MODEL = 1024
GMLP_WIDTH = 2048
GMLP_GROUPS = 4
GROUP_WIDTH = GMLP_WIDTH // GMLP_GROUPS
CHUNK = 128
BRANCHES = ((128, 1), (512, 4), (2048, 16))
N_BRANCH = 3
N_STEPS = 128
HEADS = 8
HEAD_DIM = 64
ATTN_WIDTH = HEADS * HEAD_DIM
NUM_BUCKETS = 32
MAX_EXACT = NUM_BUCKETS // 2
REL_MAX_DIST = 2048
N_GROUPS = 4
EXPERTS_PER_GROUP = 8
N_EXPERTS = N_GROUPS * EXPERTS_PER_GROUP
TOP_K = 2
D_EXPERT = 512
EPS = 1e-6
NEG = -1e30

LANES = 128
ROW_TILES = D_MODEL // LANES
TOKEN_TILE = 512
MOE_BLOCK = 256
DMA_UNROLL = 16
VMEM_LIMIT = 52 * 1024 * 1024


def _cparams(sem):
    return pltpu.CompilerParams(dimension_semantics=sem, vmem_limit_bytes=VMEM_LIMIT)


def _resident(shape):
    nd = len(shape)
    return pl.BlockSpec(shape, lambda *_, _nd=nd: (0,) * _nd, pipeline_mode=pl.Buffered(1))


def _gelu_tanh(x):
    return 0.5 * x * (1.0 + jnp.tanh(0.7978845608028654 * (x + 0.044715 * (x * x * x))))


def _rms(x, g):
    return x * lax.rsqrt(jnp.mean(x * x, axis=-1, keepdims=True) + EPS) * g


def _store_rows8(ref, val, n, base=0):
    for s in range(ROW_TILES):
        ref[pl.ds(base + s, n, stride=ROW_TILES), :] = val[:, s * LANES:(s + 1) * LANES]


def _load_rows8(ref, n, base=0):
    return jnp.concatenate([ref[pl.ds(base + s, n, stride=ROW_TILES), :] for s in range(ROW_TILES)], axis=1)


def _split(a):
    hi = a.astype(BF16)
    return hi, (a - hi.astype(F32)).astype(BF16)


def _split_dot(a, e_bf16):
    hi, lo = _split(a)
    return (jnp.dot(hi, e_bf16, preferred_element_type=F32) + jnp.dot(lo, e_bf16, preferred_element_type=F32))


ROUTE_LANE0 = 4


def _route_rows(l, tri_ref, carry):
    lane = lax.broadcasted_iota(jnp.int32, l.shape, 1).astype(F32)
    far = float(LANES)

    def first_lane(mask):
        return jnp.min(jnp.where(mask, lane, far), axis=1, keepdims=True)

    is_g = lane < N_GROUPS
    gl = jnp.where(is_g, l, NEG)
    gmax = jnp.max(gl, axis=1, keepdims=True)
    g_i = first_lane(jnp.logical_and(gl == gmax, is_g))
    g_p = 1.0 / jnp.sum(jnp.where(is_g, jnp.exp(gl - gmax), 0.0), axis=1, keepdims=True)
    lo = ROUTE_LANE0 + EXPERTS_PER_GROUP * g_i
    sel = jnp.logical_and(lane >= lo, lane < lo + EXPERTS_PER_GROUP)
    el = jnp.where(sel, l, NEG)
    m1 = jnp.max(el, axis=1, keepdims=True)
    i1 = first_lane(jnp.logical_and(el == m1, sel))
    sel2 = jnp.logical_and(sel, lane != i1)
    el2 = jnp.where(sel2, l, NEG)
    m2 = jnp.max(el2, axis=1, keepdims=True)
    i2 = first_lane(jnp.logical_and(el2 == m2, sel2))
    r = jnp.exp(m2 - m1)
    w1 = g_p / (1.0 + r)
    w2 = g_p * r / (1.0 + r)

    hit1 = lane == i1
    hit2 = lane == i2
    onehot = jnp.where(jnp.logical_or(hit1, hit2), 1.0, 0.0)
    before = carry[...] + jnp.dot(tri_ref[...], onehot.astype(BF16), preferred_element_type=F32)
    rank1 = jnp.sum(jnp.where(hit1, before, 0.0), axis=1, keepdims=True)
    rank2 = jnp.sum(jnp.where(hit2, before, 0.0), axis=1, keepdims=True)
    carry[...] = carry[...] + jnp.sum(onehot, axis=0, keepdims=True)

    out = jnp.zeros(l.shape, F32)
    for k, val in enumerate((i1 - ROUTE_LANE0, i2 - ROUTE_LANE0, w1, w2, rank1, rank2)):
        out = jnp.where(lane == k, val, out)
    return out


ROUTE_FIELDS = 8


def _start_counts(cin_ref, carry):
    @pl.when(pl.program_id(0) == 0)
    def _():
        carry[...] = cin_ref[...]


def _ffn_pre(x, mod, nf_ref, wr_ref, br_ref, tri_ref, h_ref, rt_ref, rtt_ref, cnt_ref, carry, n):
    h = _rms(x, nf_ref[...]) * (1.0 + mod[:, 4 * D_MODEL:5 * D_MODEL]) + mod[:, 3 * D_MODEL:4 * D_MODEL]
    _store_rows8(h_ref, h, n)
    logits = jnp.dot(h.astype(BF16), wr_ref[...], preferred_element_type=F32) + br_ref[...]
    route = _route_rows(logits, tri_ref, carry)
    rt_ref[...] = route
    rtt_ref[...] = route.T[:ROUTE_FIELDS, :]
    cnt_ref[...] = carry[...]


def _ada_kernel(c_ref, w_ref, b_ref, o_ref):
    c = c_ref[...]
    a = (c * jax.nn.sigmoid(c)).astype(BF16)
    o_ref[...] = jnp.dot(a, w_ref[...].astype(BF16), preferred_element_type=F32) + b_ref[...]


def _ada(c, w, b, layer):
    R = c.shape[0]
    N = w.shape[2]
    tn = 1024
    return pl.pallas_call(
        _ada_kernel,
        grid=(N // tn,),
        in_specs=[pl.BlockSpec((R, D_MODEL), lambda j: (0, 0)),
                  pl.BlockSpec((None, D_MODEL, tn), lambda j: (layer, 0, j)),
                  pl.BlockSpec((None, 1, tn), lambda j: (layer, 0, j))],
        out_specs=pl.BlockSpec((R, tn), lambda j: (0, j)),
        out_shape=jax.ShapeDtypeStruct((R, N), F32),
        compiler_params=_cparams(("arbitrary",)),
        name="ada",
    )(c, w, b.reshape(b.shape[0], 1, N))


def _mixer_a_kernel(x_ref, mod_ref, nm_ref, win_ref, bin_ref, gv_ref, ws_ref, bs_ref, wout_ref,
                    nf_ref, wr_ref, br_ref, cin_ref, tri_ref, x1_ref, h2_ref, rt_ref, rtt_ref, cnt_ref, *rest, tm):
    v_refs, carry = rest[:-1], rest[-1]
    _start_counts(cin_ref, carry)
    x = x_ref[...]
    mod = mod_ref[0]
    h = (_rms(x, nm_ref[...]) * (1.0 + mod[:, D_MODEL:2 * D_MODEL]) + mod[:, 0:D_MODEL]).astype(BF16)
    zv = jnp.dot(h, win_ref[:, GMLP_WIDTH:], preferred_element_type=F32) + bin_ref[:, GMLP_WIDTH:]
    v = _rms(_gelu_tanh(zv), gv_ref[...])
    if v_refs:
        v_refs[0][...] = v
    vb = v.astype(BF16)
    bs = bs_ref[...]
    acc = jnp.zeros((tm, D_MODEL), F32)
    for g in range(GMLP_GROUPS):
        lo, hi = g * GROUP_WIDTH, (g + 1) * GROUP_WIDTH
        u = _gelu_tanh(jnp.dot(h, win_ref[:, lo:hi], preferred_element_type=F32) + bin_ref[:, lo:hi])
        wg = ws_ref[g]
        gate = jnp.concatenate(
            [jnp.dot(wg, vb[c * CHUNK:(c + 1) * CHUNK, lo:hi], preferred_element_type=F32) + bs[:, g:g + 1]
             for c in range(tm // CHUNK)], axis=0)
        acc = acc + jnp.dot((u * gate).astype(BF16), wout_ref[lo:hi, :], preferred_element_type=F32)
    x1 = x + mod[:, 2 * D_MODEL:3 * D_MODEL] * acc
    x1_ref[...] = x1
    _ffn_pre(x1, mod, nf_ref, wr_ref, br_ref, tri_ref, h2_ref, rt_ref, rtt_ref, cnt_ref, carry, tm)


def _mod_spec(mod, tiles_per_mod):
    _, rows, width = mod.shape
    return pl.BlockSpec((1, rows, width), lambda i: (i // tiles_per_mod, 0, 0))


def _earlier_rows(tm):
    return jnp.asarray(np.tril(np.ones((tm, tm), np.float32), -1), BF16)


def _route_outs(T, tm):
    return ([pl.BlockSpec((tm, LANES), lambda i: (i, 0)), pl.BlockSpec((ROUTE_FIELDS, tm), lambda i: (0, i)),
             pl.BlockSpec((1, LANES), lambda i: (0, 0))],
            [jax.ShapeDtypeStruct((T, LANES), F32), jax.ShapeDtypeStruct((ROUTE_FIELDS, T), F32),
             jax.ShapeDtypeStruct((1, LANES), F32)])


def _mixer_a(x, mod, tiles_per_mod, tm, nm, win, bin_, gv, ws, bs_t, wout, nf, wr, br, counts, with_v):
    T = x.shape[0]
    tok = lambda w: pl.BlockSpec((tm, w), lambda i: (i, 0))
    r_specs, r_shapes = _route_outs(T, tm)
    out_shape = [jax.ShapeDtypeStruct((T, D_MODEL), F32), jax.ShapeDtypeStruct((T * ROW_TILES, LANES), F32)] + r_shapes
    out_specs = [tok(D_MODEL), pl.BlockSpec((tm * ROW_TILES, LANES), lambda i: (i, 0))] + r_specs
    if with_v:
        out_shape.append(jax.ShapeDtypeStruct((T, GMLP_WIDTH), F32))
        out_specs.append(tok(GMLP_WIDTH))
    tri = _earlier_rows(tm)
    return pl.pallas_call(
        functools.partial(_mixer_a_kernel, tm=tm),
        grid=(T // tm,),
        in_specs=[tok(D_MODEL), _mod_spec(mod, tiles_per_mod), _resident(nm.shape), _resident(win.shape),
                  _resident(bin_.shape), _resident(gv.shape), _resident(ws.shape), _resident(bs_t.shape),
                  _resident(wout.shape), _resident(nf.shape), _resident(wr.shape), _resident(br.shape),
                  _resident(counts.shape), _resident(tri.shape)],
        out_specs=out_specs,
        out_shape=out_shape,
        scratch_shapes=[pltpu.VMEM((1, LANES), F32)],
        compiler_params=_cparams(("arbitrary",)),
        name="mixer_a",
    )(x, mod, nm, win, bin_, gv, ws, bs_t, wout, nf, wr, br, counts, tri)


def _layout(rtt, counts, block):
    n_tokens = rtt.shape[1]
    counts = counts[0, ROUTE_LANE0:ROUTE_LANE0 + N_EXPERTS].astype(jnp.int32)
    padded = (counts + block - 1) // block * block
    pad_end = jnp.cumsum(padded)
    pad_start = (pad_end - padded).astype(F32)
    experts, ranks = rtt[0:TOP_K], rtt[4:4 + TOP_K]
    onehot = (experts[None] == jnp.arange(N_EXPERTS, dtype=F32)[:, None, None]).astype(F32)
    base = jnp.einsum('e,ekt->kt', pad_start, onehot, precision=lax.Precision.HIGHEST)
    slots = (ranks + base).astype(jnp.int32)
    nb = -(-n_tokens * TOP_K // block) + N_EXPERTS
    blk_e = jnp.minimum(jnp.sum(pad_end[None, :] <= (jnp.arange(nb, dtype=jnp.int32) * block)[:, None], axis=1),
                        N_EXPERTS - 1).astype(jnp.int32)
    n_used = (pad_end[-1] // block).astype(jnp.int32).reshape(1)
    starts = jnp.concatenate([jnp.ones((1,), jnp.int32), (blk_e[1:] != blk_e[:-1]).astype(jnp.int32)])
    grp = jnp.cumsum(starts) - 1
    end_blk = (pad_end // block).astype(jnp.int32)
    own = blk_e[:, None] == jnp.arange(N_EXPERTS, dtype=jnp.int32)[None, :]
    nxt_blk = jnp.sum(jnp.where(own, end_blk[None, :], 0), axis=1)
    at_nxt = nxt_blk[:, None] == jnp.arange(nb, dtype=jnp.int32)[None, :]
    nxt_e = jnp.where(nxt_blk < n_used[0], jnp.sum(jnp.where(at_nxt, blk_e[None, :], 0), axis=1), -1).astype(jnp.int32)
    zero_plan = (jnp.maximum(pad_end - block, 0).astype(jnp.int32), (padded > 0).astype(jnp.int32), n_used)
    return slots, (blk_e, n_used, nxt_e, grp.astype(jnp.int32)), zero_plan, nb


def _tile_slots(slots, ts):
    T = slots.shape[1]
    return slots.reshape(TOP_K, T // ts, ts).transpose(1, 0, 2).reshape(T // ts, 1, TOP_K * ts)


def _row_copy(src_ref, src_row, dst_ref, dst_row, sem):
    return pltpu.make_async_copy(
        src_ref.at[pl.ds(pl.multiple_of(src_row * ROW_TILES, ROW_TILES), ROW_TILES), :],
        dst_ref.at[pl.ds(pl.multiple_of(dst_row * ROW_TILES, ROW_TILES), ROW_TILES), :], sem)


def _scatter_kernel(zlo_ref, has_ref, nu_ref, slot_a_ref, slot_b_ref, src_a_ref, src_b_ref, dst_ref, zbuf, sem,
                    *, ts, n_a, n_b, nb, block):
    i = pl.program_id(0)

    def clear(row0):
        return pltpu.make_async_copy(
            zbuf, dst_ref.at[pl.ds(pl.multiple_of(row0 * ROW_TILES, ROW_TILES), block * ROW_TILES), :], sem)

    def for_clears(act):
        for e in range(N_EXPERTS):
            @pl.when(has_ref[e] > 0)
            def _(e=e):
                act(clear(zlo_ref[e]))

        def tail(j, c):
            act(clear(j * block))
            return c
        lax.fori_loop(nu_ref[0], nb, tail, 0)

    @pl.when(i == 0)
    def _():
        zbuf[...] = jnp.zeros(zbuf.shape, F32)
        for_clears(lambda c: c.start())
        for_clears(lambda c: c.wait())

    def scatter(slot_ref, src_ref, n):
        def issue(j, c):
            for u in range(DMA_UNROLL):
                t = j * DMA_UNROLL + u
                for k in range(TOP_K):
                    _row_copy(src_ref, t, dst_ref, slot_ref[0, 0, k * n + t], sem).start(priority=k)
            return c
        lax.fori_loop(0, n // DMA_UNROLL, issue, 0)
        for _ in range(TOP_K):
            pltpu.make_async_copy(src_ref, dst_ref.at[pl.ds(0, n * ROW_TILES), :], sem).wait()

    @pl.when(i < n_a)
    def _():
        scatter(slot_a_ref, src_a_ref, ts)

    @pl.when(i == n_a)
    def _():
        scatter(slot_b_ref, src_b_ref, n_b)


def _scatter(slots_a, slots_b, src_a, src_b, zero_plan, nb, block):
    n_a, _, ts2 = slots_a.shape
    ts, n_b = ts2 // TOP_K, slots_b.shape[2] // TOP_K
    tile_a = lambda i, *_: (jnp.minimum(i, n_a - 1), 0, 0)
    return pl.pallas_call(
        functools.partial(_scatter_kernel, ts=ts, n_a=n_a, n_b=n_b, nb=nb, block=block),
        grid_spec=pltpu.PrefetchScalarGridSpec(
            num_scalar_prefetch=3,
            grid=(n_a + 1,),
            in_specs=[pl.BlockSpec((1, 1, ts2), tile_a, memory_space=pltpu.SMEM),
                      pl.BlockSpec((1, 1, TOP_K * n_b), lambda i, *_: (0, 0, 0), memory_space=pltpu.SMEM),
                      pl.BlockSpec((ts * ROW_TILES, LANES), lambda i, *_: (jnp.minimum(i, n_a - 1), 0)),
                      pl.BlockSpec((n_b * ROW_TILES, LANES), lambda i, *_: (0, 0))],
            out_specs=pl.BlockSpec(memory_space=pl.ANY),
            scratch_shapes=[pltpu.VMEM((block * ROW_TILES, LANES), F32), pltpu.SemaphoreType.DMA(())]),
        out_shape=jax.ShapeDtypeStruct((nb * block * ROW_TILES, LANES), F32),
        compiler_params=_cparams(("arbitrary",)),
        name="moe_scatter",
    )(*zero_plan, slots_a, slots_b, src_a, src_b)


def _experts_kernel(be_ref, nu_ref, nxt_ref, grp_ref, x_ref, w1_hbm, w3_hbm, w2_hbm, y_ref,
                    w1b, w3b, w2b, wf1, wf3, wf2, sem, *, block, layer):
    i = pl.program_id(0)

    def weight_copies(e, buf):
        return [pltpu.make_async_copy(src.at[layer, e], dst.at[buf], sem.at[buf])
                for src, dst in ((w1_hbm, wf1), (w3_hbm, wf3), (w2_hbm, wf2))]

    @pl.when(i == 0)
    def _():
        for c in weight_copies(be_ref[0], 0):
            c.start()

    @pl.when(i < nu_ref[0])
    def _():
        @pl.when(jnp.logical_or(i == 0, be_ref[i] != be_ref[jnp.maximum(i - 1, 0)]))
        def _():
            buf = grp_ref[i] % 2
            for c in weight_copies(be_ref[i], buf):
                c.wait()

            @pl.when(nxt_ref[i] >= 0)
            def _():
                for c in weight_copies(nxt_ref[i], 1 - buf):
                    c.start()

            w1b[...] = wf1[buf].astype(BF16)
            w3b[...] = wf3[buf].astype(BF16)
            w2b[...] = wf2[buf].astype(BF16)

        x = _load_rows8(x_ref, block).astype(BF16)
        h1 = jnp.dot(x, w1b[...], preferred_element_type=F32)
        h3 = jnp.dot(x, w3b[...], preferred_element_type=F32)
        a = (h1 * jax.nn.sigmoid(h1) * h3).astype(BF16)
        _store_rows8(y_ref, jnp.dot(a, w2b[...], preferred_element_type=F32), block)


def _experts(xs, plan, nb, w1, w3, w2, layer, block):
    rows = pl.BlockSpec((block * ROW_TILES, LANES), lambda i, be, nu, nxt, grp: (jnp.minimum(i, nu[0] - 1), 0))
    hbm = pl.BlockSpec(memory_space=pl.ANY)
    return pl.pallas_call(
        functools.partial(_experts_kernel, block=block, layer=layer),
        grid_spec=pltpu.PrefetchScalarGridSpec(
            num_scalar_prefetch=4,
            grid=(nb,),
            in_specs=[rows, hbm, hbm, hbm],
            out_specs=rows,
            scratch_shapes=[pltpu.VMEM((D_MODEL, D_EXPERT), BF16), pltpu.VMEM((D_MODEL, D_EXPERT), BF16),
                            pltpu.VMEM((D_EXPERT, D_MODEL), BF16),
                            pltpu.VMEM((2, D_MODEL, D_EXPERT), F32), pltpu.VMEM((2, D_MODEL, D_EXPERT), F32),
                            pltpu.VMEM((2, D_EXPERT, D_MODEL), F32), pltpu.SemaphoreType.DMA((2,))]),
        out_shape=jax.ShapeDtypeStruct(xs.shape, F32),
        input_output_aliases={4: 0},
        compiler_params=_cparams(("arbitrary",)),
        name="moe_experts",
    )(*plan, xs, w1, w3, w2)


def _combine_kernel(slot_ref, slot_nxt_ref, y_ref, x_ref, mod_ref, rt_ref, o_ref, ybuf, sem, *, tc):
    i = pl.program_id(0)

    def gather(slots, buf):
        def issue(j, c):
            for u in range(DMA_UNROLL):
                t = j * DMA_UNROLL + u
                for k in range(TOP_K):
                    _row_copy(y_ref, slots[0, 0, k * tc + t], ybuf.at[buf], k * tc + t, sem.at[buf]).start(priority=k)
            return c
        lax.fori_loop(0, tc // DMA_UNROLL, issue, 0)

    @pl.when(i == 0)
    def _():
        gather(slot_ref, 0)

    cur = i % 2
    pltpu.make_async_copy(y_ref.at[pl.ds(0, TOP_K * tc * ROW_TILES), :], ybuf.at[cur], sem.at[cur]).wait()

    @pl.when(i + 1 < pl.num_programs(0))
    def _():
        gather(slot_nxt_ref, 1 - cur)

    rt = rt_ref[...]
    rows = ybuf.at[cur]
    y = (rt[:, 2:3] * _load_rows8(rows, tc) + rt[:, 3:4] * _load_rows8(rows, tc, base=tc * ROW_TILES))
    o_ref[...] = x_ref[...] + mod_ref[0][:, 5 * D_MODEL:6 * D_MODEL] * y


def _combine(slots, y, x, mod, tiles_per_mod, route, tc):
    T = x.shape[0]
    n = T // tc
    slot_spec = lambda f: pl.BlockSpec((1, 1, tc * TOP_K), f, memory_space=pltpu.SMEM)
    return pl.pallas_call(
        functools.partial(_combine_kernel, tc=tc),
        grid=(n,),
        in_specs=[slot_spec(lambda i: (i, 0, 0)), slot_spec(lambda i: (jnp.minimum(i + 1, n - 1), 0, 0)),
                  pl.BlockSpec(memory_space=pl.ANY),
                  pl.BlockSpec((tc, D_MODEL), lambda i: (i, 0)),
                  _mod_spec(mod, tiles_per_mod),
                  pl.BlockSpec((tc, LANES), lambda i: (i, 0))],
        out_specs=pl.BlockSpec((tc, D_MODEL), lambda i: (i, 0)),
        out_shape=jax.ShapeDtypeStruct((T, D_MODEL), F32),
        scratch_shapes=[pltpu.VMEM((2, TOP_K * tc * ROW_TILES, LANES), F32), pltpu.SemaphoreType.DMA((2,))],
        compiler_params=_cparams(("arbitrary",)),
        name="moe_combine",
    )(slots, slots, y, x, mod, route)


def _moe(part_a, part_b, counts, w1, w3, w2, layer):
    Ta = part_a[3].shape[0]
    slots, plan, zero_plan, nb = _layout(jnp.concatenate([part_a[2], part_b[2]], axis=1), counts, MOE_BLOCK)
    tiled = [_tile_slots(slots[:, :Ta], part_a[6]), _tile_slots(slots[:, Ta:], part_b[6])]
    xs = _scatter(tiled[0], tiled[1], part_a[0], part_b[0], zero_plan, nb, MOE_BLOCK)
    y = _experts(xs, plan, nb, w1, w3, w2, layer, MOE_BLOCK)
    return [_combine(sl, y, x, mod, tpm, rt, tt) for sl, (_, rt, _, x, mod, tpm, tt) in zip(tiled, (part_a, part_b))]


def _kvq_kernel(x_ref, modkv_ref, mod_ref, nkv_ref, nm_ref, wkv_ref, wq_ref, kn_ref, qn_ref, eavg_ref,
                *rest, tm, keep):
    residue = keep is not None
    x = x_ref[...]
    xn = x * lax.rsqrt(jnp.mean(x * x, axis=-1, keepdims=True) + EPS)
    modkv = modkv_ref[0]
    mod = mod_ref[0]
    hk = (xn * nkv_ref[...] * (1.0 + modkv[:, D_MODEL:]) + modkv[:, :D_MODEL]).astype(BF16)
    hq = (xn * nm_ref[...] * (1.0 + mod[:, D_MODEL:2 * D_MODEL]) + mod[:, :D_MODEL]).astype(BF16)
    eavg = eavg_ref[...]

    def head_norm(a, g):
        sq = (a * a).astype(BF16)
        half = ATTN_WIDTH // 2
        ms = jnp.concatenate([jnp.dot(sq[:, :half], eavg, preferred_element_type=F32),
                              jnp.dot(sq[:, half:], eavg, preferred_element_type=F32)], axis=1)
        return a * lax.rsqrt(ms + EPS) * g

    def residue_major(val, ref, dil, stage):
        if dil == 1:
            ref[0, 0] = val.astype(BF16)
            return
        for c in range(stage.shape[0]):
            stage[c] = val[:, c * LANES:(c + 1) * LANES]
        for r in range(dil):
            rows = [stage[c, pl.ds(r, tm // dil, stride=dil), :] for c in range(stage.shape[0])]
            ref[0, r] = jnp.concatenate(rows, axis=1).astype(BF16)

    for g, (_, dil) in enumerate(BRANCHES):
        k0 = 2 * g * ATTN_WIDTH
        kvg = jnp.dot(hk, wkv_ref[:, k0:k0 + 2 * ATTN_WIDTH], preferred_element_type=F32)
        kn = head_norm(kvg[:, :ATTN_WIDTH], kn_ref[g:g + 1, :])
        vv = kvg[:, ATTN_WIDTH:]
        q0 = g * ATTN_WIDTH
        qg = jnp.dot(hq, wq_ref[:, q0:q0 + ATTN_WIDTH], preferred_element_type=F32)
        qg = head_norm(qg, qn_ref[g:g + 1, :]) * (HEAD_DIM ** -0.5)
        if residue:
            stages = rest[4 * N_BRANCH:]
            residue_major(qg, rest[3 * g], dil, stages[3 * g])
            residue_major(kn, rest[3 * g + 1], dil, stages[3 * g + 1])
            residue_major(vv, rest[3 * g + 2], dil, stages[3 * g + 2])
            kt_ref = rest[3 * N_BRANCH + g]
            kt_ref[0, :ATTN_WIDTH, :] = kn.T
            kt_ref[0, ATTN_WIDTH:, :] = vv.T
        else:
            rest[0][:, k0:k0 + ATTN_WIDTH] = kn
            rest[0][:, k0 + ATTN_WIDTH:k0 + 2 * ATTN_WIDTH] = vv
            rest[1][:, q0:q0 + ATTN_WIDTH] = qg


def _kvq(x, modkv, mod, tiles_per_mod, tm, nkv, nm, wkv, wq, kn, qn, eavg, batch=None):
    T = x.shape[0]
    tok = lambda w: pl.BlockSpec((tm, w), lambda i: (i, 0))
    kvw = 2 * N_BRANCH * ATTN_WIDTH
    if batch is not None:
        L = T // batch
        tpb = L // tm
        kept = tuple(-(-min(w, L) // tm) for w, _ in BRANCHES)
        keep = (tpb, kept)
        out_specs, out_shape = [], []
        for _, dil in BRANCHES:
            for _ in range(3):
                out_specs.append(pl.BlockSpec((1, dil, tm // dil, ATTN_WIDTH), lambda i: (i // tpb, 0, i % tpb, 0)))
                out_shape.append(jax.ShapeDtypeStruct((batch, dil, L // dil, ATTN_WIDTH), BF16))
        for kg in kept:
            out_specs.append(pl.BlockSpec((1, 2 * ATTN_WIDTH, tm),
                                          lambda i, kg=kg: (i // tpb, 0, jnp.maximum(i % tpb - (tpb - kg), 0))))
            out_shape.append(jax.ShapeDtypeStruct((batch, 2 * ATTN_WIDTH, kg * tm), F32))
        scratch = [pltpu.VMEM((ATTN_WIDTH // LANES, tm if dil > 1 else 8, LANES), F32)
                   for _, dil in BRANCHES for _ in range(3)]
    else:
        keep = None
        out_specs = [tok(kvw), tok(N_BRANCH * ATTN_WIDTH)]
        out_shape = [jax.ShapeDtypeStruct((T, kvw), F32), jax.ShapeDtypeStruct((T, N_BRANCH * ATTN_WIDTH), F32)]
        scratch = []
    return pl.pallas_call(
        functools.partial(_kvq_kernel, tm=tm, keep=keep),
        grid=(T // tm,),
        in_specs=[tok(D_MODEL), _mod_spec(modkv, tiles_per_mod), _mod_spec(mod, tiles_per_mod),
                  _resident(nkv.shape), _resident(nm.shape), _resident(wkv.shape), _resident(wq.shape),
                  _resident(kn.shape), _resident(qn.shape), _resident(eavg.shape)],
        out_specs=out_specs,
        out_shape=out_shape,
        scratch_shapes=scratch,
        compiler_params=_cparams(("arbitrary",)),
        name="kvq",
    )(x, modkv, mod, nkv, nm, wkv, wq, kn, qn, eavg)


BAND_QBLOCKS = 8


def _band_attn_kernel(q_ref, kp_ref, kc_ref, vp_ref, vc_ref, bias_ref, o_ref, lse_ref):
    n = pl.program_id(2)
    hw = ATTN_WIDTH // 2
    hh = HEADS // 2
    col = lax.broadcasted_iota(jnp.int32, (hh * N_STEPS, 2 * N_STEPS), 1)
    first = jnp.logical_or(col >= N_STEPS, n > 0)
    lane = lax.broadcasted_iota(jnp.int32, (N_STEPS, LANES), 1)
    head_of = lax.broadcasted_iota(jnp.int32, (N_STEPS, hw), 1) // HEAD_DIM
    zero = jnp.zeros((N_STEPS, hw), BF16)
    nt = (((1,), (1,)), ((), ()))
    for r, j in [(r, j) for r in range(q_ref.shape[0]) for j in range(q_ref.shape[1] // N_STEPS)]:
        if j == 0:
            k_all = jnp.concatenate([kp_ref[r], kc_ref[r]], axis=0)
            v_all = jnp.concatenate([vp_ref[r], vc_ref[r]], axis=0)
        rows = slice(j * N_STEPS, (j + 1) * N_STEPS)
        q = q_ref[r, rows, :]
        k2 = k_all[j * N_STEPS:(j + 2) * N_STEPS]
        v2 = v_all[j * N_STEPS:(j + 2) * N_STEPS]
        lse_all = jnp.zeros((N_STEPS, LANES), F32)
        halves = []
        for c in range(2):
            cols = slice(c * hw, (c + 1) * hw)
            qc = q[:, cols]
            qbd = jnp.concatenate([jnp.where(head_of == h, qc, zero) for h in range(hh)], axis=0)
            s = (lax.dot_general(qbd, k2[:, cols], nt, preferred_element_type=F32)
                 + bias_ref[c * hh * N_STEPS:(c + 1) * hh * N_STEPS, :])
            if j == 0:
                s = jnp.where(first, s, NEG)
            m = jnp.max(s, axis=1, keepdims=True)
            p = jnp.exp(s - m)
            l = jnp.sum(p, axis=1, keepdims=True)
            pv = jnp.dot(p.astype(BF16), v2[:, cols], preferred_element_type=F32) / l
            lse = m + jnp.log(l)
            o = jnp.zeros((N_STEPS, hw), F32)
            for h in range(hh):
                blk = slice(h * N_STEPS, (h + 1) * N_STEPS)
                o = jnp.where(head_of == h, pv[blk], o)
                lse_all = jnp.where(lane == c * hh + h, lse[blk], lse_all)
            halves.append(o)
        o_ref[r, rows, :] = jnp.concatenate(halves, axis=1)
        lse_ref[r, rows, :] = lse_all


def _band_attn(q, k, v, bias, dil):
    B, _, Ld, _ = q.shape
    nq = min(BAND_QBLOCKS, Ld // N_STEPS)
    nr = min(BAND_QBLOCKS // nq, dil)
    step = nq * N_STEPS
    blk = lambda w, f: pl.BlockSpec((None, nr, step, w), f)
    cur = lambda b, r, n: (b, r, n, 0)
    prev_blk = pl.BlockSpec((None, nr, N_STEPS, ATTN_WIDTH), lambda b, r, n: (b, r, jnp.maximum(nq * n - 1, 0), 0))
    return pl.pallas_call(
        _band_attn_kernel,
        grid=(B, dil // nr, Ld // step),
        in_specs=[blk(ATTN_WIDTH, cur), prev_blk, blk(ATTN_WIDTH, cur), prev_blk,
                  blk(ATTN_WIDTH, cur), _resident(bias.shape)],
        out_specs=[blk(ATTN_WIDTH, cur), blk(LANES, cur)],
        out_shape=[jax.ShapeDtypeStruct((B, dil, Ld, ATTN_WIDTH), F32),
                   jax.ShapeDtypeStruct((B, dil, Ld, LANES), F32)],
        compiler_params=_cparams(("arbitrary", "arbitrary", "arbitrary")),
        name=f"band_attn_d{dil}",
    )(q, k, k, v, v, bias)


def _rel_buckets(dilation):
    n = np.arange(N_STEPS + 1) * dilation
    large = MAX_EXACT + (np.log(np.maximum(n, 1) / MAX_EXACT) / np.log(REL_MAX_DIST / MAX_EXACT)
                         * (NUM_BUCKETS - MAX_EXACT)).astype(np.int32)
    return np.where(n < MAX_EXACT, n, np.minimum(large, NUM_BUCKETS - 1)).astype(np.int32)


def _step_bias_row(rel_bias, g, dil):
    onehot = np.zeros((N_STEPS + 1, NUM_BUCKETS), np.float32)
    onehot[np.arange(N_STEPS + 1), _rel_buckets(dil)] = 1.0
    return jnp.dot(jnp.asarray(onehot), rel_bias[:, g, :].astype(F32), precision=lax.Precision.HIGHEST).T


def _band_bias(rel_bias, g, dil):
    bias = _step_bias_row(rel_bias, g, dil)
    P = 3 * N_STEPS
    neg = jnp.full((HEADS, N_STEPS), NEG, F32)
    ext = jnp.concatenate([neg, bias[:, ::-1], neg], axis=1)
    flat = jnp.broadcast_to(ext[:, None, :], (HEADS, N_STEPS, P + 1)).reshape(HEADS, N_STEPS * (P + 1))
    skew = flat[:, :N_STEPS * P].reshape(HEADS, N_STEPS, P)
    return skew[:, :, N_STEPS:].reshape(HEADS * N_STEPS, 2 * N_STEPS)


def _step_attn_kernel(q_ref, kvn_ref, c0_ref, c1_ref, c2_ref, b0_ref, b1_ref, b2_ref, bn_ref, o_ref, *, n_new):
    caches = (c0_ref, c1_ref, c2_ref)
    cbias = (b0_ref, b1_ref, b2_ref)
    nt = (((1,), (1,)), ((), ()))
    rows = n_new * HEADS
    row = lax.broadcasted_iota(jnp.int32, (rows, ATTN_WIDTH), 0)
    lane = lax.broadcasted_iota(jnp.int32, (rows, ATTN_WIDTH), 1)
    own_head = lane // HEAD_DIM == row % HEADS
    parts = []
    for g in range(N_BRANCH):
        klo = 2 * g * ATTN_WIDTH
        qg = q_ref[0, :, g * ATTN_WIDTH:(g + 1) * ATTN_WIDTH]
        qrep = jnp.concatenate([jnp.broadcast_to(qg[s:s + 1, :], (HEADS, ATTN_WIDTH)) for s in range(n_new)], axis=0)
        qbd = jnp.where(own_head, qrep, 0.0).astype(BF16)
        kn = kvn_ref[0, :, klo:klo + ATTN_WIDTH].astype(BF16)
        vn = kvn_ref[0, :, klo + ATTN_WIDTH:klo + 2 * ATTN_WIDTH].astype(BF16)
        kt = caches[g][0, 0].astype(BF16)
        vt = caches[g][0, 1].astype(BF16)
        sc = jnp.dot(qbd, kt, preferred_element_type=F32) + cbias[g][...]
        sn = lax.dot_general(qbd, kn, nt, preferred_element_type=F32) + bn_ref[g]
        m = jnp.maximum(jnp.max(sc, axis=1, keepdims=True), jnp.max(sn, axis=1, keepdims=True))
        pc = jnp.exp(sc - m)
        pn = jnp.exp(sn - m)
        l = jnp.sum(pc, axis=1, keepdims=True) + jnp.sum(pn, axis=1, keepdims=True)
        pv = (lax.dot_general(pc.astype(BF16), vt, nt, preferred_element_type=F32)
              + jnp.dot(pn.astype(BF16), vn, preferred_element_type=F32))
        parts.append((m + jnp.log(l), pv / l))
    mx = jnp.maximum(jnp.maximum(parts[0][0], parts[1][0]), parts[2][0])
    es = [jnp.exp(lse - mx) for lse, _ in parts]
    den = es[0] + es[1] + es[2]
    mixed = (es[0] * parts[0][1] + es[1] * parts[1][1] + es[2] * parts[2][1]) / den
    mixed = jnp.where(own_head, mixed, 0.0)
    for s in range(n_new):
        o_ref[0, s:s + 1, :] = jnp.sum(mixed[s * HEADS:(s + 1) * HEADS], axis=0, keepdims=True)


def _step_bias(rel_bias, n_new):
    cache_tabs, new_tabs = [], []
    for g, (win, dil) in enumerate(BRANCHES):
        bias = _step_bias_row(rel_bias, g, dil)
        rev = bias[:, ::-1]
        per_s = []
        for s in range(n_new):
            if dil == 1:
                neg = jnp.full((HEADS, s), NEG, F32)
                per_s.append(jnp.concatenate([neg, rev[:, :win - s]], axis=1))
            else:
                cols = [rev[:, :N_STEPS] if r == s % dil else jnp.full((HEADS, N_STEPS), NEG, F32)
                        for r in range(dil)]
                per_s.append(jnp.stack(cols, axis=2).reshape(HEADS, win))
        cache_tabs.append(jnp.stack(per_s, axis=0).reshape(n_new * HEADS, win))
        dist = np.arange(n_new)[:, None] - np.arange(8)[None, :]
        ok = (dist >= 0) & (dist % dil == 0) & (np.arange(8)[None, :] < n_new)
        onehot = np.zeros((N_STEPS + 1, n_new * 8), np.float32)
        onehot[np.where(ok, dist // dil, 0).reshape(-1), np.arange(n_new * 8)] = 1.0
        tab = jnp.dot(bias, jnp.asarray(onehot), precision=lax.Precision.HIGHEST).reshape(HEADS, n_new, 8)
        new_tabs.append(jnp.where(ok[None], tab, NEG).transpose(1, 0, 2).reshape(n_new * HEADS, 8))
    return cache_tabs, jnp.stack(new_tabs)


def _step_attn(q, kvn, caches, cache_bias, new_bias, DB, S):
    cspec = lambda c: pl.BlockSpec((1,) + c.shape[1:], lambda b: (b, 0, 0, 0))
    return pl.pallas_call(
        functools.partial(_step_attn_kernel, n_new=S),
        grid=(DB,),
        in_specs=[pl.BlockSpec((1, S, N_BRANCH * ATTN_WIDTH), lambda b: (b, 0, 0)),
                  pl.BlockSpec((1, 8, 2 * N_BRANCH * ATTN_WIDTH), lambda b: (b, 0, 0)),
                  cspec(caches[0]), cspec(caches[1]), cspec(caches[2]),
                  _resident(cache_bias[0].shape), _resident(cache_bias[1].shape), _resident(cache_bias[2].shape),
                  _resident(new_bias.shape)],
        out_specs=pl.BlockSpec((1, S, ATTN_WIDTH), lambda b: (b, 0, 0)),
        out_shape=jax.ShapeDtypeStruct((DB, S, ATTN_WIDTH), F32),
        compiler_params=_cparams(("arbitrary",)),
        name="step_attn",
    )(q, kvn, caches[0], caches[1], caches[2], *cache_bias, new_bias)


def _attn_out_kernel(*refs, dils, tm):
    n_o = max(len(dils), 1)
    n_in = n_o + len(dils) + 9
    (x_ref, mod_ref, wo_ref, ex_ref, nf_ref, wr_ref, br_ref, cin_ref, tri_ref,
     x3_ref, h3_ref, rt_ref, rtt_ref, cnt_ref, carry) = refs[n_in - 9:n_in + 6]
    stages = refs[n_in + 6:]
    _start_counts(cin_ref, carry)

    def natural(ref, stage, dil):
        if dil == 1:
            return ref[0, 0]
        for r in range(dil):
            val = ref[0, r]
            for c in range(stage.shape[0]):
                stage[c, pl.ds(r, tm // dil, stride=dil), :] = val[:, c * LANES:(c + 1) * LANES]
        return jnp.concatenate([stage[c] for c in range(stage.shape[0])], axis=1)

    if dils:
        lses = [natural(refs[n_o + g], stages[2 * g + 1], dil) for g, dil in enumerate(dils)]
        mx = functools.reduce(jnp.maximum, lses)
        es = [jnp.exp(l - mx) for l in lses]
        den = functools.reduce(lambda a, b: a + b, es)
        o = None
        for g, dil in enumerate(dils):
            term = _split_dot(es[g] / den, ex_ref[...]) * natural(refs[g], stages[2 * g], dil)
            o = term if o is None else o + term
    else:
        o = refs[0][...]
    mod = mod_ref[0]
    a = jnp.dot(o.astype(BF16), wo_ref[...], preferred_element_type=F32)
    x3 = x_ref[...] + mod[:, 2 * D_MODEL:3 * D_MODEL] * a
    x3_ref[...] = x3
    _ffn_pre(x3, mod, nf_ref, wr_ref, br_ref, tri_ref, h3_ref, rt_ref, rtt_ref, cnt_ref, carry, tm)


def _attn_out(os_, lses, x, mod, tiles_per_mod, tm, wo, ex, nf, wr, br, counts):
    T = x.shape[0]
    tok = lambda w: pl.BlockSpec((tm, w), lambda i: (i, 0))
    r_specs, r_shapes = _route_outs(T, tm)
    tri = _earlier_rows(tm)
    if lses:
        dils = tuple(o.shape[1] for o in os_)
        tpb = tiles_per_mod
        res = lambda a: pl.BlockSpec((1, a.shape[1], tm // a.shape[1], a.shape[3]), lambda i: (i // tpb, 0, i % tpb, 0))
        o_specs = [res(a) for a in os_] + [res(a) for a in lses]
        scratch = []
        for _ in dils:
            scratch += [pltpu.VMEM((ATTN_WIDTH // LANES, tm, LANES), F32), pltpu.VMEM((1, tm, LANES), F32)]
    else:
        dils, o_specs, scratch = (), [tok(ATTN_WIDTH)], []
    return pl.pallas_call(
        functools.partial(_attn_out_kernel, dils=dils, tm=tm),
        grid=(T // tm,),
        in_specs=(o_specs + [tok(D_MODEL), _mod_spec(mod, tiles_per_mod), _resident(wo.shape), _resident(ex.shape),
                             _resident(nf.shape), _resident(wr.shape), _resident(br.shape),
                             _resident(counts.shape), _resident(tri.shape)]),
        out_specs=[tok(D_MODEL), pl.BlockSpec((tm * ROW_TILES, LANES), lambda i: (i, 0))] + r_specs,
        out_shape=[jax.ShapeDtypeStruct((T, D_MODEL), F32),
                   jax.ShapeDtypeStruct((T * ROW_TILES, LANES), F32)] + r_shapes,
        scratch_shapes=[pltpu.VMEM((1, LANES), F32)] + scratch,
        compiler_params=_cparams(("arbitrary",)),
        name="attn_out",
    )(*os_, *lses, x, mod, wo, ex, nf, wr, br, counts, tri)


def kernel(x_prompt, x_sample, cache_kv_w128, cache_kv_w512, cache_kv_w2048, c_prompt, c_sample, ada_w, ada_b, norm_mix, norm_ffn, a_w_in, a_b_in, a_norm_v, a_w_s, a_b_s, a_w_out, kv_ada_w, kv_ada_b, kv_norm, w_kv, k_norm, rel_bias, b_w_q, q_norm, b_w_o, r_w_group, r_b_group, r_w_expert, r_b_expert, e_w1, e_w3, e_w2):
    B, L, _ = x_prompt.shape
    DB, S, _ = x_sample.shape
    Tp, Ts = B * L, DB * S
    tm = TOKEN_TILE
    tpm_p = L // tm

    c_all = jnp.concatenate([c_prompt, c_sample], axis=0)
    R = c_all.shape[0]
    c_all = jnp.pad(c_all, ((0, -R % 8), (0, 0)))
    mods = [_ada(c_all, ada_w, ada_b, l) for l in range(2)]
    modkv = _ada(c_all, kv_ada_w[None], kv_ada_b[None], 0)

    def split_mod(m):
        return m[:B, None, :], jnp.repeat(m[B:B + DB], S, axis=0)[None]
    mod_p, mod_s = zip(*[split_mod(m) for m in mods])
    modkv_p, modkv_s = split_mod(modkv)

    row = lambda a: a.reshape(1, -1)

    def router(l):
        wr = jnp.zeros((D_MODEL, LANES), F32)
        wr = wr.at[:, :N_GROUPS].set(r_w_group[l]).at[:, N_GROUPS:N_GROUPS + N_EXPERTS].set(r_w_expert[l])
        br = jnp.zeros((1, LANES), F32)
        br = br.at[0, :N_GROUPS].set(r_b_group[l]).at[0, N_GROUPS:N_GROUPS + N_EXPERTS].set(r_b_expert[l])
        return wr.astype(BF16), br

    no_counts = jnp.zeros((1, LANES), F32)

    win = a_w_in[0].astype(BF16)
    wout = a_w_out[0].astype(BF16)
    tril = jnp.tril(jnp.ones((CHUNK, CHUNK), bool))
    ws_p = jnp.where(tril, a_w_s[0], 0).astype(BF16)
    bs_p = a_b_s[0].T
    cs = min(CHUNK, S)
    ws_small = jnp.where(jnp.tril(jnp.ones((cs, cs), bool)), a_w_s[0][:, :cs, :cs], 0)
    ws_s = jnp.stack([jnp.kron(jnp.eye(Ts // cs, dtype=F32), ws_small[g]) for g in range(GMLP_GROUPS)]).astype(BF16)
    bs_s = jnp.tile(a_b_s[0][:, :cs], (1, Ts // cs)).T
    wr0, br0 = router(0)
    common = (row(norm_mix[0]), win, row(a_b_in[0]), row(a_norm_v[0]))
    xp = x_prompt.reshape(Tp, D_MODEL)
    xs_ = x_sample.reshape(Ts, D_MODEL)
    x1_p, h2_p, rt_p, rtt_p, cnt = _mixer_a(xp, mod_p[0], tpm_p, tm, *common, ws_p, bs_p, wout, row(norm_ffn[0]),
                                            wr0, br0, no_counts, with_v=False)
    x1_s, h2_s, rt_s, rtt_s, cnt, v_s = _mixer_a(xs_, mod_s[0], 1, Ts, *common, ws_s, bs_s, wout, row(norm_ffn[0]),
                                                 wr0, br0, cnt, with_v=True)
    x2_p, x2_s = _moe((h2_p, rt_p, rtt_p, x1_p, mod_p[0], tpm_p, tm), (h2_s, rt_s, rtt_s, x1_s, mod_s[0], 1, Ts), cnt,
                      e_w1, e_w3, e_w2, 0)

    wkv = w_kv.astype(BF16)
    wq = b_w_q[0].astype(BF16)
    kn = jnp.tile(k_norm, (1, HEADS))
    qn = jnp.tile(q_norm[0], (1, HEADS))
    head = np.arange(ATTN_WIDTH) // HEAD_DIM
    half_head = head[:ATTN_WIDTH // 2]
    eavg = jnp.asarray((half_head[:, None] == half_head[None, :]) / HEAD_DIM, BF16)
    kvq_w = (row(kv_norm), row(norm_mix[1]), wkv, wq, kn, qn, eavg)
    *qkv_p, kt0, kt1, kt2 = _kvq(x2_p, modkv_p, mod_p[1], tpm_p, tm, *kvq_w, batch=B)
    kv_s, q_s = _kvq(x2_s, modkv_s, mod_s[1], 1, Ts, *kvq_w)

    os_, lses = [], []
    for g, (_, dil) in enumerate(BRANCHES):
        o, lse = _band_attn(*qkv_p[3 * g:3 * g + 3], _band_bias(rel_bias, g, dil), dil)
        os_.append(o)
        lses.append(lse)
    caches = [jnp.transpose(c, (0, 2, 3, 4, 1)).reshape(DB, 2, ATTN_WIDTH, c.shape[1])
              for c in (cache_kv_w128, cache_kv_w512, cache_kv_w2048)]
    kvn_s = jnp.pad(kv_s.reshape(DB, S, -1), ((0, 0), (0, 8 - S), (0, 0)))
    o_s = _step_attn(q_s.reshape(DB, S, -1), kvn_s, caches, *_step_bias(rel_bias, S), DB, S)

    wo = b_w_o[0].astype(BF16)
    ex = jnp.asarray(np.arange(LANES)[:, None] == head[None, :], BF16)
    wr1, br1 = router(1)
    x3_p, h3_p, rt1_p, rtt1_p, cnt1 = _attn_out(os_, lses, x2_p, mod_p[1], tpm_p, tm, wo, ex, row(norm_ffn[1]),
                                                wr1, br1, no_counts)
    x3_s, h3_s, rt1_s, rtt1_s, cnt1 = _attn_out([o_s.reshape(Ts, ATTN_WIDTH)], [], x2_s, mod_s[1], 1, Ts, wo, ex,
                                                row(norm_ffn[1]), wr1, br1, cnt1)
    y_p, y_s = _moe((h3_p, rt1_p, rtt1_p, x3_p, mod_p[1], tpm_p, tm), (h3_s, rt1_s, rtt1_s, x3_s, mod_s[1], 1, Ts),
                    cnt1, e_w1, e_w3, e_w2, 1)

    kv_s4 = kv_s.reshape(DB, S, N_BRANCH, 2, HEADS, HEAD_DIM)

    def window(kt, w):
        n = min(w, L)
        return jnp.transpose(kt[:, :, kt.shape[2] - n:].reshape(B, 2, HEADS, HEAD_DIM, n), (0, 4, 1, 2, 3))
    return (y_p.reshape(B, L, D_MODEL), y_s.reshape(DB, S, D_MODEL),
            window(kt0, BRANCHES[0][0]), window(kt1, BRANCHES[1][0]), window(kt2, BRANCHES[2][0]),
            kv_s4[:, :, 0], kv_s4[:, :, 1], kv_s4[:, :, 2],
            v_s.reshape(1, DB, S, GMLP_WIDTH))
```

```python
import functools

import numpy as np
import jax
import jax.numpy as jnp
from jax import lax
from jax.experimental import pallas as pl
from jax.experimental.pallas import tpu as pltpu

F32 = jnp.float32
BF16 = jnp.bfloat16

D_MODEL = 1024
GMLP_WIDTH = 2048
GMLP_GROUPS = 4
GROUP_WIDTH = GMLP_WIDTH // GMLP_GROUPS
CHUNK = 128
BRANCHES = ((128, 1), (512, 4), (2048, 16))
N_BRANCH = 3
N_STEPS = 128
HEADS = 8
HEAD_DIM = 64
ATTN_WIDTH = HEADS * HEAD_DIM
NUM_BUCKETS = 32
MAX_EXACT = NUM_BUCKETS // 2
REL_MAX_DIST = 2048
N_GROUPS = 4
EXPERTS_PER_GROUP = 8
N_EXPERTS = N_GROUPS * EXPERTS_PER_GROUP
TOP_K = 2
D_EXPERT = 512
EPS = 1e-6
NEG = -1e30

LANES = 128
ROW_TILES = D_MODEL // LANES
TOKEN_TILE = 512
MOE_BLOCK = 256
DMA_UNROLL = 16
VMEM_LIMIT = 52 * 1024 * 1024


def _cparams(sem):
    return pltpu.CompilerParams(dimension_semantics=sem, vmem_limit_bytes=VMEM_LIMIT)


def _resident(shape):
    nd = len(shape)
    return pl.BlockSpec(shape, lambda *_, _nd=nd: (0,) * _nd, pipeline_mode=pl.Buffered(1))


def _gelu_tanh(x):
    return 0.5 * x * (1.0 + jnp.tanh(0.7978845608028654 * (x + 0.044715 * (x * x * x))))


def _rms(x, g):
    return x * lax.rsqrt(jnp.mean(x * x, axis=-1, keepdims=True) + EPS) * g


def _store_rows8(ref, val, n, base=0):
    for s in range(ROW_TILES):
        ref[pl.ds(base + s, n, stride=ROW_TILES), :] = val[:, s * LANES:(s + 1) * LANES]


def _load_rows8(ref, n, base=0):
    return jnp.concatenate([ref[pl.ds(base + s, n, stride=ROW_TILES), :] for s in range(ROW_TILES)], axis=1)


def _split(a):
    hi = a.astype(BF16)
    return hi, (a - hi.astype(F32)).astype(BF16)


def _split_dot(a, e_bf16):
    hi, lo = _split(a)
    return (jnp.dot(hi, e_bf16, preferred_element_type=F32) + jnp.dot(lo, e_bf16, preferred_element_type=F32))


ROUTE_LANE0 = 4


def _route_rows(l, tri_ref, carry):
    lane = lax.broadcasted_iota(jnp.int32, l.shape, 1).astype(F32)
    far = float(LANES)

    def first_lane(mask):
        return jnp.min(jnp.where(mask, lane, far), axis=1, keepdims=True)

    is_g = lane < N_GROUPS
    gl = jnp.where(is_g, l, NEG)
    gmax = jnp.max(gl, axis=1, keepdims=True)
    g_i = first_lane(jnp.logical_and(gl == gmax, is_g))
    g_p = 1.0 / jnp.sum(jnp.where(is_g, jnp.exp(gl - gmax), 0.0), axis=1, keepdims=True)
    lo = ROUTE_LANE0 + EXPERTS_PER_GROUP * g_i
    sel = jnp.logical_and(lane >= lo, lane < lo + EXPERTS_PER_GROUP)
    el = jnp.where(sel, l, NEG)
    m1 = jnp.max(el, axis=1, keepdims=True)
    i1 = first_lane(jnp.logical_and(el == m1, sel))
    sel2 = jnp.logical_and(sel, lane != i1)
    el2 = jnp.where(sel2, l, NEG)
    m2 = jnp.max(el2, axis=1, keepdims=True)
    i2 = first_lane(jnp.logical_and(el2 == m2, sel2))
    r = jnp.exp(m2 - m1)
    w1 = g_p / (1.0 + r)
    w2 = g_p * r / (1.0 + r)

    hit1 = lane == i1
    hit2 = lane == i2
    onehot = jnp.where(jnp.logical_or(hit1, hit2), 1.0, 0.0)
    before = carry[...] + jnp.dot(tri_ref[...], onehot.astype(BF16), preferred_element_type=F32)
    rank1 = jnp.sum(jnp.where(hit1, before, 0.0), axis=1, keepdims=True)
    rank2 = jnp.sum(jnp.where(hit2, before, 0.0), axis=1, keepdims=True)
    carry[...] = carry[...] + jnp.sum(onehot, axis=0, keepdims=True)

    out = jnp.zeros(l.shape, F32)
    for k, val in enumerate((i1 - ROUTE_LANE0, i2 - ROUTE_LANE0, w1, w2, rank1, rank2)):
        out = jnp.where(lane == k, val, out)
    return out


ROUTE_FIELDS = 8


def _start_counts(cin_ref, carry):
    @pl.when(pl.program_id(0) == 0)
    def _():
        carry[...] = cin_ref[...]


def _ffn_pre(x, mod, nf_ref, wr_ref, br_ref, tri_ref, h_ref, rt_ref, rtt_ref, cnt_ref, carry, n):
    h = _rms(x, nf_ref[...]) * (1.0 + mod[:, 4 * D_MODEL:5 * D_MODEL]) + mod[:, 3 * D_MODEL:4 * D_MODEL]
    _store_rows8(h_ref, h, n)
    logits = jnp.dot(h.astype(BF16), wr_ref[...], preferred_element_type=F32) + br_ref[...]
    route = _route_rows(logits, tri_ref, carry)
    rt_ref[...] = route
    rtt_ref[...] = route.T[:ROUTE_FIELDS, :]
    cnt_ref[...] = carry[...]


def _ada_kernel(c_ref, w_ref, b_ref, o_ref):
    c = c_ref[...]
    a = (c * jax.nn.sigmoid(c)).astype(BF16)
    o_ref[...] = jnp.dot(a, w_ref[...].astype(BF16), preferred_element_type=F32) + b_ref[...]


def _ada(c, w, b, layer):
    R = c.shape[0]
    N = w.shape[2]
    tn = 1024
    return pl.pallas_call(
        _ada_kernel,
        grid=(N // tn,),
        in_specs=[pl.BlockSpec((R, D_MODEL), lambda j: (0, 0)),
                  pl.BlockSpec((None, D_MODEL, tn), lambda j: (layer, 0, j)),
                  pl.BlockSpec((None, 1, tn), lambda j: (layer, 0, j))],
        out_specs=pl.BlockSpec((R, tn), lambda j: (0, j)),
        out_shape=jax.ShapeDtypeStruct((R, N), F32),
        compiler_params=_cparams(("arbitrary",)),
        name="ada",
    )(c, w, b.reshape(b.shape[0], 1, N))


def _mixer_a_kernel(x_ref, mod_ref, nm_ref, win_ref, bin_ref, gv_ref, ws_ref, bs_ref, wout_ref,
                    nf_ref, wr_ref, br_ref, cin_ref, tri_ref, x1_ref, h2_ref, rt_ref, rtt_ref, cnt_ref, *rest, tm):
    v_refs, carry = rest[:-1], rest[-1]
    _start_counts(cin_ref, carry)
    x = x_ref[...]
    mod = mod_ref[0]
    h = (_rms(x, nm_ref[...]) * (1.0 + mod[:, D_MODEL:2 * D_MODEL]) + mod[:, 0:D_MODEL]).astype(BF16)
    zv = jnp.dot(h, win_ref[:, GMLP_WIDTH:], preferred_element_type=F32) + bin_ref[:, GMLP_WIDTH:]
    v = _rms(_gelu_tanh(zv), gv_ref[...])
    if v_refs:
        v_refs[0][...] = v
    vb = v.astype(BF16)
    bs = bs_ref[...]
    acc = jnp.zeros((tm, D_MODEL), F32)
    for g in range(GMLP_GROUPS):
        lo, hi = g * GROUP_WIDTH, (g + 1) * GROUP_WIDTH
        u = _gelu_tanh(jnp.dot(h, win_ref[:, lo:hi], preferred_element_type=F32) + bin_ref[:, lo:hi])
        wg = ws_ref[g]
        gate = jnp.concatenate(
            [jnp.dot(wg, vb[c * CHUNK:(c + 1) * CHUNK, lo:hi], preferred_element_type=F32) + bs[:, g:g + 1]
             for c in range(tm // CHUNK)], axis=0)
        acc = acc + jnp.dot((u * gate).astype(BF16), wout_ref[lo:hi, :], preferred_element_type=F32)
    x1 = x + mod[:, 2 * D_MODEL:3 * D_MODEL] * acc
    x1_ref[...] = x1
    _ffn_pre(x1, mod, nf_ref, wr_ref, br_ref, tri_ref, h2_ref, rt_ref, rtt_ref, cnt_ref, carry, tm)


def _mod_spec(mod, tiles_per_mod):
    _, rows, width = mod.shape
    return pl.BlockSpec((1, rows, width), lambda i: (i // tiles_per_mod, 0, 0))


def _earlier_rows(tm):
    return jnp.asarray(np.tril(np.ones((tm, tm), np.float32), -1), BF16)


def _route_outs(T, tm):
    return ([pl.BlockSpec((tm, LANES), lambda i: (i, 0)), pl.BlockSpec((ROUTE_FIELDS, tm), lambda i: (0, i)),
             pl.BlockSpec((1, LANES), lambda i: (0, 0))],
            [jax.ShapeDtypeStruct((T, LANES), F32), jax.ShapeDtypeStruct((ROUTE_FIELDS, T), F32),
             jax.ShapeDtypeStruct((1, LANES), F32)])


def _mixer_a(x, mod, tiles_per_mod, tm, nm, win, bin_, gv, ws, bs_t, wout, nf, wr, br, counts, with_v):
    T = x.shape[0]
    tok = lambda w: pl.BlockSpec((tm, w), lambda i: (i, 0))
    r_specs, r_shapes = _route_outs(T, tm)
    out_shape = [jax.ShapeDtypeStruct((T, D_MODEL), F32), jax.ShapeDtypeStruct((T * ROW_TILES, LANES), F32)] + r_shapes
    out_specs = [tok(D_MODEL), pl.BlockSpec((tm * ROW_TILES, LANES), lambda i: (i, 0))] + r_specs
    if with_v:
        out_shape.append(jax.ShapeDtypeStruct((T, GMLP_WIDTH), F32))
        out_specs.append(tok(GMLP_WIDTH))
    tri = _earlier_rows(tm)
    return pl.pallas_call(
        functools.partial(_mixer_a_kernel, tm=tm),
        grid=(T // tm,),
        in_specs=[tok(D_MODEL), _mod_spec(mod, tiles_per_mod), _resident(nm.shape), _resident(win.shape),
                  _resident(bin_.shape), _resident(gv.shape), _resident(ws.shape), _resident(bs_t.shape),
                  _resident(wout.shape), _resident(nf.shape), _resident(wr.shape), _resident(br.shape),
                  _resident(counts.shape), _resident(tri.shape)],
        out_specs=out_specs,
        out_shape=out_shape,
        scratch_shapes=[pltpu.VMEM((1, LANES), F32)],
        compiler_params=_cparams(("arbitrary",)),
        name="mixer_a",
    )(x, mod, nm, win, bin_, gv, ws, bs_t, wout, nf, wr, br, counts, tri)


def _layout(rtt, counts, block):
    n_tokens = rtt.shape[1]
    counts = counts[0, ROUTE_LANE0:ROUTE_LANE0 + N_EXPERTS].astype(jnp.int32)
    padded = (counts + block - 1) // block * block
    pad_end = jnp.cumsum(padded)
    pad_start = (pad_end - padded).astype(F32)
    experts, ranks = rtt[0:TOP_K], rtt[4:4 + TOP_K]
    onehot = (experts[None] == jnp.arange(N_EXPERTS, dtype=F32)[:, None, None]).astype(F32)
    base = jnp.einsum('e,ekt->kt', pad_start, onehot, precision=lax.Precision.HIGHEST)
    slots = (ranks + base).astype(jnp.int32)
    nb = -(-n_tokens * TOP_K // block) + N_EXPERTS
    blk_e = jnp.minimum(jnp.sum(pad_end[None, :] <= (jnp.arange(nb, dtype=jnp.int32) * block)[:, None], axis=1),
                        N_EXPERTS - 1).astype(jnp.int32)
    n_used = (pad_end[-1] // block).astype(jnp.int32).reshape(1)
    starts = jnp.concatenate([jnp.ones((1,), jnp.int32), (blk_e[1:] != blk_e[:-1]).astype(jnp.int32)])
    grp = jnp.cumsum(starts) - 1
    end_blk = (pad_end // block).astype(jnp.int32)
    own = blk_e[:, None] == jnp.arange(N_EXPERTS, dtype=jnp.int32)[None, :]
    nxt_blk = jnp.sum(jnp.where(own, end_blk[None, :], 0), axis=1)
    at_nxt = nxt_blk[:, None] == jnp.arange(nb, dtype=jnp.int32)[None, :]
    nxt_e = jnp.where(nxt_blk < n_used[0], jnp.sum(jnp.where(at_nxt, blk_e[None, :], 0), axis=1), -1).astype(jnp.int32)
    zero_plan = (jnp.maximum(pad_end - block, 0).astype(jnp.int32), (padded > 0).astype(jnp.int32), n_used)
    return slots, (blk_e, n_used, nxt_e, grp.astype(jnp.int32)), zero_plan, nb


def _tile_slots(slots, ts):
    T = slots.shape[1]
    return slots.reshape(TOP_K, T // ts, ts).transpose(1, 0, 2).reshape(T // ts, 1, TOP_K * ts)


def _row_copy(src_ref, src_row, dst_ref, dst_row, sem):
    return pltpu.make_async_copy(
        src_ref.at[pl.ds(pl.multiple_of(src_row * ROW_TILES, ROW_TILES), ROW_TILES), :],
        dst_ref.at[pl.ds(pl.multiple_of(dst_row * ROW_TILES, ROW_TILES), ROW_TILES), :], sem)


def _scatter_kernel(zlo_ref, has_ref, nu_ref, slot_a_ref, slot_b_ref, src_a_ref, src_b_ref, dst_ref, zbuf, sem,
                    *, ts, n_a, n_b, nb, block):
    i = pl.program_id(0)

    def clear(row0):
        return pltpu.make_async_copy(
            zbuf, dst_ref.at[pl.ds(pl.multiple_of(row0 * ROW_TILES, ROW_TILES), block * ROW_TILES), :], sem)

    def for_clears(act):
        for e in range(N_EXPERTS):
            @pl.when(has_ref[e] > 0)
            def _(e=e):
                act(clear(zlo_ref[e]))

        def tail(j, c):
            act(clear(j * block))
            return c
        lax.fori_loop(nu_ref[0], nb, tail, 0)

    @pl.when(i == 0)
    def _():
        zbuf[...] = jnp.zeros(zbuf.shape, F32)
        for_clears(lambda c: c.start())
        for_clears(lambda c: c.wait())

    def scatter(slot_ref, src_ref, n):
        def issue(j, c):
            for u in range(DMA_UNROLL):
                t = j * DMA_UNROLL + u
                for k in range(TOP_K):
                    _row_copy(src_ref, t, dst_ref, slot_ref[0, 0, k * n + t], sem).start(priority=k)
            return c
        lax.fori_loop(0, n // DMA_UNROLL, issue, 0)
        for _ in range(TOP_K):
            pltpu.make_async_copy(src_ref, dst_ref.at[pl.ds(0, n * ROW_TILES), :], sem).wait()

    @pl.when(i < n_a)
    def _():
        scatter(slot_a_ref, src_a_ref, ts)

    @pl.when(i == n_a)
    def _():
        scatter(slot_b_ref, src_b_ref, n_b)


def _scatter(slots_a, slots_b, src_a, src_b, zero_plan, nb, block):
    n_a, _, ts2 = slots_a.shape
    ts, n_b = ts2 // TOP_K, slots_b.shape[2] // TOP_K
    tile_a = lambda i, *_: (jnp.minimum(i, n_a - 1), 0, 0)
    return pl.pallas_call(
        functools.partial(_scatter_kernel, ts=ts, n_a=n_a, n_b=n_b, nb=nb, block=block),
        grid_spec=pltpu.PrefetchScalarGridSpec(
            num_scalar_prefetch=3,
            grid=(n_a + 1,),
            in_specs=[pl.BlockSpec((1, 1, ts2), tile_a, memory_space=pltpu.SMEM),
                      pl.BlockSpec((1, 1, TOP_K * n_b), lambda i, *_: (0, 0, 0), memory_space=pltpu.SMEM),
                      pl.BlockSpec((ts * ROW_TILES, LANES), lambda i, *_: (jnp.minimum(i, n_a - 1), 0)),
                      pl.BlockSpec((n_b * ROW_TILES, LANES), lambda i, *_: (0, 0))],
            out_specs=pl.BlockSpec(memory_space=pl.ANY),
            scratch_shapes=[pltpu.VMEM((block * ROW_TILES, LANES), F32), pltpu.SemaphoreType.DMA(())]),
        out_shape=jax.ShapeDtypeStruct((nb * block * ROW_TILES, LANES), F32),
        compiler_params=_cparams(("arbitrary",)),
        name="moe_scatter",
    )(*zero_plan, slots_a, slots_b, src_a, src_b)


X_BUFFERS = 3
Y_BUFFERS = 2


def _experts_kernel(be_ref, nu_ref, nxt_ref, grp_ref, x_hbm, w1_hbm, w3_hbm, w2_hbm, y_hbm,
                    w1b, w3b, w2b, wf1, wf3, wf2, sem, xbuf, xsem, ybuf, ysem, *, block, layer):
    i = pl.program_id(0)
    n = nu_ref[0]
    rows = block * ROW_TILES

    def weight_copies(e, buf):
        return [pltpu.make_async_copy(src.at[layer, e], dst.at[buf], sem.at[buf])
                for src, dst in ((w1_hbm, wf1), (w3_hbm, wf3), (w2_hbm, wf2))]

    def x_copy(blk):
        return pltpu.make_async_copy(x_hbm.at[pl.ds(pl.multiple_of(blk * rows, rows), rows), :],
                                     xbuf.at[blk % X_BUFFERS], xsem.at[blk % X_BUFFERS])

    def y_copy(blk):
        return pltpu.make_async_copy(ybuf.at[blk % Y_BUFFERS],
                                     y_hbm.at[pl.ds(pl.multiple_of(blk * rows, rows), rows), :], ysem.at[blk % Y_BUFFERS])

    @pl.when(i == 0)
    def _():
        for c in weight_copies(be_ref[0], 0):
            c.start()
        for b in range(X_BUFFERS - 1):
            @pl.when(b < n)
            def _(b=b):
                x_copy(b).start()

    @pl.when(i < n)
    def _():
        @pl.when(i + X_BUFFERS - 1 < n)
        def _():
            x_copy(i + X_BUFFERS - 1).start()

        x_copy(i).wait()

        @pl.when(i >= Y_BUFFERS)
        def _():
            y_copy(i - Y_BUFFERS).wait()

        @pl.when(jnp.logical_or(i == 0, be_ref[i] != be_ref[jnp.maximum(i - 1, 0)]))
        def _():
            buf = grp_ref[i] % 2
            for c in weight_copies(be_ref[i], buf):
                c.wait()

            @pl.when(nxt_ref[i] >= 0)
            def _():
                for c in weight_copies(nxt_ref[i], 1 - buf):
                    c.start()

            w1b[...] = wf1[buf].astype(BF16)
            w3b[...] = wf3[buf].astype(BF16)
            w2b[...] = wf2[buf].astype(BF16)

        x = _load_rows8(xbuf.at[i % X_BUFFERS], block).astype(BF16)
        h1 = jnp.dot(x, w1b[...], preferred_element_type=F32)
        h3 = jnp.dot(x, w3b[...], preferred_element_type=F32)
        a = (h1 * jax.nn.sigmoid(h1) * h3).astype(BF16)
        _store_rows8(ybuf.at[i % Y_BUFFERS], jnp.dot(a, w2b[...], preferred_element_type=F32), block)
        y_copy(i).start()

        @pl.when(i == n - 1)
        def _():
            for back in range(Y_BUFFERS):
                @pl.when(i - back >= 0)
                def _(back=back):
                    y_copy(i - back).wait()


def _experts(xs, plan, nb, w1, w3, w2, layer, block):
    hbm = pl.BlockSpec(memory_space=pl.ANY)
    row_block = (block * ROW_TILES, LANES)
    return pl.pallas_call(
        functools.partial(_experts_kernel, block=block, layer=layer),
        grid_spec=pltpu.PrefetchScalarGridSpec(
            num_scalar_prefetch=4,
            grid=(nb,),
            in_specs=[hbm, hbm, hbm, hbm],
            out_specs=hbm,
            scratch_shapes=[pltpu.VMEM((D_MODEL, D_EXPERT), BF16), pltpu.VMEM((D_MODEL, D_EXPERT), BF16),
                            pltpu.VMEM((D_EXPERT, D_MODEL), BF16),
                            pltpu.VMEM((2, D_MODEL, D_EXPERT), F32), pltpu.VMEM((2, D_MODEL, D_EXPERT), F32),
                            pltpu.VMEM((2, D_EXPERT, D_MODEL), F32), pltpu.SemaphoreType.DMA((2,)),
                            pltpu.VMEM((X_BUFFERS,) + row_block, F32), pltpu.SemaphoreType.DMA((X_BUFFERS,)),
                            pltpu.VMEM((Y_BUFFERS,) + row_block, F32), pltpu.SemaphoreType.DMA((Y_BUFFERS,))]),
        out_shape=jax.ShapeDtypeStruct(xs.shape, F32),
        input_output_aliases={4: 0},
        compiler_params=_cparams(("arbitrary",)),
        name="moe_experts",
    )(*plan, xs, w1, w3, w2)


def _combine_kernel(slot_ref, slot_nxt_ref, y_ref, x_ref, mod_ref, rt_ref, o_ref, ybuf, sem, *, tc):
    i = pl.program_id(0)

    def gather(slots, buf):
        def issue(j, c):
            for u in range(DMA_UNROLL):
                t = j * DMA_UNROLL + u
                for k in range(TOP_K):
                    _row_copy(y_ref, slots[0, 0, k * tc + t], ybuf.at[buf], k * tc + t, sem.at[buf]).start(priority=k)
            return c
        lax.fori_loop(0, tc // DMA_UNROLL, issue, 0)

    @pl.when(i == 0)
    def _():
        gather(slot_ref, 0)

    cur = i % 2
    pltpu.make_async_copy(y_ref.at[pl.ds(0, TOP_K * tc * ROW_TILES), :], ybuf.at[cur], sem.at[cur]).wait()

    @pl.when(i + 1 < pl.num_programs(0))
    def _():
        gather(slot_nxt_ref, 1 - cur)

    rt = rt_ref[...]
    rows = ybuf.at[cur]
    y = (rt[:, 2:3] * _load_rows8(rows, tc) + rt[:, 3:4] * _load_rows8(rows, tc, base=tc * ROW_TILES))
    o_ref[...] = x_ref[...] + mod_ref[0][:, 5 * D_MODEL:6 * D_MODEL] * y


def _combine(slots, y, x, mod, tiles_per_mod, route, tc):
    T = x.shape[0]
    n = T // tc
    slot_spec = lambda f: pl.BlockSpec((1, 1, tc * TOP_K), f, memory_space=pltpu.SMEM)
    return pl.pallas_call(
        functools.partial(_combine_kernel, tc=tc),
        grid=(n,),
        in_specs=[slot_spec(lambda i: (i, 0, 0)), slot_spec(lambda i: (jnp.minimum(i + 1, n - 1), 0, 0)),
                  pl.BlockSpec(memory_space=pl.ANY),
                  pl.BlockSpec((tc, D_MODEL), lambda i: (i, 0)),
                  _mod_spec(mod, tiles_per_mod),
                  pl.BlockSpec((tc, LANES), lambda i: (i, 0))],
        out_specs=pl.BlockSpec((tc, D_MODEL), lambda i: (i, 0)),
        out_shape=jax.ShapeDtypeStruct((T, D_MODEL), F32),
        scratch_shapes=[pltpu.VMEM((2, TOP_K * tc * ROW_TILES, LANES), F32), pltpu.SemaphoreType.DMA((2,))],
        compiler_params=_cparams(("arbitrary",)),
        name="moe_combine",
    )(slots, slots, y, x, mod, route)


def _moe(part_a, part_b, counts, w1, w3, w2, layer):
    Ta = part_a[3].shape[0]
    slots, plan, zero_plan, nb = _layout(jnp.concatenate([part_a[2], part_b[2]], axis=1), counts, MOE_BLOCK)
    tiled = [_tile_slots(slots[:, :Ta], part_a[6]), _tile_slots(slots[:, Ta:], part_b[6])]
    xs = _scatter(tiled[0], tiled[1], part_a[0], part_b[0], zero_plan, nb, MOE_BLOCK)
    y = _experts(xs, plan, nb, w1, w3, w2, layer, MOE_BLOCK)
    return [_combine(sl, y, x, mod, tpm, rt, tt) for sl, (_, rt, _, x, mod, tpm, tt) in zip(tiled, (part_a, part_b))]


def _kvq_kernel(x_ref, modkv_ref, mod_ref, nkv_ref, nm_ref, wkv_ref, wq_ref, kn_ref, qn_ref, eavg_ref,
                *rest, tm, keep):
    residue = keep is not None
    x = x_ref[...]
    xn = x * lax.rsqrt(jnp.mean(x * x, axis=-1, keepdims=True) + EPS)
    modkv = modkv_ref[0]
    mod = mod_ref[0]
    hk = (xn * nkv_ref[...] * (1.0 + modkv[:, D_MODEL:]) + modkv[:, :D_MODEL]).astype(BF16)
    hq = (xn * nm_ref[...] * (1.0 + mod[:, D_MODEL:2 * D_MODEL]) + mod[:, :D_MODEL]).astype(BF16)
    eavg = eavg_ref[...]

    def head_norm(a, g):
        sq = (a * a).astype(BF16)
        half = ATTN_WIDTH // 2
        ms = jnp.concatenate([jnp.dot(sq[:, :half], eavg, preferred_element_type=F32),
                              jnp.dot(sq[:, half:], eavg, preferred_element_type=F32)], axis=1)
        return a * lax.rsqrt(ms + EPS) * g

    def residue_major(val, ref, dil, stage):
        if dil == 1:
            ref[0, 0] = val.astype(BF16)
            return
        for c in range(stage.shape[0]):
            stage[c] = val[:, c * LANES:(c + 1) * LANES]
        for r in range(dil):
            rows = [stage[c, pl.ds(r, tm // dil, stride=dil), :] for c in range(stage.shape[0])]
            ref[0, r] = jnp.concatenate(rows, axis=1).astype(BF16)

    for g, (_, dil) in enumerate(BRANCHES):
        k0 = 2 * g * ATTN_WIDTH
        kvg = jnp.dot(hk, wkv_ref[:, k0:k0 + 2 * ATTN_WIDTH], preferred_element_type=F32)
        kn = head_norm(kvg[:, :ATTN_WIDTH], kn_ref[g:g + 1, :])
        vv = kvg[:, ATTN_WIDTH:]
        q0 = g * ATTN_WIDTH
        qg = jnp.dot(hq, wq_ref[:, q0:q0 + ATTN_WIDTH], preferred_element_type=F32)
        qg = head_norm(qg, qn_ref[g:g + 1, :]) * (HEAD_DIM ** -0.5)
        if residue:
            stages = rest[4 * N_BRANCH:]
            residue_major(qg, rest[3 * g], dil, stages[3 * g])
            residue_major(kn, rest[3 * g + 1], dil, stages[3 * g + 1])
            residue_major(vv, rest[3 * g + 2], dil, stages[3 * g + 2])
            kt_ref = rest[3 * N_BRANCH + g]
            kt_ref[0, :ATTN_WIDTH, :] = kn.T
            kt_ref[0, ATTN_WIDTH:, :] = vv.T
        else:
            rest[0][:, k0:k0 + ATTN_WIDTH] = kn
            rest[0][:, k0 + ATTN_WIDTH:k0 + 2 * ATTN_WIDTH] = vv
            rest[1][:, q0:q0 + ATTN_WIDTH] = qg


def _kvq(x, modkv, mod, tiles_per_mod, tm, nkv, nm, wkv, wq, kn, qn, eavg, batch=None):
    T = x.shape[0]
    tok = lambda w: pl.BlockSpec((tm, w), lambda i: (i, 0))
    kvw = 2 * N_BRANCH * ATTN_WIDTH
    if batch is not None:
        L = T // batch
        tpb = L // tm
        kept = tuple(-(-min(w, L) // tm) for w, _ in BRANCHES)
        keep = (tpb, kept)
        out_specs, out_shape = [], []
        for _, dil in BRANCHES:
            for _ in range(3):
                out_specs.append(pl.BlockSpec((1, dil, tm // dil, ATTN_WIDTH), lambda i: (i // tpb, 0, i % tpb, 0)))
                out_shape.append(jax.ShapeDtypeStruct((batch, dil, L // dil, ATTN_WIDTH), BF16))
        for kg in kept:
            out_specs.append(pl.BlockSpec((1, 2 * ATTN_WIDTH, tm),
                                          lambda i, kg=kg: (i // tpb, 0, jnp.maximum(i % tpb - (tpb - kg), 0))))
            out_shape.append(jax.ShapeDtypeStruct((batch, 2 * ATTN_WIDTH, kg * tm), F32))
        scratch = [pltpu.VMEM((ATTN_WIDTH // LANES, tm if dil > 1 else 8, LANES), F32)
                   for _, dil in BRANCHES for _ in range(3)]
    else:
        keep = None
        out_specs = [tok(kvw), tok(N_BRANCH * ATTN_WIDTH)]
        out_shape = [jax.ShapeDtypeStruct((T, kvw), F32), jax.ShapeDtypeStruct((T, N_BRANCH * ATTN_WIDTH), F32)]
        scratch = []
    return pl.pallas_call(
        functools.partial(_kvq_kernel, tm=tm, keep=keep),
        grid=(T // tm,),
        in_specs=[tok(D_MODEL), _mod_spec(modkv, tiles_per_mod), _mod_spec(mod, tiles_per_mod),
                  _resident(nkv.shape), _resident(nm.shape), _resident(wkv.shape), _resident(wq.shape),
                  _resident(kn.shape), _resident(qn.shape), _resident(eavg.shape)],
        out_specs=out_specs,
        out_shape=out_shape,
        scratch_shapes=scratch,
        compiler_params=_cparams(("arbitrary",)),
        name="kvq",
    )(x, modkv, mod, nkv, nm, wkv, wq, kn, qn, eavg)


BAND_QBLOCKS = 8


def _band_attn_kernel(q_ref, kp_ref, kc_ref, vp_ref, vc_ref, bias_ref, o_ref, lse_ref):
    n = pl.program_id(2)
    hw = ATTN_WIDTH // 2
    hh = HEADS // 2
    col = lax.broadcasted_iota(jnp.int32, (hh * N_STEPS, 2 * N_STEPS), 1)
    first = jnp.logical_or(col >= N_STEPS, n > 0)
    lane = lax.broadcasted_iota(jnp.int32, (N_STEPS, LANES), 1)
    head_of = lax.broadcasted_iota(jnp.int32, (N_STEPS, hw), 1) // HEAD_DIM
    zero = jnp.zeros((N_STEPS, hw), BF16)
    nt = (((1,), (1,)), ((), ()))
    for r, j in [(r, j) for r in range(q_ref.shape[0]) for j in range(q_ref.shape[1] // N_STEPS)]:
        if j == 0:
            k_all = jnp.concatenate([kp_ref[r], kc_ref[r]], axis=0)
            v_all = jnp.concatenate([vp_ref[r], vc_ref[r]], axis=0)
        rows = slice(j * N_STEPS, (j + 1) * N_STEPS)
        q = q_ref[r, rows, :]
        k2 = k_all[j * N_STEPS:(j + 2) * N_STEPS]
        v2 = v_all[j * N_STEPS:(j + 2) * N_STEPS]
        lse_all = jnp.zeros((N_STEPS, LANES), F32)
        halves = []
        for c in range(2):
            cols = slice(c * hw, (c + 1) * hw)
            qc = q[:, cols]
            qbd = jnp.concatenate([jnp.where(head_of == h, qc, zero) for h in range(hh)], axis=0)
            s = (lax.dot_general(qbd, k2[:, cols], nt, preferred_element_type=F32)
                 + bias_ref[c * hh * N_STEPS:(c + 1) * hh * N_STEPS, :])
            if j == 0:
                s = jnp.where(first, s, NEG)
            m = jnp.max(s, axis=1, keepdims=True)
            p = jnp.exp(s - m)
            l = jnp.sum(p, axis=1, keepdims=True)
            pv = jnp.dot(p.astype(BF16), v2[:, cols], preferred_element_type=F32) / l
            lse = m + jnp.log(l)
            o = jnp.zeros((N_STEPS, hw), F32)
            for h in range(hh):
                blk = slice(h * N_STEPS, (h + 1) * N_STEPS)
                o = jnp.where(head_of == h, pv[blk], o)
                lse_all = jnp.where(lane == c * hh + h, lse[blk], lse_all)
            halves.append(o)
        o_ref[r, rows, :] = jnp.concatenate(halves, axis=1)
        lse_ref[r, rows, :] = lse_all


def _band_attn(q, k, v, bias, dil):
    B, _, Ld, _ = q.shape
    nq = min(BAND_QBLOCKS, Ld // N_STEPS)
    nr = min(BAND_QBLOCKS // nq, dil)
    step = nq * N_STEPS
    blk = lambda w, f: pl.BlockSpec((None, nr, step, w), f)
    cur = lambda b, r, n: (b, r, n, 0)
    prev_blk = pl.BlockSpec((None, nr, N_STEPS, ATTN_WIDTH), lambda b, r, n: (b, r, jnp.maximum(nq * n - 1, 0), 0))
    return pl.pallas_call(
        _band_attn_kernel,
        grid=(B, dil // nr, Ld // step),
        in_specs=[blk(ATTN_WIDTH, cur), prev_blk, blk(ATTN_WIDTH, cur), prev_blk,
                  blk(ATTN_WIDTH, cur), _resident(bias.shape)],
        out_specs=[blk(ATTN_WIDTH, cur), blk(LANES, cur)],
        out_shape=[jax.ShapeDtypeStruct((B, dil, Ld, ATTN_WIDTH), F32),
                   jax.ShapeDtypeStruct((B, dil, Ld, LANES), F32)],
        compiler_params=_cparams(("arbitrary", "arbitrary", "arbitrary")),
        name=f"band_attn_d{dil}",
    )(q, k, k, v, v, bias)


def _rel_buckets(dilation):
    n = np.arange(N_STEPS + 1) * dilation
    large = MAX_EXACT + (np.log(np.maximum(n, 1) / MAX_EXACT) / np.log(REL_MAX_DIST / MAX_EXACT)
                         * (NUM_BUCKETS - MAX_EXACT)).astype(np.int32)
    return np.where(n < MAX_EXACT, n, np.minimum(large, NUM_BUCKETS - 1)).astype(np.int32)


def _step_bias_row(rel_bias, g, dil):
    onehot = np.zeros((N_STEPS + 1, NUM_BUCKETS), np.float32)
    onehot[np.arange(N_STEPS + 1), _rel_buckets(dil)] = 1.0
    return jnp.dot(jnp.asarray(onehot), rel_bias[:, g, :].astype(F32), precision=lax.Precision.HIGHEST).T


def _band_bias(rel_bias, g, dil):
    bias = _step_bias_row(rel_bias, g, dil)
    P = 3 * N_STEPS
    neg = jnp.full((HEADS, N_STEPS), NEG, F32)
    ext = jnp.concatenate([neg, bias[:, ::-1], neg], axis=1)
    flat = jnp.broadcast_to(ext[:, None, :], (HEADS, N_STEPS, P + 1)).reshape(HEADS, N_STEPS * (P + 1))
    skew = flat[:, :N_STEPS * P].reshape(HEADS, N_STEPS, P)
    return skew[:, :, N_STEPS:].reshape(HEADS * N_STEPS, 2 * N_STEPS)


def _step_attn_kernel(q_ref, kvn_ref, c0_ref, c1_ref, c2_ref, b0_ref, b1_ref, b2_ref, bn_ref, o_ref, *, n_new):
    caches = (c0_ref, c1_ref, c2_ref)
    cbias = (b0_ref, b1_ref, b2_ref)
    nt = (((1,), (1,)), ((), ()))
    rows = n_new * HEADS
    row = lax.broadcasted_iota(jnp.int32, (rows, ATTN_WIDTH), 0)
    lane = lax.broadcasted_iota(jnp.int32, (rows, ATTN_WIDTH), 1)
    own_head = lane // HEAD_DIM == row % HEADS
    parts = []
    for g in range(N_BRANCH):
        klo = 2 * g * ATTN_WIDTH
        qg = q_ref[0, :, g * ATTN_WIDTH:(g + 1) * ATTN_WIDTH]
        qrep = jnp.concatenate([jnp.broadcast_to(qg[s:s + 1, :], (HEADS, ATTN_WIDTH)) for s in range(n_new)], axis=0)
        qbd = jnp.where(own_head, qrep, 0.0).astype(BF16)
        kn = kvn_ref[0, :, klo:klo + ATTN_WIDTH].astype(BF16)
        vn = kvn_ref[0, :, klo + ATTN_WIDTH:klo + 2 * ATTN_WIDTH].astype(BF16)
        kt = caches[g][0, 0].astype(BF16)
        vt = caches[g][0, 1].astype(BF16)
        sc = jnp.dot(qbd, kt, preferred_element_type=F32) + cbias[g][...]
        sn = lax.dot_general(qbd, kn, nt, preferred_element_type=F32) + bn_ref[g]
        m = jnp.maximum(jnp.max(sc, axis=1, keepdims=True), jnp.max(sn, axis=1, keepdims=True))
        pc = jnp.exp(sc - m)
        pn = jnp.exp(sn - m)
        l = jnp.sum(pc, axis=1, keepdims=True) + jnp.sum(pn, axis=1, keepdims=True)
        pv = (lax.dot_general(pc.astype(BF16), vt, nt, preferred_element_type=F32)
              + jnp.dot(pn.astype(BF16), vn, preferred_element_type=F32))
        parts.append((m + jnp.log(l), pv / l))
    mx = jnp.maximum(jnp.maximum(parts[0][0], parts[1][0]), parts[2][0])
    es = [jnp.exp(lse - mx) for lse, _ in parts]
    den = es[0] + es[1] + es[2]
    mixed = (es[0] * parts[0][1] + es[1] * parts[1][1] + es[2] * parts[2][1]) / den
    mixed = jnp.where(own_head, mixed, 0.0)
    for s in range(n_new):
        o_ref[0, s:s + 1, :] = jnp.sum(mixed[s * HEADS:(s + 1) * HEADS], axis=0, keepdims=True)


def _step_bias(rel_bias, n_new):
    cache_tabs, new_tabs = [], []
    for g, (win, dil) in enumerate(BRANCHES):
        bias = _step_bias_row(rel_bias, g, dil)
        rev = bias[:, ::-1]
        per_s = []
        for s in range(n_new):
            if dil == 1:
                neg = jnp.full((HEADS, s), NEG, F32)
                per_s.append(jnp.concatenate([neg, rev[:, :win - s]], axis=1))
            else:
                cols = [rev[:, :N_STEPS] if r == s % dil else jnp.full((HEADS, N_STEPS), NEG, F32)
                        for r in range(dil)]
                per_s.append(jnp.stack(cols, axis=2).reshape(HEADS, win))
        cache_tabs.append(jnp.stack(per_s, axis=0).reshape(n_new * HEADS, win))
        dist = np.arange(n_new)[:, None] - np.arange(8)[None, :]
        ok = (dist >= 0) & (dist % dil == 0) & (np.arange(8)[None, :] < n_new)
        onehot = np.zeros((N_STEPS + 1, n_new * 8), np.float32)
        onehot[np.where(ok, dist // dil, 0).reshape(-1), np.arange(n_new * 8)] = 1.0
        tab = jnp.dot(bias, jnp.asarray(onehot), precision=lax.Precision.HIGHEST).reshape(HEADS, n_new, 8)
        new_tabs.append(jnp.where(ok[None], tab, NEG).transpose(1, 0, 2).reshape(n_new * HEADS, 8))
    return cache_tabs, jnp.stack(new_tabs)


def _step_attn(q, kvn, caches, cache_bias, new_bias, DB, S):
    cspec = lambda c: pl.BlockSpec((1,) + c.shape[1:], lambda b: (b, 0, 0, 0))
    return pl.pallas_call(
        functools.partial(_step_attn_kernel, n_new=S),
        grid=(DB,),
        in_specs=[pl.BlockSpec((1, S, N_BRANCH * ATTN_WIDTH), lambda b: (b, 0, 0)),
                  pl.BlockSpec((1, 8, 2 * N_BRANCH * ATTN_WIDTH), lambda b: (b, 0, 0)),
                  cspec(caches[0]), cspec(caches[1]), cspec(caches[2]),
                  _resident(cache_bias[0].shape), _resident(cache_bias[1].shape), _resident(cache_bias[2].shape),
                  _resident(new_bias.shape)],
        out_specs=pl.BlockSpec((1, S, ATTN_WIDTH), lambda b: (b, 0, 0)),
        out_shape=jax.ShapeDtypeStruct((DB, S, ATTN_WIDTH), F32),
        compiler_params=_cparams(("arbitrary",)),
        name="step_attn",
    )(q, kvn, caches[0], caches[1], caches[2], *cache_bias, new_bias)


def _attn_out_kernel(*refs, dils, tm):
    n_o = max(len(dils), 1)
    n_in = n_o + len(dils) + 9
    (x_ref, mod_ref, wo_ref, ex_ref, nf_ref, wr_ref, br_ref, cin_ref, tri_ref,
     x3_ref, h3_ref, rt_ref, rtt_ref, cnt_ref, carry) = refs[n_in - 9:n_in + 6]
    stages = refs[n_in + 6:]
    _start_counts(cin_ref, carry)

    def natural(ref, stage, dil):
        if dil == 1:
            return ref[0, 0]
        for r in range(dil):
            val = ref[0, r]
            for c in range(stage.shape[0]):
                stage[c, pl.ds(r, tm // dil, stride=dil), :] = val[:, c * LANES:(c + 1) * LANES]
        return jnp.concatenate([stage[c] for c in range(stage.shape[0])], axis=1)

    if dils:
        lses = [natural(refs[n_o + g], stages[2 * g + 1], dil) for g, dil in enumerate(dils)]
        mx = functools.reduce(jnp.maximum, lses)
        es = [jnp.exp(l - mx) for l in lses]
        den = functools.reduce(lambda a, b: a + b, es)
        o = None
        for g, dil in enumerate(dils):
            term = _split_dot(es[g] / den, ex_ref[...]) * natural(refs[g], stages[2 * g], dil)
            o = term if o is None else o + term
    else:
        o = refs[0][...]
    mod = mod_ref[0]
    a = jnp.dot(o.astype(BF16), wo_ref[...], preferred_element_type=F32)
    x3 = x_ref[...] + mod[:, 2 * D_MODEL:3 * D_MODEL] * a
    x3_ref[...] = x3
    _ffn_pre(x3, mod, nf_ref, wr_ref, br_ref, tri_ref, h3_ref, rt_ref, rtt_ref, cnt_ref, carry, tm)


def _attn_out(os_, lses, x, mod, tiles_per_mod, tm, wo, ex, nf, wr, br, counts):
    T = x.shape[0]
    tok = lambda w: pl.BlockSpec((tm, w), lambda i: (i, 0))
    r_specs, r_shapes = _route_outs(T, tm)
    tri = _earlier_rows(tm)
    if lses:
        dils = tuple(o.shape[1] for o in os_)
        tpb = tiles_per_mod
        res = lambda a: pl.BlockSpec((1, a.shape[1], tm // a.shape[1], a.shape[3]), lambda i: (i // tpb, 0, i % tpb, 0))
        o_specs = [res(a) for a in os_] + [res(a) for a in lses]
        scratch = []
        for _ in dils:
            scratch += [pltpu.VMEM((ATTN_WIDTH // LANES, tm, LANES), F32), pltpu.VMEM((1, tm, LANES), F32)]
    else:
        dils, o_specs, scratch = (), [tok(ATTN_WIDTH)], []
    return pl.pallas_call(
        functools.partial(_attn_out_kernel, dils=dils, tm=tm),
        grid=(T // tm,),
        in_specs=(o_specs + [tok(D_MODEL), _mod_spec(mod, tiles_per_mod), _resident(wo.shape), _resident(ex.shape),
                             _resident(nf.shape), _resident(wr.shape), _resident(br.shape),
                             _resident(counts.shape), _resident(tri.shape)]),
        out_specs=[tok(D_MODEL), pl.BlockSpec((tm * ROW_TILES, LANES), lambda i: (i, 0))] + r_specs,
        out_shape=[jax.ShapeDtypeStruct((T, D_MODEL), F32),
                   jax.ShapeDtypeStruct((T * ROW_TILES, LANES), F32)] + r_shapes,
        scratch_shapes=[pltpu.VMEM((1, LANES), F32)] + scratch,
        compiler_params=_cparams(("arbitrary",)),
        name="attn_out",
    )(*os_, *lses, x, mod, wo, ex, nf, wr, br, counts, tri)


def kernel(x_prompt, x_sample, cache_kv_w128, cache_kv_w512, cache_kv_w2048, c_prompt, c_sample, ada_w, ada_b, norm_mix, norm_ffn, a_w_in, a_b_in, a_norm_v, a_w_s, a_b_s, a_w_out, kv_ada_w, kv_ada_b, kv_norm, w_kv, k_norm, rel_bias, b_w_q, q_norm, b_w_o, r_w_group, r_b_group, r_w_expert, r_b_expert, e_w1, e_w3, e_w2):
    B, L, _ = x_prompt.shape
    DB, S, _ = x_sample.shape
    Tp, Ts = B * L, DB * S
    tm = TOKEN_TILE
    tpm_p = L // tm

    c_all = jnp.concatenate([c_prompt, c_sample], axis=0)
    R = c_all.shape[0]
    c_all = jnp.pad(c_all, ((0, -R % 8), (0, 0)))
    mods = [_ada(c_all, ada_w, ada_b, l) for l in range(2)]
    modkv = _ada(c_all, kv_ada_w[None], kv_ada_b[None], 0)

    def split_mod(m):
        return m[:B, None, :], jnp.repeat(m[B:B + DB], S, axis=0)[None]
    mod_p, mod_s = zip(*[split_mod(m) for m in mods])
    modkv_p, modkv_s = split_mod(modkv)

    row = lambda a: a.reshape(1, -1)

    def router(l):
        wr = jnp.zeros((D_MODEL, LANES), F32)
        wr = wr.at[:, :N_GROUPS].set(r_w_group[l]).at[:, N_GROUPS:N_GROUPS + N_EXPERTS].set(r_w_expert[l])
        br = jnp.zeros((1, LANES), F32)
        br = br.at[0, :N_GROUPS].set(r_b_group[l]).at[0, N_GROUPS:N_GROUPS + N_EXPERTS].set(r_b_expert[l])
        return wr.astype(BF16), br

    no_counts = jnp.zeros((1, LANES), F32)

    win = a_w_in[0].astype(BF16)
    wout = a_w_out[0].astype(BF16)
    tril = jnp.tril(jnp.ones((CHUNK, CHUNK), bool))
    ws_p = jnp.where(tril, a_w_s[0], 0).astype(BF16)
    bs_p = a_b_s[0].T
    cs = min(CHUNK, S)
    ws_small = jnp.where(jnp.tril(jnp.ones((cs, cs), bool)), a_w_s[0][:, :cs, :cs], 0)
    ws_s = jnp.stack([jnp.kron(jnp.eye(Ts // cs, dtype=F32), ws_small[g]) for g in range(GMLP_GROUPS)]).astype(BF16)
    bs_s = jnp.tile(a_b_s[0][:, :cs], (1, Ts // cs)).T
    wr0, br0 = router(0)
    common = (row(norm_mix[0]), win, row(a_b_in[0]), row(a_norm_v[0]))
    xp = x_prompt.reshape(Tp, D_MODEL)
    xs_ = x_sample.reshape(Ts, D_MODEL)
    x1_p, h2_p, rt_p, rtt_p, cnt = _mixer_a(xp, mod_p[0], tpm_p, tm, *common, ws_p, bs_p, wout, row(norm_ffn[0]),
                                            wr0, br0, no_counts, with_v=False)
    x1_s, h2_s, rt_s, rtt_s, cnt, v_s = _mixer_a(xs_, mod_s[0], 1, Ts, *common, ws_s, bs_s, wout, row(norm_ffn[0]),
                                                 wr0, br0, cnt, with_v=True)
    x2_p, x2_s = _moe((h2_p, rt_p, rtt_p, x1_p, mod_p[0], tpm_p, tm), (h2_s, rt_s, rtt_s, x1_s, mod_s[0], 1, Ts), cnt,
                      e_w1, e_w3, e_w2, 0)

    wkv = w_kv.astype(BF16)
    wq = b_w_q[0].astype(BF16)
    kn = jnp.tile(k_norm, (1, HEADS))
    qn = jnp.tile(q_norm[0], (1, HEADS))
    head = np.arange(ATTN_WIDTH) // HEAD_DIM
    half_head = head[:ATTN_WIDTH // 2]
    eavg = jnp.asarray((half_head[:, None] == half_head[None, :]) / HEAD_DIM, BF16)
    kvq_w = (row(kv_norm), row(norm_mix[1]), wkv, wq, kn, qn, eavg)
    *qkv_p, kt0, kt1, kt2 = _kvq(x2_p, modkv_p, mod_p[1], tpm_p, tm, *kvq_w, batch=B)
    kv_s, q_s = _kvq(x2_s, modkv_s, mod_s[1], 1, Ts, *kvq_w)

    os_, lses = [], []
    for g, (_, dil) in enumerate(BRANCHES):
        o, lse = _band_attn(*qkv_p[3 * g:3 * g + 3], _band_bias(rel_bias, g, dil), dil)
        os_.append(o)
        lses.append(lse)
    caches = [jnp.transpose(c, (0, 2, 3, 4, 1)).reshape(DB, 2, ATTN_WIDTH, c.shape[1])
              for c in (cache_kv_w128, cache_kv_w512, cache_kv_w2048)]
    kvn_s = jnp.pad(kv_s.reshape(DB, S, -1), ((0, 0), (0, 8 - S), (0, 0)))
    o_s = _step_attn(q_s.reshape(DB, S, -1), kvn_s, caches, *_step_bias(rel_bias, S), DB, S)

    wo = b_w_o[0].astype(BF16)
    ex = jnp.asarray(np.arange(LANES)[:, None] == head[None, :], BF16)
    wr1, br1 = router(1)
    x3_p, h3_p, rt1_p, rtt1_p, cnt1 = _attn_out(os_, lses, x2_p, mod_p[1], tpm_p, tm, wo, ex, row(norm_ffn[1]),
                                                wr1, br1, no_counts)
    x3_s, h3_s, rt1_s, rtt1_s, cnt1 = _attn_out([o_s.reshape(Ts, ATTN_WIDTH)], [], x2_s, mod_s[1], 1, Ts, wo, ex,
                                                row(norm_ffn[1]), wr1, br1, cnt1)
    y_p, y_s = _moe((h3_p, rt1_p, rtt1_p, x3_p, mod_p[1], tpm_p, tm), (h3_s, rt1_s, rtt1_s, x3_s, mod_s[1], 1, Ts),
                    cnt1, e_w1, e_w3, e_w2, 1)

    kv_s4 = kv_s.reshape(DB, S, N_BRANCH, 2, HEADS, HEAD_DIM)

    def window(kt, w):
        n = min(w, L)
        return jnp.transpose(kt[:, :, kt.shape[2] - n:].reshape(B, 2, HEADS, HEAD_DIM, n), (0, 4, 1, 2, 3))
    return (y_p.reshape(B, L, D_MODEL), y_s.reshape(DB, S, D_MODEL),
            window(kt0, BRANCHES[0][0]), window(kt1, BRANCHES[1][0]), window(kt2, BRANCHES[2][0]),
            kv_s4[:, :, 0], kv_s4[:, :, 1], kv_s4[:, :, 2],
            v_s.reshape(1, DB, S, GMLP_WIDTH))
```

```python
import functools

import numpy as np
import jax
import jax.numpy as jnp
from jax import lax
from jax.experimental import pallas as pl
from jax.experimental.pallas import tpu as pltpu

F32 = jnp.float32
BF16 = jnp.bfloat16

D_MODEL = 1024
GMLP_WIDTH = 2048
GMLP_GROUPS = 4
GROUP_WIDTH = GMLP_WIDTH // GMLP_GROUPS
CHUNK = 128
BRANCHES = ((128, 1), (512, 4), (2048, 16))
N_BRANCH = 3
N_STEPS = 128
HEADS = 8
HEAD_DIM = 64
ATTN_WIDTH = HEADS * HEAD_DIM
NUM_BUCKETS = 32
MAX_EXACT = NUM_BUCKETS // 2
REL_MAX_DIST = 2048
N_GROUPS = 4
EXPERTS_PER_GROUP = 8
N_EXPERTS = N_GROUPS * EXPERTS_PER_GROUP
TOP_K = 2
D_EXPERT = 512
EPS = 1e-6
NEG = -1e30

LANES = 128
ROW_TILES = D_MODEL // LANES
TOKEN_TILE = 512
MOE_BLOCK = 256
DMA_UNROLL = 16
VMEM_LIMIT = 52 * 1024 * 1024


def _cparams(sem):
    return pltpu.CompilerParams(dimension_semantics=sem, vmem_limit_bytes=VMEM_LIMIT)


def _resident(shape):
    nd = len(shape)
    return pl.BlockSpec(shape, lambda *_, _nd=nd: (0,) * _nd, pipeline_mode=pl.Buffered(1))


def _gelu_tanh(x):
    return 0.5 * x * (1.0 + jnp.tanh(0.7978845608028654 * (x + 0.044715 * (x * x * x))))


def _rms(x, g):
    return x * lax.rsqrt(jnp.mean(x * x, axis=-1, keepdims=True) + EPS) * g


def _store_rows8(ref, val, n, base=0):
    for s in range(ROW_TILES):
        ref[pl.ds(base + s, n, stride=ROW_TILES), :] = val[:, s * LANES:(s + 1) * LANES]


def _load_rows8(ref, n, base=0):
    return jnp.concatenate([ref[pl.ds(base + s, n, stride=ROW_TILES), :] for s in range(ROW_TILES)], axis=1)


def _split(a):
    hi = a.astype(BF16)
    return hi, (a - hi.astype(F32)).astype(BF16)


def _split_dot(a, e_bf16):
    hi, lo = _split(a)
    return (jnp.dot(hi, e_bf16, preferred_element_type=F32) + jnp.dot(lo, e_bf16, preferred_element_type=F32))


ROUTE_LANE0 = 4


def _route_rows(l, tri_ref, carry):
    lane = lax.broadcasted_iota(jnp.int32, l.shape, 1).astype(F32)
    far = float(LANES)

    def first_lane(mask):
        return jnp.min(jnp.where(mask, lane, far), axis=1, keepdims=True)

    is_g = lane < N_GROUPS
    gl = jnp.where(is_g, l, NEG)
    gmax = jnp.max(gl, axis=1, keepdims=True)
    g_i = first_lane(jnp.logical_and(gl == gmax, is_g))
    g_p = 1.0 / jnp.sum(jnp.where(is_g, jnp.exp(gl - gmax), 0.0), axis=1, keepdims=True)
    lo = ROUTE_LANE0 + EXPERTS_PER_GROUP * g_i
    sel = jnp.logical_and(lane >= lo, lane < lo + EXPERTS_PER_GROUP)
    el = jnp.where(sel, l, NEG)
    m1 = jnp.max(el, axis=1, keepdims=True)
    i1 = first_lane(jnp.logical_and(el == m1, sel))
    sel2 = jnp.logical_and(sel, lane != i1)
    el2 = jnp.where(sel2, l, NEG)
    m2 = jnp.max(el2, axis=1, keepdims=True)
    i2 = first_lane(jnp.logical_and(el2 == m2, sel2))
    r = jnp.exp(m2 - m1)
    w1 = g_p / (1.0 + r)
    w2 = g_p * r / (1.0 + r)

    hit1 = lane == i1
    hit2 = lane == i2
    onehot = jnp.where(jnp.logical_or(hit1, hit2), 1.0, 0.0)
    before = carry[...] + jnp.dot(tri_ref[...], onehot.astype(BF16), preferred_element_type=F32)
    rank1 = jnp.sum(jnp.where(hit1, before, 0.0), axis=1, keepdims=True)
    rank2 = jnp.sum(jnp.where(hit2, before, 0.0), axis=1, keepdims=True)
    carry[...] = carry[...] + jnp.sum(onehot, axis=0, keepdims=True)

    out = jnp.zeros(l.shape, F32)
    for k, val in enumerate((i1 - ROUTE_LANE0, i2 - ROUTE_LANE0, w1, w2, rank1, rank2)):
        out = jnp.where(lane == k, val, out)
    return out


ROUTE_FIELDS = 8


def _start_counts(cin_ref, carry):
    @pl.when(pl.program_id(0) == 0)
    def _():
        carry[...] = cin_ref[...]


def _ffn_pre(x, mod, nf_ref, wr_ref, br_ref, tri_ref, h_ref, rt_ref, rtt_ref, cnt_ref, carry, n):
    h = _rms(x, nf_ref[...]) * (1.0 + mod[:, 4 * D_MODEL:5 * D_MODEL]) + mod[:, 3 * D_MODEL:4 * D_MODEL]
    _store_rows8(h_ref, h, n)
    logits = jnp.dot(h.astype(BF16), wr_ref[...], preferred_element_type=F32) + br_ref[...]
    route = _route_rows(logits, tri_ref, carry)
    rt_ref[...] = route
    rtt_ref[...] = route.T[:ROUTE_FIELDS, :]
    cnt_ref[...] = carry[...]


def _ada_kernel(c_ref, w_ref, b_ref, o_ref):
    c = c_ref[...]
    a = (c * jax.nn.sigmoid(c)).astype(BF16)
    o_ref[...] = jnp.dot(a, w_ref[...].astype(BF16), preferred_element_type=F32) + b_ref[...]


def _ada(c, w, b, layer):
    R = c.shape[0]
    N = w.shape[2]
    tn = 1024
    return pl.pallas_call(
        _ada_kernel,
        grid=(N // tn,),
        in_specs=[pl.BlockSpec((R, D_MODEL), lambda j: (0, 0)),
                  pl.BlockSpec((None, D_MODEL, tn), lambda j: (layer, 0, j)),
                  pl.BlockSpec((None, 1, tn), lambda j: (layer, 0, j))],
        out_specs=pl.BlockSpec((R, tn), lambda j: (0, j)),
        out_shape=jax.ShapeDtypeStruct((R, N), F32),
        compiler_params=_cparams(("arbitrary",)),
        name="ada",
    )(c, w, b.reshape(b.shape[0], 1, N))


def _mixer_a_kernel(x_ref, mod_ref, nm_ref, win_ref, bin_ref, gv_ref, ws_ref, bs_ref, wout_ref,
                    nf_ref, wr_ref, br_ref, cin_ref, tri_ref, x1_ref, h2_ref, rt_ref, rtt_ref, cnt_ref, *rest, tm):
    v_refs, carry = rest[:-1], rest[-1]
    _start_counts(cin_ref, carry)
    x = x_ref[...]
    mod = mod_ref[0]
    h = (_rms(x, nm_ref[...]) * (1.0 + mod[:, D_MODEL:2 * D_MODEL]) + mod[:, 0:D_MODEL]).astype(BF16)
    zv = jnp.dot(h, win_ref[:, GMLP_WIDTH:], preferred_element_type=F32) + bin_ref[:, GMLP_WIDTH:]
    v = _rms(_gelu_tanh(zv), gv_ref[...])
    if v_refs:
        v_refs[0][...] = v
    vb = v.astype(BF16)
    bs = bs_ref[...]
    acc = jnp.zeros((tm, D_MODEL), F32)
    for g in range(GMLP_GROUPS):
        lo, hi = g * GROUP_WIDTH, (g + 1) * GROUP_WIDTH
        u = _gelu_tanh(jnp.dot(h, win_ref[:, lo:hi], preferred_element_type=F32) + bin_ref[:, lo:hi])
        wg = ws_ref[g]
        gate = jnp.concatenate(
            [jnp.dot(wg, vb[c * CHUNK:(c + 1) * CHUNK, lo:hi], preferred_element_type=F32) + bs[:, g:g + 1]
             for c in range(tm // CHUNK)], axis=0)
        acc = acc + jnp.dot((u * gate).astype(BF16), wout_ref[lo:hi, :], preferred_element_type=F32)
    x1 = x + mod[:, 2 * D_MODEL:3 * D_MODEL] * acc
    x1_ref[...] = x1
    _ffn_pre(x1, mod, nf_ref, wr_ref, br_ref, tri_ref, h2_ref, rt_ref, rtt_ref, cnt_ref, carry, tm)


def _mod_spec(mod, tiles_per_mod):
    _, rows, width = mod.shape
    return pl.BlockSpec((1, rows, width), lambda i: (i // tiles_per_mod, 0, 0))


def _earlier_rows(tm):
    return jnp.asarray(np.tril(np.ones((tm, tm), np.float32), -1), BF16)


def _route_outs(T, tm):
    return ([pl.BlockSpec((tm, LANES), lambda i: (i, 0)), pl.BlockSpec((ROUTE_FIELDS, tm), lambda i: (0, i)),
             pl.BlockSpec((1, LANES), lambda i: (0, 0))],
            [jax.ShapeDtypeStruct((T, LANES), F32), jax.ShapeDtypeStruct((ROUTE_FIELDS, T), F32),
             jax.ShapeDtypeStruct((1, LANES), F32)])


def _mixer_a(x, mod, tiles_per_mod, tm, nm, win, bin_, gv, ws, bs_t, wout, nf, wr, br, counts, with_v):
    T = x.shape[0]
    tok = lambda w: pl.BlockSpec((tm, w), lambda i: (i, 0))
    r_specs, r_shapes = _route_outs(T, tm)
    out_shape = [jax.ShapeDtypeStruct((T, D_MODEL), F32), jax.ShapeDtypeStruct((T * ROW_TILES, LANES), F32)] + r_shapes
    out_specs = [tok(D_MODEL), pl.BlockSpec((tm * ROW_TILES, LANES), lambda i: (i, 0))] + r_specs
    if with_v:
        out_shape.append(jax.ShapeDtypeStruct((T, GMLP_WIDTH), F32))
        out_specs.append(tok(GMLP_WIDTH))
    tri = _earlier_rows(tm)
    return pl.pallas_call(
        functools.partial(_mixer_a_kernel, tm=tm),
        grid=(T // tm,),
        in_specs=[tok(D_MODEL), _mod_spec(mod, tiles_per_mod), _resident(nm.shape), _resident(win.shape),
                  _resident(bin_.shape), _resident(gv.shape), _resident(ws.shape), _resident(bs_t.shape),
                  _resident(wout.shape), _resident(nf.shape), _resident(wr.shape), _resident(br.shape),
                  _resident(counts.shape), _resident(tri.shape)],
        out_specs=out_specs,
        out_shape=out_shape,
        scratch_shapes=[pltpu.VMEM((1, LANES), F32)],
        compiler_params=_cparams(("arbitrary",)),
        name="mixer_a",
    )(x, mod, nm, win, bin_, gv, ws, bs_t, wout, nf, wr, br, counts, tri)


def _layout(rtt, counts, block):
    n_tokens = rtt.shape[1]
    counts = counts[0, ROUTE_LANE0:ROUTE_LANE0 + N_EXPERTS].astype(jnp.int32)
    padded = (counts + block - 1) // block * block
    pad_end = jnp.cumsum(padded)
    pad_start = (pad_end - padded).astype(F32)
    experts, ranks = rtt[0:TOP_K], rtt[4:4 + TOP_K]
    onehot = (experts[None] == jnp.arange(N_EXPERTS, dtype=F32)[:, None, None]).astype(F32)
    base = jnp.einsum('e,ekt->kt', pad_start, onehot, precision=lax.Precision.HIGHEST)
    slots = (ranks + base).astype(jnp.int32)
    nb = -(-n_tokens * TOP_K // block) + N_EXPERTS
    blk_e = jnp.minimum(jnp.sum(pad_end[None, :] <= (jnp.arange(nb, dtype=jnp.int32) * block)[:, None], axis=1),
                        N_EXPERTS - 1).astype(jnp.int32)
    n_used = (pad_end[-1] // block).astype(jnp.int32).reshape(1)
    starts = jnp.concatenate([jnp.ones((1,), jnp.int32), (blk_e[1:] != blk_e[:-1]).astype(jnp.int32)])
    grp = jnp.cumsum(starts) - 1
    end_blk = (pad_end // block).astype(jnp.int32)
    own = blk_e[:, None] == jnp.arange(N_EXPERTS, dtype=jnp.int32)[None, :]
    nxt_blk = jnp.sum(jnp.where(own, end_blk[None, :], 0), axis=1)
    at_nxt = nxt_blk[:, None] == jnp.arange(nb, dtype=jnp.int32)[None, :]
    nxt_e = jnp.where(nxt_blk < n_used[0], jnp.sum(jnp.where(at_nxt, blk_e[None, :], 0), axis=1), -1).astype(jnp.int32)
    zero_plan = (jnp.maximum(pad_end - block, 0).astype(jnp.int32), (padded > 0).astype(jnp.int32), n_used)
    return slots, (blk_e, n_used, nxt_e, grp.astype(jnp.int32)), zero_plan, nb


def _tile_slots(slots, ts):
    T = slots.shape[1]
    return slots.reshape(TOP_K, T // ts, ts).transpose(1, 0, 2).reshape(T // ts, 1, TOP_K * ts)


def _row_copy(src_ref, src_row, dst_ref, dst_row, sem):
    return pltpu.make_async_copy(
        src_ref.at[pl.ds(pl.multiple_of(src_row * ROW_TILES, ROW_TILES), ROW_TILES), :],
        dst_ref.at[pl.ds(pl.multiple_of(dst_row * ROW_TILES, ROW_TILES), ROW_TILES), :], sem)


S_BUFFERS = 3


def _scatter_kernel(zlo_ref, has_ref, nu_ref, slot_a_ref, slot_b_ref, src_a_hbm, src_b_ref, dst_ref, zbuf, sem,
                    sbuf, ssem, rsem, *, ts, n_a, n_b, nb, block):
    i = pl.program_id(0)
    tile_rows = ts * ROW_TILES

    def tile_in(t):
        return pltpu.make_async_copy(src_a_hbm.at[pl.ds(pl.multiple_of(t * tile_rows, tile_rows), tile_rows), :],
                                     sbuf.at[t % S_BUFFERS], ssem.at[t % S_BUFFERS])

    def rows_out(t):
        return pltpu.make_async_copy(sbuf.at[t % S_BUFFERS], dst_ref.at[pl.ds(0, tile_rows), :], rsem.at[t % 2])

    def clear(row0):
        return pltpu.make_async_copy(
            zbuf, dst_ref.at[pl.ds(pl.multiple_of(row0 * ROW_TILES, ROW_TILES), block * ROW_TILES), :], sem)

    def for_clears(act):
        for e in range(N_EXPERTS):
            @pl.when(has_ref[e] > 0)
            def _(e=e):
                act(clear(zlo_ref[e]))

        def tail(j, c):
            act(clear(j * block))
            return c
        lax.fori_loop(nu_ref[0], nb, tail, 0)

    @pl.when(i == 0)
    def _():
        for t in range(min(S_BUFFERS - 1, n_a)):
            tile_in(t).start()
        zbuf[...] = jnp.zeros(zbuf.shape, F32)
        for_clears(lambda c: c.start())
        for_clears(lambda c: c.wait())

    def issue_rows(slot_ref, src_ref, n, row_sem):
        def issue(j, c):
            for u in range(DMA_UNROLL):
                t = j * DMA_UNROLL + u
                for k in range(TOP_K):
                    _row_copy(src_ref, t, dst_ref, slot_ref[0, 0, k * n + t], row_sem).start(priority=k)
            return c
        lax.fori_loop(0, n // DMA_UNROLL, issue, 0)

    def wait_rows_out(t):
        for _ in range(TOP_K):
            rows_out(t).wait()

    @pl.when(i < n_a)
    def _():
        tile_in(i).wait()
        issue_rows(slot_a_ref, sbuf.at[i % S_BUFFERS], ts, rsem.at[i % 2])

        @pl.when(i >= 1)
        def _():
            wait_rows_out(i - 1)

        @pl.when(i + S_BUFFERS - 1 < n_a)
        def _():
            tile_in(i + S_BUFFERS - 1).start()

    @pl.when(i == n_a)
    def _():
        wait_rows_out(n_a - 1)
        issue_rows(slot_b_ref, src_b_ref, n_b, sem)
        for _ in range(TOP_K):
            pltpu.make_async_copy(src_b_ref, dst_ref.at[pl.ds(0, n_b * ROW_TILES), :], sem).wait()


def _scatter(slots_a, slots_b, src_a, src_b, zero_plan, nb, block):
    n_a, _, ts2 = slots_a.shape
    ts, n_b = ts2 // TOP_K, slots_b.shape[2] // TOP_K
    tile_a = lambda i, *_: (jnp.minimum(i, n_a - 1), 0, 0)
    return pl.pallas_call(
        functools.partial(_scatter_kernel, ts=ts, n_a=n_a, n_b=n_b, nb=nb, block=block),
        grid_spec=pltpu.PrefetchScalarGridSpec(
            num_scalar_prefetch=3,
            grid=(n_a + 1,),
            in_specs=[pl.BlockSpec((1, 1, ts2), tile_a, memory_space=pltpu.SMEM),
                      pl.BlockSpec((1, 1, TOP_K * n_b), lambda i, *_: (0, 0, 0), memory_space=pltpu.SMEM),
                      pl.BlockSpec(memory_space=pl.ANY),
                      pl.BlockSpec((n_b * ROW_TILES, LANES), lambda i, *_: (0, 0))],
            out_specs=pl.BlockSpec(memory_space=pl.ANY),
            scratch_shapes=[pltpu.VMEM((block * ROW_TILES, LANES), F32), pltpu.SemaphoreType.DMA(()),
                            pltpu.VMEM((S_BUFFERS, ts * ROW_TILES, LANES), F32),
                            pltpu.SemaphoreType.DMA((S_BUFFERS,)), pltpu.SemaphoreType.DMA((2,))]),
        out_shape=jax.ShapeDtypeStruct((nb * block * ROW_TILES, LANES), F32),
        compiler_params=_cparams(("arbitrary",)),
        name="moe_scatter",
    )(*zero_plan, slots_a, slots_b, src_a, src_b)


X_BUFFERS = 3
Y_BUFFERS = 2


def _experts_kernel(be_ref, nu_ref, nxt_ref, grp_ref, x_hbm, w1_hbm, w3_hbm, w2_hbm, y_hbm,
                    w1b, w3b, w2b, wf1, wf3, wf2, sem, xbuf, xsem, ybuf, ysem, *, block, layer):
    i = pl.program_id(0)
    n = nu_ref[0]
    rows = block * ROW_TILES

    def weight_copies(e, buf):
        return [pltpu.make_async_copy(src.at[layer, e], dst.at[buf], sem.at[buf])
                for src, dst in ((w1_hbm, wf1), (w3_hbm, wf3), (w2_hbm, wf2))]

    def x_copy(blk):
        return pltpu.make_async_copy(x_hbm.at[pl.ds(pl.multiple_of(blk * rows, rows), rows), :],
                                     xbuf.at[blk % X_BUFFERS], xsem.at[blk % X_BUFFERS])

    def y_copy(blk):
        return pltpu.make_async_copy(ybuf.at[blk % Y_BUFFERS],
                                     y_hbm.at[pl.ds(pl.multiple_of(blk * rows, rows), rows), :], ysem.at[blk % Y_BUFFERS])

    @pl.when(i == 0)
    def _():
        for c in weight_copies(be_ref[0], 0):
            c.start()
        for b in range(X_BUFFERS - 1):
            @pl.when(b < n)
            def _(b=b):
                x_copy(b).start()

    @pl.when(i < n)
    def _():
        @pl.when(i + X_BUFFERS - 1 < n)
        def _():
            x_copy(i + X_BUFFERS - 1).start()

        x_copy(i).wait()

        @pl.when(i >= Y_BUFFERS)
        def _():
            y_copy(i - Y_BUFFERS).wait()

        @pl.when(jnp.logical_or(i == 0, be_ref[i] != be_ref[jnp.maximum(i - 1, 0)]))
        def _():
            buf = grp_ref[i] % 2
            for c in weight_copies(be_ref[i], buf):
                c.wait()

            @pl.when(nxt_ref[i] >= 0)
            def _():
                for c in weight_copies(nxt_ref[i], 1 - buf):
                    c.start()

            w1b[...] = wf1[buf].astype(BF16)
            w3b[...] = wf3[buf].astype(BF16)
            w2b[...] = wf2[buf].astype(BF16)

        x = _load_rows8(xbuf.at[i % X_BUFFERS], block).astype(BF16)
        h1 = jnp.dot(x, w1b[...], preferred_element_type=F32)
        h3 = jnp.dot(x, w3b[...], preferred_element_type=F32)
        a = (h1 * jax.nn.sigmoid(h1) * h3).astype(BF16)
        _store_rows8(ybuf.at[i % Y_BUFFERS], jnp.dot(a, w2b[...], preferred_element_type=F32), block)
        y_copy(i).start()

        @pl.when(i == n - 1)
        def _():
            for back in range(Y_BUFFERS):
                @pl.when(i - back >= 0)
                def _(back=back):
                    y_copy(i - back).wait()


def _experts(xs, plan, nb, w1, w3, w2, layer, block):
    hbm = pl.BlockSpec(memory_space=pl.ANY)
    row_block = (block * ROW_TILES, LANES)
    return pl.pallas_call(
        functools.partial(_experts_kernel, block=block, layer=layer),
        grid_spec=pltpu.PrefetchScalarGridSpec(
            num_scalar_prefetch=4,
            grid=(nb,),
            in_specs=[hbm, hbm, hbm, hbm],
            out_specs=hbm,
            scratch_shapes=[pltpu.VMEM((D_MODEL, D_EXPERT), BF16), pltpu.VMEM((D_MODEL, D_EXPERT), BF16),
                            pltpu.VMEM((D_EXPERT, D_MODEL), BF16),
                            pltpu.VMEM((2, D_MODEL, D_EXPERT), F32), pltpu.VMEM((2, D_MODEL, D_EXPERT), F32),
                            pltpu.VMEM((2, D_EXPERT, D_MODEL), F32), pltpu.SemaphoreType.DMA((2,)),
                            pltpu.VMEM((X_BUFFERS,) + row_block, F32), pltpu.SemaphoreType.DMA((X_BUFFERS,)),
                            pltpu.VMEM((Y_BUFFERS,) + row_block, F32), pltpu.SemaphoreType.DMA((Y_BUFFERS,))]),
        out_shape=jax.ShapeDtypeStruct(xs.shape, F32),
        input_output_aliases={4: 0},
        compiler_params=_cparams(("arbitrary",)),
        name="moe_experts",
    )(*plan, xs, w1, w3, w2)


def _combine_kernel(slot_ref, slot_nxt_ref, y_ref, x_ref, mod_ref, rt_ref, o_ref, ybuf, sem, *, tc):
    i = pl.program_id(0)

    def gather(slots, buf):
        def issue(j, c):
            for u in range(DMA_UNROLL):
                t = j * DMA_UNROLL + u
                for k in range(TOP_K):
                    _row_copy(y_ref, slots[0, 0, k * tc + t], ybuf.at[buf], k * tc + t, sem.at[buf]).start(priority=k)
            return c
        lax.fori_loop(0, tc // DMA_UNROLL, issue, 0)

    @pl.when(i == 0)
    def _():
        gather(slot_ref, 0)

    cur = i % 2
    pltpu.make_async_copy(y_ref.at[pl.ds(0, TOP_K * tc * ROW_TILES), :], ybuf.at[cur], sem.at[cur]).wait()

    @pl.when(i + 1 < pl.num_programs(0))
    def _():
        gather(slot_nxt_ref, 1 - cur)

    rt = rt_ref[...]
    rows = ybuf.at[cur]
    y = (rt[:, 2:3] * _load_rows8(rows, tc) + rt[:, 3:4] * _load_rows8(rows, tc, base=tc * ROW_TILES))
    o_ref[...] = x_ref[...] + mod_ref[0][:, 5 * D_MODEL:6 * D_MODEL] * y


def _combine(slots, y, x, mod, tiles_per_mod, route, tc):
    T = x.shape[0]
    n = T // tc
    slot_spec = lambda f: pl.BlockSpec((1, 1, tc * TOP_K), f, memory_space=pltpu.SMEM)
    return pl.pallas_call(
        functools.partial(_combine_kernel, tc=tc),
        grid=(n,),
        in_specs=[slot_spec(lambda i: (i, 0, 0)), slot_spec(lambda i: (jnp.minimum(i + 1, n - 1), 0, 0)),
                  pl.BlockSpec(memory_space=pl.ANY),
                  pl.BlockSpec((tc, D_MODEL), lambda i: (i, 0)),
                  _mod_spec(mod, tiles_per_mod),
                  pl.BlockSpec((tc, LANES), lambda i: (i, 0))],
        out_specs=pl.BlockSpec((tc, D_MODEL), lambda i: (i, 0)),
        out_shape=jax.ShapeDtypeStruct((T, D_MODEL), F32),
        scratch_shapes=[pltpu.VMEM((2, TOP_K * tc * ROW_TILES, LANES), F32), pltpu.SemaphoreType.DMA((2,))],
        compiler_params=_cparams(("arbitrary",)),
        name="moe_combine",
    )(slots, slots, y, x, mod, route)


def _moe(part_a, part_b, counts, w1, w3, w2, layer):
    Ta = part_a[3].shape[0]
    slots, plan, zero_plan, nb = _layout(jnp.concatenate([part_a[2], part_b[2]], axis=1), counts, MOE_BLOCK)
    tiled = [_tile_slots(slots[:, :Ta], part_a[6]), _tile_slots(slots[:, Ta:], part_b[6])]
    xs = _scatter(tiled[0], tiled[1], part_a[0], part_b[0], zero_plan, nb, MOE_BLOCK)
    y = _experts(xs, plan, nb, w1, w3, w2, layer, MOE_BLOCK)
    return [_combine(sl, y, x, mod, tpm, rt, tt) for sl, (_, rt, _, x, mod, tpm, tt) in zip(tiled, (part_a, part_b))]


def _kvq_kernel(x_ref, modkv_ref, mod_ref, nkv_ref, nm_ref, wkv_ref, wq_ref, kn_ref, qn_ref, eavg_ref,
                *rest, tm, keep):
    residue = keep is not None
    x = x_ref[...]
    xn = x * lax.rsqrt(jnp.mean(x * x, axis=-1, keepdims=True) + EPS)
    modkv = modkv_ref[0]
    mod = mod_ref[0]
    hk = (xn * nkv_ref[...] * (1.0 + modkv[:, D_MODEL:]) + modkv[:, :D_MODEL]).astype(BF16)
    hq = (xn * nm_ref[...] * (1.0 + mod[:, D_MODEL:2 * D_MODEL]) + mod[:, :D_MODEL]).astype(BF16)
    eavg = eavg_ref[...]

    def head_norm(a, g):
        sq = (a * a).astype(BF16)
        half = ATTN_WIDTH // 2
        ms = jnp.concatenate([jnp.dot(sq[:, :half], eavg, preferred_element_type=F32),
                              jnp.dot(sq[:, half:], eavg, preferred_element_type=F32)], axis=1)
        return a * lax.rsqrt(ms + EPS) * g

    def residue_major(val, ref, dil, stage):
        if dil == 1:
            ref[0, 0] = val.astype(BF16)
            return
        for c in range(stage.shape[0]):
            stage[c] = val[:, c * LANES:(c + 1) * LANES]
        for r in range(dil):
            rows = [stage[c, pl.ds(r, tm // dil, stride=dil), :] for c in range(stage.shape[0])]
            ref[0, r] = jnp.concatenate(rows, axis=1).astype(BF16)

    for g, (_, dil) in enumerate(BRANCHES):
        k0 = 2 * g * ATTN_WIDTH
        kvg = jnp.dot(hk, wkv_ref[:, k0:k0 + 2 * ATTN_WIDTH], preferred_element_type=F32)
        kn = head_norm(kvg[:, :ATTN_WIDTH], kn_ref[g:g + 1, :])
        vv = kvg[:, ATTN_WIDTH:]
        q0 = g * ATTN_WIDTH
        qg = jnp.dot(hq, wq_ref[:, q0:q0 + ATTN_WIDTH], preferred_element_type=F32)
        qg = head_norm(qg, qn_ref[g:g + 1, :]) * (HEAD_DIM ** -0.5)
        if residue:
            stages = rest[4 * N_BRANCH:]
            residue_major(qg, rest[3 * g], dil, stages[3 * g])
            residue_major(kn, rest[3 * g + 1], dil, stages[3 * g + 1])
            residue_major(vv, rest[3 * g + 2], dil, stages[3 * g + 2])
            kt_ref = rest[3 * N_BRANCH + g]
            kt_ref[0, :ATTN_WIDTH, :] = kn.T
            kt_ref[0, ATTN_WIDTH:, :] = vv.T
        else:
            rest[0][:, k0:k0 + ATTN_WIDTH] = kn
            rest[0][:, k0 + ATTN_WIDTH:k0 + 2 * ATTN_WIDTH] = vv
            rest[1][:, q0:q0 + ATTN_WIDTH] = qg


def _kvq(x, modkv, mod, tiles_per_mod, tm, nkv, nm, wkv, wq, kn, qn, eavg, batch=None):
    T = x.shape[0]
    tok = lambda w: pl.BlockSpec((tm, w), lambda i: (i, 0))
    kvw = 2 * N_BRANCH * ATTN_WIDTH
    if batch is not None:
        L = T // batch
        tpb = L // tm
        kept = tuple(-(-min(w, L) // tm) for w, _ in BRANCHES)
        keep = (tpb, kept)
        out_specs, out_shape = [], []
        for _, dil in BRANCHES:
            for _ in range(3):
                out_specs.append(pl.BlockSpec((1, dil, tm // dil, ATTN_WIDTH), lambda i: (i // tpb, 0, i % tpb, 0)))
                out_shape.append(jax.ShapeDtypeStruct((batch, dil, L // dil, ATTN_WIDTH), BF16))
        for kg in kept:
            out_specs.append(pl.BlockSpec((1, 2 * ATTN_WIDTH, tm),
                                          lambda i, kg=kg: (i // tpb, 0, jnp.maximum(i % tpb - (tpb - kg), 0))))
            out_shape.append(jax.ShapeDtypeStruct((batch, 2 * ATTN_WIDTH, kg * tm), F32))
        scratch = [pltpu.VMEM((ATTN_WIDTH // LANES, tm if dil > 1 else 8, LANES), F32)
                   for _, dil in BRANCHES for _ in range(3)]
    else:
        keep = None
        out_specs = [tok(kvw), tok(N_BRANCH * ATTN_WIDTH)]
        out_shape = [jax.ShapeDtypeStruct((T, kvw), F32), jax.ShapeDtypeStruct((T, N_BRANCH * ATTN_WIDTH), F32)]
        scratch = []
    return pl.pallas_call(
        functools.partial(_kvq_kernel, tm=tm, keep=keep),
        grid=(T // tm,),
        in_specs=[tok(D_MODEL), _mod_spec(modkv, tiles_per_mod), _mod_spec(mod, tiles_per_mod),
                  _resident(nkv.shape), _resident(nm.shape), _resident(wkv.shape), _resident(wq.shape),
                  _resident(kn.shape), _resident(qn.shape), _resident(eavg.shape)],
        out_specs=out_specs,
        out_shape=out_shape,
        scratch_shapes=scratch,
        compiler_params=_cparams(("arbitrary",)),
        name="kvq",
    )(x, modkv, mod, nkv, nm, wkv, wq, kn, qn, eavg)


BAND_QBLOCKS = 8


def _band_attn_kernel(q_ref, kp_ref, kc_ref, vp_ref, vc_ref, bias_ref, o_ref, lse_ref):
    n = pl.program_id(2)
    hw = ATTN_WIDTH // 2
    hh = HEADS // 2
    col = lax.broadcasted_iota(jnp.int32, (hh * N_STEPS, 2 * N_STEPS), 1)
    first = jnp.logical_or(col >= N_STEPS, n > 0)
    lane = lax.broadcasted_iota(jnp.int32, (N_STEPS, LANES), 1)
    head_of = lax.broadcasted_iota(jnp.int32, (N_STEPS, hw), 1) // HEAD_DIM
    zero = jnp.zeros((N_STEPS, hw), BF16)
    nt = (((1,), (1,)), ((), ()))
    for r, j in [(r, j) for r in range(q_ref.shape[0]) for j in range(q_ref.shape[1] // N_STEPS)]:
        if j == 0:
            k_all = jnp.concatenate([kp_ref[r], kc_ref[r]], axis=0)
            v_all = jnp.concatenate([vp_ref[r], vc_ref[r]], axis=0)
        rows = slice(j * N_STEPS, (j + 1) * N_STEPS)
        q = q_ref[r, rows, :]
        k2 = k_all[j * N_STEPS:(j + 2) * N_STEPS]
        v2 = v_all[j * N_STEPS:(j + 2) * N_STEPS]
        lse_all = jnp.zeros((N_STEPS, LANES), F32)
        halves = []
        for c in range(2):
            cols = slice(c * hw, (c + 1) * hw)
            qc = q[:, cols]
            qbd = jnp.concatenate([jnp.where(head_of == h, qc, zero) for h in range(hh)], axis=0)
            s = (lax.dot_general(qbd, k2[:, cols], nt, preferred_element_type=F32)
                 + bias_ref[c * hh * N_STEPS:(c + 1) * hh * N_STEPS, :])
            if j == 0:
                s = jnp.where(first, s, NEG)
            m = jnp.max(s, axis=1, keepdims=True)
            p = jnp.exp(s - m)
            l = jnp.sum(p, axis=1, keepdims=True)
            pv = jnp.dot(p.astype(BF16), v2[:, cols], preferred_element_type=F32) / l
            lse = m + jnp.log(l)
            o = jnp.zeros((N_STEPS, hw), F32)
            for h in range(hh):
                blk = slice(h * N_STEPS, (h + 1) * N_STEPS)
                o = jnp.where(head_of == h, pv[blk], o)
                lse_all = jnp.where(lane == c * hh + h, lse[blk], lse_all)
            halves.append(o)
        o_ref[r, rows, :] = jnp.concatenate(halves, axis=1)
        lse_ref[r, rows, :] = lse_all


def _band_attn(q, k, v, bias, dil):
    B, _, Ld, _ = q.shape
    nq = min(BAND_QBLOCKS, Ld // N_STEPS)
    nr = min(BAND_QBLOCKS // nq, dil)
    step = nq * N_STEPS
    blk = lambda w, f: pl.BlockSpec((None, nr, step, w), f)
    cur = lambda b, r, n: (b, r, n, 0)
    prev_blk = pl.BlockSpec((None, nr, N_STEPS, ATTN_WIDTH), lambda b, r, n: (b, r, jnp.maximum(nq * n - 1, 0), 0))
    return pl.pallas_call(
        _band_attn_kernel,
        grid=(B, dil // nr, Ld // step),
        in_specs=[blk(ATTN_WIDTH, cur), prev_blk, blk(ATTN_WIDTH, cur), prev_blk,
                  blk(ATTN_WIDTH, cur), _resident(bias.shape)],
        out_specs=[blk(ATTN_WIDTH, cur), blk(LANES, cur)],
        out_shape=[jax.ShapeDtypeStruct((B, dil, Ld, ATTN_WIDTH), F32),
                   jax.ShapeDtypeStruct((B, dil, Ld, LANES), F32)],
        compiler_params=_cparams(("arbitrary", "arbitrary", "arbitrary")),
        name=f"band_attn_d{dil}",
    )(q, k, k, v, v, bias)


def _rel_buckets(dilation):
    n = np.arange(N_STEPS + 1) * dilation
    large = MAX_EXACT + (np.log(np.maximum(n, 1) / MAX_EXACT) / np.log(REL_MAX_DIST / MAX_EXACT)
                         * (NUM_BUCKETS - MAX_EXACT)).astype(np.int32)
    return np.where(n < MAX_EXACT, n, np.minimum(large, NUM_BUCKETS - 1)).astype(np.int32)


def _step_bias_row(rel_bias, g, dil):
    onehot = np.zeros((N_STEPS + 1, NUM_BUCKETS), np.float32)
    onehot[np.arange(N_STEPS + 1), _rel_buckets(dil)] = 1.0
    return jnp.dot(jnp.asarray(onehot), rel_bias[:, g, :].astype(F32), precision=lax.Precision.HIGHEST).T


def _band_bias(rel_bias, g, dil):
    bias = _step_bias_row(rel_bias, g, dil)
    P = 3 * N_STEPS
    neg = jnp.full((HEADS, N_STEPS), NEG, F32)
    ext = jnp.concatenate([neg, bias[:, ::-1], neg], axis=1)
    flat = jnp.broadcast_to(ext[:, None, :], (HEADS, N_STEPS, P + 1)).reshape(HEADS, N_STEPS * (P + 1))
    skew = flat[:, :N_STEPS * P].reshape(HEADS, N_STEPS, P)
    return skew[:, :, N_STEPS:].reshape(HEADS * N_STEPS, 2 * N_STEPS)


def _step_attn_kernel(q_ref, kvn_ref, c0_ref, c1_ref, c2_ref, b0_ref, b1_ref, b2_ref, bn_ref, o_ref, *, n_new):
    caches = (c0_ref, c1_ref, c2_ref)
    cbias = (b0_ref, b1_ref, b2_ref)
    nt = (((1,), (1,)), ((), ()))
    rows = n_new * HEADS
    row = lax.broadcasted_iota(jnp.int32, (rows, ATTN_WIDTH), 0)
    lane = lax.broadcasted_iota(jnp.int32, (rows, ATTN_WIDTH), 1)
    own_head = lane // HEAD_DIM == row % HEADS
    parts = []
    for g in range(N_BRANCH):
        klo = 2 * g * ATTN_WIDTH
        qg = q_ref[0, :, g * ATTN_WIDTH:(g + 1) * ATTN_WIDTH]
        qrep = jnp.concatenate([jnp.broadcast_to(qg[s:s + 1, :], (HEADS, ATTN_WIDTH)) for s in range(n_new)], axis=0)
        qbd = jnp.where(own_head, qrep, 0.0).astype(BF16)
        kn = kvn_ref[0, :, klo:klo + ATTN_WIDTH].astype(BF16)
        vn = kvn_ref[0, :, klo + ATTN_WIDTH:klo + 2 * ATTN_WIDTH].astype(BF16)
        kt = caches[g][0, 0].astype(BF16)
        vt = caches[g][0, 1].astype(BF16)
        sc = jnp.dot(qbd, kt, preferred_element_type=F32) + cbias[g][...]
        sn = lax.dot_general(qbd, kn, nt, preferred_element_type=F32) + bn_ref[g]
        m = jnp.maximum(jnp.max(sc, axis=1, keepdims=True), jnp.max(sn, axis=1, keepdims=True))
        pc = jnp.exp(sc - m)
        pn = jnp.exp(sn - m)
        l = jnp.sum(pc, axis=1, keepdims=True) + jnp.sum(pn, axis=1, keepdims=True)
        pv = (lax.dot_general(pc.astype(BF16), vt, nt, preferred_element_type=F32)
              + jnp.dot(pn.astype(BF16), vn, preferred_element_type=F32))
        parts.append((m + jnp.log(l), pv / l))
    mx = jnp.maximum(jnp.maximum(parts[0][0], parts[1][0]), parts[2][0])
    es = [jnp.exp(lse - mx) for lse, _ in parts]
    den = es[0] + es[1] + es[2]
    mixed = (es[0] * parts[0][1] + es[1] * parts[1][1] + es[2] * parts[2][1]) / den
    mixed = jnp.where(own_head, mixed, 0.0)
    for s in range(n_new):
        o_ref[0, s:s + 1, :] = jnp.sum(mixed[s * HEADS:(s + 1) * HEADS], axis=0, keepdims=True)


def _step_bias(rel_bias, n_new):
    cache_tabs, new_tabs = [], []
    for g, (win, dil) in enumerate(BRANCHES):
        bias = _step_bias_row(rel_bias, g, dil)
        rev = bias[:, ::-1]
        per_s = []
        for s in range(n_new):
            if dil == 1:
                neg = jnp.full((HEADS, s), NEG, F32)
                per_s.append(jnp.concatenate([neg, rev[:, :win - s]], axis=1))
            else:
                cols = [rev[:, :N_STEPS] if r == s % dil else jnp.full((HEADS, N_STEPS), NEG, F32)
                        for r in range(dil)]
                per_s.append(jnp.stack(cols, axis=2).reshape(HEADS, win))
        cache_tabs.append(jnp.stack(per_s, axis=0).reshape(n_new * HEADS, win))
        dist = np.arange(n_new)[:, None] - np.arange(8)[None, :]
        ok = (dist >= 0) & (dist % dil == 0) & (np.arange(8)[None, :] < n_new)
        onehot = np.zeros((N_STEPS + 1, n_new * 8), np.float32)
        onehot[np.where(ok, dist // dil, 0).reshape(-1), np.arange(n_new * 8)] = 1.0
        tab = jnp.dot(bias, jnp.asarray(onehot), precision=lax.Precision.HIGHEST).reshape(HEADS, n_new, 8)
        new_tabs.append(jnp.where(ok[None], tab, NEG).transpose(1, 0, 2).reshape(n_new * HEADS, 8))
    return cache_tabs, jnp.stack(new_tabs)


def _step_attn(q, kvn, caches, cache_bias, new_bias, DB, S):
    cspec = lambda c: pl.BlockSpec((1,) + c.shape[1:], lambda b: (b, 0, 0, 0))
    return pl.pallas_call(
        functools.partial(_step_attn_kernel, n_new=S),
        grid=(DB,),
        in_specs=[pl.BlockSpec((1, S, N_BRANCH * ATTN_WIDTH), lambda b: (b, 0, 0)),
                  pl.BlockSpec((1, 8, 2 * N_BRANCH * ATTN_WIDTH), lambda b: (b, 0, 0)),
                  cspec(caches[0]), cspec(caches[1]), cspec(caches[2]),
                  _resident(cache_bias[0].shape), _resident(cache_bias[1].shape), _resident(cache_bias[2].shape),
                  _resident(new_bias.shape)],
        out_specs=pl.BlockSpec((1, S, ATTN_WIDTH), lambda b: (b, 0, 0)),
        out_shape=jax.ShapeDtypeStruct((DB, S, ATTN_WIDTH), F32),
        compiler_params=_cparams(("arbitrary",)),
        name="step_attn",
    )(q, kvn, caches[0], caches[1], caches[2], *cache_bias, new_bias)


def _attn_out_kernel(*refs, dils, tm):
    n_o = max(len(dils), 1)
    n_in = n_o + len(dils) + 9
    (x_ref, mod_ref, wo_ref, ex_ref, nf_ref, wr_ref, br_ref, cin_ref, tri_ref,
     x3_ref, h3_ref, rt_ref, rtt_ref, cnt_ref, carry) = refs[n_in - 9:n_in + 6]
    stages = refs[n_in + 6:]
    _start_counts(cin_ref, carry)

    def natural(ref, stage, dil):
        if dil == 1:
            return ref[0, 0]
        for r in range(dil):
            val = ref[0, r]
            for c in range(stage.shape[0]):
                stage[c, pl.ds(r, tm // dil, stride=dil), :] = val[:, c * LANES:(c + 1) * LANES]
        return jnp.concatenate([stage[c] for c in range(stage.shape[0])], axis=1)

    if dils:
        lses = [natural(refs[n_o + g], stages[2 * g + 1], dil) for g, dil in enumerate(dils)]
        mx = functools.reduce(jnp.maximum, lses)
        es = [jnp.exp(l - mx) for l in lses]
        den = functools.reduce(lambda a, b: a + b, es)
        o = None
        for g, dil in enumerate(dils):
            term = _split_dot(es[g] / den, ex_ref[...]) * natural(refs[g], stages[2 * g], dil)
            o = term if o is None else o + term
    else:
        o = refs[0][...]
    mod = mod_ref[0]
    a = jnp.dot(o.astype(BF16), wo_ref[...], preferred_element_type=F32)
    x3 = x_ref[...] + mod[:, 2 * D_MODEL:3 * D_MODEL] * a
    x3_ref[...] = x3
    _ffn_pre(x3, mod, nf_ref, wr_ref, br_ref, tri_ref, h3_ref, rt_ref, rtt_ref, cnt_ref, carry, tm)


def _attn_out(os_, lses, x, mod, tiles_per_mod, tm, wo, ex, nf, wr, br, counts):
    T = x.shape[0]
    tok = lambda w: pl.BlockSpec((tm, w), lambda i: (i, 0))
    r_specs, r_shapes = _route_outs(T, tm)
    tri = _earlier_rows(tm)
    if lses:
        dils = tuple(o.shape[1] for o in os_)
        tpb = tiles_per_mod
        res = lambda a: pl.BlockSpec((1, a.shape[1], tm // a.shape[1], a.shape[3]), lambda i: (i // tpb, 0, i % tpb, 0))
        o_specs = [res(a) for a in os_] + [res(a) for a in lses]
        scratch = []
        for _ in dils:
            scratch += [pltpu.VMEM((ATTN_WIDTH // LANES, tm, LANES), F32), pltpu.VMEM((1, tm, LANES), F32)]
    else:
        dils, o_specs, scratch = (), [tok(ATTN_WIDTH)], []
    return pl.pallas_call(
        functools.partial(_attn_out_kernel, dils=dils, tm=tm),
        grid=(T // tm,),
        in_specs=(o_specs + [tok(D_MODEL), _mod_spec(mod, tiles_per_mod), _resident(wo.shape), _resident(ex.shape),
                             _resident(nf.shape), _resident(wr.shape), _resident(br.shape),
                             _resident(counts.shape), _resident(tri.shape)]),
        out_specs=[tok(D_MODEL), pl.BlockSpec((tm * ROW_TILES, LANES), lambda i: (i, 0))] + r_specs,
        out_shape=[jax.ShapeDtypeStruct((T, D_MODEL), F32),
                   jax.ShapeDtypeStruct((T * ROW_TILES, LANES), F32)] + r_shapes,
        scratch_shapes=[pltpu.VMEM((1, LANES), F32)] + scratch,
        compiler_params=_cparams(("arbitrary",)),
        name="attn_out",
    )(*os_, *lses, x, mod, wo, ex, nf, wr, br, counts, tri)


def kernel(x_prompt, x_sample, cache_kv_w128, cache_kv_w512, cache_kv_w2048, c_prompt, c_sample, ada_w, ada_b, norm_mix, norm_ffn, a_w_in, a_b_in, a_norm_v, a_w_s, a_b_s, a_w_out, kv_ada_w, kv_ada_b, kv_norm, w_kv, k_norm, rel_bias, b_w_q, q_norm, b_w_o, r_w_group, r_b_group, r_w_expert, r_b_expert, e_w1, e_w3, e_w2):
    B, L, _ = x_prompt.shape
    DB, S, _ = x_sample.shape
    Tp, Ts = B * L, DB * S
    tm = TOKEN_TILE
    tpm_p = L // tm

    c_all = jnp.concatenate([c_prompt, c_sample], axis=0)
    R = c_all.shape[0]
    c_all = jnp.pad(c_all, ((0, -R % 8), (0, 0)))
    mods = [_ada(c_all, ada_w, ada_b, l) for l in range(2)]
    modkv = _ada(c_all, kv_ada_w[None], kv_ada_b[None], 0)

    def split_mod(m):
        return m[:B, None, :], jnp.repeat(m[B:B + DB], S, axis=0)[None]
    mod_p, mod_s = zip(*[split_mod(m) for m in mods])
    modkv_p, modkv_s = split_mod(modkv)

    row = lambda a: a.reshape(1, -1)

    def router(l):
        wr = jnp.zeros((D_MODEL, LANES), F32)
        wr = wr.at[:, :N_GROUPS].set(r_w_group[l]).at[:, N_GROUPS:N_GROUPS + N_EXPERTS].set(r_w_expert[l])
        br = jnp.zeros((1, LANES), F32)
        br = br.at[0, :N_GROUPS].set(r_b_group[l]).at[0, N_GROUPS:N_GROUPS + N_EXPERTS].set(r_b_expert[l])
        return wr.astype(BF16), br

    no_counts = jnp.zeros((1, LANES), F32)

    win = a_w_in[0].astype(BF16)
    wout = a_w_out[0].astype(BF16)
    tril = jnp.tril(jnp.ones((CHUNK, CHUNK), bool))
    ws_p = jnp.where(tril, a_w_s[0], 0).astype(BF16)
    bs_p = a_b_s[0].T
    cs = min(CHUNK, S)
    ws_small = jnp.where(jnp.tril(jnp.ones((cs, cs), bool)), a_w_s[0][:, :cs, :cs], 0)
    ws_s = jnp.stack([jnp.kron(jnp.eye(Ts // cs, dtype=F32), ws_small[g]) for g in range(GMLP_GROUPS)]).astype(BF16)
    bs_s = jnp.tile(a_b_s[0][:, :cs], (1, Ts // cs)).T
    wr0, br0 = router(0)
    common = (row(norm_mix[0]), win, row(a_b_in[0]), row(a_norm_v[0]))
    xp = x_prompt.reshape(Tp, D_MODEL)
    xs_ = x_sample.reshape(Ts, D_MODEL)
    x1_p, h2_p, rt_p, rtt_p, cnt = _mixer_a(xp, mod_p[0], tpm_p, tm, *common, ws_p, bs_p, wout, row(norm_ffn[0]),
                                            wr0, br0, no_counts, with_v=False)
    x1_s, h2_s, rt_s, rtt_s, cnt, v_s = _mixer_a(xs_, mod_s[0], 1, Ts, *common, ws_s, bs_s, wout, row(norm_ffn[0]),
                                                 wr0, br0, cnt, with_v=True)
    x2_p, x2_s = _moe((h2_p, rt_p, rtt_p, x1_p, mod_p[0], tpm_p, tm), (h2_s, rt_s, rtt_s, x1_s, mod_s[0], 1, Ts), cnt,
                      e_w1, e_w3, e_w2, 0)

    wkv = w_kv.astype(BF16)
    wq = b_w_q[0].astype(BF16)
    kn = jnp.tile(k_norm, (1, HEADS))
    qn = jnp.tile(q_norm[0], (1, HEADS))
    head = np.arange(ATTN_WIDTH) // HEAD_DIM
    half_head = head[:ATTN_WIDTH // 2]
    eavg = jnp.asarray((half_head[:, None] == half_head[None, :]) / HEAD_DIM, BF16)
    kvq_w = (row(kv_norm), row(norm_mix[1]), wkv, wq, kn, qn, eavg)
    *qkv_p, kt0, kt1, kt2 = _kvq(x2_p, modkv_p, mod_p[1], tpm_p, tm, *kvq_w, batch=B)
    kv_s, q_s = _kvq(x2_s, modkv_s, mod_s[1], 1, Ts, *kvq_w)

    os_, lses = [], []
    for g, (_, dil) in enumerate(BRANCHES):
        o, lse = _band_attn(*qkv_p[3 * g:3 * g + 3], _band_bias(rel_bias, g, dil), dil)
        os_.append(o)
        lses.append(lse)
    caches = [jnp.transpose(c, (0, 2, 3, 4, 1)).reshape(DB, 2, ATTN_WIDTH, c.shape[1])
              for c in (cache_kv_w128, cache_kv_w512, cache_kv_w2048)]
    kvn_s = jnp.pad(kv_s.reshape(DB, S, -1), ((0, 0), (0, 8 - S), (0, 0)))
    o_s = _step_attn(q_s.reshape(DB, S, -1), kvn_s, caches, *_step_bias(rel_bias, S), DB, S)

    wo = b_w_o[0].astype(BF16)
    ex = jnp.asarray(np.arange(LANES)[:, None] == head[None, :], BF16)
    wr1, br1 = router(1)
    x3_p, h3_p, rt1_p, rtt1_p, cnt1 = _attn_out(os_, lses, x2_p, mod_p[1], tpm_p, tm, wo, ex, row(norm_ffn[1]),
                                                wr1, br1, no_counts)
    x3_s, h3_s, rt1_s, rtt1_s, cnt1 = _attn_out([o_s.reshape(Ts, ATTN_WIDTH)], [], x2_s, mod_s[1], 1, Ts, wo, ex,
                                                row(norm_ffn[1]), wr1, br1, cnt1)
    y_p, y_s = _moe((h3_p, rt1_p, rtt1_p, x3_p, mod_p[1], tpm_p, tm), (h3_s, rt1_s, rtt1_s, x3_s, mod_s[1], 1, Ts),
                    cnt1, e_w1, e_w3, e_w2, 1)

    kv_s4 = kv_s.reshape(DB, S, N_BRANCH, 2, HEADS, HEAD_DIM)

    def window(kt, w):
        n = min(w, L)
        return jnp.transpose(kt[:, :, kt.shape[2] - n:].reshape(B, 2, HEADS, HEAD_DIM, n), (0, 4, 1, 2, 3))
    return (y_p.reshape(B, L, D_MODEL), y_s.reshape(DB, S, D_MODEL),
            window(kt0, BRANCHES[0][0]), window(kt1, BRANCHES[1][0]), window(kt2, BRANCHES[2][0]),
            kv_s4[:, :, 0], kv_s4[:, :, 1], kv_s4[:, :, 2],
            v_s.reshape(1, DB, S, GMLP_WIDTH))
```

```python
import functools

import numpy as np
import jax
import jax.numpy as jnp
from jax import lax
from jax.experimental import pallas as pl
from jax.experimental.pallas import tpu as pltpu

F32 = jnp.float32
BF16 = jnp.bfloat16

D_MODEL = 1024
GMLP_WIDTH = 2048
GMLP_GROUPS = 4
GROUP_WIDTH = GMLP_WIDTH // GMLP_GROUPS
CHUNK = 128
BRANCHES = ((128, 1), (512, 4), (2048, 16))
N_BRANCH = 3
N_STEPS = 128
HEADS = 8
HEAD_DIM = 64
ATTN_WIDTH = HEADS * HEAD_DIM
NUM_BUCKETS = 32
MAX_EXACT = NUM_BUCKETS // 2
REL_MAX_DIST = 2048
N_GROUPS = 4
EXPERTS_PER_GROUP = 8
N_EXPERTS = N_GROUPS * EXPERTS_PER_GROUP
TOP_K = 2
D_EXPERT = 512
EPS = 1e-6
NEG = -1e30

LANES = 128
ROW_TILES = D_MODEL // LANES
TOKEN_TILE = 512
MOE_BLOCK = 256
DMA_UNROLL = 16
VMEM_LIMIT = 52 * 1024 * 1024


def _cparams(sem):
    return pltpu.CompilerParams(dimension_semantics=sem, vmem_limit_bytes=VMEM_LIMIT)


def _resident(shape):
    nd = len(shape)
    return pl.BlockSpec(shape, lambda *_, _nd=nd: (0,) * _nd, pipeline_mode=pl.Buffered(1))


def _gelu_tanh(x):
    return 0.5 * x * (1.0 + jnp.tanh(0.7978845608028654 * (x + 0.044715 * (x * x * x))))


def _rms(x, g):
    return x * lax.rsqrt(jnp.mean(x * x, axis=-1, keepdims=True) + EPS) * g


def _store_rows8(ref, val, n, base=0):
    for s in range(ROW_TILES):
        ref[pl.ds(base + s, n, stride=ROW_TILES), :] = val[:, s * LANES:(s + 1) * LANES]


def _load_rows8(ref, n, base=0):
    return jnp.concatenate([ref[pl.ds(base + s, n, stride=ROW_TILES), :] for s in range(ROW_TILES)], axis=1)


def _split(a):
    hi = a.astype(BF16)
    return hi, (a - hi.astype(F32)).astype(BF16)


def _split_dot(a, e_bf16):
    hi, lo = _split(a)
    return (jnp.dot(hi, e_bf16, preferred_element_type=F32) + jnp.dot(lo, e_bf16, preferred_element_type=F32))


ROUTE_LANE0 = 4


def _route_rows(l, tri_ref, carry):
    lane = lax.broadcasted_iota(jnp.int32, l.shape, 1).astype(F32)
    far = float(LANES)

    def first_lane(mask):
        return jnp.min(jnp.where(mask, lane, far), axis=1, keepdims=True)

    is_g = lane < N_GROUPS
    gl = jnp.where(is_g, l, NEG)
    gmax = jnp.max(gl, axis=1, keepdims=True)
    g_i = first_lane(jnp.logical_and(gl == gmax, is_g))
    g_p = 1.0 / jnp.sum(jnp.where(is_g, jnp.exp(gl - gmax), 0.0), axis=1, keepdims=True)
    lo = ROUTE_LANE0 + EXPERTS_PER_GROUP * g_i
    sel = jnp.logical_and(lane >= lo, lane < lo + EXPERTS_PER_GROUP)
    el = jnp.where(sel, l, NEG)
    m1 = jnp.max(el, axis=1, keepdims=True)
    i1 = first_lane(jnp.logical_and(el == m1, sel))
    sel2 = jnp.logical_and(sel, lane != i1)
    el2 = jnp.where(sel2, l, NEG)
    m2 = jnp.max(el2, axis=1, keepdims=True)
    i2 = first_lane(jnp.logical_and(el2 == m2, sel2))
    r = jnp.exp(m2 - m1)
    w1 = g_p / (1.0 + r)
    w2 = g_p * r / (1.0 + r)

    hit1 = lane == i1
    hit2 = lane == i2
    onehot = jnp.where(jnp.logical_or(hit1, hit2), 1.0, 0.0)
    before = carry[...] + jnp.dot(tri_ref[...], onehot.astype(BF16), preferred_element_type=F32)
    rank1 = jnp.sum(jnp.where(hit1, before, 0.0), axis=1, keepdims=True)
    rank2 = jnp.sum(jnp.where(hit2, before, 0.0), axis=1, keepdims=True)
    carry[...] = carry[...] + jnp.sum(onehot, axis=0, keepdims=True)

    out = jnp.zeros(l.shape, F32)
    for k, val in enumerate((i1 - ROUTE_LANE0, i2 - ROUTE_LANE0, w1, w2, rank1, rank2)):
        out = jnp.where(lane == k, val, out)
    return out


ROUTE_FIELDS = 8


def _start_counts(cin_ref, carry):
    @pl.when(pl.program_id(0) == 0)
    def _():
        carry[...] = cin_ref[...]


def _ffn_pre(x, mod, nf_ref, wr_ref, br_ref, tri_ref, h_ref, rt_ref, rtt_ref, cnt_ref, carry, n):
    h = _rms(x, nf_ref[...]) * (1.0 + mod[:, 4 * D_MODEL:5 * D_MODEL]) + mod[:, 3 * D_MODEL:4 * D_MODEL]
    _store_rows8(h_ref, h, n)
    logits = jnp.dot(h.astype(BF16), wr_ref[...], preferred_element_type=F32) + br_ref[...]
    route = _route_rows(logits, tri_ref, carry)
    rt_ref[...] = route
    rtt_ref[...] = route.T[:ROUTE_FIELDS, :]
    cnt_ref[...] = carry[...]


def _ada_kernel(c_ref, w_ref, b_ref, o_ref):
    c = c_ref[...]
    a = (c * jax.nn.sigmoid(c)).astype(BF16)
    o_ref[...] = jnp.dot(a, w_ref[...].astype(BF16), preferred_element_type=F32) + b_ref[...]


def _ada(c, w, b, layer):
    R = c.shape[0]
    N = w.shape[2]
    tn = 1024
    return pl.pallas_call(
        _ada_kernel,
        grid=(N // tn,),
        in_specs=[pl.BlockSpec((R, D_MODEL), lambda j: (0, 0)),
                  pl.BlockSpec((None, D_MODEL, tn), lambda j: (layer, 0, j)),
                  pl.BlockSpec((None, 1, tn), lambda j: (layer, 0, j))],
        out_specs=pl.BlockSpec((R, tn), lambda j: (0, j)),
        out_shape=jax.ShapeDtypeStruct((R, N), F32),
        compiler_params=_cparams(("arbitrary",)),
        name="ada",
    )(c, w, b.reshape(b.shape[0], 1, N))


def _mixer_a_kernel(x_ref, mod_ref, nm_ref, win_ref, bin_ref, gv_ref, ws_ref, bs_ref, wout_ref,
                    nf_ref, wr_ref, br_ref, cin_ref, tri_ref, x1_ref, h2_ref, rt_ref, rtt_ref, cnt_ref, *rest, tm):
    v_refs, carry = rest[:-1], rest[-1]
    _start_counts(cin_ref, carry)
    x = x_ref[...]
    mod = mod_ref[0]
    h = (_rms(x, nm_ref[...]) * (1.0 + mod[:, D_MODEL:2 * D_MODEL]) + mod[:, 0:D_MODEL]).astype(BF16)
    zv = jnp.dot(h, win_ref[:, GMLP_WIDTH:], preferred_element_type=F32) + bin_ref[:, GMLP_WIDTH:]
    v = _rms(_gelu_tanh(zv), gv_ref[...])
    if v_refs:
        v_refs[0][...] = v
    vb = v.astype(BF16)
    bs = bs_ref[...]
    acc = jnp.zeros((tm, D_MODEL), F32)
    for g in range(GMLP_GROUPS):
        lo, hi = g * GROUP_WIDTH, (g + 1) * GROUP_WIDTH
        u = _gelu_tanh(jnp.dot(h, win_ref[:, lo:hi], preferred_element_type=F32) + bin_ref[:, lo:hi])
        wg = ws_ref[g]
        gate = jnp.concatenate(
            [jnp.dot(wg, vb[c * CHUNK:(c + 1) * CHUNK, lo:hi], preferred_element_type=F32) + bs[:, g:g + 1]
             for c in range(tm // CHUNK)], axis=0)
        acc = acc + jnp.dot((u * gate).astype(BF16), wout_ref[lo:hi, :], preferred_element_type=F32)
    x1 = x + mod[:, 2 * D_MODEL:3 * D_MODEL] * acc
    x1_ref[...] = x1
    _ffn_pre(x1, mod, nf_ref, wr_ref, br_ref, tri_ref, h2_ref, rt_ref, rtt_ref, cnt_ref, carry, tm)


def _mod_spec(mod, tiles_per_mod):
    _, rows, width = mod.shape
    return pl.BlockSpec((1, rows, width), lambda i: (i // tiles_per_mod, 0, 0))


def _earlier_rows(tm):
    return jnp.asarray(np.tril(np.ones((tm, tm), np.float32), -1), BF16)


def _route_outs(T, tm):
    return ([pl.BlockSpec((tm, LANES), lambda i: (i, 0)), pl.BlockSpec((ROUTE_FIELDS, tm), lambda i: (0, i)),
             pl.BlockSpec((1, LANES), lambda i: (0, 0))],
            [jax.ShapeDtypeStruct((T, LANES), F32), jax.ShapeDtypeStruct((ROUTE_FIELDS, T), F32),
             jax.ShapeDtypeStruct((1, LANES), F32)])


def _mixer_a(x, mod, tiles_per_mod, tm, nm, win, bin_, gv, ws, bs_t, wout, nf, wr, br, counts, with_v):
    T = x.shape[0]
    tok = lambda w: pl.BlockSpec((tm, w), lambda i: (i, 0))
    r_specs, r_shapes = _route_outs(T, tm)
    out_shape = [jax.ShapeDtypeStruct((T, D_MODEL), F32), jax.ShapeDtypeStruct((T * ROW_TILES, LANES), F32)] + r_shapes
    out_specs = [tok(D_MODEL), pl.BlockSpec((tm * ROW_TILES, LANES), lambda i: (i, 0))] + r_specs
    if with_v:
        out_shape.append(jax.ShapeDtypeStruct((T, GMLP_WIDTH), F32))
        out_specs.append(tok(GMLP_WIDTH))
    tri = _earlier_rows(tm)
    return pl.pallas_call(
        functools.partial(_mixer_a_kernel, tm=tm),
        grid=(T // tm,),
        in_specs=[tok(D_MODEL), _mod_spec(mod, tiles_per_mod), _resident(nm.shape), _resident(win.shape),
                  _resident(bin_.shape), _resident(gv.shape), _resident(ws.shape), _resident(bs_t.shape),
                  _resident(wout.shape), _resident(nf.shape), _resident(wr.shape), _resident(br.shape),
                  _resident(counts.shape), _resident(tri.shape)],
        out_specs=out_specs,
        out_shape=out_shape,
        scratch_shapes=[pltpu.VMEM((1, LANES), F32)],
        compiler_params=_cparams(("arbitrary",)),
        name="mixer_a",
    )(x, mod, nm, win, bin_, gv, ws, bs_t, wout, nf, wr, br, counts, tri)


def _layout(rtt, counts, block):
    n_tokens = rtt.shape[1]
    counts = counts[0, ROUTE_LANE0:ROUTE_LANE0 + N_EXPERTS].astype(jnp.int32)
    padded = (counts + block - 1) // block * block
    pad_end = jnp.cumsum(padded)
    pad_start = (pad_end - padded).astype(F32)
    experts, ranks = rtt[0:TOP_K], rtt[4:4 + TOP_K]
    onehot = (experts[None] == jnp.arange(N_EXPERTS, dtype=F32)[:, None, None]).astype(F32)
    base = jnp.einsum('e,ekt->kt', pad_start, onehot, precision=lax.Precision.HIGHEST)
    slots = (ranks + base).astype(jnp.int32)
    nb = -(-n_tokens * TOP_K // block) + N_EXPERTS
    blk_e = jnp.minimum(jnp.sum(pad_end[None, :] <= (jnp.arange(nb, dtype=jnp.int32) * block)[:, None], axis=1),
                        N_EXPERTS - 1).astype(jnp.int32)
    n_used = (pad_end[-1] // block).astype(jnp.int32).reshape(1)
    starts = jnp.concatenate([jnp.ones((1,), jnp.int32), (blk_e[1:] != blk_e[:-1]).astype(jnp.int32)])
    grp = jnp.cumsum(starts) - 1
    end_blk = (pad_end // block).astype(jnp.int32)
    own = blk_e[:, None] == jnp.arange(N_EXPERTS, dtype=jnp.int32)[None, :]
    nxt_blk = jnp.sum(jnp.where(own, end_blk[None, :], 0), axis=1)
    at_nxt = nxt_blk[:, None] == jnp.arange(nb, dtype=jnp.int32)[None, :]
    nxt_e = jnp.where(nxt_blk < n_used[0], jnp.sum(jnp.where(at_nxt, blk_e[None, :], 0), axis=1), -1).astype(jnp.int32)
    zero_plan = (jnp.maximum(pad_end - block, 0).astype(jnp.int32), (padded > 0).astype(jnp.int32), n_used)
    return slots, (blk_e, n_used, nxt_e, grp.astype(jnp.int32)), zero_plan, nb


def _tile_slots(slots, ts):
    T = slots.shape[1]
    return slots.reshape(TOP_K, T // ts, ts).transpose(1, 0, 2).reshape(T // ts, 1, TOP_K * ts)


def _row_copy(src_ref, src_row, dst_ref, dst_row, sem):
    return pltpu.make_async_copy(
        src_ref.at[pl.ds(pl.multiple_of(src_row * ROW_TILES, ROW_TILES), ROW_TILES), :],
        dst_ref.at[pl.ds(pl.multiple_of(dst_row * ROW_TILES, ROW_TILES), ROW_TILES), :], sem)


S_BUFFERS = 3


def _scatter_kernel(zlo_ref, has_ref, nu_ref, slot_a_ref, slot_b_ref, src_a_hbm, src_b_ref, dst_ref, zbuf, sem,
                    sbuf, ssem, rsem, *, ts, n_a, n_b, nb, block):
    i = pl.program_id(0)
    tile_rows = ts * ROW_TILES

    def tile_in(t):
        return pltpu.make_async_copy(src_a_hbm.at[pl.ds(pl.multiple_of(t * tile_rows, tile_rows), tile_rows), :],
                                     sbuf.at[t % S_BUFFERS], ssem.at[t % S_BUFFERS])

    def rows_out(t):
        return pltpu.make_async_copy(sbuf.at[t % S_BUFFERS], dst_ref.at[pl.ds(0, tile_rows), :], rsem.at[t % 2])

    def clear(row0):
        return pltpu.make_async_copy(
            zbuf, dst_ref.at[pl.ds(pl.multiple_of(row0 * ROW_TILES, ROW_TILES), block * ROW_TILES), :], sem)

    def for_clears(act):
        for e in range(N_EXPERTS):
            @pl.when(has_ref[e] > 0)
            def _(e=e):
                act(clear(zlo_ref[e]))

        def tail(j, c):
            act(clear(j * block))
            return c
        lax.fori_loop(nu_ref[0], nb, tail, 0)

    @pl.when(i == 0)
    def _():
        for t in range(min(S_BUFFERS - 1, n_a)):
            tile_in(t).start()
        zbuf[...] = jnp.zeros(zbuf.shape, F32)
        for_clears(lambda c: c.start())
        for_clears(lambda c: c.wait())

    def issue_rows(slot_ref, src_ref, n, row_sem):
        def issue(j, c):
            for u in range(DMA_UNROLL):
                t = j * DMA_UNROLL + u
                for k in range(TOP_K):
                    _row_copy(src_ref, t, dst_ref, slot_ref[0, 0, k * n + t], row_sem).start(priority=k)
            return c
        lax.fori_loop(0, n // DMA_UNROLL, issue, 0)

    def wait_rows_out(t):
        for _ in range(TOP_K):
            rows_out(t).wait()

    @pl.when(i < n_a)
    def _():
        tile_in(i).wait()
        issue_rows(slot_a_ref, sbuf.at[i % S_BUFFERS], ts, rsem.at[i % 2])

        @pl.when(i >= 1)
        def _():
            wait_rows_out(i - 1)

        @pl.when(i + S_BUFFERS - 1 < n_a)
        def _():
            tile_in(i + S_BUFFERS - 1).start()

    @pl.when(i == n_a)
    def _():
        wait_rows_out(n_a - 1)
        issue_rows(slot_b_ref, src_b_ref, n_b, sem)
        for _ in range(TOP_K):
            pltpu.make_async_copy(src_b_ref, dst_ref.at[pl.ds(0, n_b * ROW_TILES), :], sem).wait()


def _scatter(slots_a, slots_b, src_a, src_b, zero_plan, nb, block):
    n_a, _, ts2 = slots_a.shape
    ts, n_b = ts2 // TOP_K, slots_b.shape[2] // TOP_K
    tile_a = lambda i, *_: (jnp.minimum(i, n_a - 1), 0, 0)
    return pl.pallas_call(
        functools.partial(_scatter_kernel, ts=ts, n_a=n_a, n_b=n_b, nb=nb, block=block),
        grid_spec=pltpu.PrefetchScalarGridSpec(
            num_scalar_prefetch=3,
            grid=(n_a + 1,),
            in_specs=[pl.BlockSpec((1, 1, ts2), tile_a, memory_space=pltpu.SMEM),
                      pl.BlockSpec((1, 1, TOP_K * n_b), lambda i, *_: (0, 0, 0), memory_space=pltpu.SMEM),
                      pl.BlockSpec(memory_space=pl.ANY),
                      pl.BlockSpec((n_b * ROW_TILES, LANES), lambda i, *_: (0, 0))],
            out_specs=pl.BlockSpec(memory_space=pl.ANY),
            scratch_shapes=[pltpu.VMEM((block * ROW_TILES, LANES), F32), pltpu.SemaphoreType.DMA(()),
                            pltpu.VMEM((S_BUFFERS, ts * ROW_TILES, LANES), F32),
                            pltpu.SemaphoreType.DMA((S_BUFFERS,)), pltpu.SemaphoreType.DMA((2,))]),
        out_shape=jax.ShapeDtypeStruct((nb * block * ROW_TILES, LANES), F32),
        compiler_params=_cparams(("arbitrary",)),
        name="moe_scatter",
    )(*zero_plan, slots_a, slots_b, src_a, src_b)


X_BUFFERS = 4
Y_BUFFERS = 3


def _experts_kernel(be_ref, nu_ref, nxt_ref, grp_ref, x_hbm, w1_hbm, w3_hbm, w2_hbm, y_hbm,
                    w1b, w3b, w2b, wf1, wf3, wf2, sem, xbuf, xsem, ybuf, ysem, *, block, layer):
    i = pl.program_id(0)
    n = nu_ref[0]
    rows = block * ROW_TILES

    def weight_copies(e, buf):
        return [pltpu.make_async_copy(src.at[layer, e], dst.at[buf], sem.at[buf])
                for src, dst in ((w1_hbm, wf1), (w3_hbm, wf3), (w2_hbm, wf2))]

    def x_copy(blk):
        return pltpu.make_async_copy(x_hbm.at[pl.ds(pl.multiple_of(blk * rows, rows), rows), :],
                                     xbuf.at[blk % X_BUFFERS], xsem.at[blk % X_BUFFERS])

    def y_copy(blk):
        return pltpu.make_async_copy(ybuf.at[blk % Y_BUFFERS],
                                     y_hbm.at[pl.ds(pl.multiple_of(blk * rows, rows), rows), :], ysem.at[blk % Y_BUFFERS])

    @pl.when(i == 0)
    def _():
        for c in weight_copies(be_ref[0], 0):
            c.start()
        for b in range(X_BUFFERS - 1):
            @pl.when(b < n)
            def _(b=b):
                x_copy(b).start()

    @pl.when(i < n)
    def _():
        @pl.when(i + X_BUFFERS - 1 < n)
        def _():
            x_copy(i + X_BUFFERS - 1).start()

        x_copy(i).wait()

        @pl.when(i >= Y_BUFFERS)
        def _():
            y_copy(i - Y_BUFFERS).wait()

        @pl.when(jnp.logical_or(i == 0, be_ref[i] != be_ref[jnp.maximum(i - 1, 0)]))
        def _():
            buf = grp_ref[i] % 2
            for c in weight_copies(be_ref[i], buf):
                c.wait()

            @pl.when(nxt_ref[i] >= 0)
            def _():
                for c in weight_copies(nxt_ref[i], 1 - buf):
                    c.start()

            w1b[...] = wf1[buf].astype(BF16)
            w3b[...] = wf3[buf].astype(BF16)
            w2b[...] = wf2[buf].astype(BF16)

        x = _load_rows8(xbuf.at[i % X_BUFFERS], block).astype(BF16)
        h1 = jnp.dot(x, w1b[...], preferred_element_type=F32)
        h3 = jnp.dot(x, w3b[...], preferred_element_type=F32)
        a = (h1 * jax.nn.sigmoid(h1) * h3).astype(BF16)
        _store_rows8(ybuf.at[i % Y_BUFFERS], jnp.dot(a, w2b[...], preferred_element_type=F32), block)
        y_copy(i).start()

        @pl.when(i == n - 1)
        def _():
            for back in range(Y_BUFFERS):
                @pl.when(i - back >= 0)
                def _(back=back):
                    y_copy(i - back).wait()


def _experts(xs, plan, nb, w1, w3, w2, layer, block):
    hbm = pl.BlockSpec(memory_space=pl.ANY)
    row_block = (block * ROW_TILES, LANES)
    return pl.pallas_call(
        functools.partial(_experts_kernel, block=block, layer=layer),
        grid_spec=pltpu.PrefetchScalarGridSpec(
            num_scalar_prefetch=4,
            grid=(nb,),
            in_specs=[hbm, hbm, hbm, hbm],
            out_specs=hbm,
            scratch_shapes=[pltpu.VMEM((D_MODEL, D_EXPERT), BF16), pltpu.VMEM((D_MODEL, D_EXPERT), BF16),
                            pltpu.VMEM((D_EXPERT, D_MODEL), BF16),
                            pltpu.VMEM((2, D_MODEL, D_EXPERT), F32), pltpu.VMEM((2, D_MODEL, D_EXPERT), F32),
                            pltpu.VMEM((2, D_EXPERT, D_MODEL), F32), pltpu.SemaphoreType.DMA((2,)),
                            pltpu.VMEM((X_BUFFERS,) + row_block, F32), pltpu.SemaphoreType.DMA((X_BUFFERS,)),
                            pltpu.VMEM((Y_BUFFERS,) + row_block, F32), pltpu.SemaphoreType.DMA((Y_BUFFERS,))]),
        out_shape=jax.ShapeDtypeStruct(xs.shape, F32),
        input_output_aliases={4: 0},
        compiler_params=_cparams(("arbitrary",)),
        name="moe_experts",
    )(*plan, xs, w1, w3, w2)


def _combine_kernel(slot_ref, slot_nxt_ref, y_ref, x_ref, mod_ref, rt_ref, o_ref, ybuf, sem, *, tc):
    i = pl.program_id(0)

    def gather(slots, buf):
        def issue(j, c):
            for u in range(DMA_UNROLL):
                t = j * DMA_UNROLL + u
                for k in range(TOP_K):
                    _row_copy(y_ref, slots[0, 0, k * tc + t], ybuf.at[buf], k * tc + t, sem.at[buf]).start(priority=k)
            return c
        lax.fori_loop(0, tc // DMA_UNROLL, issue, 0)

    @pl.when(i == 0)
    def _():
        gather(slot_ref, 0)

    cur = i % 2
    pltpu.make_async_copy(y_ref.at[pl.ds(0, TOP_K * tc * ROW_TILES), :], ybuf.at[cur], sem.at[cur]).wait()

    @pl.when(i + 1 < pl.num_programs(0))
    def _():
        gather(slot_nxt_ref, 1 - cur)

    rt = rt_ref[...]
    rows = ybuf.at[cur]
    y = (rt[:, 2:3] * _load_rows8(rows, tc) + rt[:, 3:4] * _load_rows8(rows, tc, base=tc * ROW_TILES))
    o_ref[...] = x_ref[...] + mod_ref[0][:, 5 * D_MODEL:6 * D_MODEL] * y


def _combine(slots, y, x, mod, tiles_per_mod, route, tc):
    T = x.shape[0]
    n = T // tc
    slot_spec = lambda f: pl.BlockSpec((1, 1, tc * TOP_K), f, memory_space=pltpu.SMEM)
    return pl.pallas_call(
        functools.partial(_combine_kernel, tc=tc),
        grid=(n,),
        in_specs=[slot_spec(lambda i: (i, 0, 0)), slot_spec(lambda i: (jnp.minimum(i + 1, n - 1), 0, 0)),
                  pl.BlockSpec(memory_space=pl.ANY),
                  pl.BlockSpec((tc, D_MODEL), lambda i: (i, 0)),
                  _mod_spec(mod, tiles_per_mod),
                  pl.BlockSpec((tc, LANES), lambda i: (i, 0))],
        out_specs=pl.BlockSpec((tc, D_MODEL), lambda i: (i, 0)),
        out_shape=jax.ShapeDtypeStruct((T, D_MODEL), F32),
        scratch_shapes=[pltpu.VMEM((2, TOP_K * tc * ROW_TILES, LANES), F32), pltpu.SemaphoreType.DMA((2,))],
        compiler_params=_cparams(("arbitrary",)),
        name="moe_combine",
    )(slots, slots, y, x, mod, route)


def _moe(part_a, part_b, counts, w1, w3, w2, layer):
    Ta = part_a[3].shape[0]
    slots, plan, zero_plan, nb = _layout(jnp.concatenate([part_a[2], part_b[2]], axis=1), counts, MOE_BLOCK)
    tiled = [_tile_slots(slots[:, :Ta], part_a[6]), _tile_slots(slots[:, Ta:], part_b[6])]
    xs = _scatter(tiled[0], tiled[1], part_a[0], part_b[0], zero_plan, nb, MOE_BLOCK)
    y = _experts(xs, plan, nb, w1, w3, w2, layer, MOE_BLOCK)
    return [_combine(sl, y, x, mod, tpm, rt, tt) for sl, (_, rt, _, x, mod, tpm, tt) in zip(tiled, (part_a, part_b))]


def _kvq_kernel(x_ref, modkv_ref, mod_ref, nkv_ref, nm_ref, wkv_ref, wq_ref, kn_ref, qn_ref, eavg_ref,
                *rest, tm, keep):
    residue = keep is not None
    x = x_ref[...]
    xn = x * lax.rsqrt(jnp.mean(x * x, axis=-1, keepdims=True) + EPS)
    modkv = modkv_ref[0]
    mod = mod_ref[0]
    hk = (xn * nkv_ref[...] * (1.0 + modkv[:, D_MODEL:]) + modkv[:, :D_MODEL]).astype(BF16)
    hq = (xn * nm_ref[...] * (1.0 + mod[:, D_MODEL:2 * D_MODEL]) + mod[:, :D_MODEL]).astype(BF16)
    eavg = eavg_ref[...]

    def head_norm(a, g):
        sq = (a * a).astype(BF16)
        half = ATTN_WIDTH // 2
        ms = jnp.concatenate([jnp.dot(sq[:, :half], eavg, preferred_element_type=F32),
                              jnp.dot(sq[:, half:], eavg, preferred_element_type=F32)], axis=1)
        return a * lax.rsqrt(ms + EPS) * g

    def residue_major(val, ref, dil, stage):
        if dil == 1:
            ref[0, 0] = val.astype(BF16)
            return
        for c in range(stage.shape[0]):
            stage[c] = val[:, c * LANES:(c + 1) * LANES]
        for r in range(dil):
            rows = [stage[c, pl.ds(r, tm // dil, stride=dil), :] for c in range(stage.shape[0])]
            ref[0, r] = jnp.concatenate(rows, axis=1).astype(BF16)

    for g, (_, dil) in enumerate(BRANCHES):
        k0 = 2 * g * ATTN_WIDTH
        kvg = jnp.dot(hk, wkv_ref[:, k0:k0 + 2 * ATTN_WIDTH], preferred_element_type=F32)
        kn = head_norm(kvg[:, :ATTN_WIDTH], kn_ref[g:g + 1, :])
        vv = kvg[:, ATTN_WIDTH:]
        q0 = g * ATTN_WIDTH
        qg = jnp.dot(hq, wq_ref[:, q0:q0 + ATTN_WIDTH], preferred_element_type=F32)
        qg = head_norm(qg, qn_ref[g:g + 1, :]) * (HEAD_DIM ** -0.5)
        if residue:
            stages = rest[4 * N_BRANCH:]
            residue_major(qg, rest[3 * g], dil, stages[3 * g])
            residue_major(kn, rest[3 * g + 1], dil, stages[3 * g + 1])
            residue_major(vv, rest[3 * g + 2], dil, stages[3 * g + 2])
            kt_ref = rest[3 * N_BRANCH + g]
            kt_ref[0, :ATTN_WIDTH, :] = kn.T
            kt_ref[0, ATTN_WIDTH:, :] = vv.T
        else:
            rest[0][:, k0:k0 + ATTN_WIDTH] = kn
            rest[0][:, k0 + ATTN_WIDTH:k0 + 2 * ATTN_WIDTH] = vv
            rest[1][:, q0:q0 + ATTN_WIDTH] = qg


def _kvq(x, modkv, mod, tiles_per_mod, tm, nkv, nm, wkv, wq, kn, qn, eavg, batch=None):
    T = x.shape[0]
    tok = lambda w: pl.BlockSpec((tm, w), lambda i: (i, 0))
    kvw = 2 * N_BRANCH * ATTN_WIDTH
    if batch is not None:
        L = T // batch
        tpb = L // tm
        kept = tuple(-(-min(w, L) // tm) for w, _ in BRANCHES)
        keep = (tpb, kept)
        out_specs, out_shape = [], []
        for _, dil in BRANCHES:
            for _ in range(3):
                out_specs.append(pl.BlockSpec((1, dil, tm // dil, ATTN_WIDTH), lambda i: (i // tpb, 0, i % tpb, 0)))
                out_shape.append(jax.ShapeDtypeStruct((batch, dil, L // dil, ATTN_WIDTH), BF16))
        for kg in kept:
            out_specs.append(pl.BlockSpec((1, 2 * ATTN_WIDTH, tm),
                                          lambda i, kg=kg: (i // tpb, 0, jnp.maximum(i % tpb - (tpb - kg), 0))))
            out_shape.append(jax.ShapeDtypeStruct((batch, 2 * ATTN_WIDTH, kg * tm), F32))
        scratch = [pltpu.VMEM((ATTN_WIDTH // LANES, tm if dil > 1 else 8, LANES), F32)
                   for _, dil in BRANCHES for _ in range(3)]
    else:
        keep = None
        out_specs = [tok(kvw), tok(N_BRANCH * ATTN_WIDTH)]
        out_shape = [jax.ShapeDtypeStruct((T, kvw), F32), jax.ShapeDtypeStruct((T, N_BRANCH * ATTN_WIDTH), F32)]
        scratch = []
    return pl.pallas_call(
        functools.partial(_kvq_kernel, tm=tm, keep=keep),
        grid=(T // tm,),
        in_specs=[tok(D_MODEL), _mod_spec(modkv, tiles_per_mod), _mod_spec(mod, tiles_per_mod),
                  _resident(nkv.shape), _resident(nm.shape), _resident(wkv.shape), _resident(wq.shape),
                  _resident(kn.shape), _resident(qn.shape), _resident(eavg.shape)],
        out_specs=out_specs,
        out_shape=out_shape,
        scratch_shapes=scratch,
        compiler_params=_cparams(("arbitrary",)),
        name="kvq",
    )(x, modkv, mod, nkv, nm, wkv, wq, kn, qn, eavg)


BAND_QBLOCKS = 16


def _band_attn_kernel(q_ref, kp_ref, kc_ref, vp_ref, vc_ref, bias_ref, o_ref, lse_ref):
    n = pl.program_id(2)
    hw = ATTN_WIDTH // 2
    hh = HEADS // 2
    col = lax.broadcasted_iota(jnp.int32, (hh * N_STEPS, 2 * N_STEPS), 1)
    first = jnp.logical_or(col >= N_STEPS, n > 0)
    lane = lax.broadcasted_iota(jnp.int32, (N_STEPS, LANES), 1)
    head_of = lax.broadcasted_iota(jnp.int32, (N_STEPS, hw), 1) // HEAD_DIM
    zero = jnp.zeros((N_STEPS, hw), BF16)
    nt = (((1,), (1,)), ((), ()))
    for r, j in [(r, j) for r in range(q_ref.shape[0]) for j in range(q_ref.shape[1] // N_STEPS)]:
        if j == 0:
            k_all = jnp.concatenate([kp_ref[r], kc_ref[r]], axis=0)
            v_all = jnp.concatenate([vp_ref[r], vc_ref[r]], axis=0)
        rows = slice(j * N_STEPS, (j + 1) * N_STEPS)
        q = q_ref[r, rows, :]
        k2 = k_all[j * N_STEPS:(j + 2) * N_STEPS]
        v2 = v_all[j * N_STEPS:(j + 2) * N_STEPS]
        lse_all = jnp.zeros((N_STEPS, LANES), F32)
        halves = []
        for c in range(2):
            cols = slice(c * hw, (c + 1) * hw)
            qc = q[:, cols]
            qbd = jnp.concatenate([jnp.where(head_of == h, qc, zero) for h in range(hh)], axis=0)
            s = (lax.dot_general(qbd, k2[:, cols], nt, preferred_element_type=F32)
                 + bias_ref[c * hh * N_STEPS:(c + 1) * hh * N_STEPS, :])
            if j == 0:
                s = jnp.where(first, s, NEG)
            m = jnp.max(s, axis=1, keepdims=True)
            p = jnp.exp(s - m)
            l = jnp.sum(p, axis=1, keepdims=True)
            pv = jnp.dot(p.astype(BF16), v2[:, cols], preferred_element_type=F32) / l
            lse = m + jnp.log(l)
            o = jnp.zeros((N_STEPS, hw), F32)
            for h in range(hh):
                blk = slice(h * N_STEPS, (h + 1) * N_STEPS)
                o = jnp.where(head_of == h, pv[blk], o)
                lse_all = jnp.where(lane == c * hh + h, lse[blk], lse_all)
            halves.append(o)
        o_ref[r, rows, :] = jnp.concatenate(halves, axis=1)
        lse_ref[r, rows, :] = lse_all


def _band_attn(q, k, v, bias, dil):
    B, _, Ld, _ = q.shape
    nq = min(BAND_QBLOCKS, Ld // N_STEPS)
    nr = min(BAND_QBLOCKS // nq, dil)
    step = nq * N_STEPS
    blk = lambda w, f: pl.BlockSpec((None, nr, step, w), f)
    cur = lambda b, r, n: (b, r, n, 0)
    prev_blk = pl.BlockSpec((None, nr, N_STEPS, ATTN_WIDTH), lambda b, r, n: (b, r, jnp.maximum(nq * n - 1, 0), 0))
    return pl.pallas_call(
        _band_attn_kernel,
        grid=(B, dil // nr, Ld // step),
        in_specs=[blk(ATTN_WIDTH, cur), prev_blk, blk(ATTN_WIDTH, cur), prev_blk,
                  blk(ATTN_WIDTH, cur), _resident(bias.shape)],
        out_specs=[blk(ATTN_WIDTH, cur), blk(LANES, cur)],
        out_shape=[jax.ShapeDtypeStruct((B, dil, Ld, ATTN_WIDTH), F32),
                   jax.ShapeDtypeStruct((B, dil, Ld, LANES), F32)],
        compiler_params=_cparams(("arbitrary", "arbitrary", "arbitrary")),
        name=f"band_attn_d{dil}",
    )(q, k, k, v, v, bias)


def _rel_buckets(dilation):
    n = np.arange(N_STEPS + 1) * dilation
    large = MAX_EXACT + (np.log(np.maximum(n, 1) / MAX_EXACT) / np.log(REL_MAX_DIST / MAX_EXACT)
                         * (NUM_BUCKETS - MAX_EXACT)).astype(np.int32)
    return np.where(n < MAX_EXACT, n, np.minimum(large, NUM_BUCKETS - 1)).astype(np.int32)


def _step_bias_row(rel_bias, g, dil):
    onehot = np.zeros((N_STEPS + 1, NUM_BUCKETS), np.float32)
    onehot[np.arange(N_STEPS + 1), _rel_buckets(dil)] = 1.0
    return jnp.dot(jnp.asarray(onehot), rel_bias[:, g, :].astype(F32), precision=lax.Precision.HIGHEST).T


def _band_bias(rel_bias, g, dil):
    bias = _step_bias_row(rel_bias, g, dil)
    P = 3 * N_STEPS
    neg = jnp.full((HEADS, N_STEPS), NEG, F32)
    ext = jnp.concatenate([neg, bias[:, ::-1], neg], axis=1)
    flat = jnp.broadcast_to(ext[:, None, :], (HEADS, N_STEPS, P + 1)).reshape(HEADS, N_STEPS * (P + 1))
    skew = flat[:, :N_STEPS * P].reshape(HEADS, N_STEPS, P)
    return skew[:, :, N_STEPS:].reshape(HEADS * N_STEPS, 2 * N_STEPS)


def _step_attn_kernel(q_ref, kvn_ref, c0_ref, c1_ref, c2_ref, b0_ref, b1_ref, b2_ref, bn_ref, o_ref, *, n_new):
    caches = (c0_ref, c1_ref, c2_ref)
    cbias = (b0_ref, b1_ref, b2_ref)
    nt = (((1,), (1,)), ((), ()))
    rows = n_new * HEADS
    row = lax.broadcasted_iota(jnp.int32, (rows, ATTN_WIDTH), 0)
    lane = lax.broadcasted_iota(jnp.int32, (rows, ATTN_WIDTH), 1)
    own_head = lane // HEAD_DIM == row % HEADS
    parts = []
    for g in range(N_BRANCH):
        klo = 2 * g * ATTN_WIDTH
        qg = q_ref[0, :, g * ATTN_WIDTH:(g + 1) * ATTN_WIDTH]
        qrep = jnp.concatenate([jnp.broadcast_to(qg[s:s + 1, :], (HEADS, ATTN_WIDTH)) for s in range(n_new)], axis=0)
        qbd = jnp.where(own_head, qrep, 0.0).astype(BF16)
        kn = kvn_ref[0, :, klo:klo + ATTN_WIDTH].astype(BF16)
        vn = kvn_ref[0, :, klo + ATTN_WIDTH:klo + 2 * ATTN_WIDTH].astype(BF16)
        kt = caches[g][0, 0].astype(BF16)
        vt = caches[g][0, 1].astype(BF16)
        sc = jnp.dot(qbd, kt, preferred_element_type=F32) + cbias[g][...]
        sn = lax.dot_general(qbd, kn, nt, preferred_element_type=F32) + bn_ref[g]
        m = jnp.maximum(jnp.max(sc, axis=1, keepdims=True), jnp.max(sn, axis=1, keepdims=True))
        pc = jnp.exp(sc - m)
        pn = jnp.exp(sn - m)
        l = jnp.sum(pc, axis=1, keepdims=True) + jnp.sum(pn, axis=1, keepdims=True)
        pv = (lax.dot_general(pc.astype(BF16), vt, nt, preferred_element_type=F32)
              + jnp.dot(pn.astype(BF16), vn, preferred_element_type=F32))
        parts.append((m + jnp.log(l), pv / l))
    mx = jnp.maximum(jnp.maximum(parts[0][0], parts[1][0]), parts[2][0])
    es = [jnp.exp(lse - mx) for lse, _ in parts]
    den = es[0] + es[1] + es[2]
    mixed = (es[0] * parts[0][1] + es[1] * parts[1][1] + es[2] * parts[2][1]) / den
    mixed = jnp.where(own_head, mixed, 0.0)
    for s in range(n_new):
        o_ref[0, s:s + 1, :] = jnp.sum(mixed[s * HEADS:(s + 1) * HEADS], axis=0, keepdims=True)


def _step_bias(rel_bias, n_new):
    cache_tabs, new_tabs = [], []
    for g, (win, dil) in enumerate(BRANCHES):
        bias = _step_bias_row(rel_bias, g, dil)
        rev = bias[:, ::-1]
        per_s = []
        for s in range(n_new):
            if dil == 1:
                neg = jnp.full((HEADS, s), NEG, F32)
                per_s.append(jnp.concatenate([neg, rev[:, :win - s]], axis=1))
            else:
                cols = [rev[:, :N_STEPS] if r == s % dil else jnp.full((HEADS, N_STEPS), NEG, F32)
                        for r in range(dil)]
                per_s.append(jnp.stack(cols, axis=2).reshape(HEADS, win))
        cache_tabs.append(jnp.stack(per_s, axis=0).reshape(n_new * HEADS, win))
        dist = np.arange(n_new)[:, None] - np.arange(8)[None, :]
        ok = (dist >= 0) & (dist % dil == 0) & (np.arange(8)[None, :] < n_new)
        onehot = np.zeros((N_STEPS + 1, n_new * 8), np.float32)
        onehot[np.where(ok, dist // dil, 0).reshape(-1), np.arange(n_new * 8)] = 1.0
        tab = jnp.dot(bias, jnp.asarray(onehot), precision=lax.Precision.HIGHEST).reshape(HEADS, n_new, 8)
        new_tabs.append(jnp.where(ok[None], tab, NEG).transpose(1, 0, 2).reshape(n_new * HEADS, 8))
    return cache_tabs, jnp.stack(new_tabs)


def _step_attn(q, kvn, caches, cache_bias, new_bias, DB, S):
    cspec = lambda c: pl.BlockSpec((1,) + c.shape[1:], lambda b: (b, 0, 0, 0))
    return pl.pallas_call(
        functools.partial(_step_attn_kernel, n_new=S),
        grid=(DB,),
        in_specs=[pl.BlockSpec((1, S, N_BRANCH * ATTN_WIDTH), lambda b: (b, 0, 0)),
                  pl.BlockSpec((1, 8, 2 * N_BRANCH * ATTN_WIDTH), lambda b: (b, 0, 0)),
                  cspec(caches[0]), cspec(caches[1]), cspec(caches[2]),
                  _resident(cache_bias[0].shape), _resident(cache_bias[1].shape), _resident(cache_bias[2].shape),
                  _resident(new_bias.shape)],
        out_specs=pl.BlockSpec((1, S, ATTN_WIDTH), lambda b: (b, 0, 0)),
        out_shape=jax.ShapeDtypeStruct((DB, S, ATTN_WIDTH), F32),
        compiler_params=_cparams(("arbitrary",)),
        name="step_attn",
    )(q, kvn, caches[0], caches[1], caches[2], *cache_bias, new_bias)


def _attn_out_kernel(*refs, dils, tm):
    n_o = max(len(dils), 1)
    n_in = n_o + len(dils) + 9
    (x_ref, mod_ref, wo_ref, ex_ref, nf_ref, wr_ref, br_ref, cin_ref, tri_ref,
     x3_ref, h3_ref, rt_ref, rtt_ref, cnt_ref, carry) = refs[n_in - 9:n_in + 6]
    stages = refs[n_in + 6:]
    _start_counts(cin_ref, carry)

    def natural(ref, stage, dil):
        if dil == 1:
            return ref[0, 0]
        for r in range(dil):
            val = ref[0, r]
            for c in range(stage.shape[0]):
                stage[c, pl.ds(r, tm // dil, stride=dil), :] = val[:, c * LANES:(c + 1) * LANES]
        return jnp.concatenate([stage[c] for c in range(stage.shape[0])], axis=1)

    if dils:
        lses = [natural(refs[n_o + g], stages[2 * g + 1], dil) for g, dil in enumerate(dils)]
        mx = functools.reduce(jnp.maximum, lses)
        es = [jnp.exp(l - mx) for l in lses]
        den = functools.reduce(lambda a, b: a + b, es)
        o = None
        for g, dil in enumerate(dils):
            term = _split_dot(es[g] / den, ex_ref[...]) * natural(refs[g], stages[2 * g], dil)
            o = term if o is None else o + term
    else:
        o = refs[0][...]
    mod = mod_ref[0]
    a = jnp.dot(o.astype(BF16), wo_ref[...], preferred_element_type=F32)
    x3 = x_ref[...] + mod[:, 2 * D_MODEL:3 * D_MODEL] * a
    x3_ref[...] = x3
    _ffn_pre(x3, mod, nf_ref, wr_ref, br_ref, tri_ref, h3_ref, rt_ref, rtt_ref, cnt_ref, carry, tm)


def _attn_out(os_, lses, x, mod, tiles_per_mod, tm, wo, ex, nf, wr, br, counts):
    T = x.shape[0]
    tok = lambda w: pl.BlockSpec((tm, w), lambda i: (i, 0))
    r_specs, r_shapes = _route_outs(T, tm)
    tri = _earlier_rows(tm)
    if lses:
        dils = tuple(o.shape[1] for o in os_)
        tpb = tiles_per_mod
        res = lambda a: pl.BlockSpec((1, a.shape[1], tm // a.shape[1], a.shape[3]), lambda i: (i // tpb, 0, i % tpb, 0))
        o_specs = [res(a) for a in os_] + [res(a) for a in lses]
        scratch = []
        for _ in dils:
            scratch += [pltpu.VMEM((ATTN_WIDTH // LANES, tm, LANES), F32), pltpu.VMEM((1, tm, LANES), F32)]
    else:
        dils, o_specs, scratch = (), [tok(ATTN_WIDTH)], []
    return pl.pallas_call(
        functools.partial(_attn_out_kernel, dils=dils, tm=tm),
        grid=(T // tm,),
        in_specs=(o_specs + [tok(D_MODEL), _mod_spec(mod, tiles_per_mod), _resident(wo.shape), _resident(ex.shape),
                             _resident(nf.shape), _resident(wr.shape), _resident(br.shape),
                             _resident(counts.shape), _resident(tri.shape)]),
        out_specs=[tok(D_MODEL), pl.BlockSpec((tm * ROW_TILES, LANES), lambda i: (i, 0))] + r_specs,
        out_shape=[jax.ShapeDtypeStruct((T, D_MODEL), F32),
                   jax.ShapeDtypeStruct((T * ROW_TILES, LANES), F32)] + r_shapes,
        scratch_shapes=[pltpu.VMEM((1, LANES), F32)] + scratch,
        compiler_params=_cparams(("arbitrary",)),
        name="attn_out",
    )(*os_, *lses, x, mod, wo, ex, nf, wr, br, counts, tri)


def kernel(x_prompt, x_sample, cache_kv_w128, cache_kv_w512, cache_kv_w2048, c_prompt, c_sample, ada_w, ada_b, norm_mix, norm_ffn, a_w_in, a_b_in, a_norm_v, a_w_s, a_b_s, a_w_out, kv_ada_w, kv_ada_b, kv_norm, w_kv, k_norm, rel_bias, b_w_q, q_norm, b_w_o, r_w_group, r_b_group, r_w_expert, r_b_expert, e_w1, e_w3, e_w2):
    B, L, _ = x_prompt.shape
    DB, S, _ = x_sample.shape
    Tp, Ts = B * L, DB * S
    tm = TOKEN_TILE
    tpm_p = L // tm

    c_all = jnp.concatenate([c_prompt, c_sample], axis=0)
    R = c_all.shape[0]
    c_all = jnp.pad(c_all, ((0, -R % 8), (0, 0)))
    mods = [_ada(c_all, ada_w, ada_b, l) for l in range(2)]
    modkv = _ada(c_all, kv_ada_w[None], kv_ada_b[None], 0)

    def split_mod(m):
        return m[:B, None, :], jnp.repeat(m[B:B + DB], S, axis=0)[None]
    mod_p, mod_s = zip(*[split_mod(m) for m in mods])
    modkv_p, modkv_s = split_mod(modkv)

    row = lambda a: a.reshape(1, -1)

    def router(l):
        wr = jnp.zeros((D_MODEL, LANES), F32)
        wr = wr.at[:, :N_GROUPS].set(r_w_group[l]).at[:, N_GROUPS:N_GROUPS + N_EXPERTS].set(r_w_expert[l])
        br = jnp.zeros((1, LANES), F32)
        br = br.at[0, :N_GROUPS].set(r_b_group[l]).at[0, N_GROUPS:N_GROUPS + N_EXPERTS].set(r_b_expert[l])
        return wr.astype(BF16), br

    no_counts = jnp.zeros((1, LANES), F32)

    win = a_w_in[0].astype(BF16)
    wout = a_w_out[0].astype(BF16)
    tril = jnp.tril(jnp.ones((CHUNK, CHUNK), bool))
    ws_p = jnp.where(tril, a_w_s[0], 0).astype(BF16)
    bs_p = a_b_s[0].T
    cs = min(CHUNK, S)
    ws_small = jnp.where(jnp.tril(jnp.ones((cs, cs), bool)), a_w_s[0][:, :cs, :cs], 0)
    ws_s = jnp.stack([jnp.kron(jnp.eye(Ts // cs, dtype=F32), ws_small[g]) for g in range(GMLP_GROUPS)]).astype(BF16)
    bs_s = jnp.tile(a_b_s[0][:, :cs], (1, Ts // cs)).T
    wr0, br0 = router(0)
    common = (row(norm_mix[0]), win, row(a_b_in[0]), row(a_norm_v[0]))
    xp = x_prompt.reshape(Tp, D_MODEL)
    xs_ = x_sample.reshape(Ts, D_MODEL)
    x1_p, h2_p, rt_p, rtt_p, cnt = _mixer_a(xp, mod_p[0], tpm_p, tm, *common, ws_p, bs_p, wout, row(norm_ffn[0]),
                                            wr0, br0, no_counts, with_v=False)
    x1_s, h2_s, rt_s, rtt_s, cnt, v_s = _mixer_a(xs_, mod_s[0], 1, Ts, *common, ws_s, bs_s, wout, row(norm_ffn[0]),
                                                 wr0, br0, cnt, with_v=True)
    x2_p, x2_s = _moe((h2_p, rt_p, rtt_p, x1_p, mod_p[0], tpm_p, tm), (h2_s, rt_s, rtt_s, x1_s, mod_s[0], 1, Ts), cnt,
                      e_w1, e_w3, e_w2, 0)

    wkv = w_kv.astype(BF16)
    wq = b_w_q[0].astype(BF16)
    kn = jnp.tile(k_norm, (1, HEADS))
    qn = jnp.tile(q_norm[0], (1, HEADS))
    head = np.arange(ATTN_WIDTH) // HEAD_DIM
    half_head = head[:ATTN_WIDTH // 2]
    eavg = jnp.asarray((half_head[:, None] == half_head[None, :]) / HEAD_DIM, BF16)
    kvq_w = (row(kv_norm), row(norm_mix[1]), wkv, wq, kn, qn, eavg)
    *qkv_p, kt0, kt1, kt2 = _kvq(x2_p, modkv_p, mod_p[1], tpm_p, tm, *kvq_w, batch=B)
    kv_s, q_s = _kvq(x2_s, modkv_s, mod_s[1], 1, Ts, *kvq_w)

    os_, lses = [], []
    for g, (_, dil) in enumerate(BRANCHES):
        o, lse = _band_attn(*qkv_p[3 * g:3 * g + 3], _band_bias(rel_bias, g, dil), dil)
        os_.append(o)
        lses.append(lse)
    caches = [jnp.transpose(c, (0, 2, 3, 4, 1)).reshape(DB, 2, ATTN_WIDTH, c.shape[1])
              for c in (cache_kv_w128, cache_kv_w512, cache_kv_w2048)]
    kvn_s = jnp.pad(kv_s.reshape(DB, S, -1), ((0, 0), (0, 8 - S), (0, 0)))
    o_s = _step_attn(q_s.reshape(DB, S, -1), kvn_s, caches, *_step_bias(rel_bias, S), DB, S)

    wo = b_w_o[0].astype(BF16)
    ex = jnp.asarray(np.arange(LANES)[:, None] == head[None, :], BF16)
    wr1, br1 = router(1)
    x3_p, h3_p, rt1_p, rtt1_p, cnt1 = _attn_out(os_, lses, x2_p, mod_p[1], tpm_p, tm, wo, ex, row(norm_ffn[1]),
                                                wr1, br1, no_counts)
    x3_s, h3_s, rt1_s, rtt1_s, cnt1 = _attn_out([o_s.reshape(Ts, ATTN_WIDTH)], [], x2_s, mod_s[1], 1, Ts, wo, ex,
                                                row(norm_ffn[1]), wr1, br1, cnt1)
    y_p, y_s = _moe((h3_p, rt1_p, rtt1_p, x3_p, mod_p[1], tpm_p, tm), (h3_s, rt1_s, rtt1_s, x3_s, mod_s[1], 1, Ts),
                    cnt1, e_w1, e_w3, e_w2, 1)

    kv_s4 = kv_s.reshape(DB, S, N_BRANCH, 2, HEADS, HEAD_DIM)

    def window(kt, w):
        n = min(w, L)
        return jnp.transpose(kt[:, :, kt.shape[2] - n:].reshape(B, 2, HEADS, HEAD_DIM, n), (0, 4, 1, 2, 3))
    return (y_p.reshape(B, L, D_MODEL), y_s.reshape(DB, S, D_MODEL),
            window(kt0, BRANCHES[0][0]), window(kt1, BRANCHES[1][0]), window(kt2, BRANCHES[2][0]),
            kv_s4[:, :, 0], kv_s4[:, :, 1], kv_s4[:, :, 2],
            v_s.reshape(1, DB, S, GMLP_WIDTH))
```

```python
import functools

import numpy as np
import jax
import jax.numpy as jnp
from jax import lax
from jax.experimental import pallas as pl
from jax.experimental.pallas import tpu as pltpu

F32 = jnp.float32
BF16 = jnp.bfloat16

D_MODEL = 1024
GMLP_WIDTH = 2048
GMLP_GROUPS = 4
GROUP_WIDTH = GMLP_WIDTH // GMLP_GROUPS
CHUNK = 128
BRANCHES = ((128, 1), (512, 4), (2048, 16))
N_BRANCH = 3
N_STEPS = 128
HEADS = 8
HEAD_DIM = 64
ATTN_WIDTH = HEADS * HEAD_DIM
NUM_BUCKETS = 32
MAX_EXACT = NUM_BUCKETS // 2
REL_MAX_DIST = 2048
N_GROUPS = 4
EXPERTS_PER_GROUP = 8
N_EXPERTS = N_GROUPS * EXPERTS_PER_GROUP
TOP_K = 2
D_EXPERT = 512
EPS = 1e-6
NEG = -1e30

LANES = 128
ROW_TILES = D_MODEL // LANES
TOKEN_TILE = 512
MOE_BLOCK = 256
MOE_TILE = 1024
DMA_UNROLL = 16
VMEM_LIMIT = 52 * 1024 * 1024


def _cparams(sem):
    return pltpu.CompilerParams(dimension_semantics=sem, vmem_limit_bytes=VMEM_LIMIT)


def _resident(shape):
    nd = len(shape)
    return pl.BlockSpec(shape, lambda *_, _nd=nd: (0,) * _nd, pipeline_mode=pl.Buffered(1))


def _gelu_tanh(x):
    return 0.5 * x * (1.0 + jnp.tanh(0.7978845608028654 * (x + 0.044715 * (x * x * x))))


def _rms(x, g):
    return x * lax.rsqrt(jnp.mean(x * x, axis=-1, keepdims=True) + EPS) * g


def _store_rows8(ref, val, n, base=0):
    for s in range(ROW_TILES):
        ref[pl.ds(base + s, n, stride=ROW_TILES), :] = val[:, s * LANES:(s + 1) * LANES]


def _load_rows8(ref, n, base=0):
    return jnp.concatenate([ref[pl.ds(base + s, n, stride=ROW_TILES), :] for s in range(ROW_TILES)], axis=1)


def _split(a):
    hi = a.astype(BF16)
    return hi, (a - hi.astype(F32)).astype(BF16)


def _split_dot(a, e_bf16):
    hi, lo = _split(a)
    return (jnp.dot(hi, e_bf16, preferred_element_type=F32) + jnp.dot(lo, e_bf16, preferred_element_type=F32))


ROUTE_LANE0 = 4


def _route_rows(l, tri_ref, carry):
    lane = lax.broadcasted_iota(jnp.int32, l.shape, 1).astype(F32)
    far = float(LANES)

    def first_lane(mask):
        return jnp.min(jnp.where(mask, lane, far), axis=1, keepdims=True)

    is_g = lane < N_GROUPS
    gl = jnp.where(is_g, l, NEG)
    gmax = jnp.max(gl, axis=1, keepdims=True)
    g_i = first_lane(jnp.logical_and(gl == gmax, is_g))
    g_p = 1.0 / jnp.sum(jnp.where(is_g, jnp.exp(gl - gmax), 0.0), axis=1, keepdims=True)
    lo = ROUTE_LANE0 + EXPERTS_PER_GROUP * g_i
    sel = jnp.logical_and(lane >= lo, lane < lo + EXPERTS_PER_GROUP)
    el = jnp.where(sel, l, NEG)
    m1 = jnp.max(el, axis=1, keepdims=True)
    i1 = first_lane(jnp.logical_and(el == m1, sel))
    sel2 = jnp.logical_and(sel, lane != i1)
    el2 = jnp.where(sel2, l, NEG)
    m2 = jnp.max(el2, axis=1, keepdims=True)
    i2 = first_lane(jnp.logical_and(el2 == m2, sel2))
    r = jnp.exp(m2 - m1)
    w1 = g_p / (1.0 + r)
    w2 = g_p * r / (1.0 + r)

    hit1 = lane == i1
    hit2 = lane == i2
    onehot = jnp.where(jnp.logical_or(hit1, hit2), 1.0, 0.0)
    before = carry[...] + jnp.dot(tri_ref[...], onehot.astype(BF16), preferred_element_type=F32)
    rank1 = jnp.sum(jnp.where(hit1, before, 0.0), axis=1, keepdims=True)
    rank2 = jnp.sum(jnp.where(hit2, before, 0.0), axis=1, keepdims=True)
    carry[...] = carry[...] + jnp.sum(onehot, axis=0, keepdims=True)

    out = jnp.zeros(l.shape, F32)
    for k, val in enumerate((i1 - ROUTE_LANE0, i2 - ROUTE_LANE0, w1, w2, rank1, rank2)):
        out = jnp.where(lane == k, val, out)
    return out


ROUTE_FIELDS = 8


def _start_counts(cin_ref, carry):
    @pl.when(pl.program_id(0) == 0)
    def _():
        carry[...] = cin_ref[...]


def _ffn_pre(x, mod, nf_ref, wr_ref, br_ref, tri_ref, h_ref, rt_ref, rtt_ref, cnt_ref, carry, n):
    h = _rms(x, nf_ref[...]) * (1.0 + mod[:, 4 * D_MODEL:5 * D_MODEL]) + mod[:, 3 * D_MODEL:4 * D_MODEL]
    _store_rows8(h_ref, h, n)
    logits = jnp.dot(h.astype(BF16), wr_ref[...], preferred_element_type=F32) + br_ref[...]
    route = _route_rows(logits, tri_ref, carry)
    rt_ref[...] = route
    rtt_ref[...] = route.T[:ROUTE_FIELDS, :]
    cnt_ref[...] = carry[...]


def _ada_kernel(c_ref, w_ref, b_ref, o_ref):
    c = c_ref[...]
    a = (c * jax.nn.sigmoid(c)).astype(BF16)
    o_ref[...] = jnp.dot(a, w_ref[...].astype(BF16), preferred_element_type=F32) + b_ref[...]


def _ada(c, w, b, layer):
    R = c.shape[0]
    N = w.shape[2]
    tn = 1024
    return pl.pallas_call(
        _ada_kernel,
        grid=(N // tn,),
        in_specs=[pl.BlockSpec((R, D_MODEL), lambda j: (0, 0)),
                  pl.BlockSpec((None, D_MODEL, tn), lambda j: (layer, 0, j)),
                  pl.BlockSpec((None, 1, tn), lambda j: (layer, 0, j))],
        out_specs=pl.BlockSpec((R, tn), lambda j: (0, j)),
        out_shape=jax.ShapeDtypeStruct((R, N), F32),
        compiler_params=_cparams(("arbitrary",)),
        name="ada",
    )(c, w, b.reshape(b.shape[0], 1, N))


def _mixer_a_kernel(x_ref, mod_ref, nm_ref, win_ref, bin_ref, gv_ref, ws_ref, bs_ref, wout_ref,
                    nf_ref, wr_ref, br_ref, cin_ref, tri_ref, x1_ref, h2_ref, rt_ref, rtt_ref, cnt_ref, *rest, tm):
    v_refs, carry = rest[:-1], rest[-1]
    _start_counts(cin_ref, carry)
    x = x_ref[...]
    mod = mod_ref[0]
    h = (_rms(x, nm_ref[...]) * (1.0 + mod[:, D_MODEL:2 * D_MODEL]) + mod[:, 0:D_MODEL]).astype(BF16)
    zv = jnp.dot(h, win_ref[:, GMLP_WIDTH:], preferred_element_type=F32) + bin_ref[:, GMLP_WIDTH:]
    v = _rms(_gelu_tanh(zv), gv_ref[...])
    if v_refs:
        v_refs[0][...] = v
    vb = v.astype(BF16)
    bs = bs_ref[...]
    acc = jnp.zeros((tm, D_MODEL), F32)
    for g in range(GMLP_GROUPS):
        lo, hi = g * GROUP_WIDTH, (g + 1) * GROUP_WIDTH
        u = _gelu_tanh(jnp.dot(h, win_ref[:, lo:hi], preferred_element_type=F32) + bin_ref[:, lo:hi])
        wg = ws_ref[g]
        gate = jnp.concatenate(
            [jnp.dot(wg, vb[c * CHUNK:(c + 1) * CHUNK, lo:hi], preferred_element_type=F32) + bs[:, g:g + 1]
             for c in range(tm // CHUNK)], axis=0)
        acc = acc + jnp.dot((u * gate).astype(BF16), wout_ref[lo:hi, :], preferred_element_type=F32)
    x1 = x + mod[:, 2 * D_MODEL:3 * D_MODEL] * acc
    x1_ref[...] = x1
    _ffn_pre(x1, mod, nf_ref, wr_ref, br_ref, tri_ref, h2_ref, rt_ref, rtt_ref, cnt_ref, carry, tm)


def _mod_spec(mod, tiles_per_mod):
    _, rows, width = mod.shape
    return pl.BlockSpec((1, rows, width), lambda i: (i // tiles_per_mod, 0, 0))


def _earlier_rows(tm):
    return jnp.asarray(np.tril(np.ones((tm, tm), np.float32), -1), BF16)


def _route_outs(T, tm):
    return ([pl.BlockSpec((tm, LANES), lambda i: (i, 0)), pl.BlockSpec((ROUTE_FIELDS, tm), lambda i: (0, i)),
             pl.BlockSpec((1, LANES), lambda i: (0, 0))],
            [jax.ShapeDtypeStruct((T, LANES), F32), jax.ShapeDtypeStruct((ROUTE_FIELDS, T), F32),
             jax.ShapeDtypeStruct((1, LANES), F32)])


def _mixer_a(x, mod, tiles_per_mod, tm, nm, win, bin_, gv, ws, bs_t, wout, nf, wr, br, counts, with_v):
    T = x.shape[0]
    tok = lambda w: pl.BlockSpec((tm, w), lambda i: (i, 0))
    r_specs, r_shapes = _route_outs(T, tm)
    out_shape = [jax.ShapeDtypeStruct((T, D_MODEL), F32), jax.ShapeDtypeStruct((T * ROW_TILES, LANES), F32)] + r_shapes
    out_specs = [tok(D_MODEL), pl.BlockSpec((tm * ROW_TILES, LANES), lambda i: (i, 0))] + r_specs
    if with_v:
        out_shape.append(jax.ShapeDtypeStruct((T, GMLP_WIDTH), F32))
        out_specs.append(tok(GMLP_WIDTH))
    tri = _earlier_rows(tm)
    return pl.pallas_call(
        functools.partial(_mixer_a_kernel, tm=tm),
        grid=(T // tm,),
        in_specs=[tok(D_MODEL), _mod_spec(mod, tiles_per_mod), _resident(nm.shape), _resident(win.shape),
                  _resident(bin_.shape), _resident(gv.shape), _resident(ws.shape), _resident(bs_t.shape),
                  _resident(wout.shape), _resident(nf.shape), _resident(wr.shape), _resident(br.shape),
                  _resident(counts.shape), _resident(tri.shape)],
        out_specs=out_specs,
        out_shape=out_shape,
        scratch_shapes=[pltpu.VMEM((1, LANES), F32)],
        compiler_params=_cparams(("arbitrary",)),
        name="mixer_a",
    )(x, mod, nm, win, bin_, gv, ws, bs_t, wout, nf, wr, br, counts, tri)


def _layout(rtt, counts, block):
    n_tokens = rtt.shape[1]
    counts = counts[0, ROUTE_LANE0:ROUTE_LANE0 + N_EXPERTS].astype(jnp.int32)
    padded = (counts + block - 1) // block * block
    pad_end = jnp.cumsum(padded)
    pad_start = (pad_end - padded).astype(F32)
    experts, ranks = rtt[0:TOP_K], rtt[4:4 + TOP_K]
    onehot = (experts[None] == jnp.arange(N_EXPERTS, dtype=F32)[:, None, None]).astype(F32)
    base = jnp.einsum('e,ekt->kt', pad_start, onehot, precision=lax.Precision.HIGHEST)
    slots = (ranks + base).astype(jnp.int32)
    nb = -(-n_tokens * TOP_K // block) + N_EXPERTS
    blk_e = jnp.minimum(jnp.sum(pad_end[None, :] <= (jnp.arange(nb, dtype=jnp.int32) * block)[:, None], axis=1),
                        N_EXPERTS - 1).astype(jnp.int32)
    n_used = (pad_end[-1] // block).astype(jnp.int32).reshape(1)
    starts = jnp.concatenate([jnp.ones((1,), jnp.int32), (blk_e[1:] != blk_e[:-1]).astype(jnp.int32)])
    grp = jnp.cumsum(starts) - 1
    end_blk = (pad_end // block).astype(jnp.int32)
    own = blk_e[:, None] == jnp.arange(N_EXPERTS, dtype=jnp.int32)[None, :]
    nxt_blk = jnp.sum(jnp.where(own, end_blk[None, :], 0), axis=1)
    at_nxt = nxt_blk[:, None] == jnp.arange(nb, dtype=jnp.int32)[None, :]
    nxt_e = jnp.where(nxt_blk < n_used[0], jnp.sum(jnp.where(at_nxt, blk_e[None, :], 0), axis=1), -1).astype(jnp.int32)
    zero_plan = (jnp.maximum(pad_end - block, 0).astype(jnp.int32), (padded > 0).astype(jnp.int32), n_used)
    return slots, (blk_e, n_used, nxt_e, grp.astype(jnp.int32)), zero_plan, nb


def _tile_slots(slots, ts):
    T = slots.shape[1]
    return slots.reshape(TOP_K, T // ts, ts).transpose(1, 0, 2).reshape(T // ts, 1, TOP_K * ts)


def _row_copy(src_ref, src_row, dst_ref, dst_row, sem):
    return pltpu.make_async_copy(
        src_ref.at[pl.ds(pl.multiple_of(src_row * ROW_TILES, ROW_TILES), ROW_TILES), :],
        dst_ref.at[pl.ds(pl.multiple_of(dst_row * ROW_TILES, ROW_TILES), ROW_TILES), :], sem)


S_BUFFERS = 3


def _scatter_kernel(zlo_ref, has_ref, nu_ref, slot_a_ref, slot_b_ref, src_a_hbm, src_b_ref, dst_ref, zbuf, sem,
                    sbuf, ssem, rsem, *, ts, n_a, n_b, nb, block):
    i = pl.program_id(0)
    tile_rows = ts * ROW_TILES

    def tile_in(t):
        return pltpu.make_async_copy(src_a_hbm.at[pl.ds(pl.multiple_of(t * tile_rows, tile_rows), tile_rows), :],
                                     sbuf.at[t % S_BUFFERS], ssem.at[t % S_BUFFERS])

    def rows_out(t):
        return pltpu.make_async_copy(sbuf.at[t % S_BUFFERS], dst_ref.at[pl.ds(0, tile_rows), :], rsem.at[t % 2])

    def clear(row0):
        return pltpu.make_async_copy(
            zbuf, dst_ref.at[pl.ds(pl.multiple_of(row0 * ROW_TILES, ROW_TILES), block * ROW_TILES), :], sem)

    def for_clears(act):
        for e in range(N_EXPERTS):
            @pl.when(has_ref[e] > 0)
            def _(e=e):
                act(clear(zlo_ref[e]))

        def tail(j, c):
            act(clear(j * block))
            return c
        lax.fori_loop(nu_ref[0], nb, tail, 0)

    @pl.when(i == 0)
    def _():
        for t in range(min(S_BUFFERS - 1, n_a)):
            tile_in(t).start()
        zbuf[...] = jnp.zeros(zbuf.shape, F32)
        for_clears(lambda c: c.start())
        for_clears(lambda c: c.wait())

    def issue_rows(slot_ref, src_ref, n, row_sem):
        def issue(j, c):
            for u in range(DMA_UNROLL):
                t = j * DMA_UNROLL + u
                for k in range(TOP_K):
                    _row_copy(src_ref, t, dst_ref, slot_ref[0, 0, k * n + t], row_sem).start(priority=k)
            return c
        lax.fori_loop(0, n // DMA_UNROLL, issue, 0)

    def wait_rows_out(t):
        for _ in range(TOP_K):
            rows_out(t).wait()

    @pl.when(i < n_a)
    def _():
        tile_in(i).wait()
        issue_rows(slot_a_ref, sbuf.at[i % S_BUFFERS], ts, rsem.at[i % 2])

        @pl.when(i >= 1)
        def _():
            wait_rows_out(i - 1)

        @pl.when(i + S_BUFFERS - 1 < n_a)
        def _():
            tile_in(i + S_BUFFERS - 1).start()

    @pl.when(i == n_a)
    def _():
        wait_rows_out(n_a - 1)
        issue_rows(slot_b_ref, src_b_ref, n_b, sem)
        for _ in range(TOP_K):
            pltpu.make_async_copy(src_b_ref, dst_ref.at[pl.ds(0, n_b * ROW_TILES), :], sem).wait()


def _scatter(slots_a, slots_b, src_a, src_b, zero_plan, nb, block):
    n_a, _, ts2 = slots_a.shape
    ts, n_b = ts2 // TOP_K, slots_b.shape[2] // TOP_K
    tile_a = lambda i, *_: (jnp.minimum(i, n_a - 1), 0, 0)
    return pl.pallas_call(
        functools.partial(_scatter_kernel, ts=ts, n_a=n_a, n_b=n_b, nb=nb, block=block),
        grid_spec=pltpu.PrefetchScalarGridSpec(
            num_scalar_prefetch=3,
            grid=(n_a + 1,),
            in_specs=[pl.BlockSpec((1, 1, ts2), tile_a, memory_space=pltpu.SMEM),
                      pl.BlockSpec((1, 1, TOP_K * n_b), lambda i, *_: (0, 0, 0), memory_space=pltpu.SMEM),
                      pl.BlockSpec(memory_space=pl.ANY),
                      pl.BlockSpec((n_b * ROW_TILES, LANES), lambda i, *_: (0, 0))],
            out_specs=pl.BlockSpec(memory_space=pl.ANY),
            scratch_shapes=[pltpu.VMEM((block * ROW_TILES, LANES), F32), pltpu.SemaphoreType.DMA(()),
                            pltpu.VMEM((S_BUFFERS, ts * ROW_TILES, LANES), F32),
                            pltpu.SemaphoreType.DMA((S_BUFFERS,)), pltpu.SemaphoreType.DMA((2,))]),
        out_shape=jax.ShapeDtypeStruct((nb * block * ROW_TILES, LANES), F32),
        compiler_params=_cparams(("arbitrary",)),
        name="moe_scatter",
    )(*zero_plan, slots_a, slots_b, src_a, src_b)


X_BUFFERS = 4
Y_BUFFERS = 3


def _experts_kernel(be_ref, nu_ref, nxt_ref, grp_ref, x_hbm, w1_hbm, w3_hbm, w2_hbm, y_hbm,
                    w1b, w3b, w2b, wf1, wf3, wf2, sem, xbuf, xsem, ybuf, ysem, *, block, layer):
    i = pl.program_id(0)
    n = nu_ref[0]
    rows = block * ROW_TILES

    def weight_copies(e, buf):
        return [pltpu.make_async_copy(src.at[layer, e], dst.at[buf], sem.at[buf])
                for src, dst in ((w1_hbm, wf1), (w3_hbm, wf3), (w2_hbm, wf2))]

    def x_copy(blk):
        return pltpu.make_async_copy(x_hbm.at[pl.ds(pl.multiple_of(blk * rows, rows), rows), :],
                                     xbuf.at[blk % X_BUFFERS], xsem.at[blk % X_BUFFERS])

    def y_copy(blk):
        return pltpu.make_async_copy(ybuf.at[blk % Y_BUFFERS],
                                     y_hbm.at[pl.ds(pl.multiple_of(blk * rows, rows), rows), :], ysem.at[blk % Y_BUFFERS])

    @pl.when(i == 0)
    def _():
        for c in weight_copies(be_ref[0], 0):
            c.start()
        for b in range(X_BUFFERS - 1):
            @pl.when(b < n)
            def _(b=b):
                x_copy(b).start()

    @pl.when(i < n)
    def _():
        @pl.when(i + X_BUFFERS - 1 < n)
        def _():
            x_copy(i + X_BUFFERS - 1).start()

        x_copy(i).wait()

        @pl.when(i >= Y_BUFFERS)
        def _():
            y_copy(i - Y_BUFFERS).wait()

        @pl.when(jnp.logical_or(i == 0, be_ref[i] != be_ref[jnp.maximum(i - 1, 0)]))
        def _():
            buf = grp_ref[i] % 2
            for c in weight_copies(be_ref[i], buf):
                c.wait()

            @pl.when(nxt_ref[i] >= 0)
            def _():
                for c in weight_copies(nxt_ref[i], 1 - buf):
                    c.start()

            w1b[...] = wf1[buf].astype(BF16)
            w3b[...] = wf3[buf].astype(BF16)
            w2b[...] = wf2[buf].astype(BF16)

        x = _load_rows8(xbuf.at[i % X_BUFFERS], block).astype(BF16)
        h1 = jnp.dot(x, w1b[...], preferred_element_type=F32)
        h3 = jnp.dot(x, w3b[...], preferred_element_type=F32)
        a = (h1 * jax.nn.sigmoid(h1) * h3).astype(BF16)
        _store_rows8(ybuf.at[i % Y_BUFFERS], jnp.dot(a, w2b[...], preferred_element_type=F32), block)
        y_copy(i).start()

        @pl.when(i == n - 1)
        def _():
            for back in range(Y_BUFFERS):
                @pl.when(i - back >= 0)
                def _(back=back):
                    y_copy(i - back).wait()


def _experts(xs, plan, nb, w1, w3, w2, layer, block):
    hbm = pl.BlockSpec(memory_space=pl.ANY)
    row_block = (block * ROW_TILES, LANES)
    return pl.pallas_call(
        functools.partial(_experts_kernel, block=block, layer=layer),
        grid_spec=pltpu.PrefetchScalarGridSpec(
            num_scalar_prefetch=4,
            grid=(nb,),
            in_specs=[hbm, hbm, hbm, hbm],
            out_specs=hbm,
            scratch_shapes=[pltpu.VMEM((D_MODEL, D_EXPERT), BF16), pltpu.VMEM((D_MODEL, D_EXPERT), BF16),
                            pltpu.VMEM((D_EXPERT, D_MODEL), BF16),
                            pltpu.VMEM((2, D_MODEL, D_EXPERT), F32), pltpu.VMEM((2, D_MODEL, D_EXPERT), F32),
                            pltpu.VMEM((2, D_EXPERT, D_MODEL), F32), pltpu.SemaphoreType.DMA((2,)),
                            pltpu.VMEM((X_BUFFERS,) + row_block, F32), pltpu.SemaphoreType.DMA((X_BUFFERS,)),
                            pltpu.VMEM((Y_BUFFERS,) + row_block, F32), pltpu.SemaphoreType.DMA((Y_BUFFERS,))]),
        out_shape=jax.ShapeDtypeStruct(xs.shape, F32),
        input_output_aliases={4: 0},
        compiler_params=_cparams(("arbitrary",)),
        name="moe_experts",
    )(*plan, xs, w1, w3, w2)


def _combine_kernel(slot_ref, slot_nxt_ref, y_ref, x_ref, mod_ref, rt_ref, o_ref, ybuf, sem, *, tc):
    i = pl.program_id(0)

    def gather(slots, buf):
        def issue(j, c):
            for u in range(DMA_UNROLL):
                t = j * DMA_UNROLL + u
                for k in range(TOP_K):
                    _row_copy(y_ref, slots[0, 0, k * tc + t], ybuf.at[buf], k * tc + t, sem.at[buf]).start(priority=k)
            return c
        lax.fori_loop(0, tc // DMA_UNROLL, issue, 0)

    @pl.when(i == 0)
    def _():
        gather(slot_ref, 0)

    cur = i % 2
    pltpu.make_async_copy(y_ref.at[pl.ds(0, TOP_K * tc * ROW_TILES), :], ybuf.at[cur], sem.at[cur]).wait()

    @pl.when(i + 1 < pl.num_programs(0))
    def _():
        gather(slot_nxt_ref, 1 - cur)

    rt = rt_ref[...]
    rows = ybuf.at[cur]
    y = (rt[:, 2:3] * _load_rows8(rows, tc) + rt[:, 3:4] * _load_rows8(rows, tc, base=tc * ROW_TILES))
    o_ref[...] = x_ref[...] + mod_ref[0][:, 5 * D_MODEL:6 * D_MODEL] * y


def _combine(slots, y, x, mod, tiles_per_mod, route, tc):
    T = x.shape[0]
    n = T // tc
    slot_spec = lambda f: pl.BlockSpec((1, 1, tc * TOP_K), f, memory_space=pltpu.SMEM)
    return pl.pallas_call(
        functools.partial(_combine_kernel, tc=tc),
        grid=(n,),
        in_specs=[slot_spec(lambda i: (i, 0, 0)), slot_spec(lambda i: (jnp.minimum(i + 1, n - 1), 0, 0)),
                  pl.BlockSpec(memory_space=pl.ANY),
                  pl.BlockSpec((tc, D_MODEL), lambda i: (i, 0)),
                  _mod_spec(mod, tiles_per_mod),
                  pl.BlockSpec((tc, LANES), lambda i: (i, 0))],
        out_specs=pl.BlockSpec((tc, D_MODEL), lambda i: (i, 0)),
        out_shape=jax.ShapeDtypeStruct((T, D_MODEL), F32),
        scratch_shapes=[pltpu.VMEM((2, TOP_K * tc * ROW_TILES, LANES), F32), pltpu.SemaphoreType.DMA((2,))],
        compiler_params=_cparams(("arbitrary",)),
        name="moe_combine",
    )(slots, slots, y, x, mod, route)


def _moe(part_a, part_b, counts, w1, w3, w2, layer):
    Ta = part_a[3].shape[0]
    slots, plan, zero_plan, nb = _layout(jnp.concatenate([part_a[2], part_b[2]], axis=1), counts, MOE_BLOCK)
    tiled = [_tile_slots(slots[:, :Ta], part_a[6]), _tile_slots(slots[:, Ta:], part_b[6])]
    xs = _scatter(tiled[0], tiled[1], part_a[0], part_b[0], zero_plan, nb, MOE_BLOCK)
    y = _experts(xs, plan, nb, w1, w3, w2, layer, MOE_BLOCK)
    return [_combine(sl, y, x, mod, tpm, rt, tt) for sl, (_, rt, _, x, mod, tpm, tt) in zip(tiled, (part_a, part_b))]


def _kvq_kernel(x_ref, modkv_ref, mod_ref, nkv_ref, nm_ref, wkv_ref, wq_ref, kn_ref, qn_ref, eavg_ref,
                *rest, tm, keep):
    residue = keep is not None
    x = x_ref[...]
    xn = x * lax.rsqrt(jnp.mean(x * x, axis=-1, keepdims=True) + EPS)
    modkv = modkv_ref[0]
    mod = mod_ref[0]
    hk = (xn * nkv_ref[...] * (1.0 + modkv[:, D_MODEL:]) + modkv[:, :D_MODEL]).astype(BF16)
    hq = (xn * nm_ref[...] * (1.0 + mod[:, D_MODEL:2 * D_MODEL]) + mod[:, :D_MODEL]).astype(BF16)
    eavg = eavg_ref[...]

    def head_norm(a, g):
        sq = (a * a).astype(BF16)
        half = ATTN_WIDTH // 2
        ms = jnp.concatenate([jnp.dot(sq[:, :half], eavg, preferred_element_type=F32),
                              jnp.dot(sq[:, half:], eavg, preferred_element_type=F32)], axis=1)
        return a * lax.rsqrt(ms + EPS) * g

    def residue_major(val, ref, dil, stage):
        if dil == 1:
            ref[0, 0] = val.astype(BF16)
            return
        for c in range(stage.shape[0]):
            stage[c] = val[:, c * LANES:(c + 1) * LANES]
        for r in range(dil):
            rows = [stage[c, pl.ds(r, tm // dil, stride=dil), :] for c in range(stage.shape[0])]
            ref[0, r] = jnp.concatenate(rows, axis=1).astype(BF16)

    for g, (_, dil) in enumerate(BRANCHES):
        k0 = 2 * g * ATTN_WIDTH
        kvg = jnp.dot(hk, wkv_ref[:, k0:k0 + 2 * ATTN_WIDTH], preferred_element_type=F32)
        kn = head_norm(kvg[:, :ATTN_WIDTH], kn_ref[g:g + 1, :])
        vv = kvg[:, ATTN_WIDTH:]
        q0 = g * ATTN_WIDTH
        qg = jnp.dot(hq, wq_ref[:, q0:q0 + ATTN_WIDTH], preferred_element_type=F32)
        qg = head_norm(qg, qn_ref[g:g + 1, :]) * (HEAD_DIM ** -0.5)
        if residue:
            stages = rest[4 * N_BRANCH:]
            residue_major(qg, rest[3 * g], dil, stages[3 * g])
            residue_major(kn, rest[3 * g + 1], dil, stages[3 * g + 1])
            residue_major(vv, rest[3 * g + 2], dil, stages[3 * g + 2])
            kt_ref = rest[3 * N_BRANCH + g]
            kt_ref[0, :ATTN_WIDTH, :] = kn.T
            kt_ref[0, ATTN_WIDTH:, :] = vv.T
        else:
            rest[0][:, k0:k0 + ATTN_WIDTH] = kn
            rest[0][:, k0 + ATTN_WIDTH:k0 + 2 * ATTN_WIDTH] = vv
            rest[1][:, q0:q0 + ATTN_WIDTH] = qg


def _kvq(x, modkv, mod, tiles_per_mod, tm, nkv, nm, wkv, wq, kn, qn, eavg, batch=None):
    T = x.shape[0]
    tok = lambda w: pl.BlockSpec((tm, w), lambda i: (i, 0))
    kvw = 2 * N_BRANCH * ATTN_WIDTH
    if batch is not None:
        L = T // batch
        tpb = L // tm
        kept = tuple(-(-min(w, L) // tm) for w, _ in BRANCHES)
        keep = (tpb, kept)
        out_specs, out_shape = [], []
        for _, dil in BRANCHES:
            for _ in range(3):
                out_specs.append(pl.BlockSpec((1, dil, tm // dil, ATTN_WIDTH), lambda i: (i // tpb, 0, i % tpb, 0)))
                out_shape.append(jax.ShapeDtypeStruct((batch, dil, L // dil, ATTN_WIDTH), BF16))
        for kg in kept:
            out_specs.append(pl.BlockSpec((1, 2 * ATTN_WIDTH, tm),
                                          lambda i, kg=kg: (i // tpb, 0, jnp.maximum(i % tpb - (tpb - kg), 0))))
            out_shape.append(jax.ShapeDtypeStruct((batch, 2 * ATTN_WIDTH, kg * tm), F32))
        scratch = [pltpu.VMEM((ATTN_WIDTH // LANES, tm if dil > 1 else 8, LANES), F32)
                   for _, dil in BRANCHES for _ in range(3)]
    else:
        keep = None
        out_specs = [tok(kvw), tok(N_BRANCH * ATTN_WIDTH)]
        out_shape = [jax.ShapeDtypeStruct((T, kvw), F32), jax.ShapeDtypeStruct((T, N_BRANCH * ATTN_WIDTH), F32)]
        scratch = []
    return pl.pallas_call(
        functools.partial(_kvq_kernel, tm=tm, keep=keep),
        grid=(T // tm,),
        in_specs=[tok(D_MODEL), _mod_spec(modkv, tiles_per_mod), _mod_spec(mod, tiles_per_mod),
                  _resident(nkv.shape), _resident(nm.shape), _resident(wkv.shape), _resident(wq.shape),
                  _resident(kn.shape), _resident(qn.shape), _resident(eavg.shape)],
        out_specs=out_specs,
        out_shape=out_shape,
        scratch_shapes=scratch,
        compiler_params=_cparams(("arbitrary",)),
        name="kvq",
    )(x, modkv, mod, nkv, nm, wkv, wq, kn, qn, eavg)


BAND_QBLOCKS = 16


def _band_attn_kernel(q_ref, kp_ref, kc_ref, vp_ref, vc_ref, bias_ref, o_ref, lse_ref):
    n = pl.program_id(2)
    hw = ATTN_WIDTH // 2
    hh = HEADS // 2
    col = lax.broadcasted_iota(jnp.int32, (hh * N_STEPS, 2 * N_STEPS), 1)
    first = jnp.logical_or(col >= N_STEPS, n > 0)
    lane = lax.broadcasted_iota(jnp.int32, (N_STEPS, LANES), 1)
    head_of = lax.broadcasted_iota(jnp.int32, (N_STEPS, hw), 1) // HEAD_DIM
    zero = jnp.zeros((N_STEPS, hw), BF16)
    nt = (((1,), (1,)), ((), ()))
    for r, j in [(r, j) for r in range(q_ref.shape[0]) for j in range(q_ref.shape[1] // N_STEPS)]:
        if j == 0:
            k_all = jnp.concatenate([kp_ref[r], kc_ref[r]], axis=0)
            v_all = jnp.concatenate([vp_ref[r], vc_ref[r]], axis=0)
        rows = slice(j * N_STEPS, (j + 1) * N_STEPS)
        q = q_ref[r, rows, :]
        k2 = k_all[j * N_STEPS:(j + 2) * N_STEPS]
        v2 = v_all[j * N_STEPS:(j + 2) * N_STEPS]
        lse_all = jnp.zeros((N_STEPS, LANES), F32)
        halves = []
        for c in range(2):
            cols = slice(c * hw, (c + 1) * hw)
            qc = q[:, cols]
            qbd = jnp.concatenate([jnp.where(head_of == h, qc, zero) for h in range(hh)], axis=0)
            s = (lax.dot_general(qbd, k2[:, cols], nt, preferred_element_type=F32)
                 + bias_ref[c * hh * N_STEPS:(c + 1) * hh * N_STEPS, :])
            if j == 0:
                s = jnp.where(first, s, NEG)
            m = jnp.max(s, axis=1, keepdims=True)
            p = jnp.exp(s - m)
            l = jnp.sum(p, axis=1, keepdims=True)
            pv = jnp.dot(p.astype(BF16), v2[:, cols], preferred_element_type=F32) / l
            lse = m + jnp.log(l)
            o = jnp.zeros((N_STEPS, hw), F32)
            for h in range(hh):
                blk = slice(h * N_STEPS, (h + 1) * N_STEPS)
                o = jnp.where(head_of == h, pv[blk], o)
                lse_all = jnp.where(lane == c * hh + h, lse[blk], lse_all)
            halves.append(o)
        o_ref[r, rows, :] = jnp.concatenate(halves, axis=1)
        lse_ref[r, rows, :] = lse_all


def _band_attn(q, k, v, bias, dil):
    B, _, Ld, _ = q.shape
    nq = min(BAND_QBLOCKS, Ld // N_STEPS)
    nr = min(BAND_QBLOCKS // nq, dil)
    step = nq * N_STEPS
    blk = lambda w, f: pl.BlockSpec((None, nr, step, w), f)
    cur = lambda b, r, n: (b, r, n, 0)
    prev_blk = pl.BlockSpec((None, nr, N_STEPS, ATTN_WIDTH), lambda b, r, n: (b, r, jnp.maximum(nq * n - 1, 0), 0))
    return pl.pallas_call(
        _band_attn_kernel,
        grid=(B, dil // nr, Ld // step),
        in_specs=[blk(ATTN_WIDTH, cur), prev_blk, blk(ATTN_WIDTH, cur), prev_blk,
                  blk(ATTN_WIDTH, cur), _resident(bias.shape)],
        out_specs=[blk(ATTN_WIDTH, cur), blk(LANES, cur)],
        out_shape=[jax.ShapeDtypeStruct((B, dil, Ld, ATTN_WIDTH), F32),
                   jax.ShapeDtypeStruct((B, dil, Ld, LANES), F32)],
        compiler_params=_cparams(("arbitrary", "arbitrary", "arbitrary")),
        name=f"band_attn_d{dil}",
    )(q, k, k, v, v, bias)


def _rel_buckets(dilation):
    n = np.arange(N_STEPS + 1) * dilation
    large = MAX_EXACT + (np.log(np.maximum(n, 1) / MAX_EXACT) / np.log(REL_MAX_DIST / MAX_EXACT)
                         * (NUM_BUCKETS - MAX_EXACT)).astype(np.int32)
    return np.where(n < MAX_EXACT, n, np.minimum(large, NUM_BUCKETS - 1)).astype(np.int32)


def _step_bias_row(rel_bias, g, dil):
    onehot = np.zeros((N_STEPS + 1, NUM_BUCKETS), np.float32)
    onehot[np.arange(N_STEPS + 1), _rel_buckets(dil)] = 1.0
    return jnp.dot(jnp.asarray(onehot), rel_bias[:, g, :].astype(F32), precision=lax.Precision.HIGHEST).T


def _band_bias(rel_bias, g, dil):
    bias = _step_bias_row(rel_bias, g, dil)
    P = 3 * N_STEPS
    neg = jnp.full((HEADS, N_STEPS), NEG, F32)
    ext = jnp.concatenate([neg, bias[:, ::-1], neg], axis=1)
    flat = jnp.broadcast_to(ext[:, None, :], (HEADS, N_STEPS, P + 1)).reshape(HEADS, N_STEPS * (P + 1))
    skew = flat[:, :N_STEPS * P].reshape(HEADS, N_STEPS, P)
    return skew[:, :, N_STEPS:].reshape(HEADS * N_STEPS, 2 * N_STEPS)


def _step_attn_kernel(q_ref, kvn_ref, c0_ref, c1_ref, c2_ref, b0_ref, b1_ref, b2_ref, bn_ref, o_ref, *, n_new):
    caches = (c0_ref, c1_ref, c2_ref)
    cbias = (b0_ref, b1_ref, b2_ref)
    nt = (((1,), (1,)), ((), ()))
    rows = n_new * HEADS
    row = lax.broadcasted_iota(jnp.int32, (rows, ATTN_WIDTH), 0)
    lane = lax.broadcasted_iota(jnp.int32, (rows, ATTN_WIDTH), 1)
    own_head = lane // HEAD_DIM == row % HEADS
    parts = []
    for g in range(N_BRANCH):
        klo = 2 * g * ATTN_WIDTH
        qg = q_ref[0, :, g * ATTN_WIDTH:(g + 1) * ATTN_WIDTH]
        qrep = jnp.concatenate([jnp.broadcast_to(qg[s:s + 1, :], (HEADS, ATTN_WIDTH)) for s in range(n_new)], axis=0)
        qbd = jnp.where(own_head, qrep, 0.0).astype(BF16)
        kn = kvn_ref[0, :, klo:klo + ATTN_WIDTH].astype(BF16)
        vn = kvn_ref[0, :, klo + ATTN_WIDTH:klo + 2 * ATTN_WIDTH].astype(BF16)
        kt = caches[g][0, 0].astype(BF16)
        vt = caches[g][0, 1].astype(BF16)
        sc = jnp.dot(qbd, kt, preferred_element_type=F32) + cbias[g][...]
        sn = lax.dot_general(qbd, kn, nt, preferred_element_type=F32) + bn_ref[g]
        m = jnp.maximum(jnp.max(sc, axis=1, keepdims=True), jnp.max(sn, axis=1, keepdims=True))
        pc = jnp.exp(sc - m)
        pn = jnp.exp(sn - m)
        l = jnp.sum(pc, axis=1, keepdims=True) + jnp.sum(pn, axis=1, keepdims=True)
        pv = (lax.dot_general(pc.astype(BF16), vt, nt, preferred_element_type=F32)
              + jnp.dot(pn.astype(BF16), vn, preferred_element_type=F32))
        parts.append((m + jnp.log(l), pv / l))
    mx = jnp.maximum(jnp.maximum(parts[0][0], parts[1][0]), parts[2][0])
    es = [jnp.exp(lse - mx) for lse, _ in parts]
    den = es[0] + es[1] + es[2]
    mixed = (es[0] * parts[0][1] + es[1] * parts[1][1] + es[2] * parts[2][1]) / den
    mixed = jnp.where(own_head, mixed, 0.0)
    for s in range(n_new):
        o_ref[0, s:s + 1, :] = jnp.sum(mixed[s * HEADS:(s + 1) * HEADS], axis=0, keepdims=True)


def _step_bias(rel_bias, n_new):
    cache_tabs, new_tabs = [], []
    for g, (win, dil) in enumerate(BRANCHES):
        bias = _step_bias_row(rel_bias, g, dil)
        rev = bias[:, ::-1]
        per_s = []
        for s in range(n_new):
            if dil == 1:
                neg = jnp.full((HEADS, s), NEG, F32)
                per_s.append(jnp.concatenate([neg, rev[:, :win - s]], axis=1))
            else:
                cols = [rev[:, :N_STEPS] if r == s % dil else jnp.full((HEADS, N_STEPS), NEG, F32)
                        for r in range(dil)]
                per_s.append(jnp.stack(cols, axis=2).reshape(HEADS, win))
        cache_tabs.append(jnp.stack(per_s, axis=0).reshape(n_new * HEADS, win))
        dist = np.arange(n_new)[:, None] - np.arange(8)[None, :]
        ok = (dist >= 0) & (dist % dil == 0) & (np.arange(8)[None, :] < n_new)
        onehot = np.zeros((N_STEPS + 1, n_new * 8), np.float32)
        onehot[np.where(ok, dist // dil, 0).reshape(-1), np.arange(n_new * 8)] = 1.0
        tab = jnp.dot(bias, jnp.asarray(onehot), precision=lax.Precision.HIGHEST).reshape(HEADS, n_new, 8)
        new_tabs.append(jnp.where(ok[None], tab, NEG).transpose(1, 0, 2).reshape(n_new * HEADS, 8))
    return cache_tabs, jnp.stack(new_tabs)


def _step_attn(q, kvn, caches, cache_bias, new_bias, DB, S):
    cspec = lambda c: pl.BlockSpec((1,) + c.shape[1:], lambda b: (b, 0, 0, 0))
    return pl.pallas_call(
        functools.partial(_step_attn_kernel, n_new=S),
        grid=(DB,),
        in_specs=[pl.BlockSpec((1, S, N_BRANCH * ATTN_WIDTH), lambda b: (b, 0, 0)),
                  pl.BlockSpec((1, 8, 2 * N_BRANCH * ATTN_WIDTH), lambda b: (b, 0, 0)),
                  cspec(caches[0]), cspec(caches[1]), cspec(caches[2]),
                  _resident(cache_bias[0].shape), _resident(cache_bias[1].shape), _resident(cache_bias[2].shape),
                  _resident(new_bias.shape)],
        out_specs=pl.BlockSpec((1, S, ATTN_WIDTH), lambda b: (b, 0, 0)),
        out_shape=jax.ShapeDtypeStruct((DB, S, ATTN_WIDTH), F32),
        compiler_params=_cparams(("arbitrary",)),
        name="step_attn",
    )(q, kvn, caches[0], caches[1], caches[2], *cache_bias, new_bias)


def _attn_out_kernel(*refs, dils, tm):
    n_o = max(len(dils), 1)
    n_in = n_o + len(dils) + 9
    (x_ref, mod_ref, wo_ref, ex_ref, nf_ref, wr_ref, br_ref, cin_ref, tri_ref,
     x3_ref, h3_ref, rt_ref, rtt_ref, cnt_ref, carry) = refs[n_in - 9:n_in + 6]
    stages = refs[n_in + 6:]
    _start_counts(cin_ref, carry)

    def natural(ref, stage, dil):
        if dil == 1:
            return ref[0, 0]
        for r in range(dil):
            val = ref[0, r]
            for c in range(stage.shape[0]):
                stage[c, pl.ds(r, tm // dil, stride=dil), :] = val[:, c * LANES:(c + 1) * LANES]
        return jnp.concatenate([stage[c] for c in range(stage.shape[0])], axis=1)

    if dils:
        lses = [natural(refs[n_o + g], stages[2 * g + 1], dil) for g, dil in enumerate(dils)]
        mx = functools.reduce(jnp.maximum, lses)
        es = [jnp.exp(l - mx) for l in lses]
        den = functools.reduce(lambda a, b: a + b, es)
        o = None
        for g, dil in enumerate(dils):
            term = _split_dot(es[g] / den, ex_ref[...]) * natural(refs[g], stages[2 * g], dil)
            o = term if o is None else o + term
    else:
        o = refs[0][...]
    mod = mod_ref[0]
    a = jnp.dot(o.astype(BF16), wo_ref[...], preferred_element_type=F32)
    x3 = x_ref[...] + mod[:, 2 * D_MODEL:3 * D_MODEL] * a
    x3_ref[...] = x3
    _ffn_pre(x3, mod, nf_ref, wr_ref, br_ref, tri_ref, h3_ref, rt_ref, rtt_ref, cnt_ref, carry, tm)


def _attn_out(os_, lses, x, mod, tiles_per_mod, tm, wo, ex, nf, wr, br, counts):
    T = x.shape[0]
    tok = lambda w: pl.BlockSpec((tm, w), lambda i: (i, 0))
    r_specs, r_shapes = _route_outs(T, tm)
    tri = _earlier_rows(tm)
    if lses:
        dils = tuple(o.shape[1] for o in os_)
        tpb = tiles_per_mod
        res = lambda a: pl.BlockSpec((1, a.shape[1], tm // a.shape[1], a.shape[3]), lambda i: (i // tpb, 0, i % tpb, 0))
        o_specs = [res(a) for a in os_] + [res(a) for a in lses]
        scratch = []
        for _ in dils:
            scratch += [pltpu.VMEM((ATTN_WIDTH // LANES, tm, LANES), F32), pltpu.VMEM((1, tm, LANES), F32)]
    else:
        dils, o_specs, scratch = (), [tok(ATTN_WIDTH)], []
    return pl.pallas_call(
        functools.partial(_attn_out_kernel, dils=dils, tm=tm),
        grid=(T // tm,),
        in_specs=(o_specs + [tok(D_MODEL), _mod_spec(mod, tiles_per_mod), _resident(wo.shape), _resident(ex.shape),
                             _resident(nf.shape), _resident(wr.shape), _resident(br.shape),
                             _resident(counts.shape), _resident(tri.shape)]),
        out_specs=[tok(D_MODEL), pl.BlockSpec((tm * ROW_TILES, LANES), lambda i: (i, 0))] + r_specs,
        out_shape=[jax.ShapeDtypeStruct((T, D_MODEL), F32),
                   jax.ShapeDtypeStruct((T * ROW_TILES, LANES), F32)] + r_shapes,
        scratch_shapes=[pltpu.VMEM((1, LANES), F32)] + scratch,
        compiler_params=_cparams(("arbitrary",)),
        name="attn_out",
    )(*os_, *lses, x, mod, wo, ex, nf, wr, br, counts, tri)


def kernel(x_prompt, x_sample, cache_kv_w128, cache_kv_w512, cache_kv_w2048, c_prompt, c_sample, ada_w, ada_b, norm_mix, norm_ffn, a_w_in, a_b_in, a_norm_v, a_w_s, a_b_s, a_w_out, kv_ada_w, kv_ada_b, kv_norm, w_kv, k_norm, rel_bias, b_w_q, q_norm, b_w_o, r_w_group, r_b_group, r_w_expert, r_b_expert, e_w1, e_w3, e_w2):
    B, L, _ = x_prompt.shape
    DB, S, _ = x_sample.shape
    Tp, Ts = B * L, DB * S
    tm = TOKEN_TILE
    tpm_p = L // tm

    c_all = jnp.concatenate([c_prompt, c_sample], axis=0)
    R = c_all.shape[0]
    c_all = jnp.pad(c_all, ((0, -R % 8), (0, 0)))
    mods = [_ada(c_all, ada_w, ada_b, l) for l in range(2)]
    modkv = _ada(c_all, kv_ada_w[None], kv_ada_b[None], 0)

    def split_mod(m):
        return m[:B, None, :], jnp.repeat(m[B:B + DB], S, axis=0)[None]
    mod_p, mod_s = zip(*[split_mod(m) for m in mods])
    modkv_p, modkv_s = split_mod(modkv)

    row = lambda a: a.reshape(1, -1)

    def router(l):
        wr = jnp.zeros((D_MODEL, LANES), F32)
        wr = wr.at[:, :N_GROUPS].set(r_w_group[l]).at[:, N_GROUPS:N_GROUPS + N_EXPERTS].set(r_w_expert[l])
        br = jnp.zeros((1, LANES), F32)
        br = br.at[0, :N_GROUPS].set(r_b_group[l]).at[0, N_GROUPS:N_GROUPS + N_EXPERTS].set(r_b_expert[l])
        return wr.astype(BF16), br

    no_counts = jnp.zeros((1, LANES), F32)

    win = a_w_in[0].astype(BF16)
    wout = a_w_out[0].astype(BF16)
    tril = jnp.tril(jnp.ones((CHUNK, CHUNK), bool))
    ws_p = jnp.where(tril, a_w_s[0], 0).astype(BF16)
    bs_p = a_b_s[0].T
    cs = min(CHUNK, S)
    ws_small = jnp.where(jnp.tril(jnp.ones((cs, cs), bool)), a_w_s[0][:, :cs, :cs], 0)
    ws_s = jnp.stack([jnp.kron(jnp.eye(Ts // cs, dtype=F32), ws_small[g]) for g in range(GMLP_GROUPS)]).astype(BF16)
    bs_s = jnp.tile(a_b_s[0][:, :cs], (1, Ts // cs)).T
    wr0, br0 = router(0)
    common = (row(norm_mix[0]), win, row(a_b_in[0]), row(a_norm_v[0]))
    xp = x_prompt.reshape(Tp, D_MODEL)
    xs_ = x_sample.reshape(Ts, D_MODEL)
    x1_p, h2_p, rt_p, rtt_p, cnt = _mixer_a(xp, mod_p[0], tpm_p, tm, *common, ws_p, bs_p, wout, row(norm_ffn[0]),
                                            wr0, br0, no_counts, with_v=False)
    x1_s, h2_s, rt_s, rtt_s, cnt, v_s = _mixer_a(xs_, mod_s[0], 1, Ts, *common, ws_s, bs_s, wout, row(norm_ffn[0]),
                                                 wr0, br0, cnt, with_v=True)
    x2_p, x2_s = _moe((h2_p, rt_p, rtt_p, x1_p, mod_p[0], L // MOE_TILE, MOE_TILE),
                      (h2_s, rt_s, rtt_s, x1_s, mod_s[0], 1, Ts), cnt,
                      e_w1, e_w3, e_w2, 0)

    wkv = w_kv.astype(BF16)
    wq = b_w_q[0].astype(BF16)
    kn = jnp.tile(k_norm, (1, HEADS))
    qn = jnp.tile(q_norm[0], (1, HEADS))
    head = np.arange(ATTN_WIDTH) // HEAD_DIM
    half_head = head[:ATTN_WIDTH // 2]
    eavg = jnp.asarray((half_head[:, None] == half_head[None, :]) / HEAD_DIM, BF16)
    kvq_w = (row(kv_norm), row(norm_mix[1]), wkv, wq, kn, qn, eavg)
    *qkv_p, kt0, kt1, kt2 = _kvq(x2_p, modkv_p, mod_p[1], tpm_p, tm, *kvq_w, batch=B)
    kv_s, q_s = _kvq(x2_s, modkv_s, mod_s[1], 1, Ts, *kvq_w)

    os_, lses = [], []
    for g, (_, dil) in enumerate(BRANCHES):
        o, lse = _band_attn(*qkv_p[3 * g:3 * g + 3], _band_bias(rel_bias, g, dil), dil)
        os_.append(o)
        lses.append(lse)
    caches = [jnp.transpose(c, (0, 2, 3, 4, 1)).reshape(DB, 2, ATTN_WIDTH, c.shape[1])
              for c in (cache_kv_w128, cache_kv_w512, cache_kv_w2048)]
    kvn_s = jnp.pad(kv_s.reshape(DB, S, -1), ((0, 0), (0, 8 - S), (0, 0)))
    o_s = _step_attn(q_s.reshape(DB, S, -1), kvn_s, caches, *_step_bias(rel_bias, S), DB, S)

    wo = b_w_o[0].astype(BF16)
    ex = jnp.asarray(np.arange(LANES)[:, None] == head[None, :], BF16)
    wr1, br1 = router(1)
    x3_p, h3_p, rt1_p, rtt1_p, cnt1 = _attn_out(os_, lses, x2_p, mod_p[1], tpm_p, tm, wo, ex, row(norm_ffn[1]),
                                                wr1, br1, no_counts)
    x3_s, h3_s, rt1_s, rtt1_s, cnt1 = _attn_out([o_s.reshape(Ts, ATTN_WIDTH)], [], x2_s, mod_s[1], 1, Ts, wo, ex,
                                                row(norm_ffn[1]), wr1, br1, cnt1)
    y_p, y_s = _moe((h3_p, rt1_p, rtt1_p, x3_p, mod_p[1], L // MOE_TILE, MOE_TILE),
                    (h3_s, rt1_s, rtt1_s, x3_s, mod_s[1], 1, Ts),
                    cnt1, e_w1, e_w3, e_w2, 1)

    kv_s4 = kv_s.reshape(DB, S, N_BRANCH, 2, HEADS, HEAD_DIM)

    def window(kt, w):
        n = min(w, L)
        return jnp.transpose(kt[:, :, kt.shape[2] - n:].reshape(B, 2, HEADS, HEAD_DIM, n), (0, 4, 1, 2, 3))
    return (y_p.reshape(B, L, D_MODEL), y_s.reshape(DB, S, D_MODEL),
            window(kt0, BRANCHES[0][0]), window(kt1, BRANCHES[1][0]), window(kt2, BRANCHES[2][0]),
            kv_s4[:, :, 0], kv_s4[:, :, 1], kv_s4[:, :, 2],
            v_s.reshape(1, DB, S, GMLP_WIDTH))
```
